```python
import math
import jax, jax.numpy as jnp
from jax import lax
import numpy as np

D_MODEL = 1024
BATCH = 1
SEQ = 16384
DEPTH = 1
DEC_BATCH = 32
DEC_SEQ = 2048
PAST_LEN = 128

HEAD_DIM = 64
HEADS_PER_GROUP = 4
DILATED_GROUPS = ((128, 1), (512, 4), (2048, 16))
N_GROUPS = len(DILATED_GROUPS)
ATTN_QKV_WIDTH = N_GROUPS * HEADS_PER_GROUP * HEAD_DIM
ATTN_OUT_WIDTH = HEADS_PER_GROUP * HEAD_DIM
ROPE_THETA = 10000.0
NEG_INF = -1e30
F_GROUPS = 4
F_GROUP_DIM = 64
F_WIDTH = F_GROUPS * F_GROUP_DIM
N_BRANCHES = 2
IN_WIDTH = 3 * ATTN_QKV_WIDTH + F_WIDTH + N_BRANCHES * D_MODEL
N_EXPERTS = 32
TOP_K = 4
D_FF = D_MODEL
SWIGLU_LIMIT = 7.0
SWIGLU_ALPHA = 1.702
MOE_BLOCK = 128
LN_EPS = 1e-5
DEEPNORM_ALPHA = (2 * DEPTH) ** 0.25
DEEPNORM_BETA = (8 * DEPTH) ** -0.25

kernel_name = 'hybrid_dilated_fourier_moe_encoder'


def layer_norm(x, g, b):
    xf = x.astype(jnp.float32)
    mu = jnp.mean(xf, axis=-1, keepdims=True)
    var = jnp.mean(jnp.square(xf - mu), axis=-1, keepdims=True)
    y = (xf - mu) * lax.rsqrt(var + LN_EPS) * g.astype(jnp.float32) + b.astype(jnp.float32)
    return y.astype(x.dtype)


def rotary(t, positions):
    dh = t.shape[-1]
    inv = ROPE_THETA ** (-jnp.arange(0, dh, 2, dtype=jnp.float32) / dh)
    ang = positions.astype(jnp.float32)[:, None] * inv[None, :]
    cos = jnp.cos(ang)[None, :, None, :]
    sin = jnp.sin(ang)[None, :, None, :]
    tf = t.astype(jnp.float32)
    t1, t2 = tf[..., : dh // 2], tf[..., dh // 2:]
    return jnp.concatenate([t1 * cos - t2 * sin, t1 * sin + t2 * cos], axis=-1).astype(t.dtype)


def dilated_window_attention(q, k, v, dil, half):
    b_, s_, h_, dh = q.shape
    L = s_ // dil
    nblk = -(-L // half)
    Lp = nblk * half

    def to_sub(t):
        return t.reshape(b_, L, dil, h_, dh).transpose(0, 2, 3, 1, 4)

    qs, ks, vs = to_sub(q), to_sub(k), to_sub(v)
    qb = jnp.pad(qs, ((0, 0), (0, 0), (0, 0), (0, Lp - L), (0, 0))).reshape(b_, dil, h_, nblk, half, dh)

    def windows(t):
        tp = jnp.pad(t, ((0, 0), (0, 0), (0, 0), (half, Lp - L + half), (0, 0)))
        tp = tp.reshape(b_, dil, h_, nblk + 2, half, dh)
        return jnp.concatenate([tp[:, :, :, :-2], tp[:, :, :, 1:-1], tp[:, :, :, 2:]], axis=-2)

    kw, vw = windows(ks), windows(vs)
    s = jnp.einsum('bdhnqc,bdhnkc->bdhnqk', qb, kw, preferred_element_type=jnp.float32) * (dh ** -0.5)
    qi = jnp.arange(half)[:, None]
    ki = jnp.arange(3 * half)[None, :]
    rel = ki - half - qi
    key_pos = jnp.arange(nblk)[:, None] * half + jnp.arange(3 * half)[None, :] - half
    valid = (jnp.abs(rel) <= half)[None, :, :] & ((key_pos >= 0) & (key_pos < L))[:, None, :]
    s = jnp.where(valid, s, NEG_INF)
    lse = jax.nn.logsumexp(s, axis=-1)
    p = jnp.exp(s - lse[..., None])
    o = jnp.einsum('bdhnqk,bdhnkc->bdhnqc', p.astype(v.dtype), vw)
    o = o.reshape(b_, dil, h_, Lp, dh)[:, :, :, :L].transpose(0, 3, 1, 2, 4).reshape(b_, s_, h_, dh)
    lse = lse.reshape(b_, dil, h_, Lp)[:, :, :, :L].transpose(0, 3, 1, 2).reshape(b_, s_, h_)
    return o, lse


def fourier_mix(f):
    b_, s_, _ = f.shape
    fg = f.reshape(b_, s_, F_GROUPS, F_GROUP_DIM).astype(jnp.float32)
    fr = jnp.real(jnp.fft.fft2(fg, axes=(1, 3), norm='ortho'))
    return fr.reshape(b_, s_, F_WIDTH).astype(f.dtype)


def clamped_swiglu(h):
    glu = jnp.minimum(h[..., ::2], SWIGLU_LIMIT)
    lin = jnp.clip(h[..., 1::2], -SWIGLU_LIMIT, SWIGLU_LIMIT)
    return glu * jax.nn.sigmoid(SWIGLU_ALPHA * glu) * (lin + 1.0)


def moe_ffn(x, w_router, b_router, w_up, b_up, w_down, b_down):
    b_, s_, d = x.shape
    xt = x.reshape(-1, d)
    n = xt.shape[0]
    logits = jnp.matmul(xt, w_router, preferred_element_type=jnp.float32) + b_router.astype(jnp.float32)
    top_vals, top_idx = lax.top_k(logits, TOP_K)
    gates = jax.nn.softmax(top_vals, axis=-1)
    nk = n * TOP_K
    flat_e = top_idx.reshape(nk)
    flat_tok = (jnp.arange(nk, dtype=jnp.int32) // TOP_K).astype(jnp.int32)
    flat_g = gates.reshape(nk)
    order = jnp.argsort(flat_e)
    se = flat_e[order]
    counts = jnp.bincount(flat_e, length=N_EXPERTS)
    padded = (counts + MOE_BLOCK - 1) // MOE_BLOCK * MOE_BLOCK
    pad_end = jnp.cumsum(padded)
    pad_start = pad_end - padded
    start = jnp.cumsum(counts) - counts
    dest = pad_start[se] + jnp.arange(nk) - start[se]
    n_blocks = -(-nk // MOE_BLOCK) + N_EXPERTS
    rows = n_blocks * MOE_BLOCK
    row_tok = jnp.full((rows,), n, dtype=jnp.int32).at[dest].set(flat_tok[order])
    row_gate = jnp.zeros((rows,), jnp.float32).at[dest].set(flat_g[order])
    block_e = jnp.minimum(jnp.searchsorted(pad_end, jnp.arange(n_blocks) * MOE_BLOCK, side='right'), N_EXPERTS - 1)
    x_pad = jnp.concatenate([xt, jnp.zeros((1, d), xt.dtype)], axis=0)
    xb = x_pad[row_tok].reshape(n_blocks, MOE_BLOCK, d)

    def expert_block(args):
        xblk, e = args
        h = jnp.matmul(xblk, w_up[e]) + b_up[e]
        return jnp.matmul(clamped_swiglu(h), w_down[e]) + b_down[e]

    yb = lax.map(expert_block, (xb, block_e))
    y = yb.reshape(rows, d).astype(jnp.float32) * row_gate[:, None]
    out = jnp.zeros((n + 1, d), jnp.float32).at[row_tok].add(y)[:n]
    return out.astype(x.dtype).reshape(b_, s_, d)


def encoder_layer(x, w_in, w_attn_out, w_four_out, w_o, ln1_g, ln1_b,
                  w_router, b_router, w_up, b_up, w_down, b_down, ln2_g, ln2_b):
    b_, s_, d = x.shape
    proj = jnp.matmul(x, w_in)
    a = ATTN_QKV_WIDTH
    q = proj[..., 0:a]
    k = proj[..., a:2 * a]
    v = proj[..., 2 * a:3 * a]
    f = proj[..., 3 * a:3 * a + F_WIDTH]
    g = proj[..., 3 * a + F_WIDTH:]
    hshape = (b_, s_, N_GROUPS * HEADS_PER_GROUP, HEAD_DIM)
    pos = jnp.arange(s_)
    q = rotary(q.reshape(hshape), pos)
    k = rotary(k.reshape(hshape), pos)
    v = v.reshape(hshape)
    outs, lses = [], []
    for gi, (window, dil) in enumerate(DILATED_GROUPS):
        sl = slice(gi * HEADS_PER_GROUP, (gi + 1) * HEADS_PER_GROUP)
        o, l = dilated_window_attention(q[:, :, sl], k[:, :, sl], v[:, :, sl], dil, window // (2 * dil))
        outs.append(o)
        lses.append(l)
    w = jax.nn.softmax(jnp.stack(lses, axis=0), axis=0)
    attn = jnp.einsum('gbsh,gbshc->bshc', w, jnp.stack(outs, axis=0).astype(jnp.float32))
    attn = attn.reshape(b_, s_, ATTN_OUT_WIDTH).astype(x.dtype)
    four = fourier_mix(f)
    gate_a = jax.nn.sigmoid(g[..., :d])
    gate_f = jax.nn.sigmoid(g[..., d:])
    merged = gate_a * jnp.matmul(attn, w_attn_out) + gate_f * jnp.matmul(four, w_four_out)
    mix = jnp.matmul(merged, w_o)
    x = layer_norm(DEEPNORM_ALPHA * x + mix, ln1_g, ln1_b)
    ffn = moe_ffn(x, w_router, b_router, w_up, b_up, w_down, b_down)
    x = layer_norm(DEEPNORM_ALPHA * x + ffn, ln2_g, ln2_b)
    return x


def setup_inputs(seed: int = 0) -> dict:
    key = jax.random.key(seed)
    ks = jax.random.split(key, 16)
    f32 = jnp.float32
    x_prompt = jax.random.normal(ks[0], (BATCH, SEQ, D_MODEL), f32)
    x_sample = jax.random.normal(ks[1], (DEC_BATCH, DEC_SEQ, D_MODEL), f32)
    col_scale = jnp.ones((IN_WIDTH,), f32).at[2 * ATTN_QKV_WIDTH:3 * ATTN_QKV_WIDTH].set(DEEPNORM_BETA)
    w_in = jax.random.normal(ks[2], (DEPTH, D_MODEL, IN_WIDTH), f32) * (D_MODEL ** -0.5) * col_scale
    w_attn_out = jax.random.normal(ks[3], (DEPTH, ATTN_OUT_WIDTH, D_MODEL), f32) * (ATTN_OUT_WIDTH ** -0.5) * DEEPNORM_BETA
    w_four_out = jax.random.normal(ks[4], (DEPTH, F_WIDTH, D_MODEL), f32) * (F_WIDTH ** -0.5) * DEEPNORM_BETA
    w_o = jax.random.normal(ks[5], (DEPTH, D_MODEL, D_MODEL), f32) * (D_MODEL ** -0.5) * DEEPNORM_BETA
    ln1_g = 1.0 + 0.02 * jax.random.normal(ks[6], (DEPTH, D_MODEL), f32)
    ln1_b = 0.02 * jax.random.normal(ks[7], (DEPTH, D_MODEL), f32)
    w_router = jax.random.normal(ks[8], (DEPTH, D_MODEL, N_EXPERTS), f32) * (D_MODEL ** -0.5)
    b_router = 0.01 * jax.random.normal(ks[9], (DEPTH, N_EXPERTS), f32)
    w_up = jax.random.normal(ks[10], (DEPTH, N_EXPERTS, D_MODEL, 2 * D_FF), f32) * (D_MODEL ** -0.5) * DEEPNORM_BETA
    b_up = 0.02 * jax.random.normal(ks[11], (DEPTH, N_EXPERTS, 2 * D_FF), f32)
    w_down = jax.random.normal(ks[12], (DEPTH, N_EXPERTS, D_FF, D_MODEL), f32) * (D_FF ** -0.5) * DEEPNORM_BETA
    b_down = 0.02 * jax.random.normal(ks[13], (DEPTH, N_EXPERTS, D_MODEL), f32)
    ln2_g = 1.0 + 0.02 * jax.random.normal(ks[14], (DEPTH, D_MODEL), f32)
    ln2_b = 0.02 * jax.random.normal(ks[15], (DEPTH, D_MODEL), f32)
    return {'x_prompt': x_prompt, 'x_sample': x_sample, 'w_in': w_in, 'w_attn_out': w_attn_out,
            'w_four_out': w_four_out, 'w_o': w_o, 'ln1_g': ln1_g, 'ln1_b': ln1_b,
            'w_router': w_router, 'b_router': b_router, 'w_up': w_up, 'b_up': b_up,
            'w_down': w_down, 'b_down': b_down, 'ln2_g': ln2_g, 'ln2_b': ln2_b}


def reference(x_prompt, x_sample, w_in, w_attn_out, w_four_out, w_o, ln1_g, ln1_b,
              w_router, b_router, w_up, b_up, w_down, b_down, ln2_g, ln2_b):
    y_prompt = x_prompt
    y_sample = x_sample
    for l in range(DEPTH):
        params = (w_in[l], w_attn_out[l], w_four_out[l], w_o[l], ln1_g[l], ln1_b[l],
                  w_router[l], b_router[l], w_up[l], b_up[l], w_down[l], b_down[l], ln2_g[l], ln2_b[l])
        y_prompt = encoder_layer(y_prompt, *params)
        y_sample = encoder_layer(y_sample, *params)
    return (y_prompt, y_sample)
```

```python
import functools
import math

import jax
import jax.numpy as jnp
from jax import lax
from jax.experimental import pallas as pl
from jax.experimental.pallas import tpu as pltpu
from jax.experimental.pallas import tpu_sc as plsc

F32 = jnp.float32
BF16 = jnp.bfloat16
I32 = jnp.int32
U32 = jnp.uint32

D_MODEL = 1024
HEAD_DIM = 64
HEADS_PER_GROUP = 4
GROUP_WIDTH = HEADS_PER_GROUP * HEAD_DIM
DILATIONS = (1, 4, 16)
HALF_WINDOW = 64
N_GROUPS = len(DILATIONS)
QKV_WIDTH = N_GROUPS * GROUP_WIDTH
F_GROUPS = 4
F_GROUP_DIM = 64
F_WIDTH = F_GROUPS * F_GROUP_DIM
PROJ_WIDTH = 3 * QKV_WIDTH + F_WIDTH
N_EXPERTS = 32
TOP_K = 4
D_FF = 1024
SWIGLU_LIMIT = 7.0
SWIGLU_ALPHA = 1.702
LN_EPS = 1e-5
ROPE_THETA = 10000.0
NEG_INF = -1e30
DEPTH = 1
DEEPNORM_ALPHA = (2 * DEPTH) ** 0.25

LANES = 128
TOKEN_TILE = 512
ATTN_TILE = 512
ATTN_SUB = 128
EXPERT_BLOCK = 512
FF_CHUNK = 512
DFT_STAGE1 = 128
SC_CHUNK = 64
VMEM_LIMIT = 56 * 1024 * 1024


def _cparams(*sem):
    return pltpu.CompilerParams(dimension_semantics=sem, vmem_limit_bytes=VMEM_LIMIT)


def _inproj_kernel(xp_ref, xs_ref, w_ref, cos_ref, sin_ref, q_ref, k_ref, v_ref, f_ref, *, n_prompt_tiles):
    i = pl.program_id(0)
    x = jnp.where(i < n_prompt_tiles, xp_ref[...], xs_ref[...]).astype(BF16)
    proj = jnp.dot(x, w_ref[...], preferred_element_type=F32)
    cos = cos_ref[...]
    sin = sin_ref[...]
    lane = lax.broadcasted_iota(I32, cos.shape, 1)
    first_half = (lane % HEAD_DIM) < (HEAD_DIM // 2)

    def rope(t):
        swapped = jnp.where(first_half, pltpu.roll(t, LANES - HEAD_DIM // 2, 1), pltpu.roll(t, HEAD_DIM // 2, 1))
        return t * cos + swapped * sin

    for c in range(QKV_WIDTH // LANES):
        sl = slice(c * LANES, (c + 1) * LANES)
        q_ref[:, sl] = (rope(proj[:, sl]) * (HEAD_DIM ** -0.5)).astype(BF16)
        k_ref[:, sl] = rope(proj[:, QKV_WIDTH + c * LANES:QKV_WIDTH + (c + 1) * LANES]).astype(BF16)
    v_ref[...] = proj[:, 2 * QKV_WIDTH:3 * QKV_WIDTH].astype(BF16)
    f_ref[...] = proj[:, 3 * QKV_WIDTH:PROJ_WIDTH].astype(BF16)


def _inproj(xp, xs, w_proj, cos_t, sin_t, seq_sample):
    n_p, n_s = xp.shape[0], xs.shape[0]
    n_t = n_p + n_s
    tm = TOKEN_TILE
    npt = n_p // tm
    tiles_per_sample_seq = seq_sample // tm

    def table_idx(i):
        return (jnp.where(i < npt, i, (i - npt) % tiles_per_sample_seq), 0)

    row = lambda w: pl.BlockSpec((tm, w), lambda i: (i, 0))
    return pl.pallas_call(
        functools.partial(_inproj_kernel, n_prompt_tiles=npt),
        grid=(n_t // tm,),
        in_specs=[
            pl.BlockSpec((tm, D_MODEL), lambda i: (jnp.minimum(i, npt - 1), 0)),
            pl.BlockSpec((tm, D_MODEL), lambda i: (jnp.maximum(i - npt, 0), 0)),
            pl.BlockSpec((D_MODEL, PROJ_WIDTH), lambda i: (0, 0)),
            pl.BlockSpec((tm, LANES), table_idx),
            pl.BlockSpec((tm, LANES), table_idx),
        ],
        out_specs=[row(QKV_WIDTH), row(QKV_WIDTH), row(QKV_WIDTH), row(F_WIDTH)],
        out_shape=[jax.ShapeDtypeStruct((n_t, QKV_WIDTH), BF16)] * 3 + [jax.ShapeDtypeStruct((n_t, F_WIDTH), BF16)],
        compiler_params=_cparams("parallel"),
        name="inproj_rope",
    )(xp, xs, w_proj, cos_t, sin_t)


def _attn_kernel(q_ref, kl_ref, km_ref, kr_ref, vl_ref, vm_ref, vr_ref, o_ref, lse_ref, *, prompt_rows, sample_rows):
    i = pl.program_id(1)
    kext = jnp.concatenate([kl_ref[...], km_ref[...], kr_ref[...]], axis=0)
    vext = jnp.concatenate([vl_ref[...], vm_ref[...], vr_ref[...]], axis=0)
    kw = ATTN_SUB + 2 * HALF_WINDOW
    a = lax.broadcasted_iota(I32, (ATTN_SUB, kw), 0)
    c = lax.broadcasted_iota(I32, (ATTN_SUB, kw), 1)
    in_band = jnp.abs(c - HALF_WINDOW - a) <= HALF_WINDOW
    head_of_lane = lax.broadcasted_iota(I32, (1, GROUP_WIDTH), 1) // HEAD_DIM
    for j in range(ATTN_TILE // ATTN_SUB):
        r0 = i * ATTN_TILE + j * ATTN_SUB
        in_prompt = r0 < prompt_rows
        b = jnp.maximum(r0 - prompt_rows, 0) // sample_rows
        lo = jnp.where(in_prompt, 0, prompt_rows + b * sample_rows)
        hi = jnp.where(in_prompt, prompt_rows, prompt_rows + (b + 1) * sample_rows)
        key_row = r0 - HALF_WINDOW + c
        valid = in_band & (key_row >= lo) & (key_row < hi)
        qj = q_ref[j * ATTN_SUB:(j + 1) * ATTN_SUB, :]
        kj = kext[j * ATTN_SUB:j * ATTN_SUB + kw, :]
        vj = vext[j * ATTN_SUB:j * ATTN_SUB + kw, :]
        acc = jnp.zeros((ATTN_SUB, GROUP_WIDTH), F32)
        lse_full = jnp.zeros((ATTN_SUB, GROUP_WIDTH), F32)
        for h in range(HEADS_PER_GROUP):
            mine = head_of_lane == h
            qh = jnp.where(mine, qj, jnp.zeros_like(qj))
            s = lax.dot_general(qh, kj, (((1,), (1,)), ((), ())), preferred_element_type=F32)
            s = jnp.where(valid, s, NEG_INF)
            m = jnp.max(s, axis=1, keepdims=True)
            p = jnp.exp(s - m)
            l = jnp.sum(p, axis=1, keepdims=True)
            pn = (p * (1.0 / l)).astype(BF16)
            vh = jnp.where(mine, vj, jnp.zeros_like(vj))
            acc = acc + jnp.dot(pn, vh, preferred_element_type=F32)
            lse_full = jnp.where(mine, m + jnp.log(l), lse_full)
        o_ref[j * ATTN_SUB:(j + 1) * ATTN_SUB, :] = acc.astype(BF16)
        lse_ref[j * ATTN_SUB:(j + 1) * ATTN_SUB, :] = lse_full


def _attention_group(q, k, v, gi, dil, n_prompt, seq_sample):
    n_t = q.shape[0]
    rows = n_t // dil
    tq = ATTN_TILE
    halo_per_tile = tq // HALF_WINDOW
    n_halo_blocks = rows // HALF_WINDOW
    qv = q.reshape(rows, dil * QKV_WIDTH)
    kv = k.reshape(rows, dil * QKV_WIDTH)
    vv = v.reshape(rows, dil * QKV_WIDTH)
    col = lambda r: r * N_GROUPS + gi
    main = pl.BlockSpec((tq, GROUP_WIDTH), lambda r, i: (i, col(r)))
    left = pl.BlockSpec((HALF_WINDOW, GROUP_WIDTH), lambda r, i: (jnp.maximum(i * halo_per_tile - 1, 0), col(r)))
    right = pl.BlockSpec((HALF_WINDOW, GROUP_WIDTH),
                         lambda r, i: (jnp.minimum((i + 1) * halo_per_tile, n_halo_blocks - 1), col(r)))
    out = pl.BlockSpec((tq, GROUP_WIDTH), lambda r, i: (i, r))
    o, lse = pl.pallas_call(
        functools.partial(_attn_kernel, prompt_rows=n_prompt // dil, sample_rows=seq_sample // dil),
        grid=(dil, rows // tq),
        in_specs=[main, left, main, right, left, main, right],
        out_specs=[out, out],
        out_shape=[jax.ShapeDtypeStruct((rows, dil * GROUP_WIDTH), BF16),
                   jax.ShapeDtypeStruct((rows, dil * GROUP_WIDTH), F32)],
        compiler_params=_cparams("parallel", "parallel"),
        name=f"dilated_attention_{dil}",
    )(qv, kv, kv, kv, vv, vv, vv)
    return o.reshape(n_t, GROUP_WIDTH), lse.reshape(n_t, GROUP_WIDTH)


def _angle_table(idx, period):
    return (idx % period).astype(F32) * (2.0 * math.pi / period)


def _channel_dft(seq_len):
    c = jnp.arange(F_GROUP_DIM, dtype=I32)
    th = _angle_table(c[:, None] * c[None, :], F_GROUP_DIM)
    scale = (seq_len * F_GROUP_DIM) ** -0.5
    eye = jnp.eye(F_GROUPS, dtype=F32)
    cb = jnp.kron(eye, jnp.cos(th)) * scale
    sb = jnp.kron(eye, jnp.sin(th)) * scale
    return jnp.concatenate([cb, sb], axis=0).astype(BF16)


def _dft_stage1_kernel(m_ref, x_ref, a_ref):
    a_ref[...] = jnp.dot(m_ref[...], x_ref[...], preferred_element_type=F32).astype(BF16)


def _dft_stage2_kernel(g_ref, a_ref, cs_ref, o_ref):
    s2 = a_ref.shape[2]
    a = jnp.concatenate([a_ref[0, 0], a_ref[1, 0]], axis=0)
    x = jnp.dot(g_ref[0], a, preferred_element_type=F32)
    xr = x[:s2].astype(BF16)
    xi = x[s2:].astype(BF16)
    o_ref[...] = (jnp.dot(xr, cs_ref[:F_WIDTH, :], preferred_element_type=F32)
                  + jnp.dot(xi, cs_ref[F_WIDTH:, :], preferred_element_type=F32)).astype(BF16)


def _fourier_prompt(f, seq):
    s1 = DFT_STAGE1
    s2 = seq // s1
    n1 = jnp.arange(s1, dtype=I32)
    th1 = _angle_table(n1[:, None] * n1[None, :], s1)
    m1 = jnp.concatenate([jnp.cos(th1), -jnp.sin(th1)], axis=0).astype(BF16)
    k1 = jnp.arange(s1, dtype=I32)[:, None, None]
    k2 = jnp.arange(s2, dtype=I32)[None, :, None]
    n2 = jnp.arange(s2, dtype=I32)[None, None, :]
    th = _angle_table((k1 + s1 * k2) * n2, seq)
    gr, gi = jnp.cos(th), -jnp.sin(th)
    g = jnp.concatenate([jnp.concatenate([gr, -gi], axis=2), jnp.concatenate([gi, gr], axis=2)], axis=1).astype(BF16)
    cs = _channel_dft(seq)

    cols = s2 * F_WIDTH
    fv = f.reshape(f.shape[0] // s2, cols)
    tn = min(cols, 4096)
    a = pl.pallas_call(
        _dft_stage1_kernel,
        grid=(cols // tn,),
        in_specs=[pl.BlockSpec((2 * s1, s1), lambda j: (0, 0)), pl.BlockSpec((s1, tn), lambda j: (0, j))],
        out_specs=pl.BlockSpec((2 * s1, tn), lambda j: (0, j)),
        out_shape=jax.ShapeDtypeStruct((2 * s1, cols), BF16),
        compiler_params=_cparams("parallel"),
        name="dft_stage1",
    )(m1, fv)
    a4 = a.reshape(2, s1, s2, F_WIDTH)
    out = pl.pallas_call(
        _dft_stage2_kernel,
        grid=(s1,),
        in_specs=[pl.BlockSpec((1, 2 * s2, 2 * s2), lambda k: (k, 0, 0)),
                  pl.BlockSpec((2, 1, s2, F_WIDTH), lambda k: (0, k, 0, 0)),
                  pl.BlockSpec((2 * F_WIDTH, F_WIDTH), lambda k: (0, 0))],
        out_specs=pl.BlockSpec((s2, F_WIDTH), lambda k: (0, k)),
        out_shape=jax.ShapeDtypeStruct((s2, s1 * F_WIDTH), BF16),
        compiler_params=_cparams("parallel"),
        name="dft_stage2",
    )(g, a4, cs)
    return out.reshape(seq, F_WIDTH)


def _dft_direct_kernel(g_ref, x_ref, cs_ref, o_ref):
    tk = o_ref.shape[1]
    x = jnp.dot(g_ref[...], x_ref[0], preferred_element_type=F32)
    xr = x[:tk].astype(BF16)
    xi = x[tk:].astype(BF16)
    o_ref[0] = (jnp.dot(xr, cs_ref[:F_WIDTH, :], preferred_element_type=F32)
                + jnp.dot(xi, cs_ref[F_WIDTH:, :], preferred_element_type=F32)).astype(BF16)


def _fourier_sample(f, n_prompt, batch, seq):
    tk = 512
    kk = jnp.arange(seq, dtype=I32)
    th = _angle_table(kk[:, None] * kk[None, :], seq).reshape(seq // tk, tk, seq)
    g = jnp.concatenate([jnp.cos(th), -jnp.sin(th)], axis=1).reshape(2 * seq, seq).astype(BF16)
    cs = _channel_dft(seq)
    f3 = f.reshape(f.shape[0] // seq, seq, F_WIDTH)
    first = n_prompt // seq
    return pl.pallas_call(
        _dft_direct_kernel,
        grid=(seq // tk, batch),
        in_specs=[pl.BlockSpec((2 * tk, seq), lambda t, b: (t, 0)),
                  pl.BlockSpec((1, seq, F_WIDTH), lambda t, b: (first + b, 0, 0)),
                  pl.BlockSpec((2 * F_WIDTH, F_WIDTH), lambda t, b: (0, 0))],
        out_specs=pl.BlockSpec((1, tk, F_WIDTH), lambda t, b: (b, t, 0)),
        out_shape=jax.ShapeDtypeStruct((batch, seq, F_WIDTH), BF16),
        compiler_params=_cparams("parallel", "parallel"),
        name="dft_direct",
    )(g, f3, cs).reshape(batch * seq, F_WIDTH)


def _pack_bf16_pairs(x):
    w = x.shape[1] // 2
    bits = lax.bitcast_convert_type(x.astype(BF16).astype(F32), U32)
    return (bits[:, :w] >> 16) | (bits[:, w:] & jnp.uint32(0xFFFF0000))


def _unpack_bf16_pairs(u):
    lo = lax.bitcast_convert_type(u << 16, F32)
    hi = lax.bitcast_convert_type(u & jnp.uint32(0xFFFF0000), F32)
    return jnp.concatenate([lo, hi], axis=1)


def _layer_norm(h, g, b):
    mu = jnp.mean(h, axis=-1, keepdims=True)
    d = h - mu
    var = jnp.mean(d * d, axis=-1, keepdims=True)
    return d * lax.rsqrt(var + LN_EPS) * g + b


def _merge_kernel(xp_ref, xs_ref, wg_ref, o1_ref, o2_ref, o3_ref, l1_ref, l2_ref, l3_ref, four_ref,
                  wao_ref, wfo_ref, wo_ref, g_ref, b_ref, wrh_ref, wrl_ref, br_ref, tri_ref,
                  x1_ref, pk_ref, idx_ref, gate_ref, rank_ref, cnt_ref, run_ref, *, n_prompt_tiles):
    i = pl.program_id(0)

    @pl.when(i == 0)
    def _():
        run_ref[...] = jnp.zeros_like(run_ref)

    x = jnp.where(i < n_prompt_tiles, xp_ref[...], xs_ref[...])
    gpre = jnp.dot(x.astype(BF16), wg_ref[...], preferred_element_type=F32)
    l1, l2, l3 = l1_ref[...], l2_ref[...], l3_ref[...]
    m = jnp.maximum(jnp.maximum(l1, l2), l3)
    e1, e2, e3 = jnp.exp(l1 - m), jnp.exp(l2 - m), jnp.exp(l3 - m)
    attn = (e1 * o1_ref[...].astype(F32) + e2 * o2_ref[...].astype(F32) + e3 * o3_ref[...].astype(F32)) \
        * (1.0 / (e1 + e2 + e3))
    a = jnp.dot(attn.astype(BF16), wao_ref[...], preferred_element_type=F32)
    ff = jnp.dot(four_ref[...], wfo_ref[...], preferred_element_type=F32)
    merged = jax.nn.sigmoid(gpre[:, :D_MODEL]) * a + jax.nn.sigmoid(gpre[:, D_MODEL:]) * ff
    mix = jnp.dot(merged.astype(BF16), wo_ref[...], preferred_element_type=F32)
    x1 = _layer_norm(DEEPNORM_ALPHA * x + mix, g_ref[...], b_ref[...])
    x1_ref[...] = x1
    pk_ref[...] = _pack_bf16_pairs(x1)

    xh = x1.astype(BF16)
    xl = (x1 - xh.astype(F32)).astype(BF16)
    nt = (((1,), (1,)), ((), ()))
    logits = (lax.dot_general(wrh_ref[...], xh, nt, preferred_element_type=F32)
              + lax.dot_general(wrl_ref[...], xh, nt, preferred_element_type=F32)
              + lax.dot_general(wrh_ref[...], xl, nt, preferred_element_type=F32)) + br_ref[...]
    tm = logits.shape[1]
    eio = lax.broadcasted_iota(I32, (N_EXPERTS, tm), 0)
    work = logits
    vals, sels, picks = [], [], []
    for _ in range(TOP_K):
        mk = jnp.max(work, axis=0, keepdims=True)
        ik = jnp.min(jnp.where(work == mk, eio, N_EXPERTS), axis=0, keepdims=True)
        sel = eio == ik
        vals.append(mk)
        sels.append(sel)
        picks.append(ik)
        work = jnp.where(sel, -jnp.inf, work)
    es = [jnp.exp(v - vals[0]) for v in vals]
    inv_den = 1.0 / (es[0] + es[1] + es[2] + es[3])
    chosen = (sels[0] | sels[1] | sels[2] | sels[3])
    chosen_f = chosen.astype(F32)
    prefix = jnp.dot(chosen_f.astype(BF16), tri_ref[...], preferred_element_type=F32)
    before = prefix + run_ref[...]
    for kk in range(TOP_K):
        idx_ref[kk:kk + 1, :] = picks[kk]
        gate_ref[kk:kk + 1, :] = es[kk] * inv_den
        rank_ref[kk:kk + 1, :] = jnp.sum(jnp.where(sels[kk], before, 0.0), axis=0, keepdims=True).astype(I32)
    run_ref[...] = run_ref[...] + jnp.sum(chosen_f, axis=1, keepdims=True)
    cnt_ref[...] = jnp.broadcast_to(run_ref[...], cnt_ref.shape).astype(I32)


def _merge_route(xp, xs, w_gate, o_list, l_list, four, w_ao, w_fo, w_o, ln_g, ln_b, w_r, b_r):
    n_p, n_s = xp.shape[0], xs.shape[0]
    n_t = n_p + n_s
    tm = TOKEN_TILE
    npt = n_p // tm
    w_rt = w_r.T
    w_rh = w_rt.astype(BF16)
    w_rl = (w_rt - w_rh.astype(F32)).astype(BF16)
    tri = (jnp.arange(tm)[:, None] < jnp.arange(tm)[None, :]).astype(BF16)
    row = lambda w: pl.BlockSpec((tm, w), lambda i: (i, 0))
    full = lambda r, c: pl.BlockSpec((r, c), lambda i: (0, 0))
    lane_row = pl.BlockSpec((TOP_K, tm), lambda i: (0, i))
    return pl.pallas_call(
        functools.partial(_merge_kernel, n_prompt_tiles=npt),
        grid=(n_t // tm,),
        in_specs=[
            pl.BlockSpec((tm, D_MODEL), lambda i: (jnp.minimum(i, npt - 1), 0)),
            pl.BlockSpec((tm, D_MODEL), lambda i: (jnp.maximum(i - npt, 0), 0)),
            full(D_MODEL, 2 * D_MODEL),
            row(GROUP_WIDTH), row(GROUP_WIDTH), row(GROUP_WIDTH),
            row(GROUP_WIDTH), row(GROUP_WIDTH), row(GROUP_WIDTH),
            row(F_WIDTH),
            full(GROUP_WIDTH, D_MODEL), full(F_WIDTH, D_MODEL), full(D_MODEL, D_MODEL),
            full(1, D_MODEL), full(1, D_MODEL),
            full(N_EXPERTS, D_MODEL), full(N_EXPERTS, D_MODEL), full(N_EXPERTS, 1),
            full(tm, tm),
        ],
        out_specs=[row(D_MODEL), row(D_MODEL // 2), lane_row, lane_row, lane_row, full(N_EXPERTS, LANES)],
        out_shape=[
            jax.ShapeDtypeStruct((n_t, D_MODEL), F32),
            jax.ShapeDtypeStruct((n_t, D_MODEL // 2), U32),
            jax.ShapeDtypeStruct((TOP_K, n_t), I32),
            jax.ShapeDtypeStruct((TOP_K, n_t), F32),
            jax.ShapeDtypeStruct((TOP_K, n_t), I32),
            jax.ShapeDtypeStruct((N_EXPERTS, LANES), I32),
        ],
        scratch_shapes=[pltpu.VMEM((N_EXPERTS, 1), F32)],
        compiler_params=_cparams("arbitrary"),
        name="merge_ln_route",
    )(xp, xs, w_gate, *o_list, *l_list, four, w_ao, w_fo, w_o, ln_g.reshape(1, -1), ln_b.reshape(1, -1),
      w_rh, w_rl, b_r.reshape(-1, 1), tri)


def _sc_workers():
    info = plsc.get_sparse_core_info()
    return info.num_cores, info.num_cores * info.num_subcores


def _sc_dispatch(rows_src, pos_chunks, n_rows_out):
    n_t, w = rows_src.shape
    n_cores, n_workers = _sc_workers()
    chunks_per_worker = n_t // (n_workers * SC_CHUNK)
    mesh = plsc.VectorSubcoreMesh(core_axis_name="c", subcore_axis_name="s")

    @functools.partial(
        pl.kernel, mesh=mesh,
        out_type=jax.ShapeDtypeStruct((n_rows_out, w), rows_src.dtype),
        scratch_types=[pltpu.VMEM((TOP_K, SC_CHUNK), I32), pltpu.VMEM((SC_CHUNK, w), rows_src.dtype)],
    )
    def dispatch(src_hbm, pos_hbm, out_hbm, idx_v, rows_v):
        wid = lax.axis_index("s") * n_cores + lax.axis_index("c")

        @pl.loop(0, chunks_per_worker)
        def _(j):
            chunk = wid * chunks_per_worker + j
            pltpu.sync_copy(pos_hbm.at[chunk], idx_v)
            pltpu.sync_copy(src_hbm.at[pl.ds(chunk * SC_CHUNK, SC_CHUNK)], rows_v)
            for kk in range(TOP_K):
                pltpu.sync_copy(rows_v, out_hbm.at[idx_v.at[kk]])

    return dispatch(rows_src, pos_chunks)


def _sc_combine(rows_src, pos_chunks, n_t):
    w = rows_src.shape[1]
    n_cores, n_workers = _sc_workers()
    chunks_per_worker = n_t // (n_workers * SC_CHUNK)
    mesh = plsc.VectorSubcoreMesh(core_axis_name="c", subcore_axis_name="s")

    @functools.partial(
        pl.kernel, mesh=mesh,
        out_type=jax.ShapeDtypeStruct((TOP_K, n_t, w), rows_src.dtype),
        scratch_types=[pltpu.VMEM((TOP_K, SC_CHUNK), I32), pltpu.VMEM((SC_CHUNK, w), rows_src.dtype)],
    )
    def combine(src_hbm, pos_hbm, out_hbm, idx_v, rows_v):
        wid = lax.axis_index("s") * n_cores + lax.axis_index("c")

        @pl.loop(0, chunks_per_worker)
        def _(j):
            chunk = wid * chunks_per_worker + j
            pltpu.sync_copy(pos_hbm.at[chunk], idx_v)
            for kk in range(TOP_K):
                pltpu.sync_copy(src_hbm.at[idx_v.at[kk]], rows_v)
                pltpu.sync_copy(rows_v, out_hbm.at[kk, pl.ds(chunk * SC_CHUNK, SC_CHUNK)])

    return combine(rows_src, pos_chunks)


def _expert_kernel(be_ref, nv_ref, x_ref, wg_ref, wl_ref, bg_ref, bl_ref, wd_ref, bd_ref, y_ref):
    blk = pl.program_id(0)
    nv = nv_ref[blk]

    @pl.when(nv > 0)
    def _():
        x = _unpack_bf16_pairs(x_ref[...])
        rows = lax.broadcasted_iota(I32, (x.shape[0], 1), 0)
        x = jnp.where(rows < nv, x, 0.0).astype(BF16)
        acc = jnp.zeros((x.shape[0], D_MODEL), F32) + bd_ref[0]
        for c in range(D_FF // FF_CHUNK):
            sl = slice(c * FF_CHUNK, (c + 1) * FF_CHUNK)
            hg = jnp.dot(x, wg_ref[0, :, sl], preferred_element_type=F32) + bg_ref[0, :, sl]
            hl = jnp.dot(x, wl_ref[0, :, sl], preferred_element_type=F32) + bl_ref[0, :, sl]
            glu = jnp.minimum(hg, SWIGLU_LIMIT)
            lin = jnp.clip(hl, -SWIGLU_LIMIT, SWIGLU_LIMIT)
            act = glu * jax.nn.sigmoid(SWIGLU_ALPHA * glu) * (lin + 1.0)
            acc = acc + jnp.dot(act.astype(BF16), wd_ref[0, sl, :], preferred_element_type=F32)
        y_ref[...] = _pack_bf16_pairs(acc)

    @pl.when(nv == 0)
    def _():
        y_ref[...] = jnp.zeros_like(y_ref)


def _expert_ffn(xs_rows, block_e, n_valid, w_glu, w_lin, b_glu, b_lin, w_down, b_down):
    n_rows = xs_rows.shape[0]
    bm = EXPERT_BLOCK
    wspec = lambda r, c: pl.BlockSpec((1, r, c), lambda b, be, nv: (be[b], 0, 0))
    grid_spec = pltpu.PrefetchScalarGridSpec(
        num_scalar_prefetch=2,
        grid=(n_rows // bm,),
        in_specs=[
            pl.BlockSpec((bm, D_MODEL // 2), lambda b, be, nv: (b, 0)),
            wspec(D_MODEL, D_FF), wspec(D_MODEL, D_FF), wspec(1, D_FF), wspec(1, D_FF),
            wspec(D_FF, D_MODEL), wspec(1, D_MODEL),
        ],
        out_specs=pl.BlockSpec((bm, D_MODEL // 2), lambda b, be, nv: (b, 0)),
    )
    return pl.pallas_call(
        _expert_kernel,
        grid_spec=grid_spec,
        out_shape=jax.ShapeDtypeStruct((n_rows, D_MODEL // 2), U32),
        compiler_params=_cparams("arbitrary"),
        name="expert_ffn",
    )(block_e, n_valid, xs_rows, w_glu, w_lin, b_glu, b_lin, w_down, b_down)


def _final_kernel(x1_ref, yg_ref, gt_ref, g_ref, b_ref, o_ref):
    gt = gt_ref[...]
    ffn = jnp.zeros(x1_ref.shape, F32)
    for kk in range(TOP_K):
        ffn = ffn + gt[:, kk:kk + 1] * _unpack_bf16_pairs(yg_ref[kk])
    o_ref[...] = _layer_norm(DEEPNORM_ALPHA * x1_ref[...] + ffn, g_ref[...], b_ref[...])


def _final(x1, yg, gates_t, ln_g, ln_b, first_row, n_rows):
    tm = TOKEN_TILE
    off = first_row // tm
    return pl.pallas_call(
        _final_kernel,
        grid=(n_rows // tm,),
        in_specs=[
            pl.BlockSpec((tm, D_MODEL), lambda i: (off + i, 0)),
            pl.BlockSpec((TOP_K, tm, D_MODEL // 2), lambda i: (0, off + i, 0)),
            pl.BlockSpec((tm, TOP_K), lambda i: (off + i, 0)),
            pl.BlockSpec((1, D_MODEL), lambda i: (0, 0)),
            pl.BlockSpec((1, D_MODEL), lambda i: (0, 0)),
        ],
        out_specs=pl.BlockSpec((tm, D_MODEL), lambda i: (i, 0)),
        out_shape=jax.ShapeDtypeStruct((n_rows, D_MODEL), F32),
        compiler_params=_cparams("parallel"),
        name="combine_ln2",
    )(x1, yg, gates_t, ln_g.reshape(1, -1), ln_b.reshape(1, -1))


def _routing_tables(idx, rank, counts, n_blocks):
    bm = EXPERT_BLOCK
    padded = (counts + bm - 1) // bm * bm
    pad_end = jnp.cumsum(padded)
    pad_start = pad_end - padded
    pos = pad_start[idx] + rank
    blk_row0 = jnp.arange(n_blocks, dtype=I32) * bm
    block_e = jnp.minimum(jnp.searchsorted(pad_end, blk_row0, side='right'), N_EXPERTS - 1).astype(I32)
    n_valid = jnp.clip(counts[block_e] - (blk_row0 - pad_start[block_e]), 0, bm).astype(I32)
    return pos.astype(I32), block_e, n_valid


def _moe(pk, idx, rank, counts, w_glu, w_lin, b_glu, b_lin, w_down, b_down):
    n_t = pk.shape[0]
    n_blocks = (n_t * TOP_K) // EXPERT_BLOCK + N_EXPERTS
    pos, block_e, n_valid = _routing_tables(idx, rank, counts, n_blocks)
    pos_chunks = pos.reshape(TOP_K, n_t // SC_CHUNK, SC_CHUNK).transpose(1, 0, 2)
    xs_rows = _sc_dispatch(pk, pos_chunks, n_blocks * EXPERT_BLOCK)
    y_rows = _expert_ffn(xs_rows, block_e, n_valid, w_glu, w_lin, b_glu, b_lin, w_down, b_down)
    return _sc_combine(y_rows, pos_chunks, n_t)


def _rope_tables(seq_max):
    inv = ROPE_THETA ** (-jnp.arange(0, HEAD_DIM, 2, dtype=F32) / HEAD_DIM)
    ang = jnp.arange(seq_max).astype(F32)[:, None] * inv[None, :]
    cos, sin = jnp.cos(ang), jnp.sin(ang)
    reps = LANES // HEAD_DIM
    cos_t = jnp.tile(jnp.concatenate([cos, cos], axis=1), (1, reps))
    sin_t = jnp.tile(jnp.concatenate([-sin, sin], axis=1), (1, reps))
    return cos_t, sin_t


def _encoder_layer(xp, xs, seq_prompt, batch_sample, seq_sample, w_in, w_attn_out, w_four_out, w_o, ln1_g, ln1_b,
                   w_router, b_router, w_up, b_up, w_down, b_down, ln2_g, ln2_b):
    n_p, n_s = xp.shape[0], xs.shape[0]
    assert n_p == seq_prompt and n_s == batch_sample * seq_sample
    assert seq_prompt % (DFT_STAGE1 * 16) == 0 and seq_sample % TOKEN_TILE == 0 and n_p % seq_sample == 0
    n_t = n_p + n_s
    assert n_t % (max(DILATIONS) * ATTN_TILE) == 0 and seq_sample % (max(DILATIONS) * ATTN_SUB) == 0

    w_proj = w_in[:, :PROJ_WIDTH].astype(BF16)
    w_gate = w_in[:, PROJ_WIDTH:].astype(BF16)
    cos_t, sin_t = _rope_tables(max(seq_prompt, seq_sample))
    q, k, v, f = _inproj(xp, xs, w_proj, cos_t, sin_t, seq_sample)

    o_list, l_list = [], []
    for gi, dil in enumerate(DILATIONS):
        o, lse = _attention_group(q, k, v, gi, dil, n_p, seq_sample)
        o_list.append(o)
        l_list.append(lse)

    four = jnp.concatenate([_fourier_prompt(f, seq_prompt), _fourier_sample(f, n_p, batch_sample, seq_sample)], axis=0)

    x1, pk, idx, gates, rank, cnt = _merge_route(
        xp, xs, w_gate, o_list, l_list, four, w_attn_out.astype(BF16), w_four_out.astype(BF16), w_o.astype(BF16),
        ln1_g, ln1_b, w_router, b_router)

    w_glu = w_up[:, :, 0::2].astype(BF16)
    w_lin = w_up[:, :, 1::2].astype(BF16)
    b_glu = b_up[:, None, 0::2]
    b_lin = b_up[:, None, 1::2]
    yg = _moe(pk, idx, rank, cnt[:, 0], w_glu, w_lin, b_glu, b_lin, w_down.astype(BF16), b_down[:, None, :])

    gates_t = gates.T
    y_p = _final(x1, yg, gates_t, ln2_g, ln2_b, 0, n_p)
    y_s = _final(x1, yg, gates_t, ln2_g, ln2_b, n_p, n_s)
    return y_p, y_s


def kernel(x_prompt, x_sample, w_in, w_attn_out, w_four_out, w_o, ln1_g, ln1_b, w_router, b_router, w_up, b_up,
           w_down, b_down, ln2_g, ln2_b):
    assert w_in.shape[0] == DEPTH
    bp, sp, d = x_prompt.shape
    bs, ss, _ = x_sample.shape
    assert bp == 1 and d == D_MODEL
    y_p, y_s = _encoder_layer(
        x_prompt.reshape(sp, d), x_sample.reshape(bs * ss, d), sp, bs, ss,
        w_in[0], w_attn_out[0], w_four_out[0], w_o[0], ln1_g[0], ln1_b[0], w_router[0], b_router[0],
        w_up[0], b_up[0], w_down[0], b_down[0], ln2_g[0], ln2_b[0])
    return y_p.reshape(x_prompt.shape), y_s.reshape(x_sample.shape)
```

```python
import functools
import math

import jax
import jax.numpy as jnp
from jax import lax
from jax.experimental import pallas as pl
from jax.experimental.pallas import tpu as pltpu
from jax.experimental.pallas import tpu_sc as plsc

F32 = jnp.float32
BF16 = jnp.bfloat16
I32 = jnp.int32
U32 = jnp.uint32

D_MODEL = 1024
HEAD_DIM = 64
HEADS_PER_GROUP = 4
GROUP_WIDTH = HEADS_PER_GROUP * HEAD_DIM
DILATIONS = (1, 4, 16)
HALF_WINDOW = 64
N_GROUPS = len(DILATIONS)
QKV_WIDTH = N_GROUPS * GROUP_WIDTH
F_GROUPS = 4
F_GROUP_DIM = 64
F_WIDTH = F_GROUPS * F_GROUP_DIM
PROJ_WIDTH = 3 * QKV_WIDTH + F_WIDTH
N_EXPERTS = 32
TOP_K = 4
D_FF = 1024
SWIGLU_LIMIT = 7.0
SWIGLU_ALPHA = 1.702
LN_EPS = 1e-5
ROPE_THETA = 10000.0
NEG_INF = -1e30
DEPTH = 1
DEEPNORM_ALPHA = (2 * DEPTH) ** 0.25

LANES = 128
TOKEN_TILE = 512
ATTN_TILE = 512
ATTN_SUB = 128
EXPERT_BLOCK = 512
FF_CHUNK = 512
SPLIT_CHUNK = 256
DFT_STAGE1 = 128
SC_CHUNK = 64
VMEM_LIMIT = 56 * 1024 * 1024


def _cparams(*sem):
    return pltpu.CompilerParams(dimension_semantics=sem, vmem_limit_bytes=VMEM_LIMIT)


def _to_dilated_view(stage_ref, out_ref, val, dil):
    if dil == 1:
        out_ref[...] = val.astype(BF16)
        return
    rows = val.shape[0] // dil
    for c in range(GROUP_WIDTH // LANES):
        stage_ref[c] = val[:, c * LANES:(c + 1) * LANES]
    for r in range(dil):
        for c in range(GROUP_WIDTH // LANES):
            lo = r * GROUP_WIDTH + c * LANES
            out_ref[:, lo:lo + LANES] = stage_ref[c, pl.ds(r, rows, stride=dil), :].astype(BF16)


def _inproj_kernel(xp_ref, xs_ref, w_ref, cos_ref, sin_ref, *refs, n_prompt_tiles):
    qkv_refs = refs[:3 * N_GROUPS]
    f_ref = refs[3 * N_GROUPS]
    stage_refs = refs[3 * N_GROUPS + 1:]
    i = pl.program_id(0)
    x = jnp.where(i < n_prompt_tiles, xp_ref[...], xs_ref[...]).astype(BF16)
    proj = jnp.dot(x, w_ref[...], preferred_element_type=F32)
    cos = cos_ref[...]
    sin = sin_ref[...]
    lane = lax.broadcasted_iota(I32, cos.shape, 1)
    first_half = (lane % HEAD_DIM) < (HEAD_DIM // 2)

    def rope(t):
        parts = []
        for c in range(GROUP_WIDTH // LANES):
            tc = t[:, c * LANES:(c + 1) * LANES]
            swapped = jnp.where(first_half, pltpu.roll(tc, LANES - HEAD_DIM // 2, 1), pltpu.roll(tc, HEAD_DIM // 2, 1))
            parts.append(tc * cos + swapped * sin)
        return jnp.concatenate(parts, axis=1)

    for gi, dil in enumerate(DILATIONS):
        sl = slice(gi * GROUP_WIDTH, (gi + 1) * GROUP_WIDTH)
        qg = rope(proj[:, sl]) * (HEAD_DIM ** -0.5)
        kg = rope(proj[:, QKV_WIDTH + gi * GROUP_WIDTH:QKV_WIDTH + (gi + 1) * GROUP_WIDTH])
        vg = proj[:, 2 * QKV_WIDTH + gi * GROUP_WIDTH:2 * QKV_WIDTH + (gi + 1) * GROUP_WIDTH]
        for which, val in enumerate((qg, kg, vg)):
            _to_dilated_view(stage_refs[which * N_GROUPS + gi], qkv_refs[which * N_GROUPS + gi], val, dil)
    f_ref[...] = proj[:, 3 * QKV_WIDTH:PROJ_WIDTH].astype(BF16)


def _inproj(xp, xs, w_proj, cos_t, sin_t, seq_sample):
    n_p, n_s = xp.shape[0], xs.shape[0]
    n_t = n_p + n_s
    tm = TOKEN_TILE
    npt = n_p // tm
    tiles_per_sample_seq = seq_sample // tm

    def table_idx(i):
        return (jnp.where(i < npt, i, (i - npt) % tiles_per_sample_seq), 0)

    view_specs = [pl.BlockSpec((tm // d, d * GROUP_WIDTH), lambda i: (i, 0)) for d in DILATIONS] * 3
    view_shapes = [jax.ShapeDtypeStruct((n_t // d, d * GROUP_WIDTH), BF16) for d in DILATIONS] * 3
    outs = pl.pallas_call(
        functools.partial(_inproj_kernel, n_prompt_tiles=npt),
        grid=(n_t // tm,),
        in_specs=[
            pl.BlockSpec((tm, D_MODEL), lambda i: (jnp.minimum(i, npt - 1), 0)),
            pl.BlockSpec((tm, D_MODEL), lambda i: (jnp.maximum(i - npt, 0), 0)),
            pl.BlockSpec((D_MODEL, PROJ_WIDTH), lambda i: (0, 0)),
            pl.BlockSpec((tm, LANES), table_idx),
            pl.BlockSpec((tm, LANES), table_idx),
        ],
        out_specs=view_specs + [pl.BlockSpec((tm, F_WIDTH), lambda i: (i, 0))],
        out_shape=view_shapes + [jax.ShapeDtypeStruct((n_t, F_WIDTH), BF16)],
        scratch_shapes=[pltpu.VMEM((GROUP_WIDTH // LANES, tm, LANES), F32)] * (3 * N_GROUPS),
        compiler_params=_cparams("parallel"),
        name="inproj_rope",
    )(xp, xs, w_proj, cos_t, sin_t)
    q_views, k_views, v_views = outs[0:N_GROUPS], outs[N_GROUPS:2 * N_GROUPS], outs[2 * N_GROUPS:3 * N_GROUPS]
    return q_views, k_views, v_views, outs[3 * N_GROUPS]


def _attn_kernel(q_ref, kl_ref, km_ref, kr_ref, vl_ref, vm_ref, vr_ref, o_ref, lse_ref, *, prompt_rows, sample_rows):
    i = pl.program_id(1)
    kext = jnp.concatenate([kl_ref[...], km_ref[...], kr_ref[...]], axis=0)
    vext = jnp.concatenate([vl_ref[...], vm_ref[...], vr_ref[...]], axis=0)
    kw = ATTN_SUB + 2 * HALF_WINDOW
    a = lax.broadcasted_iota(I32, (ATTN_SUB, kw), 0)
    c = lax.broadcasted_iota(I32, (ATTN_SUB, kw), 1)
    in_band = jnp.abs(c - HALF_WINDOW - a) <= HALF_WINDOW
    head_of_lane = lax.broadcasted_iota(I32, (1, GROUP_WIDTH), 1) // HEAD_DIM
    for j in range(ATTN_TILE // ATTN_SUB):
        r0 = i * ATTN_TILE + j * ATTN_SUB
        in_prompt = r0 < prompt_rows
        b = jnp.maximum(r0 - prompt_rows, 0) // sample_rows
        lo = jnp.where(in_prompt, 0, prompt_rows + b * sample_rows)
        hi = jnp.where(in_prompt, prompt_rows, prompt_rows + (b + 1) * sample_rows)
        key_row = r0 - HALF_WINDOW + c
        valid = in_band & (key_row >= lo) & (key_row < hi)
        qj = q_ref[j * ATTN_SUB:(j + 1) * ATTN_SUB, :]
        kj = kext[j * ATTN_SUB:j * ATTN_SUB + kw, :]
        vj = vext[j * ATTN_SUB:j * ATTN_SUB + kw, :]
        acc = jnp.zeros((ATTN_SUB, GROUP_WIDTH), F32)
        lse_full = jnp.zeros((ATTN_SUB, GROUP_WIDTH), F32)
        for h in range(HEADS_PER_GROUP):
            mine = head_of_lane == h
            qh = jnp.where(mine, qj, jnp.zeros_like(qj))
            s = lax.dot_general(qh, kj, (((1,), (1,)), ((), ())), preferred_element_type=F32)
            s = jnp.where(valid, s, NEG_INF)
            m = jnp.max(s, axis=1, keepdims=True)
            p = jnp.exp(s - m)
            l = jnp.sum(p, axis=1, keepdims=True)
            pn = (p * (1.0 / l)).astype(BF16)
            vh = jnp.where(mine, vj, jnp.zeros_like(vj))
            acc = acc + jnp.dot(pn, vh, preferred_element_type=F32)
            lse_full = jnp.where(mine, m + jnp.log(l), lse_full)
        o_ref[j * ATTN_SUB:(j + 1) * ATTN_SUB, :] = acc.astype(BF16)
        lse_ref[j * ATTN_SUB:(j + 1) * ATTN_SUB, :] = lse_full


def _attention_group(qv, kv, vv, dil, n_prompt, seq_sample):
    rows = qv.shape[0]
    tq = ATTN_TILE
    halo_per_tile = tq // HALF_WINDOW
    n_halo_blocks = rows // HALF_WINDOW
    main = pl.BlockSpec((tq, GROUP_WIDTH), lambda r, i: (i, r))
    left = pl.BlockSpec((HALF_WINDOW, GROUP_WIDTH), lambda r, i: (jnp.maximum(i * halo_per_tile - 1, 0), r))
    right = pl.BlockSpec((HALF_WINDOW, GROUP_WIDTH),
                         lambda r, i: (jnp.minimum((i + 1) * halo_per_tile, n_halo_blocks - 1), r))
    return pl.pallas_call(
        functools.partial(_attn_kernel, prompt_rows=n_prompt // dil, sample_rows=seq_sample // dil),
        grid=(dil, rows // tq),
        in_specs=[main, left, main, right, left, main, right],
        out_specs=[main, main],
        out_shape=[jax.ShapeDtypeStruct((rows, dil * GROUP_WIDTH), BF16),
                   jax.ShapeDtypeStruct((rows, dil * GROUP_WIDTH), F32)],
        compiler_params=_cparams("parallel", "parallel"),
        name=f"dilated_attention_{dil}",
    )(qv, kv, kv, kv, vv, vv, vv)


def _angle_table(idx, period):
    return (idx % period).astype(F32) * (2.0 * math.pi / period)


def _channel_dft(seq_len):
    c = jnp.arange(F_GROUP_DIM, dtype=I32)
    th = _angle_table(c[:, None] * c[None, :], F_GROUP_DIM)
    scale = (seq_len * F_GROUP_DIM) ** -0.5
    eye = jnp.eye(F_GROUPS, dtype=F32)
    cb = jnp.kron(eye, jnp.cos(th)) * scale
    sb = jnp.kron(eye, jnp.sin(th)) * scale
    return jnp.concatenate([cb, sb], axis=0).astype(BF16)


def _dft_stage1_kernel(m_ref, x_ref, a_ref):
    a_ref[...] = jnp.dot(m_ref[...], x_ref[...], preferred_element_type=F32).astype(BF16)


def _dft_stage2_kernel(g_ref, a_ref, cs_ref, o_ref):
    s2 = a_ref.shape[2]
    a = jnp.concatenate([a_ref[0, 0], a_ref[1, 0]], axis=0)
    x = jnp.dot(g_ref[0], a, preferred_element_type=F32)
    xr = x[:s2].astype(BF16)
    xi = x[s2:].astype(BF16)
    o_ref[...] = (jnp.dot(xr, cs_ref[:F_WIDTH, :], preferred_element_type=F32)
                  + jnp.dot(xi, cs_ref[F_WIDTH:, :], preferred_element_type=F32)).astype(BF16)


def _fourier_prompt(f, seq):
    s1 = DFT_STAGE1
    s2 = seq // s1
    n1 = jnp.arange(s1, dtype=I32)
    th1 = _angle_table(n1[:, None] * n1[None, :], s1)
    m1 = jnp.concatenate([jnp.cos(th1), -jnp.sin(th1)], axis=0).astype(BF16)
    k1 = jnp.arange(s1, dtype=I32)[:, None, None]
    k2 = jnp.arange(s2, dtype=I32)[None, :, None]
    n2 = jnp.arange(s2, dtype=I32)[None, None, :]
    th = _angle_table((k1 + s1 * k2) * n2, seq)
    gr, gi = jnp.cos(th), -jnp.sin(th)
    g = jnp.concatenate([jnp.concatenate([gr, -gi], axis=2), jnp.concatenate([gi, gr], axis=2)], axis=1).astype(BF16)
    cs = _channel_dft(seq)

    cols = s2 * F_WIDTH
    fv = f.reshape(f.shape[0] // s2, cols)
    tn = min(cols, 4096)
    a = pl.pallas_call(
        _dft_stage1_kernel,
        grid=(cols // tn,),
        in_specs=[pl.BlockSpec((2 * s1, s1), lambda j: (0, 0)), pl.BlockSpec((s1, tn), lambda j: (0, j))],
        out_specs=pl.BlockSpec((2 * s1, tn), lambda j: (0, j)),
        out_shape=jax.ShapeDtypeStruct((2 * s1, cols), BF16),
        compiler_params=_cparams("parallel"),
        name="dft_stage1",
    )(m1, fv)
    a4 = a.reshape(2, s1, s2, F_WIDTH)
    out = pl.pallas_call(
        _dft_stage2_kernel,
        grid=(s1,),
        in_specs=[pl.BlockSpec((1, 2 * s2, 2 * s2), lambda k: (k, 0, 0)),
                  pl.BlockSpec((2, 1, s2, F_WIDTH), lambda k: (0, k, 0, 0)),
                  pl.BlockSpec((2 * F_WIDTH, F_WIDTH), lambda k: (0, 0))],
        out_specs=pl.BlockSpec((s2, F_WIDTH), lambda k: (0, k)),
        out_shape=jax.ShapeDtypeStruct((s2, s1 * F_WIDTH), BF16),
        compiler_params=_cparams("parallel"),
        name="dft_stage2",
    )(g, a4, cs)
    return out.reshape(seq, F_WIDTH)


def _dft_direct_kernel(g_ref, x_ref, cs_ref, o_ref):
    tk = o_ref.shape[1]
    x = jnp.dot(g_ref[...], x_ref[0], preferred_element_type=F32)
    xr = x[:tk].astype(BF16)
    xi = x[tk:].astype(BF16)
    o_ref[0] = (jnp.dot(xr, cs_ref[:F_WIDTH, :], preferred_element_type=F32)
                + jnp.dot(xi, cs_ref[F_WIDTH:, :], preferred_element_type=F32)).astype(BF16)


def _fourier_sample(f, n_prompt, batch, seq):
    tk = 512
    kk = jnp.arange(seq, dtype=I32)
    th = _angle_table(kk[:, None] * kk[None, :], seq).reshape(seq // tk, tk, seq)
    g = jnp.concatenate([jnp.cos(th), -jnp.sin(th)], axis=1).reshape(2 * seq, seq).astype(BF16)
    cs = _channel_dft(seq)
    f3 = f.reshape(f.shape[0] // seq, seq, F_WIDTH)
    first = n_prompt // seq
    return pl.pallas_call(
        _dft_direct_kernel,
        grid=(seq // tk, batch),
        in_specs=[pl.BlockSpec((2 * tk, seq), lambda t, b: (t, 0)),
                  pl.BlockSpec((1, seq, F_WIDTH), lambda t, b: (first + b, 0, 0)),
                  pl.BlockSpec((2 * F_WIDTH, F_WIDTH), lambda t, b: (0, 0))],
        out_specs=pl.BlockSpec((1, tk, F_WIDTH), lambda t, b: (b, t, 0)),
        out_shape=jax.ShapeDtypeStruct((batch, seq, F_WIDTH), BF16),
        compiler_params=_cparams("parallel", "parallel"),
        name="dft_direct",
    )(g, f3, cs).reshape(batch * seq, F_WIDTH)


def _pack_bf16_pairs(x):
    w = x.shape[1] // 2
    bits = lax.bitcast_convert_type(x.astype(BF16).astype(F32), U32)
    return (bits[:, :w] >> 16) | (bits[:, w:] & jnp.uint32(0xFFFF0000))


def _unpack_bf16_pairs(u):
    lo = lax.bitcast_convert_type(u << 16, F32)
    hi = lax.bitcast_convert_type(u & jnp.uint32(0xFFFF0000), F32)
    return jnp.concatenate([lo, hi], axis=1)


def _layer_norm(h, g, b):
    mu = jnp.mean(h, axis=-1, keepdims=True)
    d = h - mu
    var = jnp.mean(d * d, axis=-1, keepdims=True)
    return d * lax.rsqrt(var + LN_EPS) * g + b


def _from_dilated_view(stage_ref, blk_ref, dil):
    if dil == 1:
        return blk_ref[...].astype(F32)
    rows = blk_ref.shape[0]
    for r in range(dil):
        for c in range(GROUP_WIDTH // LANES):
            lo = r * GROUP_WIDTH + c * LANES
            stage_ref[c, pl.ds(r, rows, stride=dil), :] = blk_ref[:, lo:lo + LANES].astype(F32)
    return jnp.concatenate([stage_ref[c] for c in range(GROUP_WIDTH // LANES)], axis=1)


def _merge_kernel(xp_ref, xs_ref, wg_ref, o1_ref, o2_ref, o3_ref, l1_ref, l2_ref, l3_ref, four_ref,
                  wao_ref, wfo_ref, wo_ref, g_ref, b_ref, wrh_ref, wrl_ref, br_ref, tri_ref,
                  x1_ref, pk_ref, idx_ref, gate_ref, rank_ref, cnt_ref, run_ref, *stage_refs, n_prompt_tiles):
    i = pl.program_id(0)

    @pl.when(i == 0)
    def _():
        run_ref[...] = jnp.zeros_like(run_ref)

    x = jnp.where(i < n_prompt_tiles, xp_ref[...], xs_ref[...])
    gpre = jnp.dot(x.astype(BF16), wg_ref[...], preferred_element_type=F32)
    o1, o2, o3 = [_from_dilated_view(stage_refs[gi], ref, d)
                  for gi, (ref, d) in enumerate(zip((o1_ref, o2_ref, o3_ref), DILATIONS))]
    l1, l2, l3 = [_from_dilated_view(stage_refs[N_GROUPS + gi], ref, d)
                  for gi, (ref, d) in enumerate(zip((l1_ref, l2_ref, l3_ref), DILATIONS))]
    m = jnp.maximum(jnp.maximum(l1, l2), l3)
    e1, e2, e3 = jnp.exp(l1 - m), jnp.exp(l2 - m), jnp.exp(l3 - m)
    attn = (e1 * o1 + e2 * o2 + e3 * o3) * (1.0 / (e1 + e2 + e3))
    a = jnp.dot(attn.astype(BF16), wao_ref[...], preferred_element_type=F32)
    ff = jnp.dot(four_ref[...], wfo_ref[...], preferred_element_type=F32)
    merged = jax.nn.sigmoid(gpre[:, :D_MODEL]) * a + jax.nn.sigmoid(gpre[:, D_MODEL:]) * ff
    mix = jnp.dot(merged.astype(BF16), wo_ref[...], preferred_element_type=F32)
    x1 = _layer_norm(DEEPNORM_ALPHA * x + mix, g_ref[...], b_ref[...])
    x1_ref[...] = x1
    pk_ref[...] = _pack_bf16_pairs(x1)

    xh = x1.astype(BF16)
    xl = (x1 - xh.astype(F32)).astype(BF16)
    nt = (((1,), (1,)), ((), ()))
    logits = (lax.dot_general(wrh_ref[...], xh, nt, preferred_element_type=F32)
              + lax.dot_general(wrl_ref[...], xh, nt, preferred_element_type=F32)
              + lax.dot_general(wrh_ref[...], xl, nt, preferred_element_type=F32)) + br_ref[...]
    tm = logits.shape[1]
    eio = lax.broadcasted_iota(I32, (N_EXPERTS, tm), 0)
    work = logits
    vals, sels, picks = [], [], []
    for _ in range(TOP_K):
        mk = jnp.max(work, axis=0, keepdims=True)
        ik = jnp.min(jnp.where(work == mk, eio, N_EXPERTS), axis=0, keepdims=True)
        sel = eio == ik
        vals.append(mk)
        sels.append(sel)
        picks.append(ik)
        work = jnp.where(sel, -jnp.inf, work)
    es = [jnp.exp(v - vals[0]) for v in vals]
    inv_den = 1.0 / (es[0] + es[1] + es[2] + es[3])
    chosen = (sels[0] | sels[1] | sels[2] | sels[3])
    chosen_f = chosen.astype(F32)
    prefix = jnp.dot(chosen_f.astype(BF16), tri_ref[...], preferred_element_type=F32)
    before = prefix + run_ref[...]
    for kk in range(TOP_K):
        idx_ref[kk:kk + 1, :] = picks[kk]
        gate_ref[kk:kk + 1, :] = es[kk] * inv_den
        rank_ref[kk:kk + 1, :] = jnp.sum(jnp.where(sels[kk], before, 0.0), axis=0, keepdims=True).astype(I32)
    run_ref[...] = run_ref[...] + jnp.sum(chosen_f, axis=1, keepdims=True)
    cnt_ref[...] = jnp.broadcast_to(run_ref[...], cnt_ref.shape).astype(I32)


def _merge_route(xp, xs, w_gate, o_list, l_list, four, w_ao, w_fo, w_o, ln_g, ln_b, w_r, b_r):
    n_p, n_s = xp.shape[0], xs.shape[0]
    n_t = n_p + n_s
    tm = TOKEN_TILE
    npt = n_p // tm
    w_rt = w_r.T
    w_rh = w_rt.astype(BF16)
    w_rl = (w_rt - w_rh.astype(F32)).astype(BF16)
    tri = (jnp.arange(tm)[:, None] < jnp.arange(tm)[None, :]).astype(BF16)
    row = lambda w: pl.BlockSpec((tm, w), lambda i: (i, 0))
    full = lambda r, c: pl.BlockSpec((r, c), lambda i: (0, 0))
    lane_row = pl.BlockSpec((TOP_K, tm), lambda i: (0, i))
    views = [pl.BlockSpec((tm // d, d * GROUP_WIDTH), lambda i: (i, 0)) for d in DILATIONS]
    return pl.pallas_call(
        functools.partial(_merge_kernel, n_prompt_tiles=npt),
        grid=(n_t // tm,),
        in_specs=[
            pl.BlockSpec((tm, D_MODEL), lambda i: (jnp.minimum(i, npt - 1), 0)),
            pl.BlockSpec((tm, D_MODEL), lambda i: (jnp.maximum(i - npt, 0), 0)),
            full(D_MODEL, 2 * D_MODEL),
            *views, *views,
            row(F_WIDTH),
            full(GROUP_WIDTH, D_MODEL), full(F_WIDTH, D_MODEL), full(D_MODEL, D_MODEL),
            full(1, D_MODEL), full(1, D_MODEL),
            full(N_EXPERTS, D_MODEL), full(N_EXPERTS, D_MODEL), full(N_EXPERTS, 1),
            full(tm, tm),
        ],
        out_specs=[row(D_MODEL), row(D_MODEL // 2), lane_row, lane_row, lane_row, full(N_EXPERTS, LANES)],
        out_shape=[
            jax.ShapeDtypeStruct((n_t, D_MODEL), F32),
            jax.ShapeDtypeStruct((n_t, D_MODEL // 2), U32),
            jax.ShapeDtypeStruct((TOP_K, n_t), I32),
            jax.ShapeDtypeStruct((TOP_K, n_t), F32),
            jax.ShapeDtypeStruct((TOP_K, n_t), I32),
            jax.ShapeDtypeStruct((N_EXPERTS, LANES), I32),
        ],
        scratch_shapes=[pltpu.VMEM((N_EXPERTS, 1), F32)]
        + [pltpu.VMEM((GROUP_WIDTH // LANES, tm, LANES), F32)] * (2 * N_GROUPS),
        compiler_params=_cparams("arbitrary"),
        name="merge_ln_route",
    )(xp, xs, w_gate, *o_list, *l_list, four, w_ao, w_fo, w_o, ln_g.reshape(1, -1), ln_b.reshape(1, -1),
      w_rh, w_rl, b_r.reshape(-1, 1), tri)


def _sc_workers():
    info = plsc.get_sparse_core_info()
    return info.num_cores, info.num_cores * info.num_subcores


def _sc_dispatch(rows_src, pos_chunks, n_rows_out):
    n_t, w = rows_src.shape
    n_cores, n_workers = _sc_workers()
    chunks_per_worker = n_t // (n_workers * SC_CHUNK)
    mesh = plsc.VectorSubcoreMesh(core_axis_name="c", subcore_axis_name="s")

    @functools.partial(
        pl.kernel, mesh=mesh,
        out_type=jax.ShapeDtypeStruct((n_rows_out, w), rows_src.dtype),
        scratch_types=[pltpu.VMEM((TOP_K, SC_CHUNK), I32), pltpu.VMEM((SC_CHUNK, w), rows_src.dtype)],
    )
    def dispatch(src_hbm, pos_hbm, out_hbm, idx_v, rows_v):
        wid = lax.axis_index("s") * n_cores + lax.axis_index("c")

        @pl.loop(0, chunks_per_worker)
        def _(j):
            chunk = wid * chunks_per_worker + j
            pltpu.sync_copy(pos_hbm.at[chunk], idx_v)
            pltpu.sync_copy(src_hbm.at[pl.ds(chunk * SC_CHUNK, SC_CHUNK)], rows_v)
            for kk in range(TOP_K):
                pltpu.sync_copy(rows_v, out_hbm.at[idx_v.at[kk]])

    return dispatch(rows_src, pos_chunks)


def _sc_combine(rows_src, pos_chunks, n_t):
    w = rows_src.shape[1]
    n_cores, n_workers = _sc_workers()
    chunks_per_worker = n_t // (n_workers * SC_CHUNK)
    mesh = plsc.VectorSubcoreMesh(core_axis_name="c", subcore_axis_name="s")

    @functools.partial(
        pl.kernel, mesh=mesh,
        out_type=jax.ShapeDtypeStruct((TOP_K, n_t, w), rows_src.dtype),
        scratch_types=[pltpu.VMEM((TOP_K, SC_CHUNK), I32), pltpu.VMEM((SC_CHUNK, w), rows_src.dtype)],
    )
    def combine(src_hbm, pos_hbm, out_hbm, idx_v, rows_v):
        wid = lax.axis_index("s") * n_cores + lax.axis_index("c")

        @pl.loop(0, chunks_per_worker)
        def _(j):
            chunk = wid * chunks_per_worker + j
            pltpu.sync_copy(pos_hbm.at[chunk], idx_v)
            for kk in range(TOP_K):
                pltpu.sync_copy(src_hbm.at[idx_v.at[kk]], rows_v)
                pltpu.sync_copy(rows_v, out_hbm.at[kk, pl.ds(chunk * SC_CHUNK, SC_CHUNK)])

    return combine(rows_src, pos_chunks)


def _split_up_kernel(w_ref, p_ref, wg_ref, wl_ref):
    half = SPLIT_CHUNK // 2
    for c in range(w_ref.shape[2] // SPLIT_CHUNK):
        w = w_ref[0, :, c * SPLIT_CHUNK:(c + 1) * SPLIT_CHUNK].astype(BF16)
        r = jnp.dot(w, p_ref[...], preferred_element_type=F32)
        wg_ref[0, :, c * half:(c + 1) * half] = r[:, :half].astype(BF16)
        wl_ref[0, :, c * half:(c + 1) * half] = r[:, half:].astype(BF16)


def _split_up_weights(w_up):
    n_e, d_in, d_h2 = w_up.shape
    half = SPLIT_CHUNK // 2
    src = jnp.arange(SPLIT_CHUNK, dtype=I32)[:, None]
    dst = jnp.arange(SPLIT_CHUNK, dtype=I32)[None, :]
    sel = jnp.where(dst < half, src == 2 * dst, src == 2 * (dst - half) + 1).astype(BF16)
    out = jax.ShapeDtypeStruct((n_e, d_in, d_h2 // 2), BF16)
    return pl.pallas_call(
        _split_up_kernel,
        grid=(n_e,),
        in_specs=[pl.BlockSpec((1, d_in, d_h2), lambda e: (e, 0, 0)),
                  pl.BlockSpec((SPLIT_CHUNK, SPLIT_CHUNK), lambda e: (0, 0))],
        out_specs=[pl.BlockSpec((1, d_in, d_h2 // 2), lambda e: (e, 0, 0))] * 2,
        out_shape=[out, out],
        compiler_params=_cparams("parallel"),
        name="split_up_weights",
    )(w_up, sel)


def _expert_kernel(be_ref, nv_ref, x_ref, wg_ref, wl_ref, bg_ref, bl_ref, wd_ref, bd_ref, y_ref):
    blk = pl.program_id(0)
    nv = nv_ref[blk]

    @pl.when(nv > 0)
    def _():
        x = _unpack_bf16_pairs(x_ref[...])
        rows = lax.broadcasted_iota(I32, (x.shape[0], 1), 0)
        x = jnp.where(rows < nv, x, 0.0).astype(BF16)
        acc = jnp.zeros((x.shape[0], D_MODEL), F32) + bd_ref[0]
        for c in range(D_FF // FF_CHUNK):
            sl = slice(c * FF_CHUNK, (c + 1) * FF_CHUNK)
            hg = jnp.dot(x, wg_ref[0, :, sl], preferred_element_type=F32) + bg_ref[0, :, sl]
            hl = jnp.dot(x, wl_ref[0, :, sl], preferred_element_type=F32) + bl_ref[0, :, sl]
            glu = jnp.minimum(hg, SWIGLU_LIMIT)
            lin = jnp.clip(hl, -SWIGLU_LIMIT, SWIGLU_LIMIT)
            act = glu * jax.nn.sigmoid(SWIGLU_ALPHA * glu) * (lin + 1.0)
            acc = acc + jnp.dot(act.astype(BF16), wd_ref[0, sl, :], preferred_element_type=F32)
        y_ref[...] = _pack_bf16_pairs(acc)

    @pl.when(nv == 0)
    def _():
        y_ref[...] = jnp.zeros_like(y_ref)


def _expert_ffn(xs_rows, block_e, n_valid, w_glu, w_lin, b_glu, b_lin, w_down, b_down):
    n_rows = xs_rows.shape[0]
    bm = EXPERT_BLOCK
    wspec = lambda r, c: pl.BlockSpec((1, r, c), lambda b, be, nv: (be[b], 0, 0))
    grid_spec = pltpu.PrefetchScalarGridSpec(
        num_scalar_prefetch=2,
        grid=(n_rows // bm,),
        in_specs=[
            pl.BlockSpec((bm, D_MODEL // 2), lambda b, be, nv: (b, 0)),
            wspec(D_MODEL, D_FF), wspec(D_MODEL, D_FF), wspec(1, D_FF), wspec(1, D_FF),
            wspec(D_FF, D_MODEL), wspec(1, D_MODEL),
        ],
        out_specs=pl.BlockSpec((bm, D_MODEL // 2), lambda b, be, nv: (b, 0)),
    )
    return pl.pallas_call(
        _expert_kernel,
        grid_spec=grid_spec,
        out_shape=jax.ShapeDtypeStruct((n_rows, D_MODEL // 2), U32),
        compiler_params=_cparams("arbitrary"),
        name="expert_ffn",
    )(block_e, n_valid, xs_rows, w_glu, w_lin, b_glu, b_lin, w_down, b_down)


def _final_kernel(x1_ref, yg_ref, gt_ref, g_ref, b_ref, o_ref):
    gt = gt_ref[...]
    ffn = jnp.zeros(x1_ref.shape, F32)
    for kk in range(TOP_K):
        ffn = ffn + gt[:, kk:kk + 1] * _unpack_bf16_pairs(yg_ref[kk])
    o_ref[...] = _layer_norm(DEEPNORM_ALPHA * x1_ref[...] + ffn, g_ref[...], b_ref[...])


def _final(x1, yg, gates_t, ln_g, ln_b, first_row, n_rows):
    tm = TOKEN_TILE
    off = first_row // tm
    return pl.pallas_call(
        _final_kernel,
        grid=(n_rows // tm,),
        in_specs=[
            pl.BlockSpec((tm, D_MODEL), lambda i: (off + i, 0)),
            pl.BlockSpec((TOP_K, tm, D_MODEL // 2), lambda i: (0, off + i, 0)),
            pl.BlockSpec((tm, TOP_K), lambda i: (off + i, 0)),
            pl.BlockSpec((1, D_MODEL), lambda i: (0, 0)),
            pl.BlockSpec((1, D_MODEL), lambda i: (0, 0)),
        ],
        out_specs=pl.BlockSpec((tm, D_MODEL), lambda i: (i, 0)),
        out_shape=jax.ShapeDtypeStruct((n_rows, D_MODEL), F32),
        compiler_params=_cparams("parallel"),
        name="combine_ln2",
    )(x1, yg, gates_t, ln_g.reshape(1, -1), ln_b.reshape(1, -1))


def _routing_tables(idx, rank, counts, n_blocks):
    bm = EXPERT_BLOCK
    padded = (counts + bm - 1) // bm * bm
    pad_end = jnp.cumsum(padded)
    pad_start = pad_end - padded
    experts = jnp.arange(N_EXPERTS, dtype=I32)

    def lookup(table, e):
        shape = (N_EXPERTS,) + (1,) * e.ndim
        return jnp.sum(jnp.where(e[None] == experts.reshape(shape), table.reshape(shape), 0), axis=0)

    pos = lookup(pad_start, idx) + rank
    blk_row0 = jnp.arange(n_blocks, dtype=I32) * bm
    block_e = jnp.minimum(jnp.sum((pad_end[:, None] <= blk_row0[None, :]).astype(I32), axis=0), N_EXPERTS - 1)
    n_valid = jnp.clip(lookup(counts, block_e) - (blk_row0 - lookup(pad_start, block_e)), 0, bm)
    return pos.astype(I32), block_e.astype(I32), n_valid.astype(I32)


def _moe(pk, idx, rank, counts, w_glu, w_lin, b_glu, b_lin, w_down, b_down):
    n_t = pk.shape[0]
    n_blocks = (n_t * TOP_K) // EXPERT_BLOCK + N_EXPERTS
    pos, block_e, n_valid = _routing_tables(idx, rank, counts, n_blocks)
    pos_chunks = pos.reshape(TOP_K, n_t // SC_CHUNK, SC_CHUNK).transpose(1, 0, 2)
    xs_rows = _sc_dispatch(pk, pos_chunks, n_blocks * EXPERT_BLOCK)
    y_rows = _expert_ffn(xs_rows, block_e, n_valid, w_glu, w_lin, b_glu, b_lin, w_down, b_down)
    return _sc_combine(y_rows, pos_chunks, n_t)


def _rope_tables(seq_max):
    inv = ROPE_THETA ** (-jnp.arange(0, HEAD_DIM, 2, dtype=F32) / HEAD_DIM)
    ang = jnp.arange(seq_max).astype(F32)[:, None] * inv[None, :]
    cos, sin = jnp.cos(ang), jnp.sin(ang)
    reps = LANES // HEAD_DIM
    cos_t = jnp.tile(jnp.concatenate([cos, cos], axis=1), (1, reps))
    sin_t = jnp.tile(jnp.concatenate([-sin, sin], axis=1), (1, reps))
    return cos_t, sin_t


def _encoder_layer(xp, xs, seq_prompt, batch_sample, seq_sample, w_in, w_attn_out, w_four_out, w_o, ln1_g, ln1_b,
                   w_router, b_router, w_up, b_up, w_down, b_down, ln2_g, ln2_b):
    n_p, n_s = xp.shape[0], xs.shape[0]
    assert n_p == seq_prompt and n_s == batch_sample * seq_sample
    assert seq_prompt % (DFT_STAGE1 * 16) == 0 and seq_sample % TOKEN_TILE == 0 and n_p % seq_sample == 0
    n_t = n_p + n_s
    assert n_t % (max(DILATIONS) * ATTN_TILE) == 0 and seq_sample % (max(DILATIONS) * ATTN_SUB) == 0

    w_proj = w_in[:, :PROJ_WIDTH].astype(BF16)
    w_gate = w_in[:, PROJ_WIDTH:].astype(BF16)
    cos_t, sin_t = _rope_tables(max(seq_prompt, seq_sample))
    q_views, k_views, v_views, f = _inproj(xp, xs, w_proj, cos_t, sin_t, seq_sample)

    o_list, l_list = [], []
    for gi, dil in enumerate(DILATIONS):
        o, lse = _attention_group(q_views[gi], k_views[gi], v_views[gi], dil, n_p, seq_sample)
        o_list.append(o)
        l_list.append(lse)

    four = jnp.concatenate([_fourier_prompt(f, seq_prompt), _fourier_sample(f, n_p, batch_sample, seq_sample)], axis=0)

    x1, pk, idx, gates, rank, cnt = _merge_route(
        xp, xs, w_gate, o_list, l_list, four, w_attn_out.astype(BF16), w_four_out.astype(BF16), w_o.astype(BF16),
        ln1_g, ln1_b, w_router, b_router)

    w_glu, w_lin = _split_up_weights(w_up)
    b_glu = b_up[:, None, 0::2]
    b_lin = b_up[:, None, 1::2]
    yg = _moe(pk, idx, rank, cnt[:, 0], w_glu, w_lin, b_glu, b_lin, w_down.astype(BF16), b_down[:, None, :])

    gates_t = gates.T
    y_p = _final(x1, yg, gates_t, ln2_g, ln2_b, 0, n_p)
    y_s = _final(x1, yg, gates_t, ln2_g, ln2_b, n_p, n_s)
    return y_p, y_s


def kernel(x_prompt, x_sample, w_in, w_attn_out, w_four_out, w_o, ln1_g, ln1_b, w_router, b_router, w_up, b_up,
           w_down, b_down, ln2_g, ln2_b):
    assert w_in.shape[0] == DEPTH
    bp, sp, d = x_prompt.shape
    bs, ss, _ = x_sample.shape
    assert bp == 1 and d == D_MODEL
    y_p, y_s = _encoder_layer(
        x_prompt.reshape(sp, d), x_sample.reshape(bs * ss, d), sp, bs, ss,
        w_in[0], w_attn_out[0], w_four_out[0], w_o[0], ln1_g[0], ln1_b[0], w_router[0], b_router[0],
        w_up[0], b_up[0], w_down[0], b_down[0], ln2_g[0], ln2_b[0])
    return y_p.reshape(x_prompt.shape), y_s.reshape(x_sample.shape)
```

```python
import functools
import math

import jax
import jax.numpy as jnp
from jax import lax
from jax.experimental import pallas as pl
from jax.experimental.pallas import tpu as pltpu
from jax.experimental.pallas import tpu_sc as plsc

F32 = jnp.float32
BF16 = jnp.bfloat16
I32 = jnp.int32
U32 = jnp.uint32

D_MODEL = 1024
HEAD_DIM = 64
HEADS_PER_GROUP = 4
GROUP_WIDTH = HEADS_PER_GROUP * HEAD_DIM
DILATIONS = (1, 4, 16)
HALF_WINDOW = 64
N_GROUPS = len(DILATIONS)
QKV_WIDTH = N_GROUPS * GROUP_WIDTH
F_GROUPS = 4
F_GROUP_DIM = 64
F_WIDTH = F_GROUPS * F_GROUP_DIM
PROJ_WIDTH = 3 * QKV_WIDTH + F_WIDTH
N_EXPERTS = 32
TOP_K = 4
D_FF = 1024
SWIGLU_LIMIT = 7.0
SWIGLU_ALPHA = 1.702
LN_EPS = 1e-5
ROPE_THETA = 10000.0
NEG_INF = -1e30
DEPTH = 1
DEEPNORM_ALPHA = (2 * DEPTH) ** 0.25

LANES = 128
TOKEN_TILE = 512
ATTN_TILE = 512
ATTN_SUB = 128
EXPERT_BLOCK = 512
FF_CHUNK = 1024
SPLIT_CHUNK = 256
DFT_STAGE1 = 128
SC_CHUNK = 64
MOE_PARTS = 2
VMEM_LIMIT = 56 * 1024 * 1024


def _cparams(*sem):
    return pltpu.CompilerParams(dimension_semantics=sem, vmem_limit_bytes=VMEM_LIMIT)


def _to_dilated_view(stage_ref, out_ref, val, dil):
    if dil == 1:
        out_ref[...] = val.astype(BF16)
        return
    rows = val.shape[0] // dil
    for c in range(GROUP_WIDTH // LANES):
        stage_ref[c] = val[:, c * LANES:(c + 1) * LANES]
    for r in range(dil):
        for c in range(GROUP_WIDTH // LANES):
            lo = r * GROUP_WIDTH + c * LANES
            out_ref[:, lo:lo + LANES] = stage_ref[c, pl.ds(r, rows, stride=dil), :].astype(BF16)


def _inproj_kernel(xp_ref, xs_ref, w_ref, cos_ref, sin_ref, *refs, n_prompt_tiles):
    qkv_refs = refs[:3 * N_GROUPS]
    f_ref = refs[3 * N_GROUPS]
    stage_refs = refs[3 * N_GROUPS + 1:]
    i = pl.program_id(0)
    x = jnp.where(i < n_prompt_tiles, xp_ref[...], xs_ref[...]).astype(BF16)
    proj = jnp.dot(x, w_ref[...], preferred_element_type=F32)
    cos = cos_ref[...]
    sin = sin_ref[...]
    lane = lax.broadcasted_iota(I32, cos.shape, 1)
    first_half = (lane % HEAD_DIM) < (HEAD_DIM // 2)

    def rope(t):
        parts = []
        for c in range(GROUP_WIDTH // LANES):
            tc = t[:, c * LANES:(c + 1) * LANES]
            swapped = jnp.where(first_half, pltpu.roll(tc, LANES - HEAD_DIM // 2, 1), pltpu.roll(tc, HEAD_DIM // 2, 1))
            parts.append(tc * cos + swapped * sin)
        return jnp.concatenate(parts, axis=1)

    for gi, dil in enumerate(DILATIONS):
        sl = slice(gi * GROUP_WIDTH, (gi + 1) * GROUP_WIDTH)
        qg = rope(proj[:, sl]) * (HEAD_DIM ** -0.5)
        kg = rope(proj[:, QKV_WIDTH + gi * GROUP_WIDTH:QKV_WIDTH + (gi + 1) * GROUP_WIDTH])
        vg = proj[:, 2 * QKV_WIDTH + gi * GROUP_WIDTH:2 * QKV_WIDTH + (gi + 1) * GROUP_WIDTH]
        for which, val in enumerate((qg, kg, vg)):
            _to_dilated_view(stage_refs[which * N_GROUPS + gi], qkv_refs[which * N_GROUPS + gi], val, dil)
    f_ref[...] = proj[:, 3 * QKV_WIDTH:PROJ_WIDTH].astype(BF16)


def _inproj(xp, xs, w_proj, cos_t, sin_t, seq_sample):
    n_p, n_s = xp.shape[0], xs.shape[0]
    n_t = n_p + n_s
    tm = TOKEN_TILE
    npt = n_p // tm
    tiles_per_sample_seq = seq_sample // tm

    def table_idx(i):
        return (jnp.where(i < npt, i, (i - npt) % tiles_per_sample_seq), 0)

    view_specs = [pl.BlockSpec((tm // d, d * GROUP_WIDTH), lambda i: (i, 0)) for d in DILATIONS] * 3
    view_shapes = [jax.ShapeDtypeStruct((n_t // d, d * GROUP_WIDTH), BF16) for d in DILATIONS] * 3
    outs = pl.pallas_call(
        functools.partial(_inproj_kernel, n_prompt_tiles=npt),
        grid=(n_t // tm,),
        in_specs=[
            pl.BlockSpec((tm, D_MODEL), lambda i: (jnp.minimum(i, npt - 1), 0)),
            pl.BlockSpec((tm, D_MODEL), lambda i: (jnp.maximum(i - npt, 0), 0)),
            pl.BlockSpec((D_MODEL, PROJ_WIDTH), lambda i: (0, 0)),
            pl.BlockSpec((tm, LANES), table_idx),
            pl.BlockSpec((tm, LANES), table_idx),
        ],
        out_specs=view_specs + [pl.BlockSpec((tm, F_WIDTH), lambda i: (i, 0))],
        out_shape=view_shapes + [jax.ShapeDtypeStruct((n_t, F_WIDTH), BF16)],
        scratch_shapes=[pltpu.VMEM((GROUP_WIDTH // LANES, tm, LANES), F32)] * (3 * N_GROUPS),
        compiler_params=_cparams("parallel"),
        name="inproj_rope",
    )(xp, xs, w_proj, cos_t, sin_t)
    q_views, k_views, v_views = outs[0:N_GROUPS], outs[N_GROUPS:2 * N_GROUPS], outs[2 * N_GROUPS:3 * N_GROUPS]
    return q_views, k_views, v_views, outs[3 * N_GROUPS]


def _attn_kernel(q_ref, kl_ref, km_ref, kr_ref, vl_ref, vm_ref, vr_ref, o_ref, lse_ref, *, prompt_rows, sample_rows):
    i = pl.program_id(1)
    kext = jnp.concatenate([kl_ref[...], km_ref[...], kr_ref[...]], axis=0)
    vext = jnp.concatenate([vl_ref[...], vm_ref[...], vr_ref[...]], axis=0)
    kw = ATTN_SUB + 2 * HALF_WINDOW
    a = lax.broadcasted_iota(I32, (ATTN_SUB, kw), 0)
    c = lax.broadcasted_iota(I32, (ATTN_SUB, kw), 1)
    in_band = jnp.abs(c - HALF_WINDOW - a) <= HALF_WINDOW
    head_of_lane = lax.broadcasted_iota(I32, (1, GROUP_WIDTH), 1) // HEAD_DIM
    for j in range(ATTN_TILE // ATTN_SUB):
        r0 = i * ATTN_TILE + j * ATTN_SUB
        in_prompt = r0 < prompt_rows
        b = jnp.maximum(r0 - prompt_rows, 0) // sample_rows
        lo = jnp.where(in_prompt, 0, prompt_rows + b * sample_rows)
        hi = jnp.where(in_prompt, prompt_rows, prompt_rows + (b + 1) * sample_rows)
        key_row = r0 - HALF_WINDOW + c
        valid = in_band & (key_row >= lo) & (key_row < hi)
        qj = q_ref[j * ATTN_SUB:(j + 1) * ATTN_SUB, :]
        kj = kext[j * ATTN_SUB:j * ATTN_SUB + kw, :]
        vj = vext[j * ATTN_SUB:j * ATTN_SUB + kw, :]
        acc = jnp.zeros((ATTN_SUB, GROUP_WIDTH), F32)
        lse_full = jnp.zeros((ATTN_SUB, GROUP_WIDTH), F32)
        for h in range(HEADS_PER_GROUP):
            mine = head_of_lane == h
            qh = jnp.where(mine, qj, jnp.zeros_like(qj))
            s = lax.dot_general(qh, kj, (((1,), (1,)), ((), ())), preferred_element_type=F32)
            s = jnp.where(valid, s, NEG_INF)
            m = jnp.max(s, axis=1, keepdims=True)
            p = jnp.exp(s - m)
            l = jnp.sum(p, axis=1, keepdims=True)
            pn = (p * (1.0 / l)).astype(BF16)
            vh = jnp.where(mine, vj, jnp.zeros_like(vj))
            acc = acc + jnp.dot(pn, vh, preferred_element_type=F32)
            lse_full = jnp.where(mine, m + jnp.log(l), lse_full)
        o_ref[j * ATTN_SUB:(j + 1) * ATTN_SUB, :] = acc.astype(BF16)
        lse_ref[j * ATTN_SUB:(j + 1) * ATTN_SUB, :] = lse_full


def _attention_group(qv, kv, vv, dil, n_prompt, seq_sample):
    rows = qv.shape[0]
    tq = ATTN_TILE
    halo_per_tile = tq // HALF_WINDOW
    n_halo_blocks = rows // HALF_WINDOW
    main = pl.BlockSpec((tq, GROUP_WIDTH), lambda r, i: (i, r))
    left = pl.BlockSpec((HALF_WINDOW, GROUP_WIDTH), lambda r, i: (jnp.maximum(i * halo_per_tile - 1, 0), r))
    right = pl.BlockSpec((HALF_WINDOW, GROUP_WIDTH),
                         lambda r, i: (jnp.minimum((i + 1) * halo_per_tile, n_halo_blocks - 1), r))
    return pl.pallas_call(
        functools.partial(_attn_kernel, prompt_rows=n_prompt // dil, sample_rows=seq_sample // dil),
        grid=(dil, rows // tq),
        in_specs=[main, left, main, right, left, main, right],
        out_specs=[main, main],
        out_shape=[jax.ShapeDtypeStruct((rows, dil * GROUP_WIDTH), BF16),
                   jax.ShapeDtypeStruct((rows, dil * GROUP_WIDTH), F32)],
        compiler_params=_cparams("parallel", "parallel"),
        name=f"dilated_attention_{dil}",
    )(qv, kv, kv, kv, vv, vv, vv)


def _angle_table(idx, period):
    return (idx % period).astype(F32) * (2.0 * math.pi / period)


def _channel_dft(seq_len):
    c = jnp.arange(F_GROUP_DIM, dtype=I32)
    th = _angle_table(c[:, None] * c[None, :], F_GROUP_DIM)
    scale = (seq_len * F_GROUP_DIM) ** -0.5
    eye = jnp.eye(F_GROUPS, dtype=F32)
    cb = jnp.kron(eye, jnp.cos(th)) * scale
    sb = jnp.kron(eye, jnp.sin(th)) * scale
    return jnp.concatenate([cb, sb], axis=0).astype(BF16)


def _dft_stage1_kernel(m_ref, x_ref, a_ref):
    a_ref[...] = jnp.dot(m_ref[...], x_ref[...], preferred_element_type=F32).astype(BF16)


def _dft_stage2_kernel(g_ref, a_ref, cs_ref, o_ref):
    s2 = a_ref.shape[2]
    a = jnp.concatenate([a_ref[0, 0], a_ref[1, 0]], axis=0)
    x = jnp.dot(g_ref[0], a, preferred_element_type=F32)
    xr = x[:s2].astype(BF16)
    xi = x[s2:].astype(BF16)
    o_ref[...] = (jnp.dot(xr, cs_ref[:F_WIDTH, :], preferred_element_type=F32)
                  + jnp.dot(xi, cs_ref[F_WIDTH:, :], preferred_element_type=F32)).astype(BF16)


def _fourier_prompt(f, seq):
    s1 = DFT_STAGE1
    s2 = seq // s1
    n1 = jnp.arange(s1, dtype=I32)
    th1 = _angle_table(n1[:, None] * n1[None, :], s1)
    m1 = jnp.concatenate([jnp.cos(th1), -jnp.sin(th1)], axis=0).astype(BF16)
    k1 = jnp.arange(s1, dtype=I32)[:, None, None]
    k2 = jnp.arange(s2, dtype=I32)[None, :, None]
    n2 = jnp.arange(s2, dtype=I32)[None, None, :]
    th = _angle_table((k1 + s1 * k2) * n2, seq)
    gr, gi = jnp.cos(th), -jnp.sin(th)
    g = jnp.concatenate([jnp.concatenate([gr, -gi], axis=2), jnp.concatenate([gi, gr], axis=2)], axis=1).astype(BF16)
    cs = _channel_dft(seq)

    cols = s2 * F_WIDTH
    fv = f.reshape(f.shape[0] // s2, cols)
    tn = min(cols, 4096)
    a = pl.pallas_call(
        _dft_stage1_kernel,
        grid=(cols // tn,),
        in_specs=[pl.BlockSpec((2 * s1, s1), lambda j: (0, 0)), pl.BlockSpec((s1, tn), lambda j: (0, j))],
        out_specs=pl.BlockSpec((2 * s1, tn), lambda j: (0, j)),
        out_shape=jax.ShapeDtypeStruct((2 * s1, cols), BF16),
        compiler_params=_cparams("parallel"),
        name="dft_stage1",
    )(m1, fv)
    a4 = a.reshape(2, s1, s2, F_WIDTH)
    out = pl.pallas_call(
        _dft_stage2_kernel,
        grid=(s1,),
        in_specs=[pl.BlockSpec((1, 2 * s2, 2 * s2), lambda k: (k, 0, 0)),
                  pl.BlockSpec((2, 1, s2, F_WIDTH), lambda k: (0, k, 0, 0)),
                  pl.BlockSpec((2 * F_WIDTH, F_WIDTH), lambda k: (0, 0))],
        out_specs=pl.BlockSpec((s2, F_WIDTH), lambda k: (0, k)),
        out_shape=jax.ShapeDtypeStruct((s2, s1 * F_WIDTH), BF16),
        compiler_params=_cparams("parallel"),
        name="dft_stage2",
    )(g, a4, cs)
    return out.reshape(seq, F_WIDTH)


def _dft_direct_kernel(g_ref, x_ref, cs_ref, o_ref):
    tk = o_ref.shape[1]
    x = jnp.dot(g_ref[...], x_ref[0], preferred_element_type=F32)
    xr = x[:tk].astype(BF16)
    xi = x[tk:].astype(BF16)
    o_ref[0] = (jnp.dot(xr, cs_ref[:F_WIDTH, :], preferred_element_type=F32)
                + jnp.dot(xi, cs_ref[F_WIDTH:, :], preferred_element_type=F32)).astype(BF16)


def _fourier_sample(f, n_prompt, batch, seq):
    tk = 512
    kk = jnp.arange(seq, dtype=I32)
    th = _angle_table(kk[:, None] * kk[None, :], seq).reshape(seq // tk, tk, seq)
    g = jnp.concatenate([jnp.cos(th), -jnp.sin(th)], axis=1).reshape(2 * seq, seq).astype(BF16)
    cs = _channel_dft(seq)
    f3 = f.reshape(f.shape[0] // seq, seq, F_WIDTH)
    first = n_prompt // seq
    return pl.pallas_call(
        _dft_direct_kernel,
        grid=(seq // tk, batch),
        in_specs=[pl.BlockSpec((2 * tk, seq), lambda t, b: (t, 0)),
                  pl.BlockSpec((1, seq, F_WIDTH), lambda t, b: (first + b, 0, 0)),
                  pl.BlockSpec((2 * F_WIDTH, F_WIDTH), lambda t, b: (0, 0))],
        out_specs=pl.BlockSpec((1, tk, F_WIDTH), lambda t, b: (b, t, 0)),
        out_shape=jax.ShapeDtypeStruct((batch, seq, F_WIDTH), BF16),
        compiler_params=_cparams("parallel", "parallel"),
        name="dft_direct",
    )(g, f3, cs).reshape(batch * seq, F_WIDTH)


def _pack_bf16_pairs(x):
    w = x.shape[1] // 2
    bits = lax.bitcast_convert_type(x.astype(BF16).astype(F32), U32)
    return (bits[:, :w] >> 16) | (bits[:, w:] & jnp.uint32(0xFFFF0000))


def _unpack_bf16_pairs(u):
    lo = lax.bitcast_convert_type(u << 16, F32)
    hi = lax.bitcast_convert_type(u & jnp.uint32(0xFFFF0000), F32)
    return jnp.concatenate([lo, hi], axis=1)


def _layer_norm(h, g, b):
    mu = jnp.mean(h, axis=-1, keepdims=True)
    d = h - mu
    var = jnp.mean(d * d, axis=-1, keepdims=True)
    return d * lax.rsqrt(var + LN_EPS) * g + b


def _from_dilated_view(stage_ref, blk_ref, dil):
    if dil == 1:
        return blk_ref[...].astype(F32)
    rows = blk_ref.shape[0]
    for r in range(dil):
        for c in range(GROUP_WIDTH // LANES):
            lo = r * GROUP_WIDTH + c * LANES
            stage_ref[c, pl.ds(r, rows, stride=dil), :] = blk_ref[:, lo:lo + LANES].astype(F32)
    return jnp.concatenate([stage_ref[c] for c in range(GROUP_WIDTH // LANES)], axis=1)


def _merge_kernel(xp_ref, xs_ref, wg_ref, o1_ref, o2_ref, o3_ref, l1_ref, l2_ref, l3_ref, four_ref,
                  wao_ref, wfo_ref, wo_ref, g_ref, b_ref, wrh_ref, wrl_ref, br_ref, tri_ref,
                  x1_ref, pk_ref, idx_ref, gate_ref, rank_ref, cnt_ref, run_ref, *stage_refs, n_prompt_tiles):
    i = pl.program_id(0)

    @pl.when(i == 0)
    def _():
        run_ref[...] = jnp.zeros_like(run_ref)

    x = jnp.where(i < n_prompt_tiles, xp_ref[...], xs_ref[...])
    gpre = jnp.dot(x.astype(BF16), wg_ref[...], preferred_element_type=F32)
    o1, o2, o3 = [_from_dilated_view(stage_refs[gi], ref, d)
                  for gi, (ref, d) in enumerate(zip((o1_ref, o2_ref, o3_ref), DILATIONS))]
    l1, l2, l3 = [_from_dilated_view(stage_refs[N_GROUPS + gi], ref, d)
                  for gi, (ref, d) in enumerate(zip((l1_ref, l2_ref, l3_ref), DILATIONS))]
    m = jnp.maximum(jnp.maximum(l1, l2), l3)
    e1, e2, e3 = jnp.exp(l1 - m), jnp.exp(l2 - m), jnp.exp(l3 - m)
    attn = (e1 * o1 + e2 * o2 + e3 * o3) * (1.0 / (e1 + e2 + e3))
    a = jnp.dot(attn.astype(BF16), wao_ref[...], preferred_element_type=F32)
    ff = jnp.dot(four_ref[...], wfo_ref[...], preferred_element_type=F32)
    merged = jax.nn.sigmoid(gpre[:, :D_MODEL]) * a + jax.nn.sigmoid(gpre[:, D_MODEL:]) * ff
    mix = jnp.dot(merged.astype(BF16), wo_ref[...], preferred_element_type=F32)
    x1 = _layer_norm(DEEPNORM_ALPHA * x + mix, g_ref[...], b_ref[...])
    x1_ref[...] = x1
    pk_ref[...] = _pack_bf16_pairs(x1)

    xh = x1.astype(BF16)
    xl = (x1 - xh.astype(F32)).astype(BF16)
    nt = (((1,), (1,)), ((), ()))
    logits = (lax.dot_general(wrh_ref[...], xh, nt, preferred_element_type=F32)
              + lax.dot_general(wrl_ref[...], xh, nt, preferred_element_type=F32)
              + lax.dot_general(wrh_ref[...], xl, nt, preferred_element_type=F32)) + br_ref[...]
    tm = logits.shape[1]
    eio = lax.broadcasted_iota(I32, (N_EXPERTS, tm), 0)
    work = logits
    vals, sels, picks = [], [], []
    for _ in range(TOP_K):
        mk = jnp.max(work, axis=0, keepdims=True)
        ik = jnp.min(jnp.where(work == mk, eio, N_EXPERTS), axis=0, keepdims=True)
        sel = eio == ik
        vals.append(mk)
        sels.append(sel)
        picks.append(ik)
        work = jnp.where(sel, -jnp.inf, work)
    es = [jnp.exp(v - vals[0]) for v in vals]
    inv_den = 1.0 / (es[0] + es[1] + es[2] + es[3])
    chosen = (sels[0] | sels[1] | sels[2] | sels[3])
    chosen_f = chosen.astype(F32)
    prefix = jnp.dot(chosen_f.astype(BF16), tri_ref[...], preferred_element_type=F32)
    before = prefix + run_ref[...]
    for kk in range(TOP_K):
        idx_ref[kk:kk + 1, :] = picks[kk]
        gate_ref[kk:kk + 1, :] = es[kk] * inv_den
        rank_ref[kk:kk + 1, :] = jnp.sum(jnp.where(sels[kk], before, 0.0), axis=0, keepdims=True).astype(I32)
    run_ref[...] = run_ref[...] + jnp.sum(chosen_f, axis=1, keepdims=True)
    cnt_ref[...] = jnp.broadcast_to(run_ref[...], cnt_ref.shape).astype(I32)


def _merge_route(xp, xs, w_gate, o_list, l_list, four, w_ao, w_fo, w_o, ln_g, ln_b, w_r, b_r, tile0, n_tiles):
    n_p = xp.shape[0]
    tm = TOKEN_TILE
    n_t = n_tiles * tm
    npt = n_p // tm
    w_rt = w_r.T
    w_rh = w_rt.astype(BF16)
    w_rl = (w_rt - w_rh.astype(F32)).astype(BF16)
    tri = (jnp.arange(tm)[:, None] < jnp.arange(tm)[None, :]).astype(BF16)
    row = lambda w: pl.BlockSpec((tm, w), lambda i: (i, 0))
    src_row = lambda w: pl.BlockSpec((tm, w), lambda i: (tile0 + i, 0))
    full = lambda r, c: pl.BlockSpec((r, c), lambda i: (0, 0))
    lane_row = pl.BlockSpec((TOP_K, tm), lambda i: (0, i))
    views = [pl.BlockSpec((tm // d, d * GROUP_WIDTH), lambda i: (tile0 + i, 0)) for d in DILATIONS]
    return pl.pallas_call(
        functools.partial(_merge_kernel, n_prompt_tiles=npt - tile0),
        grid=(n_tiles,),
        in_specs=[
            pl.BlockSpec((tm, D_MODEL), lambda i: (jnp.clip(tile0 + i, 0, npt - 1), 0)),
            pl.BlockSpec((tm, D_MODEL), lambda i: (jnp.maximum(tile0 + i - npt, 0), 0)),
            full(D_MODEL, 2 * D_MODEL),
            *views, *views,
            src_row(F_WIDTH),
            full(GROUP_WIDTH, D_MODEL), full(F_WIDTH, D_MODEL), full(D_MODEL, D_MODEL),
            full(1, D_MODEL), full(1, D_MODEL),
            full(N_EXPERTS, D_MODEL), full(N_EXPERTS, D_MODEL), full(N_EXPERTS, 1),
            full(tm, tm),
        ],
        out_specs=[row(D_MODEL), row(D_MODEL // 2), lane_row, lane_row, lane_row, full(N_EXPERTS, LANES)],
        out_shape=[
            jax.ShapeDtypeStruct((n_t, D_MODEL), F32),
            jax.ShapeDtypeStruct((n_t, D_MODEL // 2), U32),
            jax.ShapeDtypeStruct((TOP_K, n_t), I32),
            jax.ShapeDtypeStruct((TOP_K, n_t), F32),
            jax.ShapeDtypeStruct((TOP_K, n_t), I32),
            jax.ShapeDtypeStruct((N_EXPERTS, LANES), I32),
        ],
        scratch_shapes=[pltpu.VMEM((N_EXPERTS, 1), F32)]
        + [pltpu.VMEM((GROUP_WIDTH // LANES, tm, LANES), F32)] * (2 * N_GROUPS),
        compiler_params=_cparams("arbitrary"),
        name="merge_ln_route",
    )(xp, xs, w_gate, *o_list, *l_list, four, w_ao, w_fo, w_o, ln_g.reshape(1, -1), ln_b.reshape(1, -1),
      w_rh, w_rl, b_r.reshape(-1, 1), tri)


def _sc_workers():
    info = plsc.get_sparse_core_info()
    return info.num_cores, info.num_cores * info.num_subcores


def _sc_dispatch(rows_src, pos_chunks, n_rows_out):
    n_t, w = rows_src.shape
    n_cores, n_workers = _sc_workers()
    chunks_per_worker = n_t // (n_workers * SC_CHUNK)
    mesh = plsc.VectorSubcoreMesh(core_axis_name="c", subcore_axis_name="s")

    @functools.partial(
        pl.kernel, mesh=mesh,
        out_type=jax.ShapeDtypeStruct((n_rows_out, w), rows_src.dtype),
        scratch_types=[pltpu.VMEM((TOP_K, SC_CHUNK), I32), pltpu.VMEM((SC_CHUNK, w), rows_src.dtype)],
    )
    def dispatch(src_hbm, pos_hbm, out_hbm, idx_v, rows_v):
        wid = lax.axis_index("s") * n_cores + lax.axis_index("c")

        @pl.loop(0, chunks_per_worker)
        def _(j):
            chunk = wid * chunks_per_worker + j
            pltpu.sync_copy(pos_hbm.at[chunk], idx_v)
            pltpu.sync_copy(src_hbm.at[pl.ds(chunk * SC_CHUNK, SC_CHUNK)], rows_v)
            for kk in range(TOP_K):
                pltpu.sync_copy(rows_v, out_hbm.at[idx_v.at[kk]])

    return dispatch(rows_src, pos_chunks)


def _sc_combine(rows_src, pos_chunks, n_t):
    w = rows_src.shape[1]
    n_cores, n_workers = _sc_workers()
    chunks_per_worker = n_t // (n_workers * SC_CHUNK)
    mesh = plsc.VectorSubcoreMesh(core_axis_name="c", subcore_axis_name="s")

    @functools.partial(
        pl.kernel, mesh=mesh,
        out_type=jax.ShapeDtypeStruct((TOP_K, n_t, w), rows_src.dtype),
        scratch_types=[pltpu.VMEM((TOP_K, SC_CHUNK), I32), pltpu.VMEM((SC_CHUNK, w), rows_src.dtype)],
    )
    def combine(src_hbm, pos_hbm, out_hbm, idx_v, rows_v):
        wid = lax.axis_index("s") * n_cores + lax.axis_index("c")

        @pl.loop(0, chunks_per_worker)
        def _(j):
            chunk = wid * chunks_per_worker + j
            pltpu.sync_copy(pos_hbm.at[chunk], idx_v)
            for kk in range(TOP_K):
                pltpu.sync_copy(src_hbm.at[idx_v.at[kk]], rows_v)
                pltpu.sync_copy(rows_v, out_hbm.at[kk, pl.ds(chunk * SC_CHUNK, SC_CHUNK)])

    return combine(rows_src, pos_chunks)


def _split_up_kernel(w_ref, p_ref, wg_ref, wl_ref):
    half = SPLIT_CHUNK // 2
    for c in range(w_ref.shape[2] // SPLIT_CHUNK):
        w = w_ref[0, :, c * SPLIT_CHUNK:(c + 1) * SPLIT_CHUNK].astype(BF16)
        r = jnp.dot(w, p_ref[...], preferred_element_type=F32)
        wg_ref[0, :, c * half:(c + 1) * half] = r[:, :half].astype(BF16)
        wl_ref[0, :, c * half:(c + 1) * half] = r[:, half:].astype(BF16)


def _split_up_weights(w_up):
    n_e, d_in, d_h2 = w_up.shape
    half = SPLIT_CHUNK // 2
    src = jnp.arange(SPLIT_CHUNK, dtype=I32)[:, None]
    dst = jnp.arange(SPLIT_CHUNK, dtype=I32)[None, :]
    sel = jnp.where(dst < half, src == 2 * dst, src == 2 * (dst - half) + 1).astype(BF16)
    out = jax.ShapeDtypeStruct((n_e, d_in, d_h2 // 2), BF16)
    return pl.pallas_call(
        _split_up_kernel,
        grid=(n_e,),
        in_specs=[pl.BlockSpec((1, d_in, d_h2), lambda e: (e, 0, 0)),
                  pl.BlockSpec((SPLIT_CHUNK, SPLIT_CHUNK), lambda e: (0, 0))],
        out_specs=[pl.BlockSpec((1, d_in, d_h2 // 2), lambda e: (e, 0, 0))] * 2,
        out_shape=[out, out],
        compiler_params=_cparams("parallel"),
        name="split_up_weights",
    )(w_up, sel)


def _expert_kernel(be_ref, nv_ref, x_ref, wg_ref, wl_ref, bg_ref, bl_ref, wd_ref, bd_ref, y_ref):
    blk = pl.program_id(0)
    nv = nv_ref[blk]

    @pl.when(nv > 0)
    def _():
        x = _unpack_bf16_pairs(x_ref[...])
        rows = lax.broadcasted_iota(I32, (x.shape[0], 1), 0)
        x = jnp.where(rows < nv, x, 0.0).astype(BF16)
        acc = jnp.zeros((x.shape[0], D_MODEL), F32) + bd_ref[0]
        for c in range(D_FF // FF_CHUNK):
            sl = slice(c * FF_CHUNK, (c + 1) * FF_CHUNK)
            hg = jnp.dot(x, wg_ref[0, :, sl], preferred_element_type=F32) + bg_ref[0, :, sl]
            hl = jnp.dot(x, wl_ref[0, :, sl], preferred_element_type=F32) + bl_ref[0, :, sl]
            glu = jnp.minimum(hg, SWIGLU_LIMIT)
            lin = jnp.clip(hl, -SWIGLU_LIMIT, SWIGLU_LIMIT)
            act = glu * jax.nn.sigmoid(SWIGLU_ALPHA * glu) * (lin + 1.0)
            acc = acc + jnp.dot(act.astype(BF16), wd_ref[0, sl, :], preferred_element_type=F32)
        y_ref[...] = _pack_bf16_pairs(acc)

    @pl.when(nv == 0)
    def _():
        y_ref[...] = jnp.zeros_like(y_ref)


def _expert_ffn(xs_rows, block_e, n_valid, w_glu, w_lin, b_glu, b_lin, w_down, b_down):
    n_rows = xs_rows.shape[0]
    bm = EXPERT_BLOCK
    wspec = lambda r, c: pl.BlockSpec((1, r, c), lambda b, be, nv: (be[b], 0, 0))
    grid_spec = pltpu.PrefetchScalarGridSpec(
        num_scalar_prefetch=2,
        grid=(n_rows // bm,),
        in_specs=[
            pl.BlockSpec((bm, D_MODEL // 2), lambda b, be, nv: (b, 0)),
            wspec(D_MODEL, D_FF), wspec(D_MODEL, D_FF), wspec(1, D_FF), wspec(1, D_FF),
            wspec(D_FF, D_MODEL), wspec(1, D_MODEL),
        ],
        out_specs=pl.BlockSpec((bm, D_MODEL // 2), lambda b, be, nv: (b, 0)),
    )
    return pl.pallas_call(
        _expert_kernel,
        grid_spec=grid_spec,
        out_shape=jax.ShapeDtypeStruct((n_rows, D_MODEL // 2), U32),
        compiler_params=_cparams("arbitrary"),
        name="expert_ffn",
    )(block_e, n_valid, xs_rows, w_glu, w_lin, b_glu, b_lin, w_down, b_down)


def _final_kernel(x1_ref, yg_ref, gt_ref, g_ref, b_ref, *rest):
    o_ref = rest[-1]
    gt = gt_ref[...]
    ffn = jnp.zeros(x1_ref.shape, F32)
    for kk in range(TOP_K):
        ffn = ffn + gt[:, kk:kk + 1] * _unpack_bf16_pairs(yg_ref[kk])
    o_ref[...] = _layer_norm(DEEPNORM_ALPHA * x1_ref[...] + ffn, g_ref[...], b_ref[...])


def _final(x1, yg, gates_t, ln_g, ln_b, src_row0, n_rows, out_rows, dst_row0, out_prev=None):
    tm = TOKEN_TILE
    src, dst = src_row0 // tm, dst_row0 // tm
    in_specs = [
        pl.BlockSpec((tm, D_MODEL), lambda i: (src + i, 0)),
        pl.BlockSpec((TOP_K, tm, D_MODEL // 2), lambda i: (0, src + i, 0)),
        pl.BlockSpec((tm, TOP_K), lambda i: (src + i, 0)),
        pl.BlockSpec((1, D_MODEL), lambda i: (0, 0)),
        pl.BlockSpec((1, D_MODEL), lambda i: (0, 0)),
    ]
    args = [x1, yg, gates_t, ln_g.reshape(1, -1), ln_b.reshape(1, -1)]
    aliases = {}
    if out_prev is not None:
        in_specs.append(pl.BlockSpec(memory_space=pl.ANY))
        args.append(out_prev)
        aliases = {len(args) - 1: 0}
    return pl.pallas_call(
        _final_kernel,
        grid=(n_rows // tm,),
        in_specs=in_specs,
        out_specs=pl.BlockSpec((tm, D_MODEL), lambda i: (dst + i, 0)),
        out_shape=jax.ShapeDtypeStruct((out_rows, D_MODEL), F32),
        input_output_aliases=aliases,
        compiler_params=_cparams("parallel"),
        name="combine_ln2",
    )(*args)


def _routing_tables(idx, rank, counts, n_blocks):
    bm = EXPERT_BLOCK
    padded = (counts + bm - 1) // bm * bm
    pad_end = jnp.cumsum(padded)
    pad_start = pad_end - padded
    experts = jnp.arange(N_EXPERTS, dtype=I32)

    def lookup(table, e):
        shape = (N_EXPERTS,) + (1,) * e.ndim
        return jnp.sum(jnp.where(e[None] == experts.reshape(shape), table.reshape(shape), 0), axis=0)

    pos = lookup(pad_start, idx) + rank
    blk_row0 = jnp.arange(n_blocks, dtype=I32) * bm
    block_e = jnp.minimum(jnp.sum((pad_end[:, None] <= blk_row0[None, :]).astype(I32), axis=0), N_EXPERTS - 1)
    n_valid = jnp.clip(lookup(counts, block_e) - (blk_row0 - lookup(pad_start, block_e)), 0, bm)
    return pos.astype(I32), block_e.astype(I32), n_valid.astype(I32)


def _moe(pk, idx, rank, counts, w_glu, w_lin, b_glu, b_lin, w_down, b_down):
    n_t = pk.shape[0]
    n_blocks = (n_t * TOP_K) // EXPERT_BLOCK + N_EXPERTS
    pos, block_e, n_valid = _routing_tables(idx, rank, counts, n_blocks)
    pos_chunks = pos.reshape(TOP_K, n_t // SC_CHUNK, SC_CHUNK).transpose(1, 0, 2)
    xs_rows = _sc_dispatch(pk, pos_chunks, n_blocks * EXPERT_BLOCK)
    y_rows = _expert_ffn(xs_rows, block_e, n_valid, w_glu, w_lin, b_glu, b_lin, w_down, b_down)
    return _sc_combine(y_rows, pos_chunks, n_t)


def _rope_tables(seq_max):
    inv = ROPE_THETA ** (-jnp.arange(0, HEAD_DIM, 2, dtype=F32) / HEAD_DIM)
    ang = jnp.arange(seq_max).astype(F32)[:, None] * inv[None, :]
    cos, sin = jnp.cos(ang), jnp.sin(ang)
    reps = LANES // HEAD_DIM
    cos_t = jnp.tile(jnp.concatenate([cos, cos], axis=1), (1, reps))
    sin_t = jnp.tile(jnp.concatenate([-sin, sin], axis=1), (1, reps))
    return cos_t, sin_t


def _encoder_layer(xp, xs, seq_prompt, batch_sample, seq_sample, w_in, w_attn_out, w_four_out, w_o, ln1_g, ln1_b,
                   w_router, b_router, w_up, b_up, w_down, b_down, ln2_g, ln2_b):
    n_p, n_s = xp.shape[0], xs.shape[0]
    assert n_p == seq_prompt and n_s == batch_sample * seq_sample
    assert seq_prompt % (DFT_STAGE1 * 16) == 0 and seq_sample % TOKEN_TILE == 0 and n_p % seq_sample == 0
    n_t = n_p + n_s
    assert n_t % (max(DILATIONS) * ATTN_TILE) == 0 and seq_sample % (max(DILATIONS) * ATTN_SUB) == 0

    w_proj = w_in[:, :PROJ_WIDTH].astype(BF16)
    w_gate = w_in[:, PROJ_WIDTH:].astype(BF16)
    cos_t, sin_t = _rope_tables(max(seq_prompt, seq_sample))
    q_views, k_views, v_views, f = _inproj(xp, xs, w_proj, cos_t, sin_t, seq_sample)

    o_list, l_list = [], []
    for gi, dil in enumerate(DILATIONS):
        o, lse = _attention_group(q_views[gi], k_views[gi], v_views[gi], dil, n_p, seq_sample)
        o_list.append(o)
        l_list.append(lse)

    four = jnp.concatenate([_fourier_prompt(f, seq_prompt), _fourier_sample(f, n_p, batch_sample, seq_sample)], axis=0)

    w_glu, w_lin = _split_up_weights(w_up)
    b_glu = b_up[:, None, 0::2]
    b_lin = b_up[:, None, 1::2]
    w_dn = w_down.astype(BF16)
    w_ao, w_fo, w_ob = w_attn_out.astype(BF16), w_four_out.astype(BF16), w_o.astype(BF16)

    tiles = n_t // TOKEN_TILE
    bounds = [tiles * p // MOE_PARTS for p in range(MOE_PARTS + 1)]
    routed = []
    for p in range(MOE_PARTS):
        routed.append(_merge_route(xp, xs, w_gate, o_list, l_list, four, w_ao, w_fo, w_ob, ln1_g, ln1_b,
                                   w_router, b_router, bounds[p], bounds[p + 1] - bounds[p]))
    combined = []
    for x1, pk, idx, gates, rank, cnt in routed:
        combined.append(_moe(pk, idx, rank, cnt[:, 0], w_glu, w_lin, b_glu, b_lin, w_dn, b_down[:, None, :]))

    outs = [None, None]
    spans = [(0, n_p), (n_p, n_t)]
    for p, ((x1, pk, idx, gates, rank, cnt), yg) in enumerate(zip(routed, combined)):
        lo, hi = bounds[p] * TOKEN_TILE, bounds[p + 1] * TOKEN_TILE
        gates_t = gates.T
        for which, (a, b) in enumerate(spans):
            s, e = max(lo, a), min(hi, b)
            if s < e:
                outs[which] = _final(x1, yg, gates_t, ln2_g, ln2_b, s - lo, e - s, b - a, s - a, outs[which])
    return outs[0], outs[1]


def kernel(x_prompt, x_sample, w_in, w_attn_out, w_four_out, w_o, ln1_g, ln1_b, w_router, b_router, w_up, b_up,
           w_down, b_down, ln2_g, ln2_b):
    assert w_in.shape[0] == DEPTH
    bp, sp, d = x_prompt.shape
    bs, ss, _ = x_sample.shape
    assert bp == 1 and d == D_MODEL
    y_p, y_s = _encoder_layer(
        x_prompt.reshape(sp, d), x_sample.reshape(bs * ss, d), sp, bs, ss,
        w_in[0], w_attn_out[0], w_four_out[0], w_o[0], ln1_g[0], ln1_b[0], w_router[0], b_router[0],
        w_up[0], b_up[0], w_down[0], b_down[0], ln2_g[0], ln2_b[0])
    return y_p.reshape(x_prompt.shape), y_s.reshape(x_sample.shape)
```

```python
import functools
import math

import jax
import jax.numpy as jnp
from jax import lax
from jax.experimental import pallas as pl
from jax.experimental.pallas import tpu as pltpu
from jax.experimental.pallas import tpu_sc as plsc

F32 = jnp.float32
BF16 = jnp.bfloat16
I32 = jnp.int32
U32 = jnp.uint32

D_MODEL = 1024
HEAD_DIM = 64
HEADS_PER_GROUP = 4
GROUP_WIDTH = HEADS_PER_GROUP * HEAD_DIM
DILATIONS = (1, 4, 16)
HALF_WINDOW = 64
N_GROUPS = len(DILATIONS)
QKV_WIDTH = N_GROUPS * GROUP_WIDTH
F_GROUPS = 4
F_GROUP_DIM = 64
F_WIDTH = F_GROUPS * F_GROUP_DIM
PROJ_WIDTH = 3 * QKV_WIDTH + F_WIDTH
N_EXPERTS = 32
TOP_K = 4
D_FF = 1024
SWIGLU_LIMIT = 7.0
SWIGLU_ALPHA = 1.702
LN_EPS = 1e-5
ROPE_THETA = 10000.0
NEG_INF = -1e30
DEPTH = 1
DEEPNORM_ALPHA = (2 * DEPTH) ** 0.25

LANES = 128
TOKEN_TILE = 512
ATTN_TILE = 1024
ATTN_SUB = 128
EXPERT_BLOCK = 512
FF_CHUNK = 1024
SPLIT_CHUNK = 256
DFT_STAGE1 = 128
DFT_SMALL_STAGE1 = 8
SC_CHUNK = 64
MOE_PARTS = 2
VMEM_LIMIT = 56 * 1024 * 1024


def _cparams(*sem):
    return pltpu.CompilerParams(dimension_semantics=sem, vmem_limit_bytes=VMEM_LIMIT)


def _to_dilated_view(stage_ref, out_ref, val, dil):
    if dil == 1:
        out_ref[...] = val.astype(BF16)
        return
    rows = val.shape[0] // dil
    for c in range(GROUP_WIDTH // LANES):
        stage_ref[c] = val[:, c * LANES:(c + 1) * LANES]
    for r in range(dil):
        for c in range(GROUP_WIDTH // LANES):
            lo = r * GROUP_WIDTH + c * LANES
            out_ref[:, lo:lo + LANES] = stage_ref[c, pl.ds(r, rows, stride=dil), :].astype(BF16)


def _inproj_kernel(xp_ref, xs_ref, w_ref, cos_ref, sin_ref, *refs, n_prompt_tiles):
    qkv_refs = refs[:3 * N_GROUPS]
    f_ref = refs[3 * N_GROUPS]
    stage_refs = refs[3 * N_GROUPS + 1:]
    i = pl.program_id(0)
    x = jnp.where(i < n_prompt_tiles, xp_ref[...], xs_ref[...]).astype(BF16)
    proj = jnp.dot(x, w_ref[...], preferred_element_type=F32)
    cos = cos_ref[...]
    sin = sin_ref[...]
    lane = lax.broadcasted_iota(I32, cos.shape, 1)
    first_half = (lane % HEAD_DIM) < (HEAD_DIM // 2)

    def rope(t):
        parts = []
        for c in range(GROUP_WIDTH // LANES):
            tc = t[:, c * LANES:(c + 1) * LANES]
            swapped = jnp.where(first_half, pltpu.roll(tc, LANES - HEAD_DIM // 2, 1), pltpu.roll(tc, HEAD_DIM // 2, 1))
            parts.append(tc * cos + swapped * sin)
        return jnp.concatenate(parts, axis=1)

    for gi, dil in enumerate(DILATIONS):
        sl = slice(gi * GROUP_WIDTH, (gi + 1) * GROUP_WIDTH)
        qg = rope(proj[:, sl]) * (HEAD_DIM ** -0.5)
        kg = rope(proj[:, QKV_WIDTH + gi * GROUP_WIDTH:QKV_WIDTH + (gi + 1) * GROUP_WIDTH])
        vg = proj[:, 2 * QKV_WIDTH + gi * GROUP_WIDTH:2 * QKV_WIDTH + (gi + 1) * GROUP_WIDTH]
        for which, val in enumerate((qg, kg, vg)):
            _to_dilated_view(stage_refs[which * N_GROUPS + gi], qkv_refs[which * N_GROUPS + gi], val, dil)
    f_ref[...] = proj[:, 3 * QKV_WIDTH:PROJ_WIDTH].astype(BF16)


def _inproj(xp, xs, w_proj, cos_t, sin_t, seq_sample):
    n_p, n_s = xp.shape[0], xs.shape[0]
    n_t = n_p + n_s
    tm = TOKEN_TILE
    npt = n_p // tm
    tiles_per_sample_seq = seq_sample // tm

    def table_idx(i):
        return (jnp.where(i < npt, i, (i - npt) % tiles_per_sample_seq), 0)

    view_specs = [pl.BlockSpec((tm // d, d * GROUP_WIDTH), lambda i: (i, 0)) for d in DILATIONS] * 3
    view_shapes = [jax.ShapeDtypeStruct((n_t // d, d * GROUP_WIDTH), BF16) for d in DILATIONS] * 3
    outs = pl.pallas_call(
        functools.partial(_inproj_kernel, n_prompt_tiles=npt),
        grid=(n_t // tm,),
        in_specs=[
            pl.BlockSpec((tm, D_MODEL), lambda i: (jnp.minimum(i, npt - 1), 0)),
            pl.BlockSpec((tm, D_MODEL), lambda i: (jnp.maximum(i - npt, 0), 0)),
            pl.BlockSpec((D_MODEL, PROJ_WIDTH), lambda i: (0, 0)),
            pl.BlockSpec((tm, LANES), table_idx),
            pl.BlockSpec((tm, LANES), table_idx),
        ],
        out_specs=view_specs + [pl.BlockSpec((tm, F_WIDTH), lambda i: (i, 0))],
        out_shape=view_shapes + [jax.ShapeDtypeStruct((n_t, F_WIDTH), BF16)],
        scratch_shapes=[pltpu.VMEM((GROUP_WIDTH // LANES, tm, LANES), F32)] * (3 * N_GROUPS),
        compiler_params=_cparams("parallel"),
        name="inproj_rope",
    )(xp, xs, w_proj, cos_t, sin_t)
    q_views, k_views, v_views = outs[0:N_GROUPS], outs[N_GROUPS:2 * N_GROUPS], outs[2 * N_GROUPS:3 * N_GROUPS]
    return q_views, k_views, v_views, outs[3 * N_GROUPS]


def _attn_kernel(q_ref, kl_ref, km_ref, kr_ref, vl_ref, vm_ref, vr_ref, o_ref, lse_ref, *, prompt_rows, sample_rows):
    i = pl.program_id(1)
    kext = jnp.concatenate([kl_ref[...], km_ref[...], kr_ref[...]], axis=0)
    vext = jnp.concatenate([vl_ref[...], vm_ref[...], vr_ref[...]], axis=0)
    kw = ATTN_SUB + 2 * HALF_WINDOW
    a = lax.broadcasted_iota(I32, (ATTN_SUB, kw), 0)
    c = lax.broadcasted_iota(I32, (ATTN_SUB, kw), 1)
    in_band = jnp.abs(c - HALF_WINDOW - a) <= HALF_WINDOW
    head_of_lane = lax.broadcasted_iota(I32, (1, GROUP_WIDTH), 1) // HEAD_DIM
    for j in range(ATTN_TILE // ATTN_SUB):
        r0 = i * ATTN_TILE + j * ATTN_SUB
        in_prompt = r0 < prompt_rows
        b = jnp.maximum(r0 - prompt_rows, 0) // sample_rows
        lo = jnp.where(in_prompt, 0, prompt_rows + b * sample_rows)
        hi = jnp.where(in_prompt, prompt_rows, prompt_rows + (b + 1) * sample_rows)
        key_row = r0 - HALF_WINDOW + c
        valid = in_band & (key_row >= lo) & (key_row < hi)
        qj = q_ref[j * ATTN_SUB:(j + 1) * ATTN_SUB, :]
        kj = kext[j * ATTN_SUB:j * ATTN_SUB + kw, :]
        vj = vext[j * ATTN_SUB:j * ATTN_SUB + kw, :]
        q4 = jnp.concatenate([jnp.where(head_of_lane == h, qj, jnp.zeros_like(qj)) for h in range(HEADS_PER_GROUP)],
                             axis=0)
        s4 = lax.dot_general(q4, kj, (((1,), (1,)), ((), ())), preferred_element_type=F32)
        ps, scales, lses = [], [], []
        for h in range(HEADS_PER_GROUP):
            s = jnp.where(valid, s4[h * ATTN_SUB:(h + 1) * ATTN_SUB], NEG_INF)
            m = jnp.max(s, axis=1, keepdims=True)
            p = jnp.exp(s - m)
            l = jnp.sum(p, axis=1, keepdims=True)
            ps.append(p.astype(BF16))
            scales.append(1.0 / l)
            lses.append(m + jnp.log(l))
        o4 = jnp.dot(jnp.concatenate(ps, axis=0), vj, preferred_element_type=F32)
        acc = jnp.zeros((ATTN_SUB, GROUP_WIDTH), F32)
        lse_full = jnp.zeros((ATTN_SUB, GROUP_WIDTH), F32)
        for h in range(HEADS_PER_GROUP):
            mine = head_of_lane == h
            acc = jnp.where(mine, o4[h * ATTN_SUB:(h + 1) * ATTN_SUB] * scales[h], acc)
            lse_full = jnp.where(mine, lses[h], lse_full)
        o_ref[j * ATTN_SUB:(j + 1) * ATTN_SUB, :] = acc.astype(BF16)
        lse_ref[j * ATTN_SUB:(j + 1) * ATTN_SUB, :] = lse_full


def _attention_group(qv, kv, vv, dil, n_prompt, seq_sample):
    rows = qv.shape[0]
    tq = ATTN_TILE
    halo_per_tile = tq // HALF_WINDOW
    n_halo_blocks = rows // HALF_WINDOW
    main = pl.BlockSpec((tq, GROUP_WIDTH), lambda r, i: (i, r))
    left = pl.BlockSpec((HALF_WINDOW, GROUP_WIDTH), lambda r, i: (jnp.maximum(i * halo_per_tile - 1, 0), r))
    right = pl.BlockSpec((HALF_WINDOW, GROUP_WIDTH),
                         lambda r, i: (jnp.minimum((i + 1) * halo_per_tile, n_halo_blocks - 1), r))
    return pl.pallas_call(
        functools.partial(_attn_kernel, prompt_rows=n_prompt // dil, sample_rows=seq_sample // dil),
        grid=(dil, rows // tq),
        in_specs=[main, left, main, right, left, main, right],
        out_specs=[main, main],
        out_shape=[jax.ShapeDtypeStruct((rows, dil * GROUP_WIDTH), BF16),
                   jax.ShapeDtypeStruct((rows, dil * GROUP_WIDTH), F32)],
        compiler_params=_cparams("parallel", "parallel"),
        name=f"dilated_attention_{dil}",
    )(qv, kv, kv, kv, vv, vv, vv)


def _angle_table(idx, period):
    return (idx % period).astype(F32) * (2.0 * math.pi / period)


def _channel_dft(seq_len):
    c = jnp.arange(F_GROUP_DIM, dtype=I32)
    th = _angle_table(c[:, None] * c[None, :], F_GROUP_DIM)
    scale = (seq_len * F_GROUP_DIM) ** -0.5
    eye = jnp.eye(F_GROUPS, dtype=F32)
    cb = jnp.kron(eye, jnp.cos(th)) * scale
    sb = jnp.kron(eye, jnp.sin(th)) * scale
    return jnp.concatenate([cb, sb], axis=0).astype(BF16)


def _dft_stage1_kernel(m_ref, x_ref, a_ref):
    a_ref[...] = jnp.dot(m_ref[...], x_ref[...], preferred_element_type=F32).astype(BF16)


def _dft_stage2_kernel(g_ref, a_ref, cs_ref, o_ref):
    s2 = a_ref.shape[2]
    a = jnp.concatenate([a_ref[0, 0], a_ref[1, 0]], axis=0)
    x = jnp.dot(g_ref[0], a, preferred_element_type=F32)
    xr = x[:s2].astype(BF16)
    xi = x[s2:].astype(BF16)
    o_ref[...] = (jnp.dot(xr, cs_ref[:F_WIDTH, :], preferred_element_type=F32)
                  + jnp.dot(xi, cs_ref[F_WIDTH:, :], preferred_element_type=F32)).astype(BF16)


def _fourier_prompt(f, seq):
    s1 = DFT_STAGE1
    s2 = seq // s1
    n1 = jnp.arange(s1, dtype=I32)
    th1 = _angle_table(n1[:, None] * n1[None, :], s1)
    m1 = jnp.concatenate([jnp.cos(th1), -jnp.sin(th1)], axis=0).astype(BF16)
    k1 = jnp.arange(s1, dtype=I32)[:, None, None]
    k2 = jnp.arange(s2, dtype=I32)[None, :, None]
    n2 = jnp.arange(s2, dtype=I32)[None, None, :]
    th = _angle_table((k1 + s1 * k2) * n2, seq)
    gr, gi = jnp.cos(th), -jnp.sin(th)
    g = jnp.concatenate([jnp.concatenate([gr, -gi], axis=2), jnp.concatenate([gi, gr], axis=2)], axis=1).astype(BF16)
    cs = _channel_dft(seq)

    cols = s2 * F_WIDTH
    fv = f.reshape(f.shape[0] // s2, cols)
    tn = min(cols, 4096)
    a = pl.pallas_call(
        _dft_stage1_kernel,
        grid=(cols // tn,),
        in_specs=[pl.BlockSpec((2 * s1, s1), lambda j: (0, 0)), pl.BlockSpec((s1, tn), lambda j: (0, j))],
        out_specs=pl.BlockSpec((2 * s1, tn), lambda j: (0, j)),
        out_shape=jax.ShapeDtypeStruct((2 * s1, cols), BF16),
        compiler_params=_cparams("parallel"),
        name="dft_stage1",
    )(m1, fv)
    a4 = a.reshape(2, s1, s2, F_WIDTH)
    out = pl.pallas_call(
        _dft_stage2_kernel,
        grid=(s1,),
        in_specs=[pl.BlockSpec((1, 2 * s2, 2 * s2), lambda k: (k, 0, 0)),
                  pl.BlockSpec((2, 1, s2, F_WIDTH), lambda k: (0, k, 0, 0)),
                  pl.BlockSpec((2 * F_WIDTH, F_WIDTH), lambda k: (0, 0))],
        out_specs=pl.BlockSpec((s2, F_WIDTH), lambda k: (0, k)),
        out_shape=jax.ShapeDtypeStruct((s2, s1 * F_WIDTH), BF16),
        compiler_params=_cparams("parallel"),
        name="dft_stage2",
    )(g, a4, cs)
    return out.reshape(seq, F_WIDTH)


def _dft_small_kernel(g_ref, x_ref, cs_ref, o_ref, stage_ref):
    s1 = DFT_SMALL_STAGE1
    s2 = x_ref.shape[1] // s1
    root = math.sqrt(0.5)
    cos8 = (1.0, root, 0.0, -root, -1.0, -root, 0.0, root)
    sin8 = (0.0, root, 1.0, root, 0.0, -root, -1.0, -root)
    xs = [x_ref[0, n1 * s2:(n1 + 1) * s2, :].astype(F32) for n1 in range(s1)]

    def weighted_sum(coefs):
        acc = None
        for coef, xv in zip(coefs, xs):
            if coef == 0.0:
                continue
            term = xv if abs(coef) == 1.0 else xv * abs(coef)
            if acc is None:
                acc = term if coef > 0 else -term
            else:
                acc = acc + term if coef > 0 else acc - term
        return acc if acc is not None else jnp.zeros_like(xs[0])

    for k1 in range(s1):
        ar = weighted_sum([cos8[(n1 * k1) % s1] for n1 in range(s1)])
        ai = weighted_sum([-sin8[(n1 * k1) % s1] for n1 in range(s1)])
        a = jnp.concatenate([ar.astype(BF16), ai.astype(BF16)], axis=0)
        x = jnp.dot(g_ref[k1], a, preferred_element_type=F32)
        out = (jnp.dot(x[:s2].astype(BF16), cs_ref[:F_WIDTH, :], preferred_element_type=F32)
               + jnp.dot(x[s2:].astype(BF16), cs_ref[F_WIDTH:, :], preferred_element_type=F32))
        for c in range(F_WIDTH // LANES):
            stage_ref[c, pl.ds(k1, s2, stride=s1), :] = out[:, c * LANES:(c + 1) * LANES]
    o_ref[0] = jnp.concatenate([stage_ref[c] for c in range(F_WIDTH // LANES)], axis=1).astype(BF16)


def _fourier_sample(f, n_prompt, batch, seq):
    s1 = DFT_SMALL_STAGE1
    s2 = seq // s1
    k1 = jnp.arange(s1, dtype=I32)[:, None, None]
    k2 = jnp.arange(s2, dtype=I32)[None, :, None]
    n2 = jnp.arange(s2, dtype=I32)[None, None, :]
    th = _angle_table((k1 + s1 * k2) * n2, seq)
    gr, gi = jnp.cos(th), -jnp.sin(th)
    g = jnp.concatenate([jnp.concatenate([gr, -gi], axis=2), jnp.concatenate([gi, gr], axis=2)], axis=1).astype(BF16)
    cs = _channel_dft(seq)
    f3 = f.reshape(f.shape[0] // seq, seq, F_WIDTH)
    first = n_prompt // seq
    return pl.pallas_call(
        _dft_small_kernel,
        grid=(batch,),
        in_specs=[pl.BlockSpec((s1, 2 * s2, 2 * s2), lambda b: (0, 0, 0)),
                  pl.BlockSpec((1, seq, F_WIDTH), lambda b: (first + b, 0, 0)),
                  pl.BlockSpec((2 * F_WIDTH, F_WIDTH), lambda b: (0, 0))],
        out_specs=pl.BlockSpec((1, seq, F_WIDTH), lambda b: (b, 0, 0)),
        out_shape=jax.ShapeDtypeStruct((batch, seq, F_WIDTH), BF16),
        scratch_shapes=[pltpu.VMEM((F_WIDTH // LANES, seq, LANES), F32)],
        compiler_params=_cparams("parallel"),
        name="dft_small",
    )(g, f3, cs).reshape(batch * seq, F_WIDTH)


def _pack_bf16_pairs(x):
    w = x.shape[1] // 2
    bits = lax.bitcast_convert_type(x.astype(BF16).astype(F32), U32)
    return (bits[:, :w] >> 16) | (bits[:, w:] & jnp.uint32(0xFFFF0000))


def _unpack_bf16_pairs(u):
    lo = lax.bitcast_convert_type(u << 16, F32)
    hi = lax.bitcast_convert_type(u & jnp.uint32(0xFFFF0000), F32)
    return jnp.concatenate([lo, hi], axis=1)


def _layer_norm(h, g, b):
    mu = jnp.mean(h, axis=-1, keepdims=True)
    d = h - mu
    var = jnp.mean(d * d, axis=-1, keepdims=True)
    return d * lax.rsqrt(var + LN_EPS) * g + b


def _from_dilated_view(stage_ref, blk_ref, dil):
    if dil == 1:
        return blk_ref[...].astype(F32)
    rows = blk_ref.shape[0]
    for r in range(dil):
        for c in range(GROUP_WIDTH // LANES):
            lo = r * GROUP_WIDTH + c * LANES
            stage_ref[c, pl.ds(r, rows, stride=dil), :] = blk_ref[:, lo:lo + LANES].astype(F32)
    return jnp.concatenate([stage_ref[c] for c in range(GROUP_WIDTH // LANES)], axis=1)


def _merge_kernel(xp_ref, xs_ref, wg_ref, o1_ref, o2_ref, o3_ref, l1_ref, l2_ref, l3_ref, four_ref,
                  wao_ref, wfo_ref, wo_ref, g_ref, b_ref, wr2_ref, br_ref, tri_ref,
                  x1_ref, pk_ref, idx_ref, gate_ref, rank_ref, cnt_ref, run_ref, *stage_refs, n_prompt_tiles):
    i = pl.program_id(0)

    @pl.when(i == 0)
    def _():
        run_ref[...] = jnp.zeros_like(run_ref)

    x = jnp.where(i < n_prompt_tiles, xp_ref[...], xs_ref[...])
    gpre = jnp.dot(x.astype(BF16), wg_ref[...], preferred_element_type=F32)
    o1, o2, o3 = [_from_dilated_view(stage_refs[gi], ref, d)
                  for gi, (ref, d) in enumerate(zip((o1_ref, o2_ref, o3_ref), DILATIONS))]
    l1, l2, l3 = [_from_dilated_view(stage_refs[N_GROUPS + gi], ref, d)
                  for gi, (ref, d) in enumerate(zip((l1_ref, l2_ref, l3_ref), DILATIONS))]
    m = jnp.maximum(jnp.maximum(l1, l2), l3)
    e1, e2, e3 = jnp.exp(l1 - m), jnp.exp(l2 - m), jnp.exp(l3 - m)
    attn = (e1 * o1 + e2 * o2 + e3 * o3) * (1.0 / (e1 + e2 + e3))
    a = jnp.dot(attn.astype(BF16), wao_ref[...], preferred_element_type=F32)
    ff = jnp.dot(four_ref[...], wfo_ref[...], preferred_element_type=F32)
    merged = jax.nn.sigmoid(gpre[:, :D_MODEL]) * a + jax.nn.sigmoid(gpre[:, D_MODEL:]) * ff
    mix = jnp.dot(merged.astype(BF16), wo_ref[...], preferred_element_type=F32)
    x1 = _layer_norm(DEEPNORM_ALPHA * x + mix, g_ref[...], b_ref[...])
    x1_ref[...] = x1
    pk_ref[...] = _pack_bf16_pairs(x1)

    xh = x1.astype(BF16)
    xl = (x1 - xh.astype(F32)).astype(BF16)
    tm = x1.shape[0]
    cross = lax.dot_general(wr2_ref[...], jnp.concatenate([xh, xl], axis=0), (((1,), (1,)), ((), ())),
                            preferred_element_type=F32)
    logits = ((cross[:N_EXPERTS, :tm] + cross[N_EXPERTS:, :tm])
              + (cross[:N_EXPERTS, tm:] + cross[N_EXPERTS:, tm:])) + br_ref[...]
    eio = lax.broadcasted_iota(I32, (N_EXPERTS, tm), 0)
    work = logits
    vals, sels, picks = [], [], []
    for _ in range(TOP_K):
        mk = jnp.max(work, axis=0, keepdims=True)
        ik = jnp.min(jnp.where(work == mk, eio, N_EXPERTS), axis=0, keepdims=True)
        sel = eio == ik
        vals.append(mk)
        sels.append(sel)
        picks.append(ik)
        work = jnp.where(sel, -jnp.inf, work)
    es = [jnp.exp(v - vals[0]) for v in vals]
    inv_den = 1.0 / (es[0] + es[1] + es[2] + es[3])
    chosen = (sels[0] | sels[1] | sels[2] | sels[3])
    chosen_f = chosen.astype(F32)
    prefix = jnp.dot(chosen_f.astype(BF16), tri_ref[...], preferred_element_type=F32)
    before = prefix + run_ref[...]
    for kk in range(TOP_K):
        idx_ref[kk:kk + 1, :] = picks[kk]
        gate_ref[kk:kk + 1, :] = es[kk] * inv_den
        rank_ref[kk:kk + 1, :] = jnp.sum(jnp.where(sels[kk], before, 0.0), axis=0, keepdims=True).astype(I32)
    run_ref[...] = run_ref[...] + jnp.sum(chosen_f, axis=1, keepdims=True)
    cnt_ref[...] = jnp.broadcast_to(run_ref[...], cnt_ref.shape).astype(I32)


def _merge_route(xp, xs, w_gate, o_list, l_list, four, w_ao, w_fo, w_o, ln_g, ln_b, w_r, b_r, tile0, n_tiles):
    n_p = xp.shape[0]
    tm = TOKEN_TILE
    n_t = n_tiles * tm
    npt = n_p // tm
    w_rt = w_r.T
    w_rh = w_rt.astype(BF16)
    w_rl = (w_rt - w_rh.astype(F32)).astype(BF16)
    tri = (jnp.arange(tm)[:, None] < jnp.arange(tm)[None, :]).astype(BF16)
    row = lambda w: pl.BlockSpec((tm, w), lambda i: (i, 0))
    src_row = lambda w: pl.BlockSpec((tm, w), lambda i: (tile0 + i, 0))
    full = lambda r, c: pl.BlockSpec((r, c), lambda i: (0, 0))
    lane_row = pl.BlockSpec((TOP_K, tm), lambda i: (0, i))
    views = [pl.BlockSpec((tm // d, d * GROUP_WIDTH), lambda i: (tile0 + i, 0)) for d in DILATIONS]
    return pl.pallas_call(
        functools.partial(_merge_kernel, n_prompt_tiles=npt - tile0),
        grid=(n_tiles,),
        in_specs=[
            pl.BlockSpec((tm, D_MODEL), lambda i: (jnp.clip(tile0 + i, 0, npt - 1), 0)),
            pl.BlockSpec((tm, D_MODEL), lambda i: (jnp.maximum(tile0 + i - npt, 0), 0)),
            full(D_MODEL, 2 * D_MODEL),
            *views, *views,
            src_row(F_WIDTH),
            full(GROUP_WIDTH, D_MODEL), full(F_WIDTH, D_MODEL), full(D_MODEL, D_MODEL),
            full(1, D_MODEL), full(1, D_MODEL),
            full(2 * N_EXPERTS, D_MODEL), full(N_EXPERTS, 1),
            full(tm, tm),
        ],
        out_specs=[row(D_MODEL), row(D_MODEL // 2), lane_row, lane_row, lane_row, full(N_EXPERTS, LANES)],
        out_shape=[
            jax.ShapeDtypeStruct((n_t, D_MODEL), F32),
            jax.ShapeDtypeStruct((n_t, D_MODEL // 2), U32),
            jax.ShapeDtypeStruct((TOP_K, n_t), I32),
            jax.ShapeDtypeStruct((TOP_K, n_t), F32),
            jax.ShapeDtypeStruct((TOP_K, n_t), I32),
            jax.ShapeDtypeStruct((N_EXPERTS, LANES), I32),
        ],
        scratch_shapes=[pltpu.VMEM((N_EXPERTS, 1), F32)]
        + [pltpu.VMEM((GROUP_WIDTH // LANES, tm, LANES), F32)] * (2 * N_GROUPS),
        compiler_params=_cparams("arbitrary"),
        name="merge_ln_route",
    )(xp, xs, w_gate, *o_list, *l_list, four, w_ao, w_fo, w_o, ln_g.reshape(1, -1), ln_b.reshape(1, -1),
      jnp.concatenate([w_rh, w_rl], axis=0), b_r.reshape(-1, 1), tri)


def _sc_workers():
    info = plsc.get_sparse_core_info()
    return info.num_cores, info.num_cores * info.num_subcores


def _sc_dispatch(rows_src, pos_chunks, n_rows_out):
    n_t, w = rows_src.shape
    n_cores, n_workers = _sc_workers()
    chunks_per_worker = n_t // (n_workers * SC_CHUNK)
    mesh = plsc.VectorSubcoreMesh(core_axis_name="c", subcore_axis_name="s")

    @functools.partial(
        pl.kernel, mesh=mesh,
        out_type=jax.ShapeDtypeStruct((n_rows_out, w), rows_src.dtype),
        scratch_types=[pltpu.VMEM((TOP_K, SC_CHUNK), I32), pltpu.VMEM((SC_CHUNK, w), rows_src.dtype)],
    )
    def dispatch(src_hbm, pos_hbm, out_hbm, idx_v, rows_v):
        wid = lax.axis_index("s") * n_cores + lax.axis_index("c")

        @pl.loop(0, chunks_per_worker)
        def _(j):
            chunk = wid * chunks_per_worker + j
            pltpu.sync_copy(pos_hbm.at[chunk], idx_v)
            pltpu.sync_copy(src_hbm.at[pl.ds(chunk * SC_CHUNK, SC_CHUNK)], rows_v)
            for kk in range(TOP_K):
                pltpu.sync_copy(rows_v, out_hbm.at[idx_v.at[kk]])

    return dispatch(rows_src, pos_chunks)


def _sc_combine(rows_src, pos_chunks, n_t):
    w = rows_src.shape[1]
    n_cores, n_workers = _sc_workers()
    chunks_per_worker = n_t // (n_workers * SC_CHUNK)
    mesh = plsc.VectorSubcoreMesh(core_axis_name="c", subcore_axis_name="s")

    @functools.partial(
        pl.kernel, mesh=mesh,
        out_type=jax.ShapeDtypeStruct((TOP_K, n_t, w), rows_src.dtype),
        scratch_types=[pltpu.VMEM((TOP_K, SC_CHUNK), I32), pltpu.VMEM((SC_CHUNK, w), rows_src.dtype)],
    )
    def combine(src_hbm, pos_hbm, out_hbm, idx_v, rows_v):
        wid = lax.axis_index("s") * n_cores + lax.axis_index("c")

        @pl.loop(0, chunks_per_worker)
        def _(j):
            chunk = wid * chunks_per_worker + j
            pltpu.sync_copy(pos_hbm.at[chunk], idx_v)
            for kk in range(TOP_K):
                pltpu.sync_copy(src_hbm.at[idx_v.at[kk]], rows_v)
                pltpu.sync_copy(rows_v, out_hbm.at[kk, pl.ds(chunk * SC_CHUNK, SC_CHUNK)])

    return combine(rows_src, pos_chunks)


def _split_up_kernel(w_ref, p_ref, wg_ref, wl_ref):
    half = SPLIT_CHUNK // 2
    for c in range(w_ref.shape[2] // SPLIT_CHUNK):
        w = w_ref[0, :, c * SPLIT_CHUNK:(c + 1) * SPLIT_CHUNK].astype(BF16)
        r = jnp.dot(w, p_ref[...], preferred_element_type=F32)
        wg_ref[0, :, c * half:(c + 1) * half] = r[:, :half].astype(BF16)
        wl_ref[0, :, c * half:(c + 1) * half] = r[:, half:].astype(BF16)


def _split_up_weights(w_up):
    n_e, d_in, d_h2 = w_up.shape
    half = SPLIT_CHUNK // 2
    src = jnp.arange(SPLIT_CHUNK, dtype=I32)[:, None]
    dst = jnp.arange(SPLIT_CHUNK, dtype=I32)[None, :]
    sel = jnp.where(dst < half, src == 2 * dst, src == 2 * (dst - half) + 1).astype(BF16)
    out = jax.ShapeDtypeStruct((n_e, d_in, d_h2 // 2), BF16)
    return pl.pallas_call(
        _split_up_kernel,
        grid=(n_e,),
        in_specs=[pl.BlockSpec((1, d_in, d_h2), lambda e: (e, 0, 0)),
                  pl.BlockSpec((SPLIT_CHUNK, SPLIT_CHUNK), lambda e: (0, 0))],
        out_specs=[pl.BlockSpec((1, d_in, d_h2 // 2), lambda e: (e, 0, 0))] * 2,
        out_shape=[out, out],
        compiler_params=_cparams("parallel"),
        name="split_up_weights",
    )(w_up, sel)


def _expert_kernel(be_ref, nv_ref, x_ref, wg_ref, wl_ref, bg_ref, bl_ref, wd_ref, bd_ref, y_ref):
    blk = pl.program_id(0)
    nv = nv_ref[blk]

    @pl.when(nv > 0)
    def _():
        x = _unpack_bf16_pairs(x_ref[...])
        rows = lax.broadcasted_iota(I32, (x.shape[0], 1), 0)
        x = jnp.where(rows < nv, x, 0.0).astype(BF16)
        acc = jnp.zeros((x.shape[0], D_MODEL), F32) + bd_ref[0]
        for c in range(D_FF // FF_CHUNK):
            sl = slice(c * FF_CHUNK, (c + 1) * FF_CHUNK)
            hg = jnp.dot(x, wg_ref[0, :, sl], preferred_element_type=F32) + bg_ref[0, :, sl]
            hl = jnp.dot(x, wl_ref[0, :, sl], preferred_element_type=F32) + bl_ref[0, :, sl]
            glu = jnp.minimum(hg, SWIGLU_LIMIT)
            lin = jnp.clip(hl, -SWIGLU_LIMIT, SWIGLU_LIMIT)
            act = glu * jax.nn.sigmoid(SWIGLU_ALPHA * glu) * (lin + 1.0)
            acc = acc + jnp.dot(act.astype(BF16), wd_ref[0, sl, :], preferred_element_type=F32)
        y_ref[...] = _pack_bf16_pairs(acc)

    @pl.when(nv == 0)
    def _():
        y_ref[...] = jnp.zeros_like(y_ref)


def _expert_ffn(xs_rows, block_e, n_valid, w_glu, w_lin, b_glu, b_lin, w_down, b_down):
    n_rows = xs_rows.shape[0]
    bm = EXPERT_BLOCK
    wspec = lambda r, c: pl.BlockSpec((1, r, c), lambda b, be, nv: (be[b], 0, 0))
    grid_spec = pltpu.PrefetchScalarGridSpec(
        num_scalar_prefetch=2,
        grid=(n_rows // bm,),
        in_specs=[
            pl.BlockSpec((bm, D_MODEL // 2), lambda b, be, nv: (b, 0)),
            wspec(D_MODEL, D_FF), wspec(D_MODEL, D_FF), wspec(1, D_FF), wspec(1, D_FF),
            wspec(D_FF, D_MODEL), wspec(1, D_MODEL),
        ],
        out_specs=pl.BlockSpec((bm, D_MODEL // 2), lambda b, be, nv: (b, 0)),
    )
    return pl.pallas_call(
        _expert_kernel,
        grid_spec=grid_spec,
        out_shape=jax.ShapeDtypeStruct((n_rows, D_MODEL // 2), U32),
        compiler_params=_cparams("arbitrary"),
        name="expert_ffn",
    )(block_e, n_valid, xs_rows, w_glu, w_lin, b_glu, b_lin, w_down, b_down)


def _final_kernel(x1_ref, yg_ref, gt_ref, g_ref, b_ref, *rest):
    o_ref = rest[-1]
    gt = gt_ref[...]
    ffn = jnp.zeros(x1_ref.shape, F32)
    for kk in range(TOP_K):
        ffn = ffn + gt[:, kk:kk + 1] * _unpack_bf16_pairs(yg_ref[kk])
    o_ref[...] = _layer_norm(DEEPNORM_ALPHA * x1_ref[...] + ffn, g_ref[...], b_ref[...])


def _final(x1, yg, gates_t, ln_g, ln_b, src_row0, n_rows, out_rows, dst_row0, out_prev=None):
    tm = TOKEN_TILE
    src, dst = src_row0 // tm, dst_row0 // tm
    in_specs = [
        pl.BlockSpec((tm, D_MODEL), lambda i: (src + i, 0)),
        pl.BlockSpec((TOP_K, tm, D_MODEL // 2), lambda i: (0, src + i, 0)),
        pl.BlockSpec((tm, TOP_K), lambda i: (src + i, 0)),
        pl.BlockSpec((1, D_MODEL), lambda i: (0, 0)),
        pl.BlockSpec((1, D_MODEL), lambda i: (0, 0)),
    ]
    args = [x1, yg, gates_t, ln_g.reshape(1, -1), ln_b.reshape(1, -1)]
    aliases = {}
    if out_prev is not None:
        in_specs.append(pl.BlockSpec(memory_space=pl.ANY))
        args.append(out_prev)
        aliases = {len(args) - 1: 0}
    return pl.pallas_call(
        _final_kernel,
        grid=(n_rows // tm,),
        in_specs=in_specs,
        out_specs=pl.BlockSpec((tm, D_MODEL), lambda i: (dst + i, 0)),
        out_shape=jax.ShapeDtypeStruct((out_rows, D_MODEL), F32),
        input_output_aliases=aliases,
        compiler_params=_cparams("parallel"),
        name="combine_ln2",
    )(*args)


def _routing_tables(idx, rank, counts, n_blocks):
    bm = EXPERT_BLOCK
    padded = (counts + bm - 1) // bm * bm
    pad_end = jnp.cumsum(padded)
    pad_start = pad_end - padded
    experts = jnp.arange(N_EXPERTS, dtype=I32)

    def lookup(table, e):
        shape = (N_EXPERTS,) + (1,) * e.ndim
        return jnp.sum(jnp.where(e[None] == experts.reshape(shape), table.reshape(shape), 0), axis=0)

    pos = lookup(pad_start, idx) + rank
    blk_row0 = jnp.arange(n_blocks, dtype=I32) * bm
    block_e = jnp.minimum(jnp.sum((pad_end[:, None] <= blk_row0[None, :]).astype(I32), axis=0), N_EXPERTS - 1)
    n_valid = jnp.clip(lookup(counts, block_e) - (blk_row0 - lookup(pad_start, block_e)), 0, bm)
    return pos.astype(I32), block_e.astype(I32), n_valid.astype(I32)


def _moe(pk, idx, rank, counts, w_glu, w_lin, b_glu, b_lin, w_down, b_down):
    n_t = pk.shape[0]
    n_blocks = (n_t * TOP_K) // EXPERT_BLOCK + N_EXPERTS
    pos, block_e, n_valid = _routing_tables(idx, rank, counts, n_blocks)
    pos_chunks = pos.reshape(TOP_K, n_t // SC_CHUNK, SC_CHUNK).transpose(1, 0, 2)
    xs_rows = _sc_dispatch(pk, pos_chunks, n_blocks * EXPERT_BLOCK)
    y_rows = _expert_ffn(xs_rows, block_e, n_valid, w_glu, w_lin, b_glu, b_lin, w_down, b_down)
    return _sc_combine(y_rows, pos_chunks, n_t)


def _rope_tables(seq_max):
    inv = ROPE_THETA ** (-jnp.arange(0, HEAD_DIM, 2, dtype=F32) / HEAD_DIM)
    ang = jnp.arange(seq_max).astype(F32)[:, None] * inv[None, :]
    cos, sin = jnp.cos(ang), jnp.sin(ang)
    reps = LANES // HEAD_DIM
    cos_t = jnp.tile(jnp.concatenate([cos, cos], axis=1), (1, reps))
    sin_t = jnp.tile(jnp.concatenate([-sin, sin], axis=1), (1, reps))
    return cos_t, sin_t


def _encoder_layer(xp, xs, seq_prompt, batch_sample, seq_sample, w_in, w_attn_out, w_four_out, w_o, ln1_g, ln1_b,
                   w_router, b_router, w_up, b_up, w_down, b_down, ln2_g, ln2_b):
    n_p, n_s = xp.shape[0], xs.shape[0]
    assert n_p == seq_prompt and n_s == batch_sample * seq_sample
    assert seq_prompt % (DFT_STAGE1 * 16) == 0 and seq_sample % TOKEN_TILE == 0 and n_p % seq_sample == 0
    n_t = n_p + n_s
    assert n_t % (max(DILATIONS) * ATTN_TILE) == 0 and seq_sample % (max(DILATIONS) * ATTN_SUB) == 0

    w_proj = w_in[:, :PROJ_WIDTH].astype(BF16)
    w_gate = w_in[:, PROJ_WIDTH:].astype(BF16)
    cos_t, sin_t = _rope_tables(max(seq_prompt, seq_sample))
    q_views, k_views, v_views, f = _inproj(xp, xs, w_proj, cos_t, sin_t, seq_sample)

    o_list, l_list = [], []
    for gi, dil in enumerate(DILATIONS):
        o, lse = _attention_group(q_views[gi], k_views[gi], v_views[gi], dil, n_p, seq_sample)
        o_list.append(o)
        l_list.append(lse)

    four = jnp.concatenate([_fourier_prompt(f, seq_prompt), _fourier_sample(f, n_p, batch_sample, seq_sample)], axis=0)

    w_glu, w_lin = _split_up_weights(w_up)
    b_glu = b_up[:, None, 0::2]
    b_lin = b_up[:, None, 1::2]
    w_dn = w_down.astype(BF16)
    w_ao, w_fo, w_ob = w_attn_out.astype(BF16), w_four_out.astype(BF16), w_o.astype(BF16)

    tiles = n_t // TOKEN_TILE
    bounds = [tiles * p // MOE_PARTS for p in range(MOE_PARTS + 1)]
    routed = []
    for p in range(MOE_PARTS):
        routed.append(_merge_route(xp, xs, w_gate, o_list, l_list, four, w_ao, w_fo, w_ob, ln1_g, ln1_b,
                                   w_router, b_router, bounds[p], bounds[p + 1] - bounds[p]))
    combined = []
    for x1, pk, idx, gates, rank, cnt in routed:
        combined.append(_moe(pk, idx, rank, cnt[:, 0], w_glu, w_lin, b_glu, b_lin, w_dn, b_down[:, None, :]))

    outs = [None, None]
    spans = [(0, n_p), (n_p, n_t)]
    for p, ((x1, pk, idx, gates, rank, cnt), yg) in enumerate(zip(routed, combined)):
        lo, hi = bounds[p] * TOKEN_TILE, bounds[p + 1] * TOKEN_TILE
        gates_t = gates.T
        for which, (a, b) in enumerate(spans):
            s, e = max(lo, a), min(hi, b)
            if s < e:
                outs[which] = _final(x1, yg, gates_t, ln2_g, ln2_b, s - lo, e - s, b - a, s - a, outs[which])
    return outs[0], outs[1]


def kernel(x_prompt, x_sample, w_in, w_attn_out, w_four_out, w_o, ln1_g, ln1_b, w_router, b_router, w_up, b_up,
           w_down, b_down, ln2_g, ln2_b):
    assert w_in.shape[0] == DEPTH
    bp, sp, d = x_prompt.shape
    bs, ss, _ = x_sample.shape
    assert bp == 1 and d == D_MODEL
    y_p, y_s = _encoder_layer(
        x_prompt.reshape(sp, d), x_sample.reshape(bs * ss, d), sp, bs, ss,
        w_in[0], w_attn_out[0], w_four_out[0], w_o[0], ln1_g[0], ln1_b[0], w_router[0], b_router[0],
        w_up[0], b_up[0], w_down[0], b_down[0], ln2_g[0], ln2_b[0])
    return y_p.reshape(x_prompt.shape), y_s.reshape(x_sample.shape)
```

```python
import functools
import math

import jax
import jax.numpy as jnp
from jax import lax
from jax.experimental import pallas as pl
from jax.experimental.pallas import tpu as pltpu
from jax.experimental.pallas import tpu_sc as plsc

F32 = jnp.float32
BF16 = jnp.bfloat16
I32 = jnp.int32
U32 = jnp.uint32

D_MODEL = 1024
HEAD_DIM = 64
HEADS_PER_GROUP = 4
GROUP_WIDTH = HEADS_PER_GROUP * HEAD_DIM
DILATIONS = (1, 4, 16)
HALF_WINDOW = 64
N_GROUPS = len(DILATIONS)
QKV_WIDTH = N_GROUPS * GROUP_WIDTH
F_GROUPS = 4
F_GROUP_DIM = 64
F_WIDTH = F_GROUPS * F_GROUP_DIM
PROJ_WIDTH = 3 * QKV_WIDTH + F_WIDTH
N_EXPERTS = 32
TOP_K = 4
D_FF = 1024
SWIGLU_LIMIT = 7.0
SWIGLU_ALPHA = 1.702
LN_EPS = 1e-5
ROPE_THETA = 10000.0
NEG_INF = -1e30
DEPTH = 1
DEEPNORM_ALPHA = (2 * DEPTH) ** 0.25

LANES = 128
TOKEN_TILE = 512
FINAL_TILE = 1024
ATTN_TILE = 1024
ATTN_SUB = 128
EXPERT_BLOCK = 512
FF_CHUNK = 1024
SPLIT_CHUNK = 256
DFT_STAGE1 = 128
DFT_STAGE2_BATCH = 8
DFT_SMALL_STAGE1 = 8
SC_CHUNK = 64
MOE_PARTS = 2
VMEM_LIMIT = 56 * 1024 * 1024


def _cparams(*sem):
    return pltpu.CompilerParams(dimension_semantics=sem, vmem_limit_bytes=VMEM_LIMIT)


def _to_dilated_view(stage_ref, out_ref, val, dil):
    if dil == 1:
        out_ref[...] = val.astype(BF16)
        return
    rows = val.shape[0] // dil
    for c in range(GROUP_WIDTH // LANES):
        stage_ref[c] = val[:, c * LANES:(c + 1) * LANES]
    for r in range(dil):
        for c in range(GROUP_WIDTH // LANES):
            lo = r * GROUP_WIDTH + c * LANES
            out_ref[:, lo:lo + LANES] = stage_ref[c, pl.ds(r, rows, stride=dil), :].astype(BF16)


def _inproj_kernel(xp_ref, xs_ref, w_ref, cos_ref, sin_ref, *refs, n_prompt_tiles):
    qkv_refs = refs[:3 * N_GROUPS]
    f_ref = refs[3 * N_GROUPS]
    stage_refs = refs[3 * N_GROUPS + 1:]
    i = pl.program_id(0)
    x = jnp.where(i < n_prompt_tiles, xp_ref[...], xs_ref[...]).astype(BF16)
    proj = jnp.dot(x, w_ref[...], preferred_element_type=F32)
    cos = cos_ref[...]
    sin = sin_ref[...]
    lane = lax.broadcasted_iota(I32, cos.shape, 1)
    first_half = (lane % HEAD_DIM) < (HEAD_DIM // 2)

    def rope(t):
        parts = []
        for c in range(GROUP_WIDTH // LANES):
            tc = t[:, c * LANES:(c + 1) * LANES]
            swapped = jnp.where(first_half, pltpu.roll(tc, LANES - HEAD_DIM // 2, 1), pltpu.roll(tc, HEAD_DIM // 2, 1))
            parts.append(tc * cos + swapped * sin)
        return jnp.concatenate(parts, axis=1)

    for gi, dil in enumerate(DILATIONS):
        sl = slice(gi * GROUP_WIDTH, (gi + 1) * GROUP_WIDTH)
        qg = rope(proj[:, sl]) * (HEAD_DIM ** -0.5)
        kg = rope(proj[:, QKV_WIDTH + gi * GROUP_WIDTH:QKV_WIDTH + (gi + 1) * GROUP_WIDTH])
        vg = proj[:, 2 * QKV_WIDTH + gi * GROUP_WIDTH:2 * QKV_WIDTH + (gi + 1) * GROUP_WIDTH]
        for which, val in enumerate((qg, kg, vg)):
            _to_dilated_view(stage_refs[which * N_GROUPS + gi], qkv_refs[which * N_GROUPS + gi], val, dil)
    f_ref[...] = proj[:, 3 * QKV_WIDTH:PROJ_WIDTH].astype(BF16)


def _inproj(xp, xs, w_proj, cos_t, sin_t, seq_sample):
    n_p, n_s = xp.shape[0], xs.shape[0]
    n_t = n_p + n_s
    tm = TOKEN_TILE
    npt = n_p // tm
    tiles_per_sample_seq = seq_sample // tm

    def table_idx(i):
        return (jnp.where(i < npt, i, (i - npt) % tiles_per_sample_seq), 0)

    view_specs = [pl.BlockSpec((tm // d, d * GROUP_WIDTH), lambda i: (i, 0)) for d in DILATIONS] * 3
    view_shapes = [jax.ShapeDtypeStruct((n_t // d, d * GROUP_WIDTH), BF16) for d in DILATIONS] * 3
    outs = pl.pallas_call(
        functools.partial(_inproj_kernel, n_prompt_tiles=npt),
        grid=(n_t // tm,),
        in_specs=[
            pl.BlockSpec((tm, D_MODEL), lambda i: (jnp.minimum(i, npt - 1), 0)),
            pl.BlockSpec((tm, D_MODEL), lambda i: (jnp.maximum(i - npt, 0), 0)),
            pl.BlockSpec((D_MODEL, PROJ_WIDTH), lambda i: (0, 0)),
            pl.BlockSpec((tm, LANES), table_idx),
            pl.BlockSpec((tm, LANES), table_idx),
        ],
        out_specs=view_specs + [pl.BlockSpec((tm, F_WIDTH), lambda i: (i, 0))],
        out_shape=view_shapes + [jax.ShapeDtypeStruct((n_t, F_WIDTH), BF16)],
        scratch_shapes=[pltpu.VMEM((GROUP_WIDTH // LANES, tm, LANES), F32)] * (3 * N_GROUPS),
        compiler_params=_cparams("parallel"),
        name="inproj_rope",
    )(xp, xs, w_proj, cos_t, sin_t)
    q_views, k_views, v_views = outs[0:N_GROUPS], outs[N_GROUPS:2 * N_GROUPS], outs[2 * N_GROUPS:3 * N_GROUPS]
    return q_views, k_views, v_views, outs[3 * N_GROUPS]


def _attn_kernel(q_ref, kl_ref, km_ref, kr_ref, vl_ref, vm_ref, vr_ref, o_ref, lse_ref, *, prompt_rows, sample_rows):
    i = pl.program_id(1)
    kext = jnp.concatenate([kl_ref[...], km_ref[...], kr_ref[...]], axis=0)
    vext = jnp.concatenate([vl_ref[...], vm_ref[...], vr_ref[...]], axis=0)
    kw = ATTN_SUB + 2 * HALF_WINDOW
    a = lax.broadcasted_iota(I32, (ATTN_SUB, kw), 0)
    c = lax.broadcasted_iota(I32, (ATTN_SUB, kw), 1)
    in_band = jnp.abs(c - HALF_WINDOW - a) <= HALF_WINDOW
    head_of_lane = lax.broadcasted_iota(I32, (1, GROUP_WIDTH), 1) // HEAD_DIM
    for j in range(ATTN_TILE // ATTN_SUB):
        r0 = i * ATTN_TILE + j * ATTN_SUB
        in_prompt = r0 < prompt_rows
        b = jnp.maximum(r0 - prompt_rows, 0) // sample_rows
        lo = jnp.where(in_prompt, 0, prompt_rows + b * sample_rows)
        hi = jnp.where(in_prompt, prompt_rows, prompt_rows + (b + 1) * sample_rows)
        key_row = r0 - HALF_WINDOW + c
        valid = in_band & (key_row >= lo) & (key_row < hi)
        qj = q_ref[j * ATTN_SUB:(j + 1) * ATTN_SUB, :]
        kj = kext[j * ATTN_SUB:j * ATTN_SUB + kw, :]
        vj = vext[j * ATTN_SUB:j * ATTN_SUB + kw, :]
        q4 = jnp.concatenate([jnp.where(head_of_lane == h, qj, jnp.zeros_like(qj)) for h in range(HEADS_PER_GROUP)],
                             axis=0)
        s4 = lax.dot_general(q4, kj, (((1,), (1,)), ((), ())), preferred_element_type=F32)
        ps, scales, lses = [], [], []
        for h in range(HEADS_PER_GROUP):
            s = jnp.where(valid, s4[h * ATTN_SUB:(h + 1) * ATTN_SUB], NEG_INF)
            m = jnp.max(s, axis=1, keepdims=True)
            p = jnp.exp(s - m)
            l = jnp.sum(p, axis=1, keepdims=True)
            ps.append(p.astype(BF16))
            scales.append(1.0 / l)
            lses.append(m + jnp.log(l))
        o4 = jnp.dot(jnp.concatenate(ps, axis=0), vj, preferred_element_type=F32)
        acc = jnp.zeros((ATTN_SUB, GROUP_WIDTH), F32)
        lse_full = jnp.zeros((ATTN_SUB, GROUP_WIDTH), F32)
        for h in range(HEADS_PER_GROUP):
            mine = head_of_lane == h
            acc = jnp.where(mine, o4[h * ATTN_SUB:(h + 1) * ATTN_SUB] * scales[h], acc)
            lse_full = jnp.where(mine, lses[h], lse_full)
        o_ref[j * ATTN_SUB:(j + 1) * ATTN_SUB, :] = acc.astype(BF16)
        lse_ref[j * ATTN_SUB:(j + 1) * ATTN_SUB, :] = lse_full


def _attention_group(qv, kv, vv, dil, n_prompt, seq_sample):
    rows = qv.shape[0]
    tq = ATTN_TILE
    halo_per_tile = tq // HALF_WINDOW
    n_halo_blocks = rows // HALF_WINDOW
    main = pl.BlockSpec((tq, GROUP_WIDTH), lambda r, i: (i, r))
    left = pl.BlockSpec((HALF_WINDOW, GROUP_WIDTH), lambda r, i: (jnp.maximum(i * halo_per_tile - 1, 0), r))
    right = pl.BlockSpec((HALF_WINDOW, GROUP_WIDTH),
                         lambda r, i: (jnp.minimum((i + 1) * halo_per_tile, n_halo_blocks - 1), r))
    return pl.pallas_call(
        functools.partial(_attn_kernel, prompt_rows=n_prompt // dil, sample_rows=seq_sample // dil),
        grid=(dil, rows // tq),
        in_specs=[main, left, main, right, left, main, right],
        out_specs=[main, main],
        out_shape=[jax.ShapeDtypeStruct((rows, dil * GROUP_WIDTH), BF16),
                   jax.ShapeDtypeStruct((rows, dil * GROUP_WIDTH), F32)],
        compiler_params=_cparams("parallel", "parallel"),
        name=f"dilated_attention_{dil}",
    )(qv, kv, kv, kv, vv, vv, vv)


def _angle_table(idx, period):
    return (idx % period).astype(F32) * (2.0 * math.pi / period)


def _channel_dft(seq_len):
    c = jnp.arange(F_GROUP_DIM, dtype=I32)
    th = _angle_table(c[:, None] * c[None, :], F_GROUP_DIM)
    scale = (seq_len * F_GROUP_DIM) ** -0.5
    eye = jnp.eye(F_GROUPS, dtype=F32)
    cb = jnp.kron(eye, jnp.cos(th)) * scale
    sb = jnp.kron(eye, jnp.sin(th)) * scale
    return jnp.concatenate([cb, sb], axis=0).astype(BF16)


def _dft_stage1_kernel(m_ref, x_ref, a_ref):
    a_ref[...] = jnp.dot(m_ref[...], x_ref[...], preferred_element_type=F32).astype(BF16)


def _dft_stage2_kernel(g_ref, a_ref, cs_ref, o_ref):
    s2 = a_ref.shape[2]
    for kk in range(g_ref.shape[0]):
        a = jnp.concatenate([a_ref[0, kk], a_ref[1, kk]], axis=0)
        x = jnp.dot(g_ref[kk], a, preferred_element_type=F32)
        xr = x[:s2].astype(BF16)
        xi = x[s2:].astype(BF16)
        o_ref[:, kk * F_WIDTH:(kk + 1) * F_WIDTH] = (
            jnp.dot(xr, cs_ref[:F_WIDTH, :], preferred_element_type=F32)
            + jnp.dot(xi, cs_ref[F_WIDTH:, :], preferred_element_type=F32)).astype(BF16)


def _fourier_prompt(f, seq):
    s1 = DFT_STAGE1
    s2 = seq // s1
    n1 = jnp.arange(s1, dtype=I32)
    th1 = _angle_table(n1[:, None] * n1[None, :], s1)
    m1 = jnp.concatenate([jnp.cos(th1), -jnp.sin(th1)], axis=0).astype(BF16)
    k1 = jnp.arange(s1, dtype=I32)[:, None, None]
    k2 = jnp.arange(s2, dtype=I32)[None, :, None]
    n2 = jnp.arange(s2, dtype=I32)[None, None, :]
    th = _angle_table((k1 + s1 * k2) * n2, seq)
    gr, gi = jnp.cos(th), -jnp.sin(th)
    g = jnp.concatenate([jnp.concatenate([gr, -gi], axis=2), jnp.concatenate([gi, gr], axis=2)], axis=1).astype(BF16)
    cs = _channel_dft(seq)

    cols = s2 * F_WIDTH
    fv = f[:seq].reshape(s1, cols)
    tn = min(cols, 4096)
    a = pl.pallas_call(
        _dft_stage1_kernel,
        grid=(cols // tn,),
        in_specs=[pl.BlockSpec((2 * s1, s1), lambda j: (0, 0)), pl.BlockSpec((s1, tn), lambda j: (0, j))],
        out_specs=pl.BlockSpec((2 * s1, tn), lambda j: (0, j)),
        out_shape=jax.ShapeDtypeStruct((2 * s1, cols), BF16),
        compiler_params=_cparams("parallel"),
        name="dft_stage1",
    )(m1, fv)
    a4 = a.reshape(2, s1, s2, F_WIDTH)
    kb = DFT_STAGE2_BATCH
    out = pl.pallas_call(
        _dft_stage2_kernel,
        grid=(s1 // kb,),
        in_specs=[pl.BlockSpec((kb, 2 * s2, 2 * s2), lambda k: (k, 0, 0)),
                  pl.BlockSpec((2, kb, s2, F_WIDTH), lambda k: (0, k, 0, 0)),
                  pl.BlockSpec((2 * F_WIDTH, F_WIDTH), lambda k: (0, 0))],
        out_specs=pl.BlockSpec((s2, kb * F_WIDTH), lambda k: (0, k)),
        out_shape=jax.ShapeDtypeStruct((s2, s1 * F_WIDTH), BF16),
        compiler_params=_cparams("parallel"),
        name="dft_stage2",
    )(g, a4, cs)
    return out.reshape(seq, F_WIDTH)


def _dft_small_kernel(g_ref, x_ref, cs_ref, o_ref, stage_ref):
    s1 = DFT_SMALL_STAGE1
    assert s1 == 8
    s2 = x_ref.shape[1] // s1
    root = math.sqrt(0.5)
    x0, x1, x2, x3, x4, x5, x6, x7 = [x_ref[0, n1 * s2:(n1 + 1) * s2, :].astype(F32) for n1 in range(s1)]
    u, v, p, q = x0 - x4, x2 - x6, x1 - x5, x3 - x7
    e_sum, e_dif = (x0 + x4) + (x2 + x6), (x0 + x4) - (x2 + x6)
    o_sum, o_dif = (x1 + x5) + (x3 + x7), (x1 + x5) - (x3 + x7)
    rpq_m, rpq_p = root * (p - q), root * (p + q)
    zero = jnp.zeros_like(x0)
    a_re = [e_sum + o_sum, u + rpq_m, e_dif, u - rpq_m, e_sum - o_sum, u - rpq_m, e_dif, u + rpq_m]
    im1, im3 = -v - rpq_p, v - rpq_p
    a_im = [zero, im1, -o_dif, im3, zero, -im3, o_dif, -im1]

    for k1 in range(s1):
        a = jnp.concatenate([a_re[k1].astype(BF16), a_im[k1].astype(BF16)], axis=0)
        x = jnp.dot(g_ref[k1], a, preferred_element_type=F32)
        out = (jnp.dot(x[:s2].astype(BF16), cs_ref[:F_WIDTH, :], preferred_element_type=F32)
               + jnp.dot(x[s2:].astype(BF16), cs_ref[F_WIDTH:, :], preferred_element_type=F32))
        for c in range(F_WIDTH // LANES):
            stage_ref[c, pl.ds(k1, s2, stride=s1), :] = out[:, c * LANES:(c + 1) * LANES]
    o_ref[0] = jnp.concatenate([stage_ref[c] for c in range(F_WIDTH // LANES)], axis=1).astype(BF16)


def _fourier_sample(f, n_prompt, batch, seq):
    s1 = DFT_SMALL_STAGE1
    s2 = seq // s1
    k1 = jnp.arange(s1, dtype=I32)[:, None, None]
    k2 = jnp.arange(s2, dtype=I32)[None, :, None]
    n2 = jnp.arange(s2, dtype=I32)[None, None, :]
    th = _angle_table((k1 + s1 * k2) * n2, seq)
    gr, gi = jnp.cos(th), -jnp.sin(th)
    g = jnp.concatenate([jnp.concatenate([gr, -gi], axis=2), jnp.concatenate([gi, gr], axis=2)], axis=1).astype(BF16)
    cs = _channel_dft(seq)
    f3 = f.reshape(f.shape[0] // seq, seq, F_WIDTH)
    first = n_prompt // seq
    return pl.pallas_call(
        _dft_small_kernel,
        grid=(batch,),
        in_specs=[pl.BlockSpec((s1, 2 * s2, 2 * s2), lambda b: (0, 0, 0)),
                  pl.BlockSpec((1, seq, F_WIDTH), lambda b: (first + b, 0, 0)),
                  pl.BlockSpec((2 * F_WIDTH, F_WIDTH), lambda b: (0, 0))],
        out_specs=pl.BlockSpec((1, seq, F_WIDTH), lambda b: (b, 0, 0)),
        out_shape=jax.ShapeDtypeStruct((batch, seq, F_WIDTH), BF16),
        scratch_shapes=[pltpu.VMEM((F_WIDTH // LANES, seq, LANES), F32)],
        compiler_params=_cparams("parallel"),
        name="dft_small",
    )(g, f3, cs).reshape(batch * seq, F_WIDTH)


def _pack_bf16_pairs(x):
    w = x.shape[1] // 2
    bits = lax.bitcast_convert_type(x.astype(BF16).astype(F32), U32)
    return (bits[:, :w] >> 16) | (bits[:, w:] & jnp.uint32(0xFFFF0000))


def _unpack_bf16_pairs(u):
    lo = lax.bitcast_convert_type(u << 16, F32)
    hi = lax.bitcast_convert_type(u & jnp.uint32(0xFFFF0000), F32)
    return jnp.concatenate([lo, hi], axis=1)


def _layer_norm(h, g, b):
    mu = jnp.mean(h, axis=-1, keepdims=True)
    d = h - mu
    var = jnp.mean(d * d, axis=-1, keepdims=True)
    return d * lax.rsqrt(var + LN_EPS) * g + b


def _from_dilated_view(stage_ref, blk_ref, dil):
    if dil == 1:
        return blk_ref[...].astype(F32)
    rows = blk_ref.shape[0]
    for r in range(dil):
        for c in range(GROUP_WIDTH // LANES):
            lo = r * GROUP_WIDTH + c * LANES
            stage_ref[c, pl.ds(r, rows, stride=dil), :] = blk_ref[:, lo:lo + LANES].astype(F32)
    return jnp.concatenate([stage_ref[c] for c in range(GROUP_WIDTH // LANES)], axis=1)


def _merge_kernel(xp_ref, xs_ref, wg_ref, o1_ref, o2_ref, o3_ref, l1_ref, l2_ref, l3_ref, fp_ref, fs_ref,
                  wao_ref, wfo_ref, wo_ref, g_ref, b_ref, wr2_ref, br_ref, tri_ref,
                  x1_ref, pk_ref, idx_ref, gate_ref, rank_ref, cnt_ref, run_ref, *stage_refs, n_prompt_tiles):
    i = pl.program_id(0)

    @pl.when(i == 0)
    def _():
        run_ref[...] = jnp.zeros_like(run_ref)

    is_prompt = i < n_prompt_tiles
    x = jnp.where(is_prompt, xp_ref[...], xs_ref[...])
    gpre = jnp.dot(x.astype(BF16), wg_ref[...], preferred_element_type=F32)
    o1, o2, o3 = [_from_dilated_view(stage_refs[gi], ref, d)
                  for gi, (ref, d) in enumerate(zip((o1_ref, o2_ref, o3_ref), DILATIONS))]
    l1, l2, l3 = [_from_dilated_view(stage_refs[N_GROUPS + gi], ref, d)
                  for gi, (ref, d) in enumerate(zip((l1_ref, l2_ref, l3_ref), DILATIONS))]
    m = jnp.maximum(jnp.maximum(l1, l2), l3)
    e1, e2, e3 = jnp.exp(l1 - m), jnp.exp(l2 - m), jnp.exp(l3 - m)
    attn = (e1 * o1 + e2 * o2 + e3 * o3) * (1.0 / (e1 + e2 + e3))
    a = jnp.dot(attn.astype(BF16), wao_ref[...], preferred_element_type=F32)
    four = jnp.where(is_prompt, fp_ref[...], fs_ref[...])
    ff = jnp.dot(four, wfo_ref[...], preferred_element_type=F32)
    merged = jax.nn.sigmoid(gpre[:, :D_MODEL]) * a + jax.nn.sigmoid(gpre[:, D_MODEL:]) * ff
    mix = jnp.dot(merged.astype(BF16), wo_ref[...], preferred_element_type=F32)
    x1 = _layer_norm(DEEPNORM_ALPHA * x + mix, g_ref[...], b_ref[...])
    x1_ref[...] = x1
    pk_ref[...] = _pack_bf16_pairs(x1)

    xh = x1.astype(BF16)
    xl = (x1 - xh.astype(F32)).astype(BF16)
    tm = x1.shape[0]
    cross = lax.dot_general(wr2_ref[...], jnp.concatenate([xh, xl], axis=0), (((1,), (1,)), ((), ())),
                            preferred_element_type=F32)
    logits = ((cross[:N_EXPERTS, :tm] + cross[N_EXPERTS:, :tm])
              + (cross[:N_EXPERTS, tm:] + cross[N_EXPERTS:, tm:])) + br_ref[...]
    eio = lax.broadcasted_iota(I32, (N_EXPERTS, tm), 0)
    work = logits
    vals, sels, picks = [], [], []
    for _ in range(TOP_K):
        mk = jnp.max(work, axis=0, keepdims=True)
        ik = jnp.min(jnp.where(work == mk, eio, N_EXPERTS), axis=0, keepdims=True)
        sel = eio == ik
        vals.append(mk)
        sels.append(sel)
        picks.append(ik)
        work = jnp.where(sel, -jnp.inf, work)
    es = [jnp.exp(v - vals[0]) for v in vals]
    inv_den = 1.0 / (es[0] + es[1] + es[2] + es[3])
    chosen = (sels[0] | sels[1] | sels[2] | sels[3])
    chosen_f = chosen.astype(F32)
    prefix = jnp.dot(chosen_f.astype(BF16), tri_ref[...], preferred_element_type=F32)
    before = prefix + run_ref[...]
    for kk in range(TOP_K):
        idx_ref[kk:kk + 1, :] = picks[kk]
        gate_ref[kk:kk + 1, :] = es[kk] * inv_den
        rank_ref[kk:kk + 1, :] = jnp.sum(jnp.where(sels[kk], before, 0.0), axis=0, keepdims=True).astype(I32)
    run_ref[...] = run_ref[...] + jnp.sum(chosen_f, axis=1, keepdims=True)
    cnt_ref[...] = jnp.broadcast_to(run_ref[...], cnt_ref.shape).astype(I32)


def _merge_route(xp, xs, w_gate, o_list, l_list, four_p, four_s, w_ao, w_fo, w_o, ln_g, ln_b, w_r, b_r, tile0,
                 n_tiles):
    n_p = xp.shape[0]
    tm = TOKEN_TILE
    n_t = n_tiles * tm
    npt = n_p // tm
    w_rt = w_r.T
    w_rh = w_rt.astype(BF16)
    w_rl = (w_rt - w_rh.astype(F32)).astype(BF16)
    tri = (jnp.arange(tm)[:, None] < jnp.arange(tm)[None, :]).astype(BF16)
    row = lambda w: pl.BlockSpec((tm, w), lambda i: (i, 0))
    prompt_row = lambda w: pl.BlockSpec((tm, w), lambda i: (jnp.clip(tile0 + i, 0, npt - 1), 0))
    sample_row = lambda w: pl.BlockSpec((tm, w), lambda i: (jnp.maximum(tile0 + i - npt, 0), 0))
    full = lambda r, c: pl.BlockSpec((r, c), lambda i: (0, 0))
    lane_row = pl.BlockSpec((TOP_K, tm), lambda i: (0, i))
    views = [pl.BlockSpec((tm // d, d * GROUP_WIDTH), lambda i: (tile0 + i, 0)) for d in DILATIONS]
    return pl.pallas_call(
        functools.partial(_merge_kernel, n_prompt_tiles=npt - tile0),
        grid=(n_tiles,),
        in_specs=[
            prompt_row(D_MODEL), sample_row(D_MODEL),
            full(D_MODEL, 2 * D_MODEL),
            *views, *views,
            prompt_row(F_WIDTH), sample_row(F_WIDTH),
            full(GROUP_WIDTH, D_MODEL), full(F_WIDTH, D_MODEL), full(D_MODEL, D_MODEL),
            full(1, D_MODEL), full(1, D_MODEL),
            full(2 * N_EXPERTS, D_MODEL), full(N_EXPERTS, 1),
            full(tm, tm),
        ],
        out_specs=[row(D_MODEL), row(D_MODEL // 2), lane_row, lane_row, lane_row, full(N_EXPERTS, LANES)],
        out_shape=[
            jax.ShapeDtypeStruct((n_t, D_MODEL), F32),
            jax.ShapeDtypeStruct((n_t, D_MODEL // 2), U32),
            jax.ShapeDtypeStruct((TOP_K, n_t), I32),
            jax.ShapeDtypeStruct((TOP_K, n_t), F32),
            jax.ShapeDtypeStruct((TOP_K, n_t), I32),
            jax.ShapeDtypeStruct((N_EXPERTS, LANES), I32),
        ],
        scratch_shapes=[pltpu.VMEM((N_EXPERTS, 1), F32)]
        + [pltpu.VMEM((GROUP_WIDTH // LANES, tm, LANES), F32)] * (2 * N_GROUPS),
        compiler_params=_cparams("arbitrary"),
        name="merge_ln_route",
    )(xp, xs, w_gate, *o_list, *l_list, four_p, four_s, w_ao, w_fo, w_o, ln_g.reshape(1, -1), ln_b.reshape(1, -1),
      jnp.concatenate([w_rh, w_rl], axis=0), b_r.reshape(-1, 1), tri)


def _sc_workers():
    info = plsc.get_sparse_core_info()
    return info.num_cores, info.num_cores * info.num_subcores


def _sc_dispatch(rows_src, pos_chunks, n_rows_out):
    n_t, w = rows_src.shape
    n_cores, n_workers = _sc_workers()
    chunks_per_worker = n_t // (n_workers * SC_CHUNK)
    mesh = plsc.VectorSubcoreMesh(core_axis_name="c", subcore_axis_name="s")

    @functools.partial(
        pl.kernel, mesh=mesh,
        out_type=jax.ShapeDtypeStruct((n_rows_out, w), rows_src.dtype),
        scratch_types=[pltpu.VMEM((TOP_K, SC_CHUNK), I32), pltpu.VMEM((SC_CHUNK, w), rows_src.dtype)],
    )
    def dispatch(src_hbm, pos_hbm, out_hbm, idx_v, rows_v):
        wid = lax.axis_index("s") * n_cores + lax.axis_index("c")

        @pl.loop(0, chunks_per_worker)
        def _(j):
            chunk = wid * chunks_per_worker + j
            pltpu.sync_copy(pos_hbm.at[chunk], idx_v)
            pltpu.sync_copy(src_hbm.at[pl.ds(chunk * SC_CHUNK, SC_CHUNK)], rows_v)
            for kk in range(TOP_K):
                pltpu.sync_copy(rows_v, out_hbm.at[idx_v.at[kk]])

    return dispatch(rows_src, pos_chunks)


def _sc_combine(rows_src, pos_chunks, n_t):
    w = rows_src.shape[1]
    n_cores, n_workers = _sc_workers()
    chunks_per_worker = n_t // (n_workers * SC_CHUNK)
    mesh = plsc.VectorSubcoreMesh(core_axis_name="c", subcore_axis_name="s")

    @functools.partial(
        pl.kernel, mesh=mesh,
        out_type=jax.ShapeDtypeStruct((TOP_K, n_t, w), rows_src.dtype),
        scratch_types=[pltpu.VMEM((TOP_K, SC_CHUNK), I32), pltpu.VMEM((SC_CHUNK, w), rows_src.dtype)],
    )
    def combine(src_hbm, pos_hbm, out_hbm, idx_v, rows_v):
        wid = lax.axis_index("s") * n_cores + lax.axis_index("c")

        @pl.loop(0, chunks_per_worker)
        def _(j):
            chunk = wid * chunks_per_worker + j
            pltpu.sync_copy(pos_hbm.at[chunk], idx_v)
            for kk in range(TOP_K):
                pltpu.sync_copy(src_hbm.at[idx_v.at[kk]], rows_v)
                pltpu.sync_copy(rows_v, out_hbm.at[kk, pl.ds(chunk * SC_CHUNK, SC_CHUNK)])

    return combine(rows_src, pos_chunks)


def _expert_weights_kernel(w_ref, wd_ref, p_ref, wg_ref, wl_ref, wdb_ref):
    half = SPLIT_CHUNK // 2
    for c in range(w_ref.shape[2] // SPLIT_CHUNK):
        w = w_ref[0, :, c * SPLIT_CHUNK:(c + 1) * SPLIT_CHUNK].astype(BF16)
        r = jnp.dot(w, p_ref[...], preferred_element_type=F32)
        wg_ref[0, :, c * half:(c + 1) * half] = r[:, :half].astype(BF16)
        wl_ref[0, :, c * half:(c + 1) * half] = r[:, half:].astype(BF16)
    wdb_ref[...] = wd_ref[...].astype(BF16)


def _expert_weights(w_up, w_down):
    n_e, d_in, d_h2 = w_up.shape
    half = SPLIT_CHUNK // 2
    src = jnp.arange(SPLIT_CHUNK, dtype=I32)[:, None]
    dst = jnp.arange(SPLIT_CHUNK, dtype=I32)[None, :]
    sel = jnp.where(dst < half, src == 2 * dst, src == 2 * (dst - half) + 1).astype(BF16)
    per_expert = lambda r, c: pl.BlockSpec((1, r, c), lambda e: (e, 0, 0))
    return pl.pallas_call(
        _expert_weights_kernel,
        grid=(n_e,),
        in_specs=[per_expert(d_in, d_h2), per_expert(*w_down.shape[1:]),
                  pl.BlockSpec((SPLIT_CHUNK, SPLIT_CHUNK), lambda e: (0, 0))],
        out_specs=[per_expert(d_in, d_h2 // 2), per_expert(d_in, d_h2 // 2), per_expert(*w_down.shape[1:])],
        out_shape=[jax.ShapeDtypeStruct((n_e, d_in, d_h2 // 2), BF16)] * 2
        + [jax.ShapeDtypeStruct(w_down.shape, BF16)],
        compiler_params=_cparams("parallel"),
        name="expert_weights",
    )(w_up, w_down, sel)


def _expert_kernel(be_ref, nv_ref, x_ref, wg_ref, wl_ref, bg_ref, bl_ref, wd_ref, bd_ref, y_ref):
    blk = pl.program_id(0)
    nv = nv_ref[blk]

    @pl.when(nv > 0)
    def _():
        x = _unpack_bf16_pairs(x_ref[...])
        rows = lax.broadcasted_iota(I32, (x.shape[0], 1), 0)
        x = jnp.where(rows < nv, x, 0.0).astype(BF16)
        acc = jnp.zeros((x.shape[0], D_MODEL), F32) + bd_ref[0]
        for c in range(D_FF // FF_CHUNK):
            sl = slice(c * FF_CHUNK, (c + 1) * FF_CHUNK)
            hg = jnp.dot(x, wg_ref[0, :, sl], preferred_element_type=F32) + bg_ref[0, :, sl]
            hl = jnp.dot(x, wl_ref[0, :, sl], preferred_element_type=F32) + bl_ref[0, :, sl]
            glu = jnp.minimum(hg, SWIGLU_LIMIT)
            lin = jnp.clip(hl, -SWIGLU_LIMIT, SWIGLU_LIMIT)
            act = glu * jax.nn.sigmoid(SWIGLU_ALPHA * glu) * (lin + 1.0)
            acc = acc + jnp.dot(act.astype(BF16), wd_ref[0, sl, :], preferred_element_type=F32)
        y_ref[...] = _pack_bf16_pairs(acc)

    @pl.when(nv == 0)
    def _():
        y_ref[...] = jnp.zeros_like(y_ref)


def _expert_ffn(xs_rows, block_e, n_valid, w_glu, w_lin, b_glu, b_lin, w_down, b_down):
    n_rows = xs_rows.shape[0]
    bm = EXPERT_BLOCK
    wspec = lambda r, c: pl.BlockSpec((1, r, c), lambda b, be, nv: (be[b], 0, 0))
    grid_spec = pltpu.PrefetchScalarGridSpec(
        num_scalar_prefetch=2,
        grid=(n_rows // bm,),
        in_specs=[
            pl.BlockSpec((bm, D_MODEL // 2), lambda b, be, nv: (b, 0)),
            wspec(D_MODEL, D_FF), wspec(D_MODEL, D_FF), wspec(1, D_FF), wspec(1, D_FF),
            wspec(D_FF, D_MODEL), wspec(1, D_MODEL),
        ],
        out_specs=pl.BlockSpec((bm, D_MODEL // 2), lambda b, be, nv: (b, 0)),
    )
    return pl.pallas_call(
        _expert_kernel,
        grid_spec=grid_spec,
        out_shape=jax.ShapeDtypeStruct((n_rows, D_MODEL // 2), U32),
        compiler_params=_cparams("arbitrary"),
        name="expert_ffn",
    )(block_e, n_valid, xs_rows, w_glu, w_lin, b_glu, b_lin, w_down, b_down)


def _final_kernel(x1_ref, yg_ref, gt_ref, g_ref, b_ref, *rest):
    o_ref = rest[-1]
    gt = gt_ref[...]
    ffn = jnp.zeros(x1_ref.shape, F32)
    for kk in range(TOP_K):
        ffn = ffn + gt[:, kk:kk + 1] * _unpack_bf16_pairs(yg_ref[kk])
    o_ref[...] = _layer_norm(DEEPNORM_ALPHA * x1_ref[...] + ffn, g_ref[...], b_ref[...])


def _final(x1, yg, gates_t, ln_g, ln_b, src_row0, n_rows, out_rows, dst_row0, out_prev=None):
    tm = FINAL_TILE
    assert src_row0 % tm == 0 and dst_row0 % tm == 0 and n_rows % tm == 0
    src, dst = src_row0 // tm, dst_row0 // tm
    in_specs = [
        pl.BlockSpec((tm, D_MODEL), lambda i: (src + i, 0)),
        pl.BlockSpec((TOP_K, tm, D_MODEL // 2), lambda i: (0, src + i, 0)),
        pl.BlockSpec((tm, TOP_K), lambda i: (src + i, 0)),
        pl.BlockSpec((1, D_MODEL), lambda i: (0, 0)),
        pl.BlockSpec((1, D_MODEL), lambda i: (0, 0)),
    ]
    args = [x1, yg, gates_t, ln_g.reshape(1, -1), ln_b.reshape(1, -1)]
    aliases = {}
    if out_prev is not None:
        in_specs.append(pl.BlockSpec(memory_space=pl.ANY))
        args.append(out_prev)
        aliases = {len(args) - 1: 0}
    return pl.pallas_call(
        _final_kernel,
        grid=(n_rows // tm,),
        in_specs=in_specs,
        out_specs=pl.BlockSpec((tm, D_MODEL), lambda i: (dst + i, 0)),
        out_shape=jax.ShapeDtypeStruct((out_rows, D_MODEL), F32),
        input_output_aliases=aliases,
        compiler_params=_cparams("parallel"),
        name="combine_ln2",
    )(*args)


def _routing_tables(idx, rank, counts, n_blocks):
    bm = EXPERT_BLOCK
    padded = (counts + bm - 1) // bm * bm
    pad_end = jnp.cumsum(padded)
    pad_start = pad_end - padded
    experts = jnp.arange(N_EXPERTS, dtype=I32)

    def lookup(table, e):
        shape = (N_EXPERTS,) + (1,) * e.ndim
        return jnp.sum(jnp.where(e[None] == experts.reshape(shape), table.reshape(shape), 0), axis=0)

    pos = lookup(pad_start, idx) + rank
    blk_row0 = jnp.arange(n_blocks, dtype=I32) * bm
    block_e = jnp.minimum(jnp.sum((pad_end[:, None] <= blk_row0[None, :]).astype(I32), axis=0), N_EXPERTS - 1)
    n_valid = jnp.clip(lookup(counts, block_e) - (blk_row0 - lookup(pad_start, block_e)), 0, bm)
    return pos.astype(I32), block_e.astype(I32), n_valid.astype(I32)


def _moe(pk, idx, rank, counts, w_glu, w_lin, b_glu, b_lin, w_down, b_down):
    n_t = pk.shape[0]
    n_blocks = (n_t * TOP_K) // EXPERT_BLOCK + N_EXPERTS
    pos, block_e, n_valid = _routing_tables(idx, rank, counts, n_blocks)
    pos_chunks = pos.reshape(TOP_K, n_t // SC_CHUNK, SC_CHUNK).transpose(1, 0, 2)
    xs_rows = _sc_dispatch(pk, pos_chunks, n_blocks * EXPERT_BLOCK)
    y_rows = _expert_ffn(xs_rows, block_e, n_valid, w_glu, w_lin, b_glu, b_lin, w_down, b_down)
    return _sc_combine(y_rows, pos_chunks, n_t)


def _rope_tables(seq_max):
    inv = ROPE_THETA ** (-jnp.arange(0, HEAD_DIM, 2, dtype=F32) / HEAD_DIM)
    ang = jnp.arange(seq_max).astype(F32)[:, None] * inv[None, :]
    cos, sin = jnp.cos(ang), jnp.sin(ang)
    reps = LANES // HEAD_DIM
    cos_t = jnp.tile(jnp.concatenate([cos, cos], axis=1), (1, reps))
    sin_t = jnp.tile(jnp.concatenate([-sin, sin], axis=1), (1, reps))
    return cos_t, sin_t


def _encoder_layer(xp, xs, seq_prompt, batch_sample, seq_sample, w_in, w_attn_out, w_four_out, w_o, ln1_g, ln1_b,
                   w_router, b_router, w_up, b_up, w_down, b_down, ln2_g, ln2_b):
    n_p, n_s = xp.shape[0], xs.shape[0]
    assert n_p == seq_prompt and n_s == batch_sample * seq_sample
    assert seq_prompt % (DFT_STAGE1 * 16) == 0 and seq_sample % TOKEN_TILE == 0 and n_p % seq_sample == 0
    n_t = n_p + n_s
    assert n_t % (max(DILATIONS) * ATTN_TILE) == 0 and seq_sample % (max(DILATIONS) * ATTN_SUB) == 0

    w_proj = w_in[:, :PROJ_WIDTH].astype(BF16)
    w_gate = w_in[:, PROJ_WIDTH:].astype(BF16)
    cos_t, sin_t = _rope_tables(max(seq_prompt, seq_sample))
    q_views, k_views, v_views, f = _inproj(xp, xs, w_proj, cos_t, sin_t, seq_sample)

    o_list, l_list = [], []
    for gi, dil in enumerate(DILATIONS):
        o, lse = _attention_group(q_views[gi], k_views[gi], v_views[gi], dil, n_p, seq_sample)
        o_list.append(o)
        l_list.append(lse)

    four_p = _fourier_prompt(f, seq_prompt)
    four_s = _fourier_sample(f, n_p, batch_sample, seq_sample)

    w_glu, w_lin, w_dn = _expert_weights(w_up, w_down)
    b_glu = b_up[:, None, 0::2]
    b_lin = b_up[:, None, 1::2]
    w_ao, w_fo, w_ob = w_attn_out.astype(BF16), w_four_out.astype(BF16), w_o.astype(BF16)

    tiles = n_t // TOKEN_TILE
    bounds = [tiles * p // MOE_PARTS for p in range(MOE_PARTS + 1)]
    routed = []
    for p in range(MOE_PARTS):
        routed.append(_merge_route(xp, xs, w_gate, o_list, l_list, four_p, four_s, w_ao, w_fo, w_ob, ln1_g, ln1_b,
                                   w_router, b_router, bounds[p], bounds[p + 1] - bounds[p]))
    combined = []
    for x1, pk, idx, gates, rank, cnt in routed:
        combined.append(_moe(pk, idx, rank, cnt[:, 0], w_glu, w_lin, b_glu, b_lin, w_dn, b_down[:, None, :]))

    outs = [None, None]
    spans = [(0, n_p), (n_p, n_t)]
    for p, ((x1, pk, idx, gates, rank, cnt), yg) in enumerate(zip(routed, combined)):
        lo, hi = bounds[p] * TOKEN_TILE, bounds[p + 1] * TOKEN_TILE
        gates_t = gates.T
        for which, (a, b) in enumerate(spans):
            s, e = max(lo, a), min(hi, b)
            if s < e:
                outs[which] = _final(x1, yg, gates_t, ln2_g, ln2_b, s - lo, e - s, b - a, s - a, outs[which])
    return outs[0], outs[1]


def kernel(x_prompt, x_sample, w_in, w_attn_out, w_four_out, w_o, ln1_g, ln1_b, w_router, b_router, w_up, b_up,
           w_down, b_down, ln2_g, ln2_b):
    assert w_in.shape[0] == DEPTH
    bp, sp, d = x_prompt.shape
    bs, ss, _ = x_sample.shape
    assert bp == 1 and d == D_MODEL
    y_p, y_s = _encoder_layer(
        x_prompt.reshape(sp, d), x_sample.reshape(bs * ss, d), sp, bs, ss,
        w_in[0], w_attn_out[0], w_four_out[0], w_o[0], ln1_g[0], ln1_b[0], w_router[0], b_router[0],
        w_up[0], b_up[0], w_down[0], b_down[0], ln2_g[0], ln2_b[0])
    return y_p.reshape(x_prompt.shape), y_s.reshape(x_sample.shape)
```

```python
import functools
import math

import jax
import jax.numpy as jnp
from jax import lax
from jax.experimental import pallas as pl
from jax.experimental.pallas import tpu as pltpu
from jax.experimental.pallas import tpu_sc as plsc

F32 = jnp.float32
BF16 = jnp.bfloat16
I32 = jnp.int32
U32 = jnp.uint32

D_MODEL = 1024
HEAD_DIM = 64
HEADS_PER_GROUP = 4
GROUP_WIDTH = HEADS_PER_GROUP * HEAD_DIM
DILATIONS = (1, 4, 16)
HALF_WINDOW = 64
N_GROUPS = len(DILATIONS)
QKV_WIDTH = N_GROUPS * GROUP_WIDTH
F_GROUPS = 4
F_GROUP_DIM = 64
F_WIDTH = F_GROUPS * F_GROUP_DIM
PROJ_WIDTH = 3 * QKV_WIDTH + F_WIDTH
N_EXPERTS = 32
TOP_K = 4
D_FF = 1024
SWIGLU_LIMIT = 7.0
SWIGLU_ALPHA = 1.702
LN_EPS = 1e-5
ROPE_THETA = 10000.0
NEG_INF = -1e30
DEPTH = 1
DEEPNORM_ALPHA = (2 * DEPTH) ** 0.25

LANES = 128
TOKEN_TILE = 512
FINAL_TILE = 1024
ATTN_TILE = 1024
ATTN_SUB = 128
EXPERT_BLOCK = 512
FF_CHUNK = 1024
SPLIT_CHUNK = 256
DFT_STAGE1 = 128
DFT_STAGE2_BATCH = 8
DFT_SMALL_STAGE1 = 8
SC_CHUNK = 64
MOE_PARTS = 2
VMEM_LIMIT = 56 * 1024 * 1024


def _cparams(*sem):
    return pltpu.CompilerParams(dimension_semantics=sem, vmem_limit_bytes=VMEM_LIMIT)


def _to_dilated_view(stage_ref, out_ref, val, dil):
    if dil == 1:
        out_ref[...] = val.astype(BF16)
        return
    rows = val.shape[0] // dil
    for c in range(GROUP_WIDTH // LANES):
        stage_ref[c] = val[:, c * LANES:(c + 1) * LANES]
    for r in range(dil):
        for c in range(GROUP_WIDTH // LANES):
            lo = r * GROUP_WIDTH + c * LANES
            out_ref[:, lo:lo + LANES] = stage_ref[c, pl.ds(r, rows, stride=dil), :].astype(BF16)


def _inproj_kernel(xp_ref, xs_ref, w_ref, cos_ref, sin_ref, *refs, n_prompt_tiles):
    qkv_refs = refs[:3 * N_GROUPS]
    f_ref = refs[3 * N_GROUPS]
    stage_refs = refs[3 * N_GROUPS + 1:]
    i = pl.program_id(0)
    x = jnp.where(i < n_prompt_tiles, xp_ref[...], xs_ref[...]).astype(BF16)
    proj = jnp.dot(x, w_ref[...], preferred_element_type=F32)
    cos = cos_ref[...]
    sin = sin_ref[...]
    lane = lax.broadcasted_iota(I32, cos.shape, 1)
    first_half = (lane % HEAD_DIM) < (HEAD_DIM // 2)

    def rope(t):
        parts = []
        for c in range(GROUP_WIDTH // LANES):
            tc = t[:, c * LANES:(c + 1) * LANES]
            swapped = jnp.where(first_half, pltpu.roll(tc, LANES - HEAD_DIM // 2, 1), pltpu.roll(tc, HEAD_DIM // 2, 1))
            parts.append(tc * cos + swapped * sin)
        return jnp.concatenate(parts, axis=1)

    for gi, dil in enumerate(DILATIONS):
        sl = slice(gi * GROUP_WIDTH, (gi + 1) * GROUP_WIDTH)
        qg = rope(proj[:, sl]) * (HEAD_DIM ** -0.5)
        kg = rope(proj[:, QKV_WIDTH + gi * GROUP_WIDTH:QKV_WIDTH + (gi + 1) * GROUP_WIDTH])
        vg = proj[:, 2 * QKV_WIDTH + gi * GROUP_WIDTH:2 * QKV_WIDTH + (gi + 1) * GROUP_WIDTH]
        for which, val in enumerate((qg, kg, vg)):
            _to_dilated_view(stage_refs[which * N_GROUPS + gi], qkv_refs[which * N_GROUPS + gi], val, dil)
    f_ref[...] = proj[:, 3 * QKV_WIDTH:PROJ_WIDTH].astype(BF16)


def _inproj(xp, xs, w_proj, cos_t, sin_t, seq_sample):
    n_p, n_s = xp.shape[0], xs.shape[0]
    n_t = n_p + n_s
    tm = TOKEN_TILE
    npt = n_p // tm
    tiles_per_sample_seq = seq_sample // tm

    def table_idx(i):
        return (jnp.where(i < npt, i, (i - npt) % tiles_per_sample_seq), 0)

    view_specs = [pl.BlockSpec((tm // d, d * GROUP_WIDTH), lambda i: (i, 0)) for d in DILATIONS] * 3
    view_shapes = [jax.ShapeDtypeStruct((n_t // d, d * GROUP_WIDTH), BF16) for d in DILATIONS] * 3
    outs = pl.pallas_call(
        functools.partial(_inproj_kernel, n_prompt_tiles=npt),
        grid=(n_t // tm,),
        in_specs=[
            pl.BlockSpec((tm, D_MODEL), lambda i: (jnp.minimum(i, npt - 1), 0)),
            pl.BlockSpec((tm, D_MODEL), lambda i: (jnp.maximum(i - npt, 0), 0)),
            pl.BlockSpec((D_MODEL, PROJ_WIDTH), lambda i: (0, 0)),
            pl.BlockSpec((tm, LANES), table_idx),
            pl.BlockSpec((tm, LANES), table_idx),
        ],
        out_specs=view_specs + [pl.BlockSpec((tm, F_WIDTH), lambda i: (i, 0))],
        out_shape=view_shapes + [jax.ShapeDtypeStruct((n_t, F_WIDTH), BF16)],
        scratch_shapes=[pltpu.VMEM((GROUP_WIDTH // LANES, tm, LANES), F32)] * (3 * N_GROUPS),
        compiler_params=_cparams("parallel"),
        name="inproj_rope",
    )(xp, xs, w_proj, cos_t, sin_t)
    q_views, k_views, v_views = outs[0:N_GROUPS], outs[N_GROUPS:2 * N_GROUPS], outs[2 * N_GROUPS:3 * N_GROUPS]
    return q_views, k_views, v_views, outs[3 * N_GROUPS]


def _attn_kernel(q_ref, kl_ref, km_ref, kr_ref, vl_ref, vm_ref, vr_ref, o_ref, lse_ref, *, prompt_rows, sample_rows):
    i = pl.program_id(1)
    kext = jnp.concatenate([kl_ref[...], km_ref[...], kr_ref[...]], axis=0)
    vext = jnp.concatenate([vl_ref[...], vm_ref[...], vr_ref[...]], axis=0)
    kw = ATTN_SUB + 2 * HALF_WINDOW
    a = lax.broadcasted_iota(I32, (ATTN_SUB, kw), 0)
    c = lax.broadcasted_iota(I32, (ATTN_SUB, kw), 1)
    in_band = jnp.abs(c - HALF_WINDOW - a) <= HALF_WINDOW
    head_of_lane = lax.broadcasted_iota(I32, (1, GROUP_WIDTH), 1) // HEAD_DIM
    for j in range(ATTN_TILE // ATTN_SUB):
        r0 = i * ATTN_TILE + j * ATTN_SUB
        in_prompt = r0 < prompt_rows
        b = jnp.maximum(r0 - prompt_rows, 0) // sample_rows
        lo = jnp.where(in_prompt, 0, prompt_rows + b * sample_rows)
        hi = jnp.where(in_prompt, prompt_rows, prompt_rows + (b + 1) * sample_rows)
        key_row = r0 - HALF_WINDOW + c
        valid = in_band & (key_row >= lo) & (key_row < hi)
        qj = q_ref[j * ATTN_SUB:(j + 1) * ATTN_SUB, :]
        kj = kext[j * ATTN_SUB:j * ATTN_SUB + kw, :]
        vj = vext[j * ATTN_SUB:j * ATTN_SUB + kw, :]
        q4 = jnp.concatenate([jnp.where(head_of_lane == h, qj, jnp.zeros_like(qj)) for h in range(HEADS_PER_GROUP)],
                             axis=0)
        s4 = lax.dot_general(q4, kj, (((1,), (1,)), ((), ())), preferred_element_type=F32)
        ps, scales, lses = [], [], []
        for h in range(HEADS_PER_GROUP):
            s = jnp.where(valid, s4[h * ATTN_SUB:(h + 1) * ATTN_SUB], NEG_INF)
            m = jnp.max(s, axis=1, keepdims=True)
            p = jnp.exp(s - m)
            l = jnp.sum(p, axis=1, keepdims=True)
            ps.append(p.astype(BF16))
            scales.append(1.0 / l)
            lses.append(m + jnp.log(l))
        o4 = jnp.dot(jnp.concatenate(ps, axis=0), vj, preferred_element_type=F32)
        acc = jnp.zeros((ATTN_SUB, GROUP_WIDTH), F32)
        lse_full = jnp.zeros((ATTN_SUB, GROUP_WIDTH), F32)
        for h in range(HEADS_PER_GROUP):
            mine = head_of_lane == h
            acc = jnp.where(mine, o4[h * ATTN_SUB:(h + 1) * ATTN_SUB] * scales[h], acc)
            lse_full = jnp.where(mine, lses[h], lse_full)
        o_ref[j * ATTN_SUB:(j + 1) * ATTN_SUB, :] = acc.astype(BF16)
        lse_ref[j * ATTN_SUB:(j + 1) * ATTN_SUB, :] = lse_full


def _attention_group(qv, kv, vv, dil, n_prompt, seq_sample):
    rows = qv.shape[0]
    tq = ATTN_TILE
    halo_per_tile = tq // HALF_WINDOW
    n_halo_blocks = rows // HALF_WINDOW
    main = pl.BlockSpec((tq, GROUP_WIDTH), lambda r, i: (i, r))
    left = pl.BlockSpec((HALF_WINDOW, GROUP_WIDTH), lambda r, i: (jnp.maximum(i * halo_per_tile - 1, 0), r))
    right = pl.BlockSpec((HALF_WINDOW, GROUP_WIDTH),
                         lambda r, i: (jnp.minimum((i + 1) * halo_per_tile, n_halo_blocks - 1), r))
    return pl.pallas_call(
        functools.partial(_attn_kernel, prompt_rows=n_prompt // dil, sample_rows=seq_sample // dil),
        grid=(dil, rows // tq),
        in_specs=[main, left, main, right, left, main, right],
        out_specs=[main, main],
        out_shape=[jax.ShapeDtypeStruct((rows, dil * GROUP_WIDTH), BF16),
                   jax.ShapeDtypeStruct((rows, dil * GROUP_WIDTH), F32)],
        compiler_params=_cparams("parallel", "parallel"),
        name=f"dilated_attention_{dil}",
    )(qv, kv, kv, kv, vv, vv, vv)


def _angle_table(idx, period):
    return (idx % period).astype(F32) * (2.0 * math.pi / period)


def _channel_dft(seq_len):
    c = jnp.arange(F_GROUP_DIM, dtype=I32)
    th = _angle_table(c[:, None] * c[None, :], F_GROUP_DIM)
    scale = (seq_len * F_GROUP_DIM) ** -0.5
    eye = jnp.eye(F_GROUPS, dtype=F32)
    cb = jnp.kron(eye, jnp.cos(th)) * scale
    sb = jnp.kron(eye, jnp.sin(th)) * scale
    return jnp.concatenate([cb, sb], axis=0).astype(BF16)


def _dft_stage1_kernel(m_ref, x_ref, a_ref):
    a_ref[...] = jnp.dot(m_ref[...], x_ref[...], preferred_element_type=F32).astype(BF16)


def _dft_stage2_kernel(g_ref, a_ref, cs_ref, o_ref):
    s2 = a_ref.shape[2]
    for kk in range(g_ref.shape[0]):
        a = jnp.concatenate([a_ref[0, kk], a_ref[1, kk]], axis=0)
        x = jnp.dot(g_ref[kk], a, preferred_element_type=F32)
        xr = x[:s2].astype(BF16)
        xi = x[s2:].astype(BF16)
        o_ref[:, kk * F_WIDTH:(kk + 1) * F_WIDTH] = (
            jnp.dot(xr, cs_ref[:F_WIDTH, :], preferred_element_type=F32)
            + jnp.dot(xi, cs_ref[F_WIDTH:, :], preferred_element_type=F32)).astype(BF16)


def _fourier_prompt(f, seq):
    s1 = DFT_STAGE1
    s2 = seq // s1
    n1 = jnp.arange(s1, dtype=I32)
    th1 = _angle_table(n1[:, None] * n1[None, :], s1)
    m1 = jnp.concatenate([jnp.cos(th1), -jnp.sin(th1)], axis=0).astype(BF16)
    k1 = jnp.arange(s1, dtype=I32)[:, None, None]
    k2 = jnp.arange(s2, dtype=I32)[None, :, None]
    n2 = jnp.arange(s2, dtype=I32)[None, None, :]
    th = _angle_table((k1 + s1 * k2) * n2, seq)
    gr, gi = jnp.cos(th), -jnp.sin(th)
    g = jnp.concatenate([jnp.concatenate([gr, -gi], axis=2), jnp.concatenate([gi, gr], axis=2)], axis=1).astype(BF16)
    cs = _channel_dft(seq)

    cols = s2 * F_WIDTH
    fv = f[:seq].reshape(s1, cols)
    tn = min(cols, 4096)
    a = pl.pallas_call(
        _dft_stage1_kernel,
        grid=(cols // tn,),
        in_specs=[pl.BlockSpec((2 * s1, s1), lambda j: (0, 0)), pl.BlockSpec((s1, tn), lambda j: (0, j))],
        out_specs=pl.BlockSpec((2 * s1, tn), lambda j: (0, j)),
        out_shape=jax.ShapeDtypeStruct((2 * s1, cols), BF16),
        compiler_params=_cparams("parallel"),
        name="dft_stage1",
    )(m1, fv)
    a4 = a.reshape(2, s1, s2, F_WIDTH)
    kb = DFT_STAGE2_BATCH
    out = pl.pallas_call(
        _dft_stage2_kernel,
        grid=(s1 // kb,),
        in_specs=[pl.BlockSpec((kb, 2 * s2, 2 * s2), lambda k: (k, 0, 0)),
                  pl.BlockSpec((2, kb, s2, F_WIDTH), lambda k: (0, k, 0, 0)),
                  pl.BlockSpec((2 * F_WIDTH, F_WIDTH), lambda k: (0, 0))],
        out_specs=pl.BlockSpec((s2, kb * F_WIDTH), lambda k: (0, k)),
        out_shape=jax.ShapeDtypeStruct((s2, s1 * F_WIDTH), BF16),
        compiler_params=_cparams("parallel"),
        name="dft_stage2",
    )(g, a4, cs)
    return out.reshape(seq, F_WIDTH)


def _dft_small_kernel(g_ref, x_ref, cs_ref, o_ref, stage_ref):
    s1 = DFT_SMALL_STAGE1
    assert s1 == 8
    s2 = x_ref.shape[1] // s1
    root = math.sqrt(0.5)
    x0, x1, x2, x3, x4, x5, x6, x7 = [x_ref[0, n1 * s2:(n1 + 1) * s2, :].astype(F32) for n1 in range(s1)]
    u, v, p, q = x0 - x4, x2 - x6, x1 - x5, x3 - x7
    e_sum, e_dif = (x0 + x4) + (x2 + x6), (x0 + x4) - (x2 + x6)
    o_sum, o_dif = (x1 + x5) + (x3 + x7), (x1 + x5) - (x3 + x7)
    rpq_m, rpq_p = root * (p - q), root * (p + q)
    zero = jnp.zeros_like(x0)
    a_re = [e_sum + o_sum, u + rpq_m, e_dif, u - rpq_m, e_sum - o_sum, u - rpq_m, e_dif, u + rpq_m]
    im1, im3 = -v - rpq_p, v - rpq_p
    a_im = [zero, im1, -o_dif, im3, zero, -im3, o_dif, -im1]

    for k1 in range(s1):
        a = jnp.concatenate([a_re[k1].astype(BF16), a_im[k1].astype(BF16)], axis=0)
        x = jnp.dot(g_ref[k1], a, preferred_element_type=F32)
        out = (jnp.dot(x[:s2].astype(BF16), cs_ref[:F_WIDTH, :], preferred_element_type=F32)
               + jnp.dot(x[s2:].astype(BF16), cs_ref[F_WIDTH:, :], preferred_element_type=F32))
        for c in range(F_WIDTH // LANES):
            stage_ref[c, pl.ds(k1, s2, stride=s1), :] = out[:, c * LANES:(c + 1) * LANES]
    o_ref[0] = jnp.concatenate([stage_ref[c] for c in range(F_WIDTH // LANES)], axis=1).astype(BF16)


def _fourier_sample(f, n_prompt, batch, seq):
    s1 = DFT_SMALL_STAGE1
    s2 = seq // s1
    k1 = jnp.arange(s1, dtype=I32)[:, None, None]
    k2 = jnp.arange(s2, dtype=I32)[None, :, None]
    n2 = jnp.arange(s2, dtype=I32)[None, None, :]
    th = _angle_table((k1 + s1 * k2) * n2, seq)
    gr, gi = jnp.cos(th), -jnp.sin(th)
    g = jnp.concatenate([jnp.concatenate([gr, -gi], axis=2), jnp.concatenate([gi, gr], axis=2)], axis=1).astype(BF16)
    cs = _channel_dft(seq)
    f3 = f.reshape(f.shape[0] // seq, seq, F_WIDTH)
    first = n_prompt // seq
    return pl.pallas_call(
        _dft_small_kernel,
        grid=(batch,),
        in_specs=[pl.BlockSpec((s1, 2 * s2, 2 * s2), lambda b: (0, 0, 0)),
                  pl.BlockSpec((1, seq, F_WIDTH), lambda b: (first + b, 0, 0)),
                  pl.BlockSpec((2 * F_WIDTH, F_WIDTH), lambda b: (0, 0))],
        out_specs=pl.BlockSpec((1, seq, F_WIDTH), lambda b: (b, 0, 0)),
        out_shape=jax.ShapeDtypeStruct((batch, seq, F_WIDTH), BF16),
        scratch_shapes=[pltpu.VMEM((F_WIDTH // LANES, seq, LANES), F32)],
        compiler_params=_cparams("parallel"),
        name="dft_small",
    )(g, f3, cs).reshape(batch * seq, F_WIDTH)


def _pack_bf16_pairs(x):
    w = x.shape[1] // 2
    bits = lax.bitcast_convert_type(x.astype(BF16).astype(F32), U32)
    return (bits[:, :w] >> 16) | (bits[:, w:] & jnp.uint32(0xFFFF0000))


def _unpack_bf16_pairs(u):
    lo = lax.bitcast_convert_type(u << 16, F32)
    hi = lax.bitcast_convert_type(u & jnp.uint32(0xFFFF0000), F32)
    return jnp.concatenate([lo, hi], axis=1)


def _layer_norm(h, g, b):
    mu = jnp.mean(h, axis=-1, keepdims=True)
    d = h - mu
    var = jnp.mean(d * d, axis=-1, keepdims=True)
    return d * lax.rsqrt(var + LN_EPS) * g + b


def _from_dilated_view(stage_ref, blk_ref, dil):
    if dil == 1:
        return blk_ref[...].astype(F32)
    rows = blk_ref.shape[0]
    for r in range(dil):
        for c in range(GROUP_WIDTH // LANES):
            lo = r * GROUP_WIDTH + c * LANES
            stage_ref[c, pl.ds(r, rows, stride=dil), :] = blk_ref[:, lo:lo + LANES].astype(F32)
    return jnp.concatenate([stage_ref[c] for c in range(GROUP_WIDTH // LANES)], axis=1)


def _merge_kernel(xp_ref, xs_ref, wg_ref, o1_ref, o2_ref, o3_ref, l1_ref, l2_ref, l3_ref, fp_ref, fs_ref,
                  wao_ref, wfo_ref, wo_ref, g_ref, b_ref, wr2_ref, br_ref, tri_ref,
                  x1_ref, pk_ref, idx_ref, gate_ref, rank_ref, cnt_ref, run_ref, *stage_refs, n_prompt_tiles):
    i = pl.program_id(0)

    @pl.when(i == 0)
    def _():
        run_ref[...] = jnp.zeros_like(run_ref)

    is_prompt = i < n_prompt_tiles
    x = jnp.where(is_prompt, xp_ref[...], xs_ref[...])
    gpre = jnp.dot(x.astype(BF16), wg_ref[...], preferred_element_type=F32)
    o1, o2, o3 = [_from_dilated_view(stage_refs[gi], ref, d)
                  for gi, (ref, d) in enumerate(zip((o1_ref, o2_ref, o3_ref), DILATIONS))]
    l1, l2, l3 = [_from_dilated_view(stage_refs[N_GROUPS + gi], ref, d)
                  for gi, (ref, d) in enumerate(zip((l1_ref, l2_ref, l3_ref), DILATIONS))]
    m = jnp.maximum(jnp.maximum(l1, l2), l3)
    e1, e2, e3 = jnp.exp(l1 - m), jnp.exp(l2 - m), jnp.exp(l3 - m)
    attn = (e1 * o1 + e2 * o2 + e3 * o3) * (1.0 / (e1 + e2 + e3))
    a = jnp.dot(attn.astype(BF16), wao_ref[...], preferred_element_type=F32)
    four = jnp.where(is_prompt, fp_ref[...], fs_ref[...])
    ff = jnp.dot(four, wfo_ref[...], preferred_element_type=F32)
    merged = jax.nn.sigmoid(gpre[:, :D_MODEL]) * a + jax.nn.sigmoid(gpre[:, D_MODEL:]) * ff
    mix = jnp.dot(merged.astype(BF16), wo_ref[...], preferred_element_type=F32)
    x1 = _layer_norm(DEEPNORM_ALPHA * x + mix, g_ref[...], b_ref[...])
    x1_ref[...] = x1
    pk_ref[...] = _pack_bf16_pairs(x1)

    xh = x1.astype(BF16)
    xl = (x1 - xh.astype(F32)).astype(BF16)
    tm = x1.shape[0]
    cross = lax.dot_general(wr2_ref[...], jnp.concatenate([xh, xl], axis=0), (((1,), (1,)), ((), ())),
                            preferred_element_type=F32)
    logits = ((cross[:N_EXPERTS, :tm] + cross[N_EXPERTS:, :tm])
              + (cross[:N_EXPERTS, tm:] + cross[N_EXPERTS:, tm:])) + br_ref[...]
    eio = lax.broadcasted_iota(I32, (N_EXPERTS, tm), 0)
    work = logits
    vals, sels, picks = [], [], []
    for _ in range(TOP_K):
        mk = jnp.max(work, axis=0, keepdims=True)
        ik = jnp.min(jnp.where(work == mk, eio, N_EXPERTS), axis=0, keepdims=True)
        sel = eio == ik
        vals.append(mk)
        sels.append(sel)
        picks.append(ik)
        work = jnp.where(sel, -jnp.inf, work)
    es = [jnp.exp(v - vals[0]) for v in vals]
    inv_den = 1.0 / (es[0] + es[1] + es[2] + es[3])
    chosen = (sels[0] | sels[1] | sels[2] | sels[3])
    chosen_f = chosen.astype(F32)
    prefix = jnp.dot(chosen_f.astype(BF16), tri_ref[...], preferred_element_type=F32)
    before = prefix + run_ref[...]
    for kk in range(TOP_K):
        idx_ref[kk:kk + 1, :] = picks[kk]
        gate_ref[kk:kk + 1, :] = es[kk] * inv_den
        rank_ref[kk:kk + 1, :] = jnp.sum(jnp.where(sels[kk], before, 0.0), axis=0, keepdims=True).astype(I32)
    run_ref[...] = run_ref[...] + jnp.sum(chosen_f, axis=1, keepdims=True)
    cnt_ref[...] = jnp.broadcast_to(run_ref[...], cnt_ref.shape).astype(I32)


def _merge_route(xp, xs, w_gate, o_list, l_list, four_p, four_s, w_ao, w_fo, w_o, ln_g, ln_b, w_r, b_r, tile0,
                 n_tiles):
    n_p = xp.shape[0]
    tm = TOKEN_TILE
    n_t = n_tiles * tm
    npt = n_p // tm
    w_rt = w_r.T
    w_rh = w_rt.astype(BF16)
    w_rl = (w_rt - w_rh.astype(F32)).astype(BF16)
    tri = (jnp.arange(tm)[:, None] < jnp.arange(tm)[None, :]).astype(BF16)
    row = lambda w: pl.BlockSpec((tm, w), lambda i: (i, 0))
    prompt_row = lambda w: pl.BlockSpec((tm, w), lambda i: (jnp.clip(tile0 + i, 0, npt - 1), 0))
    sample_row = lambda w: pl.BlockSpec((tm, w), lambda i: (jnp.maximum(tile0 + i - npt, 0), 0))
    full = lambda r, c: pl.BlockSpec((r, c), lambda i: (0, 0))
    lane_row = pl.BlockSpec((TOP_K, tm), lambda i: (0, i))
    views = [pl.BlockSpec((tm // d, d * GROUP_WIDTH), lambda i: (tile0 + i, 0)) for d in DILATIONS]
    return pl.pallas_call(
        functools.partial(_merge_kernel, n_prompt_tiles=npt - tile0),
        grid=(n_tiles,),
        in_specs=[
            prompt_row(D_MODEL), sample_row(D_MODEL),
            full(D_MODEL, 2 * D_MODEL),
            *views, *views,
            prompt_row(F_WIDTH), sample_row(F_WIDTH),
            full(GROUP_WIDTH, D_MODEL), full(F_WIDTH, D_MODEL), full(D_MODEL, D_MODEL),
            full(1, D_MODEL), full(1, D_MODEL),
            full(2 * N_EXPERTS, D_MODEL), full(N_EXPERTS, 1),
            full(tm, tm),
        ],
        out_specs=[row(D_MODEL), row(D_MODEL // 2), lane_row, lane_row, lane_row, full(N_EXPERTS, LANES)],
        out_shape=[
            jax.ShapeDtypeStruct((n_t, D_MODEL), F32),
            jax.ShapeDtypeStruct((n_t, D_MODEL // 2), U32),
            jax.ShapeDtypeStruct((TOP_K, n_t), I32),
            jax.ShapeDtypeStruct((TOP_K, n_t), F32),
            jax.ShapeDtypeStruct((TOP_K, n_t), I32),
            jax.ShapeDtypeStruct((N_EXPERTS, LANES), I32),
        ],
        scratch_shapes=[pltpu.VMEM((N_EXPERTS, 1), F32)]
        + [pltpu.VMEM((GROUP_WIDTH // LANES, tm, LANES), F32)] * (2 * N_GROUPS),
        compiler_params=_cparams("arbitrary"),
        name="merge_ln_route",
    )(xp, xs, w_gate, *o_list, *l_list, four_p, four_s, w_ao, w_fo, w_o, ln_g.reshape(1, -1), ln_b.reshape(1, -1),
      jnp.concatenate([w_rh, w_rl], axis=0), b_r.reshape(-1, 1), tri)


def _sc_workers():
    info = plsc.get_sparse_core_info()
    return info.num_cores, info.num_cores * info.num_subcores


def _sc_dispatch(rows_src, pos_chunks, n_rows_out):
    n_t, w = rows_src.shape
    n_cores, n_workers = _sc_workers()
    chunks_per_worker = n_t // (n_workers * SC_CHUNK)
    mesh = plsc.VectorSubcoreMesh(core_axis_name="c", subcore_axis_name="s")

    @functools.partial(
        pl.kernel, mesh=mesh,
        out_type=jax.ShapeDtypeStruct((n_rows_out, w), rows_src.dtype),
        scratch_types=[pltpu.VMEM((TOP_K, SC_CHUNK), I32), pltpu.VMEM((SC_CHUNK, w), rows_src.dtype)],
    )
    def dispatch(src_hbm, pos_hbm, out_hbm, idx_v, rows_v):
        wid = lax.axis_index("s") * n_cores + lax.axis_index("c")

        @pl.loop(0, chunks_per_worker)
        def _(j):
            chunk = wid * chunks_per_worker + j
            pltpu.sync_copy(pos_hbm.at[chunk], idx_v)
            pltpu.sync_copy(src_hbm.at[pl.ds(chunk * SC_CHUNK, SC_CHUNK)], rows_v)
            for kk in range(TOP_K):
                pltpu.sync_copy(rows_v, out_hbm.at[idx_v.at[kk]])

    return dispatch(rows_src, pos_chunks)


def _sc_combine(rows_src, pos_chunks, n_t):
    w = rows_src.shape[1]
    n_cores, n_workers = _sc_workers()
    chunks_per_worker = n_t // (n_workers * SC_CHUNK)
    mesh = plsc.VectorSubcoreMesh(core_axis_name="c", subcore_axis_name="s")

    @functools.partial(
        pl.kernel, mesh=mesh,
        out_type=jax.ShapeDtypeStruct((TOP_K, n_t, w), rows_src.dtype),
        scratch_types=[pltpu.VMEM((TOP_K, SC_CHUNK), I32), pltpu.VMEM((SC_CHUNK, w), rows_src.dtype)],
    )
    def combine(src_hbm, pos_hbm, out_hbm, idx_v, rows_v):
        wid = lax.axis_index("s") * n_cores + lax.axis_index("c")

        @pl.loop(0, chunks_per_worker)
        def _(j):
            chunk = wid * chunks_per_worker + j
            pltpu.sync_copy(pos_hbm.at[chunk], idx_v)
            for kk in range(TOP_K):
                pltpu.sync_copy(src_hbm.at[idx_v.at[kk]], rows_v)
                pltpu.sync_copy(rows_v, out_hbm.at[kk, pl.ds(chunk * SC_CHUNK, SC_CHUNK)])

    return combine(rows_src, pos_chunks)


def _expert_kernel(be_ref, nv_ref, x_ref, wu_ref, wdn_ref, p_ref, bg_ref, bl_ref, bd_ref, y_ref, wg_s, wl_s, wd_s):
    blk = pl.program_id(0)
    nv = nv_ref[blk]
    new_expert = (blk == 0) | (be_ref[blk] != be_ref[jnp.maximum(blk - 1, 0)])

    @pl.when(new_expert)
    def _():
        half = SPLIT_CHUNK // 2
        for c in range(wu_ref.shape[2] // SPLIT_CHUNK):
            w = wu_ref[0, :, c * SPLIT_CHUNK:(c + 1) * SPLIT_CHUNK].astype(BF16)
            r = jnp.dot(w, p_ref[...], preferred_element_type=F32)
            wg_s[:, c * half:(c + 1) * half] = r[:, :half].astype(BF16)
            wl_s[:, c * half:(c + 1) * half] = r[:, half:].astype(BF16)
        wd_s[...] = wdn_ref[0].astype(BF16)

    @pl.when(nv > 0)
    def _():
        x = _unpack_bf16_pairs(x_ref[...])
        rows = lax.broadcasted_iota(I32, (x.shape[0], 1), 0)
        x = jnp.where(rows < nv, x, 0.0).astype(BF16)
        acc = jnp.zeros((x.shape[0], D_MODEL), F32) + bd_ref[0]
        for c in range(D_FF // FF_CHUNK):
            sl = slice(c * FF_CHUNK, (c + 1) * FF_CHUNK)
            hg = jnp.dot(x, wg_s[:, sl], preferred_element_type=F32) + bg_ref[0, :, sl]
            hl = jnp.dot(x, wl_s[:, sl], preferred_element_type=F32) + bl_ref[0, :, sl]
            glu = jnp.minimum(hg, SWIGLU_LIMIT)
            lin = jnp.clip(hl, -SWIGLU_LIMIT, SWIGLU_LIMIT)
            act = glu * jax.nn.sigmoid(SWIGLU_ALPHA * glu) * (lin + 1.0)
            acc = acc + jnp.dot(act.astype(BF16), wd_s[sl, :], preferred_element_type=F32)
        y_ref[...] = _pack_bf16_pairs(acc)

    @pl.when(nv == 0)
    def _():
        y_ref[...] = jnp.zeros_like(y_ref)


def _expert_ffn(xs_rows, block_e, n_valid, w_up, b_glu, b_lin, w_down, b_down):
    n_rows = xs_rows.shape[0]
    bm = EXPERT_BLOCK
    half = SPLIT_CHUNK // 2
    src = jnp.arange(SPLIT_CHUNK, dtype=I32)[:, None]
    dst = jnp.arange(SPLIT_CHUNK, dtype=I32)[None, :]
    sel = jnp.where(dst < half, src == 2 * dst, src == 2 * (dst - half) + 1).astype(BF16)
    wspec = lambda r, c: pl.BlockSpec((1, r, c), lambda b, be, nv: (be[b], 0, 0))
    grid_spec = pltpu.PrefetchScalarGridSpec(
        num_scalar_prefetch=2,
        grid=(n_rows // bm,),
        in_specs=[
            pl.BlockSpec((bm, D_MODEL // 2), lambda b, be, nv: (b, 0)),
            wspec(D_MODEL, 2 * D_FF), wspec(D_FF, D_MODEL),
            pl.BlockSpec((SPLIT_CHUNK, SPLIT_CHUNK), lambda b, be, nv: (0, 0)),
            wspec(1, D_FF), wspec(1, D_FF), wspec(1, D_MODEL),
        ],
        out_specs=pl.BlockSpec((bm, D_MODEL // 2), lambda b, be, nv: (b, 0)),
        scratch_shapes=[pltpu.VMEM((D_MODEL, D_FF), BF16), pltpu.VMEM((D_MODEL, D_FF), BF16),
                        pltpu.VMEM((D_FF, D_MODEL), BF16)],
    )
    return pl.pallas_call(
        _expert_kernel,
        grid_spec=grid_spec,
        out_shape=jax.ShapeDtypeStruct((n_rows, D_MODEL // 2), U32),
        compiler_params=_cparams("arbitrary"),
        name="expert_ffn",
    )(block_e, n_valid, xs_rows, w_up, w_down, sel, b_glu, b_lin, b_down)


def _final_kernel(x1_ref, yg_ref, gt_ref, g_ref, b_ref, *rest):
    o_ref = rest[-1]
    gt = gt_ref[...]
    ffn = jnp.zeros(x1_ref.shape, F32)
    for kk in range(TOP_K):
        ffn = ffn + gt[:, kk:kk + 1] * _unpack_bf16_pairs(yg_ref[kk])
    o_ref[...] = _layer_norm(DEEPNORM_ALPHA * x1_ref[...] + ffn, g_ref[...], b_ref[...])


def _final(x1, yg, gates_t, ln_g, ln_b, src_row0, n_rows, out_rows, dst_row0, out_prev=None):
    tm = FINAL_TILE
    assert src_row0 % tm == 0 and dst_row0 % tm == 0 and n_rows % tm == 0
    src, dst = src_row0 // tm, dst_row0 // tm
    in_specs = [
        pl.BlockSpec((tm, D_MODEL), lambda i: (src + i, 0)),
        pl.BlockSpec((TOP_K, tm, D_MODEL // 2), lambda i: (0, src + i, 0)),
        pl.BlockSpec((tm, TOP_K), lambda i: (src + i, 0)),
        pl.BlockSpec((1, D_MODEL), lambda i: (0, 0)),
        pl.BlockSpec((1, D_MODEL), lambda i: (0, 0)),
    ]
    args = [x1, yg, gates_t, ln_g.reshape(1, -1), ln_b.reshape(1, -1)]
    aliases = {}
    if out_prev is not None:
        in_specs.append(pl.BlockSpec(memory_space=pl.ANY))
        args.append(out_prev)
        aliases = {len(args) - 1: 0}
    return pl.pallas_call(
        _final_kernel,
        grid=(n_rows // tm,),
        in_specs=in_specs,
        out_specs=pl.BlockSpec((tm, D_MODEL), lambda i: (dst + i, 0)),
        out_shape=jax.ShapeDtypeStruct((out_rows, D_MODEL), F32),
        input_output_aliases=aliases,
        compiler_params=_cparams("parallel"),
        name="combine_ln2",
    )(*args)


def _routing_tables(idx, rank, counts, n_blocks):
    bm = EXPERT_BLOCK
    padded = (counts + bm - 1) // bm * bm
    pad_end = jnp.cumsum(padded)
    pad_start = pad_end - padded
    experts = jnp.arange(N_EXPERTS, dtype=I32)

    def lookup(table, e):
        shape = (N_EXPERTS,) + (1,) * e.ndim
        return jnp.sum(jnp.where(e[None] == experts.reshape(shape), table.reshape(shape), 0), axis=0)

    pos = lookup(pad_start, idx) + rank
    blk_row0 = jnp.arange(n_blocks, dtype=I32) * bm
    block_e = jnp.minimum(jnp.sum((pad_end[:, None] <= blk_row0[None, :]).astype(I32), axis=0), N_EXPERTS - 1)
    n_valid = jnp.clip(lookup(counts, block_e) - (blk_row0 - lookup(pad_start, block_e)), 0, bm)
    return pos.astype(I32), block_e.astype(I32), n_valid.astype(I32)


def _moe(pk, idx, rank, counts, w_up, b_glu, b_lin, w_down, b_down):
    n_t = pk.shape[0]
    n_blocks = (n_t * TOP_K) // EXPERT_BLOCK + N_EXPERTS
    pos, block_e, n_valid = _routing_tables(idx, rank, counts, n_blocks)
    pos_chunks = pos.reshape(TOP_K, n_t // SC_CHUNK, SC_CHUNK).transpose(1, 0, 2)
    xs_rows = _sc_dispatch(pk, pos_chunks, n_blocks * EXPERT_BLOCK)
    y_rows = _expert_ffn(xs_rows, block_e, n_valid, w_up, b_glu, b_lin, w_down, b_down)
    return _sc_combine(y_rows, pos_chunks, n_t)


def _rope_tables(seq_max):
    inv = ROPE_THETA ** (-jnp.arange(0, HEAD_DIM, 2, dtype=F32) / HEAD_DIM)
    ang = jnp.arange(seq_max).astype(F32)[:, None] * inv[None, :]
    cos, sin = jnp.cos(ang), jnp.sin(ang)
    reps = LANES // HEAD_DIM
    cos_t = jnp.tile(jnp.concatenate([cos, cos], axis=1), (1, reps))
    sin_t = jnp.tile(jnp.concatenate([-sin, sin], axis=1), (1, reps))
    return cos_t, sin_t


def _encoder_layer(xp, xs, seq_prompt, batch_sample, seq_sample, w_in, w_attn_out, w_four_out, w_o, ln1_g, ln1_b,
                   w_router, b_router, w_up, b_up, w_down, b_down, ln2_g, ln2_b):
    n_p, n_s = xp.shape[0], xs.shape[0]
    assert n_p == seq_prompt and n_s == batch_sample * seq_sample
    assert seq_prompt % (DFT_STAGE1 * 16) == 0 and seq_sample % TOKEN_TILE == 0 and n_p % seq_sample == 0
    n_t = n_p + n_s
    assert n_t % (max(DILATIONS) * ATTN_TILE) == 0 and seq_sample % (max(DILATIONS) * ATTN_SUB) == 0

    w_proj = w_in[:, :PROJ_WIDTH].astype(BF16)
    w_gate = w_in[:, PROJ_WIDTH:].astype(BF16)
    cos_t, sin_t = _rope_tables(max(seq_prompt, seq_sample))
    q_views, k_views, v_views, f = _inproj(xp, xs, w_proj, cos_t, sin_t, seq_sample)

    o_list, l_list = [], []
    for gi, dil in enumerate(DILATIONS):
        o, lse = _attention_group(q_views[gi], k_views[gi], v_views[gi], dil, n_p, seq_sample)
        o_list.append(o)
        l_list.append(lse)

    four_p = _fourier_prompt(f, seq_prompt)
    four_s = _fourier_sample(f, n_p, batch_sample, seq_sample)

    b_glu = b_up[:, None, 0::2]
    b_lin = b_up[:, None, 1::2]
    w_ao, w_fo, w_ob = w_attn_out.astype(BF16), w_four_out.astype(BF16), w_o.astype(BF16)

    tiles = n_t // TOKEN_TILE
    bounds = [tiles * p // MOE_PARTS for p in range(MOE_PARTS + 1)]
    routed = []
    for p in range(MOE_PARTS):
        routed.append(_merge_route(xp, xs, w_gate, o_list, l_list, four_p, four_s, w_ao, w_fo, w_ob, ln1_g, ln1_b,
                                   w_router, b_router, bounds[p], bounds[p + 1] - bounds[p]))
    combined = []
    for x1, pk, idx, gates, rank, cnt in routed:
        combined.append(_moe(pk, idx, rank, cnt[:, 0], w_up, b_glu, b_lin, w_down, b_down[:, None, :]))

    outs = [None, None]
    spans = [(0, n_p), (n_p, n_t)]
    for p, ((x1, pk, idx, gates, rank, cnt), yg) in enumerate(zip(routed, combined)):
        lo, hi = bounds[p] * TOKEN_TILE, bounds[p + 1] * TOKEN_TILE
        gates_t = gates.T
        for which, (a, b) in enumerate(spans):
            s, e = max(lo, a), min(hi, b)
            if s < e:
                outs[which] = _final(x1, yg, gates_t, ln2_g, ln2_b, s - lo, e - s, b - a, s - a, outs[which])
    return outs[0], outs[1]


def kernel(x_prompt, x_sample, w_in, w_attn_out, w_four_out, w_o, ln1_g, ln1_b, w_router, b_router, w_up, b_up,
           w_down, b_down, ln2_g, ln2_b):
    assert w_in.shape[0] == DEPTH
    bp, sp, d = x_prompt.shape
    bs, ss, _ = x_sample.shape
    assert bp == 1 and d == D_MODEL
    y_p, y_s = _encoder_layer(
        x_prompt.reshape(sp, d), x_sample.reshape(bs * ss, d), sp, bs, ss,
        w_in[0], w_attn_out[0], w_four_out[0], w_o[0], ln1_g[0], ln1_b[0], w_router[0], b_router[0],
        w_up[0], b_up[0], w_down[0], b_down[0], ln2_g[0], ln2_b[0])
    return y_p.reshape(x_prompt.shape), y_s.reshape(x_sample.shape)
```

```python
import functools
import math

import jax
import jax.numpy as jnp
from jax import lax
from jax.experimental import pallas as pl
from jax.experimental.pallas import tpu as pltpu
from jax.experimental.pallas import tpu_sc as plsc

F32 = jnp.float32
BF16 = jnp.bfloat16
I32 = jnp.int32
U32 = jnp.uint32

D_MODEL = 1024
HEAD_DIM = 64
HEADS_PER_GROUP = 4
GROUP_WIDTH = HEADS_PER_GROUP * HEAD_DIM
DILATIONS = (1, 4, 16)
HALF_WINDOW = 64
N_GROUPS = len(DILATIONS)
QKV_WIDTH = N_GROUPS * GROUP_WIDTH
F_GROUPS = 4
F_GROUP_DIM = 64
F_WIDTH = F_GROUPS * F_GROUP_DIM
PROJ_WIDTH = 3 * QKV_WIDTH + F_WIDTH
N_EXPERTS = 32
TOP_K = 4
D_FF = 1024
SWIGLU_LIMIT = 7.0
SWIGLU_ALPHA = 1.702
LN_EPS = 1e-5
ROPE_THETA = 10000.0
NEG_INF = -1e30
DEPTH = 1
DEEPNORM_ALPHA = (2 * DEPTH) ** 0.25

LANES = 128
TOKEN_TILE = 512
FINAL_TILE = 1024
ATTN_TILE = 1024
ATTN_SUB = 128
EXPERT_BLOCK = 512
FF_CHUNK = 1024
SPLIT_CHUNK = 256
DFT_STAGE1 = 128
DFT_STAGE2_BATCH = 8
DFT_SMALL_STAGE1 = 8
DFT_SMALL_BATCH = 4
SC_CHUNK = 64
MOE_PARTS = 2
VMEM_LIMIT = 56 * 1024 * 1024


def _cparams(*sem):
    return pltpu.CompilerParams(dimension_semantics=sem, vmem_limit_bytes=VMEM_LIMIT)


def _to_dilated_view(stage_ref, out_ref, val, dil):
    if dil == 1:
        out_ref[...] = val.astype(BF16)
        return
    rows = val.shape[0] // dil
    for c in range(GROUP_WIDTH // LANES):
        stage_ref[c] = val[:, c * LANES:(c + 1) * LANES]
    for r in range(dil):
        for c in range(GROUP_WIDTH // LANES):
            lo = r * GROUP_WIDTH + c * LANES
            out_ref[:, lo:lo + LANES] = stage_ref[c, pl.ds(r, rows, stride=dil), :].astype(BF16)


def _inproj_kernel(xp_ref, xs_ref, w_ref, cos_ref, sin_ref, *refs, n_prompt_tiles):
    qkv_refs = refs[:3 * N_GROUPS]
    f_ref = refs[3 * N_GROUPS]
    stage_refs = refs[3 * N_GROUPS + 1:]
    i = pl.program_id(0)
    x = jnp.where(i < n_prompt_tiles, xp_ref[...], xs_ref[...]).astype(BF16)
    proj = jnp.dot(x, w_ref[...], preferred_element_type=F32)
    cos = cos_ref[...]
    sin = sin_ref[...]
    lane = lax.broadcasted_iota(I32, cos.shape, 1)
    first_half = (lane % HEAD_DIM) < (HEAD_DIM // 2)

    def rope(t):
        parts = []
        for c in range(GROUP_WIDTH // LANES):
            tc = t[:, c * LANES:(c + 1) * LANES]
            swapped = jnp.where(first_half, pltpu.roll(tc, LANES - HEAD_DIM // 2, 1), pltpu.roll(tc, HEAD_DIM // 2, 1))
            parts.append(tc * cos + swapped * sin)
        return jnp.concatenate(parts, axis=1)

    for gi, dil in enumerate(DILATIONS):
        sl = slice(gi * GROUP_WIDTH, (gi + 1) * GROUP_WIDTH)
        qg = rope(proj[:, sl]) * (HEAD_DIM ** -0.5)
        kg = rope(proj[:, QKV_WIDTH + gi * GROUP_WIDTH:QKV_WIDTH + (gi + 1) * GROUP_WIDTH])
        vg = proj[:, 2 * QKV_WIDTH + gi * GROUP_WIDTH:2 * QKV_WIDTH + (gi + 1) * GROUP_WIDTH]
        for which, val in enumerate((qg, kg, vg)):
            _to_dilated_view(stage_refs[which * N_GROUPS + gi], qkv_refs[which * N_GROUPS + gi], val, dil)
    f_ref[...] = proj[:, 3 * QKV_WIDTH:PROJ_WIDTH].astype(BF16)


def _inproj(xp, xs, w_proj, cos_t, sin_t, seq_sample):
    n_p, n_s = xp.shape[0], xs.shape[0]
    n_t = n_p + n_s
    tm = TOKEN_TILE
    npt = n_p // tm
    tiles_per_sample_seq = seq_sample // tm

    def table_idx(i):
        return (jnp.where(i < npt, i, (i - npt) % tiles_per_sample_seq), 0)

    view_specs = [pl.BlockSpec((tm // d, d * GROUP_WIDTH), lambda i: (i, 0)) for d in DILATIONS] * 3
    view_shapes = [jax.ShapeDtypeStruct((n_t // d, d * GROUP_WIDTH), BF16) for d in DILATIONS] * 3
    outs = pl.pallas_call(
        functools.partial(_inproj_kernel, n_prompt_tiles=npt),
        grid=(n_t // tm,),
        in_specs=[
            pl.BlockSpec((tm, D_MODEL), lambda i: (jnp.minimum(i, npt - 1), 0)),
            pl.BlockSpec((tm, D_MODEL), lambda i: (jnp.maximum(i - npt, 0), 0)),
            pl.BlockSpec((D_MODEL, PROJ_WIDTH), lambda i: (0, 0)),
            pl.BlockSpec((tm, LANES), table_idx),
            pl.BlockSpec((tm, LANES), table_idx),
        ],
        out_specs=view_specs + [pl.BlockSpec((tm, F_WIDTH), lambda i: (i, 0))],
        out_shape=view_shapes + [jax.ShapeDtypeStruct((n_t, F_WIDTH), BF16)],
        scratch_shapes=[pltpu.VMEM((GROUP_WIDTH // LANES, tm, LANES), F32)] * (3 * N_GROUPS),
        compiler_params=_cparams("parallel"),
        name="inproj_rope",
    )(xp, xs, w_proj, cos_t, sin_t)
    q_views, k_views, v_views = outs[0:N_GROUPS], outs[N_GROUPS:2 * N_GROUPS], outs[2 * N_GROUPS:3 * N_GROUPS]
    return q_views, k_views, v_views, outs[3 * N_GROUPS]


def _attn_kernel(q_ref, kl_ref, km_ref, kr_ref, vl_ref, vm_ref, vr_ref, o_ref, lse_ref, *, prompt_rows, sample_rows):
    i = pl.program_id(1)
    kext = jnp.concatenate([kl_ref[...], km_ref[...], kr_ref[...]], axis=0)
    vext = jnp.concatenate([vl_ref[...], vm_ref[...], vr_ref[...]], axis=0)
    kw = ATTN_SUB + 2 * HALF_WINDOW
    a = lax.broadcasted_iota(I32, (ATTN_SUB, kw), 0)
    c = lax.broadcasted_iota(I32, (ATTN_SUB, kw), 1)
    in_band = jnp.abs(c - HALF_WINDOW - a) <= HALF_WINDOW
    head_of_lane = lax.broadcasted_iota(I32, (1, GROUP_WIDTH), 1) // HEAD_DIM
    for j in range(ATTN_TILE // ATTN_SUB):
        r0 = i * ATTN_TILE + j * ATTN_SUB
        in_prompt = r0 < prompt_rows
        b = jnp.maximum(r0 - prompt_rows, 0) // sample_rows
        lo = jnp.where(in_prompt, 0, prompt_rows + b * sample_rows)
        hi = jnp.where(in_prompt, prompt_rows, prompt_rows + (b + 1) * sample_rows)
        key_row = r0 - HALF_WINDOW + c
        valid = in_band & (key_row >= lo) & (key_row < hi)
        qj = q_ref[j * ATTN_SUB:(j + 1) * ATTN_SUB, :]
        kj = kext[j * ATTN_SUB:j * ATTN_SUB + kw, :]
        vj = vext[j * ATTN_SUB:j * ATTN_SUB + kw, :]
        q4 = jnp.concatenate([jnp.where(head_of_lane == h, qj, jnp.zeros_like(qj)) for h in range(HEADS_PER_GROUP)],
                             axis=0)
        s4 = lax.dot_general(q4, kj, (((1,), (1,)), ((), ())), preferred_element_type=F32)
        ps, scales, lses = [], [], []
        for h in range(HEADS_PER_GROUP):
            s = jnp.where(valid, s4[h * ATTN_SUB:(h + 1) * ATTN_SUB], NEG_INF)
            m = jnp.max(s, axis=1, keepdims=True)
            p = jnp.exp(s - m)
            l = jnp.sum(p, axis=1, keepdims=True)
            ps.append(p.astype(BF16))
            scales.append(1.0 / l)
            lses.append(m + jnp.log(l))
        o4 = jnp.dot(jnp.concatenate(ps, axis=0), vj, preferred_element_type=F32)
        acc = jnp.zeros((ATTN_SUB, GROUP_WIDTH), F32)
        lse_full = jnp.zeros((ATTN_SUB, GROUP_WIDTH), F32)
        for h in range(HEADS_PER_GROUP):
            mine = head_of_lane == h
            acc = jnp.where(mine, o4[h * ATTN_SUB:(h + 1) * ATTN_SUB] * scales[h], acc)
            lse_full = jnp.where(mine, lses[h], lse_full)
        o_ref[j * ATTN_SUB:(j + 1) * ATTN_SUB, :] = acc.astype(BF16)
        lse_ref[j * ATTN_SUB:(j + 1) * ATTN_SUB, :] = lse_full


def _attention_group(qv, kv, vv, dil, n_prompt, seq_sample):
    rows = qv.shape[0]
    tq = ATTN_TILE
    halo_per_tile = tq // HALF_WINDOW
    n_halo_blocks = rows // HALF_WINDOW
    main = pl.BlockSpec((tq, GROUP_WIDTH), lambda r, i: (i, r))
    left = pl.BlockSpec((HALF_WINDOW, GROUP_WIDTH), lambda r, i: (jnp.maximum(i * halo_per_tile - 1, 0), r))
    right = pl.BlockSpec((HALF_WINDOW, GROUP_WIDTH),
                         lambda r, i: (jnp.minimum((i + 1) * halo_per_tile, n_halo_blocks - 1), r))
    return pl.pallas_call(
        functools.partial(_attn_kernel, prompt_rows=n_prompt // dil, sample_rows=seq_sample // dil),
        grid=(dil, rows // tq),
        in_specs=[main, left, main, right, left, main, right],
        out_specs=[main, main],
        out_shape=[jax.ShapeDtypeStruct((rows, dil * GROUP_WIDTH), BF16),
                   jax.ShapeDtypeStruct((rows, dil * GROUP_WIDTH), F32)],
        compiler_params=_cparams("parallel", "parallel"),
        name=f"dilated_attention_{dil}",
    )(qv, kv, kv, kv, vv, vv, vv)


def _angle_table(idx, period):
    return (idx % period).astype(F32) * (2.0 * math.pi / period)


def _channel_dft(seq_len):
    c = jnp.arange(F_GROUP_DIM, dtype=I32)
    th = _angle_table(c[:, None] * c[None, :], F_GROUP_DIM)
    scale = (seq_len * F_GROUP_DIM) ** -0.5
    eye = jnp.eye(F_GROUPS, dtype=F32)
    cb = jnp.kron(eye, jnp.cos(th)) * scale
    sb = jnp.kron(eye, jnp.sin(th)) * scale
    return jnp.concatenate([cb, sb], axis=0).astype(BF16)


def _stage2_matrices(s1, s2):
    n2 = jnp.arange(s2, dtype=I32)[None, :]
    th_t = _angle_table(jnp.arange(s1, dtype=I32)[:, None] * n2, s1 * s2)
    th_f = _angle_table(jnp.arange(s2, dtype=I32)[:, None] * n2, s2)
    tc, ts = jnp.cos(th_t)[:, None, :], jnp.sin(th_t)[:, None, :]
    fc, fs = jnp.cos(th_f)[None, :, :], jnp.sin(th_f)[None, :, :]
    gr = tc * fc - ts * fs
    gi = -(ts * fc + tc * fs)
    return jnp.concatenate([jnp.concatenate([gr, -gi], axis=2), jnp.concatenate([gi, gr], axis=2)], axis=1).astype(BF16)


def _dft_stage1_kernel(m_ref, x_ref, a_ref):
    a_ref[...] = jnp.dot(m_ref[...], x_ref[...], preferred_element_type=F32).astype(BF16)


def _dft_stage2_kernel(g_ref, a_ref, cs_ref, o_ref):
    s2 = a_ref.shape[2]
    for kk in range(g_ref.shape[0]):
        a = jnp.concatenate([a_ref[0, kk], a_ref[1, kk]], axis=0)
        x = jnp.dot(g_ref[kk], a, preferred_element_type=F32)
        xr = x[:s2].astype(BF16)
        xi = x[s2:].astype(BF16)
        o_ref[:, kk * F_WIDTH:(kk + 1) * F_WIDTH] = (
            jnp.dot(xr, cs_ref[:F_WIDTH, :], preferred_element_type=F32)
            + jnp.dot(xi, cs_ref[F_WIDTH:, :], preferred_element_type=F32)).astype(BF16)


def _fourier_prompt(f, seq):
    s1 = DFT_STAGE1
    s2 = seq // s1
    n1 = jnp.arange(s1, dtype=I32)
    th1 = _angle_table(n1[:, None] * n1[None, :], s1)
    m1 = jnp.concatenate([jnp.cos(th1), -jnp.sin(th1)], axis=0).astype(BF16)
    g = _stage2_matrices(s1, s2)
    cs = _channel_dft(seq)

    cols = s2 * F_WIDTH
    fv = f[:seq].reshape(s1, cols)
    tn = min(cols, 4096)
    a = pl.pallas_call(
        _dft_stage1_kernel,
        grid=(cols // tn,),
        in_specs=[pl.BlockSpec((2 * s1, s1), lambda j: (0, 0)), pl.BlockSpec((s1, tn), lambda j: (0, j))],
        out_specs=pl.BlockSpec((2 * s1, tn), lambda j: (0, j)),
        out_shape=jax.ShapeDtypeStruct((2 * s1, cols), BF16),
        compiler_params=_cparams("parallel"),
        name="dft_stage1",
    )(m1, fv)
    a4 = a.reshape(2, s1, s2, F_WIDTH)
    kb = DFT_STAGE2_BATCH
    out = pl.pallas_call(
        _dft_stage2_kernel,
        grid=(s1 // kb,),
        in_specs=[pl.BlockSpec((kb, 2 * s2, 2 * s2), lambda k: (k, 0, 0)),
                  pl.BlockSpec((2, kb, s2, F_WIDTH), lambda k: (0, k, 0, 0)),
                  pl.BlockSpec((2 * F_WIDTH, F_WIDTH), lambda k: (0, 0))],
        out_specs=pl.BlockSpec((s2, kb * F_WIDTH), lambda k: (0, k)),
        out_shape=jax.ShapeDtypeStruct((s2, s1 * F_WIDTH), BF16),
        compiler_params=_cparams("parallel"),
        name="dft_stage2",
    )(g, a4, cs)
    return out.reshape(seq, F_WIDTH)


def _dft_small_kernel(g_ref, x_ref, cs_ref, o_ref, stage_ref):
    s1 = DFT_SMALL_STAGE1
    assert s1 == 8
    s2 = x_ref.shape[1] // s1
    root = math.sqrt(0.5)
    n_seq = x_ref.shape[0]
    a_all = []
    for b in range(n_seq):
        x0, x1, x2, x3, x4, x5, x6, x7 = [x_ref[b, n1 * s2:(n1 + 1) * s2, :].astype(F32) for n1 in range(s1)]
        u, v, p, q = x0 - x4, x2 - x6, x1 - x5, x3 - x7
        e_sum, e_dif = (x0 + x4) + (x2 + x6), (x0 + x4) - (x2 + x6)
        o_sum, o_dif = (x1 + x5) + (x3 + x7), (x1 + x5) - (x3 + x7)
        rpq_m, rpq_p = root * (p - q), root * (p + q)
        zero = jnp.zeros_like(x0)
        a_re = [e_sum + o_sum, u + rpq_m, e_dif, u - rpq_m, e_sum - o_sum, u - rpq_m, e_dif, u + rpq_m]
        im1, im3 = -v - rpq_p, v - rpq_p
        a_im = [zero, im1, -o_dif, im3, zero, -im3, o_dif, -im1]
        a_all.append([jnp.concatenate([a_re[k1].astype(BF16), a_im[k1].astype(BF16)], axis=0) for k1 in range(s1)])

    for k1 in range(s1):
        a = jnp.concatenate([a_all[b][k1] for b in range(n_seq)], axis=1)
        x = jnp.dot(g_ref[k1], a, preferred_element_type=F32)
        for b in range(n_seq):
            xb = x[:, b * F_WIDTH:(b + 1) * F_WIDTH]
            out = (jnp.dot(xb[:s2].astype(BF16), cs_ref[:F_WIDTH, :], preferred_element_type=F32)
                   + jnp.dot(xb[s2:].astype(BF16), cs_ref[F_WIDTH:, :], preferred_element_type=F32))
            for c in range(F_WIDTH // LANES):
                stage_ref[b * (F_WIDTH // LANES) + c, pl.ds(k1, s2, stride=s1), :] = out[:, c * LANES:(c + 1) * LANES]
    for b in range(n_seq):
        o_ref[b] = jnp.concatenate([stage_ref[b * (F_WIDTH // LANES) + c] for c in range(F_WIDTH // LANES)],
                                   axis=1).astype(BF16)


def _fourier_sample(f, n_prompt, batch, seq):
    s1 = DFT_SMALL_STAGE1
    s2 = seq // s1
    g = _stage2_matrices(s1, s2)
    cs = _channel_dft(seq)
    f3 = f.reshape(f.shape[0] // seq, seq, F_WIDTH)
    first = n_prompt // seq
    nb = DFT_SMALL_BATCH
    assert batch % nb == 0 and first % nb == 0
    return pl.pallas_call(
        _dft_small_kernel,
        grid=(batch // nb,),
        in_specs=[pl.BlockSpec((s1, 2 * s2, 2 * s2), lambda b: (0, 0, 0)),
                  pl.BlockSpec((nb, seq, F_WIDTH), lambda b: (first // nb + b, 0, 0)),
                  pl.BlockSpec((2 * F_WIDTH, F_WIDTH), lambda b: (0, 0))],
        out_specs=pl.BlockSpec((nb, seq, F_WIDTH), lambda b: (b, 0, 0)),
        out_shape=jax.ShapeDtypeStruct((batch, seq, F_WIDTH), BF16),
        scratch_shapes=[pltpu.VMEM((nb * (F_WIDTH // LANES), seq, LANES), F32)],
        compiler_params=_cparams("parallel"),
        name="dft_small",
    )(g, f3, cs).reshape(batch * seq, F_WIDTH)


def _pack_bf16_pairs(x):
    w = x.shape[1] // 2
    bits = lax.bitcast_convert_type(x.astype(BF16).astype(F32), U32)
    return (bits[:, :w] >> 16) | (bits[:, w:] & jnp.uint32(0xFFFF0000))


def _unpack_bf16_pairs(u):
    lo = lax.bitcast_convert_type(u << 16, F32)
    hi = lax.bitcast_convert_type(u & jnp.uint32(0xFFFF0000), F32)
    return jnp.concatenate([lo, hi], axis=1)


def _layer_norm(h, g, b):
    mu = jnp.mean(h, axis=-1, keepdims=True)
    d = h - mu
    var = jnp.mean(d * d, axis=-1, keepdims=True)
    return d * lax.rsqrt(var + LN_EPS) * g + b


def _from_dilated_view(stage_ref, blk_ref, dil):
    if dil == 1:
        return blk_ref[...].astype(F32)
    rows = blk_ref.shape[0]
    for r in range(dil):
        for c in range(GROUP_WIDTH // LANES):
            lo = r * GROUP_WIDTH + c * LANES
            stage_ref[c, pl.ds(r, rows, stride=dil), :] = blk_ref[:, lo:lo + LANES].astype(F32)
    return jnp.concatenate([stage_ref[c] for c in range(GROUP_WIDTH // LANES)], axis=1)


def _merge_kernel(xp_ref, xs_ref, wg_ref, o1_ref, o2_ref, o3_ref, l1_ref, l2_ref, l3_ref, fp_ref, fs_ref,
                  wao_ref, wfo_ref, wo_ref, g_ref, b_ref, wr2_ref, br_ref, tri_ref,
                  x1_ref, pk_ref, idx_ref, gate_ref, rank_ref, cnt_ref, run_ref, *stage_refs, n_prompt_tiles):
    i = pl.program_id(0)

    @pl.when(i == 0)
    def _():
        run_ref[...] = jnp.zeros_like(run_ref)

    is_prompt = i < n_prompt_tiles
    x = jnp.where(is_prompt, xp_ref[...], xs_ref[...])
    gpre = jnp.dot(x.astype(BF16), wg_ref[...], preferred_element_type=F32)
    o1, o2, o3 = [_from_dilated_view(stage_refs[gi], ref, d)
                  for gi, (ref, d) in enumerate(zip((o1_ref, o2_ref, o3_ref), DILATIONS))]
    l1, l2, l3 = [_from_dilated_view(stage_refs[N_GROUPS + gi], ref, d)
                  for gi, (ref, d) in enumerate(zip((l1_ref, l2_ref, l3_ref), DILATIONS))]
    m = jnp.maximum(jnp.maximum(l1, l2), l3)
    e1, e2, e3 = jnp.exp(l1 - m), jnp.exp(l2 - m), jnp.exp(l3 - m)
    attn = (e1 * o1 + e2 * o2 + e3 * o3) * (1.0 / (e1 + e2 + e3))
    a = jnp.dot(attn.astype(BF16), wao_ref[...], preferred_element_type=F32)
    four = jnp.where(is_prompt, fp_ref[...], fs_ref[...])
    ff = jnp.dot(four, wfo_ref[...], preferred_element_type=F32)
    merged = jax.nn.sigmoid(gpre[:, :D_MODEL]) * a + jax.nn.sigmoid(gpre[:, D_MODEL:]) * ff
    mix = jnp.dot(merged.astype(BF16), wo_ref[...], preferred_element_type=F32)
    x1 = _layer_norm(DEEPNORM_ALPHA * x + mix, g_ref[...], b_ref[...])
    x1_ref[...] = x1
    pk_ref[...] = _pack_bf16_pairs(x1)

    xh = x1.astype(BF16)
    xl = (x1 - xh.astype(F32)).astype(BF16)
    tm = x1.shape[0]
    cross = lax.dot_general(wr2_ref[...], jnp.concatenate([xh, xl], axis=0), (((1,), (1,)), ((), ())),
                            preferred_element_type=F32)
    logits = ((cross[:N_EXPERTS, :tm] + cross[N_EXPERTS:, :tm])
              + (cross[:N_EXPERTS, tm:] + cross[N_EXPERTS:, tm:])) + br_ref[...]
    eio = lax.broadcasted_iota(I32, (N_EXPERTS, tm), 0)
    work = logits
    vals, sels, picks = [], [], []
    for _ in range(TOP_K):
        mk = jnp.max(work, axis=0, keepdims=True)
        ik = jnp.min(jnp.where(work == mk, eio, N_EXPERTS), axis=0, keepdims=True)
        sel = eio == ik
        vals.append(mk)
        sels.append(sel)
        picks.append(ik)
        work = jnp.where(sel, -jnp.inf, work)
    es = [jnp.exp(v - vals[0]) for v in vals]
    inv_den = 1.0 / (es[0] + es[1] + es[2] + es[3])
    chosen = (sels[0] | sels[1] | sels[2] | sels[3])
    chosen_f = chosen.astype(F32)
    prefix = jnp.dot(chosen_f.astype(BF16), tri_ref[...], preferred_element_type=F32)
    before = prefix + run_ref[...]
    for kk in range(TOP_K):
        idx_ref[kk:kk + 1, :] = picks[kk]
        gate_ref[kk:kk + 1, :] = es[kk] * inv_den
        rank_ref[kk:kk + 1, :] = jnp.sum(jnp.where(sels[kk], before, 0.0), axis=0, keepdims=True).astype(I32)
    run_ref[...] = run_ref[...] + jnp.sum(chosen_f, axis=1, keepdims=True)
    cnt_ref[...] = jnp.broadcast_to(run_ref[...], cnt_ref.shape).astype(I32)


def _merge_route(xp, xs, w_gate, o_list, l_list, four_p, four_s, w_ao, w_fo, w_o, ln_g, ln_b, w_r, b_r, tile0,
                 n_tiles):
    n_p = xp.shape[0]
    tm = TOKEN_TILE
    n_t = n_tiles * tm
    npt = n_p // tm
    w_rt = w_r.T
    w_rh = w_rt.astype(BF16)
    w_rl = (w_rt - w_rh.astype(F32)).astype(BF16)
    tri = (jnp.arange(tm)[:, None] < jnp.arange(tm)[None, :]).astype(BF16)
    row = lambda w: pl.BlockSpec((tm, w), lambda i: (i, 0))
    prompt_row = lambda w: pl.BlockSpec((tm, w), lambda i: (jnp.clip(tile0 + i, 0, npt - 1), 0))
    sample_row = lambda w: pl.BlockSpec((tm, w), lambda i: (jnp.maximum(tile0 + i - npt, 0), 0))
    full = lambda r, c: pl.BlockSpec((r, c), lambda i: (0, 0))
    lane_row = pl.BlockSpec((TOP_K, tm), lambda i: (0, i))
    views = [pl.BlockSpec((tm // d, d * GROUP_WIDTH), lambda i: (tile0 + i, 0)) for d in DILATIONS]
    return pl.pallas_call(
        functools.partial(_merge_kernel, n_prompt_tiles=npt - tile0),
        grid=(n_tiles,),
        in_specs=[
            prompt_row(D_MODEL), sample_row(D_MODEL),
            full(D_MODEL, 2 * D_MODEL),
            *views, *views,
            prompt_row(F_WIDTH), sample_row(F_WIDTH),
            full(GROUP_WIDTH, D_MODEL), full(F_WIDTH, D_MODEL), full(D_MODEL, D_MODEL),
            full(1, D_MODEL), full(1, D_MODEL),
            full(2 * N_EXPERTS, D_MODEL), full(N_EXPERTS, 1),
            full(tm, tm),
        ],
        out_specs=[row(D_MODEL), row(D_MODEL // 2), lane_row, lane_row, lane_row, full(N_EXPERTS, LANES)],
        out_shape=[
            jax.ShapeDtypeStruct((n_t, D_MODEL), F32),
            jax.ShapeDtypeStruct((n_t, D_MODEL // 2), U32),
            jax.ShapeDtypeStruct((TOP_K, n_t), I32),
            jax.ShapeDtypeStruct((TOP_K, n_t), F32),
            jax.ShapeDtypeStruct((TOP_K, n_t), I32),
            jax.ShapeDtypeStruct((N_EXPERTS, LANES), I32),
        ],
        scratch_shapes=[pltpu.VMEM((N_EXPERTS, 1), F32)]
        + [pltpu.VMEM((GROUP_WIDTH // LANES, tm, LANES), F32)] * (2 * N_GROUPS),
        compiler_params=_cparams("arbitrary"),
        name="merge_ln_route",
    )(xp, xs, w_gate, *o_list, *l_list, four_p, four_s, w_ao, w_fo, w_o, ln_g.reshape(1, -1), ln_b.reshape(1, -1),
      jnp.concatenate([w_rh, w_rl], axis=0), b_r.reshape(-1, 1), tri)


def _sc_workers():
    info = plsc.get_sparse_core_info()
    return info.num_cores, info.num_cores * info.num_subcores


def _sc_dispatch(rows_src, pos_chunks, n_rows_out):
    n_t, w = rows_src.shape
    n_cores, n_workers = _sc_workers()
    chunks_per_worker = n_t // (n_workers * SC_CHUNK)
    mesh = plsc.VectorSubcoreMesh(core_axis_name="c", subcore_axis_name="s")

    @functools.partial(
        pl.kernel, mesh=mesh,
        out_type=jax.ShapeDtypeStruct((n_rows_out, w), rows_src.dtype),
        scratch_types=[pltpu.VMEM((TOP_K, SC_CHUNK), I32), pltpu.VMEM((SC_CHUNK, w), rows_src.dtype)],
    )
    def dispatch(src_hbm, pos_hbm, out_hbm, idx_v, rows_v):
        wid = lax.axis_index("s") * n_cores + lax.axis_index("c")

        @pl.loop(0, chunks_per_worker)
        def _(j):
            chunk = wid * chunks_per_worker + j
            pltpu.sync_copy(pos_hbm.at[chunk], idx_v)
            pltpu.sync_copy(src_hbm.at[pl.ds(chunk * SC_CHUNK, SC_CHUNK)], rows_v)
            for kk in range(TOP_K):
                pltpu.sync_copy(rows_v, out_hbm.at[idx_v.at[kk]])

    return dispatch(rows_src, pos_chunks)


def _sc_combine(rows_src, pos_chunks, n_t):
    w = rows_src.shape[1]
    n_cores, n_workers = _sc_workers()
    chunks_per_worker = n_t // (n_workers * SC_CHUNK)
    mesh = plsc.VectorSubcoreMesh(core_axis_name="c", subcore_axis_name="s")

    @functools.partial(
        pl.kernel, mesh=mesh,
        out_type=jax.ShapeDtypeStruct((TOP_K, n_t, w), rows_src.dtype),
        scratch_types=[pltpu.VMEM((TOP_K, SC_CHUNK), I32), pltpu.VMEM((SC_CHUNK, w), rows_src.dtype)],
    )
    def combine(src_hbm, pos_hbm, out_hbm, idx_v, rows_v):
        wid = lax.axis_index("s") * n_cores + lax.axis_index("c")

        @pl.loop(0, chunks_per_worker)
        def _(j):
            chunk = wid * chunks_per_worker + j
            pltpu.sync_copy(pos_hbm.at[chunk], idx_v)
            for kk in range(TOP_K):
                pltpu.sync_copy(src_hbm.at[idx_v.at[kk]], rows_v)
                pltpu.sync_copy(rows_v, out_hbm.at[kk, pl.ds(chunk * SC_CHUNK, SC_CHUNK)])

    return combine(rows_src, pos_chunks)


def _expert_kernel(be_ref, nv_ref, x_ref, wu_ref, wdn_ref, p_ref, bg_ref, bl_ref, bd_ref, y_ref, wg_s, wl_s, wd_s):
    blk = pl.program_id(0)
    nv = nv_ref[blk]
    new_expert = (blk == 0) | (be_ref[blk] != be_ref[jnp.maximum(blk - 1, 0)])

    @pl.when(new_expert)
    def _():
        half = SPLIT_CHUNK // 2
        for c in range(wu_ref.shape[2] // SPLIT_CHUNK):
            w = wu_ref[0, :, c * SPLIT_CHUNK:(c + 1) * SPLIT_CHUNK].astype(BF16)
            r = jnp.dot(w, p_ref[...], preferred_element_type=F32)
            wg_s[:, c * half:(c + 1) * half] = r[:, :half].astype(BF16)
            wl_s[:, c * half:(c + 1) * half] = r[:, half:].astype(BF16)
        wd_s[...] = wdn_ref[0].astype(BF16)

    @pl.when(nv > 0)
    def _():
        x = _unpack_bf16_pairs(x_ref[...])
        rows = lax.broadcasted_iota(I32, (x.shape[0], 1), 0)
        x = jnp.where(rows < nv, x, 0.0).astype(BF16)
        acc = jnp.zeros((x.shape[0], D_MODEL), F32) + bd_ref[0]
        for c in range(D_FF // FF_CHUNK):
            sl = slice(c * FF_CHUNK, (c + 1) * FF_CHUNK)
            hg = jnp.dot(x, wg_s[:, sl], preferred_element_type=F32) + bg_ref[0, :, sl]
            hl = jnp.dot(x, wl_s[:, sl], preferred_element_type=F32) + bl_ref[0, :, sl]
            glu = jnp.minimum(hg, SWIGLU_LIMIT)
            lin = jnp.clip(hl, -SWIGLU_LIMIT, SWIGLU_LIMIT)
            act = glu * jax.nn.sigmoid(SWIGLU_ALPHA * glu) * (lin + 1.0)
            acc = acc + jnp.dot(act.astype(BF16), wd_s[sl, :], preferred_element_type=F32)
        y_ref[...] = _pack_bf16_pairs(acc)

    @pl.when(nv == 0)
    def _():
        y_ref[...] = jnp.zeros_like(y_ref)


def _expert_ffn(xs_rows, block_e, n_valid, w_up, b_glu, b_lin, w_down, b_down):
    n_rows = xs_rows.shape[0]
    bm = EXPERT_BLOCK
    half = SPLIT_CHUNK // 2
    src = jnp.arange(SPLIT_CHUNK, dtype=I32)[:, None]
    dst = jnp.arange(SPLIT_CHUNK, dtype=I32)[None, :]
    sel = jnp.where(dst < half, src == 2 * dst, src == 2 * (dst - half) + 1).astype(BF16)
    wspec = lambda r, c: pl.BlockSpec((1, r, c), lambda b, be, nv: (be[b], 0, 0))
    grid_spec = pltpu.PrefetchScalarGridSpec(
        num_scalar_prefetch=2,
        grid=(n_rows // bm,),
        in_specs=[
            pl.BlockSpec((bm, D_MODEL // 2), lambda b, be, nv: (b, 0)),
            wspec(D_MODEL, 2 * D_FF), wspec(D_FF, D_MODEL),
            pl.BlockSpec((SPLIT_CHUNK, SPLIT_CHUNK), lambda b, be, nv: (0, 0)),
            wspec(1, D_FF), wspec(1, D_FF), wspec(1, D_MODEL),
        ],
        out_specs=pl.BlockSpec((bm, D_MODEL // 2), lambda b, be, nv: (b, 0)),
        scratch_shapes=[pltpu.VMEM((D_MODEL, D_FF), BF16), pltpu.VMEM((D_MODEL, D_FF), BF16),
                        pltpu.VMEM((D_FF, D_MODEL), BF16)],
    )
    return pl.pallas_call(
        _expert_kernel,
        grid_spec=grid_spec,
        out_shape=jax.ShapeDtypeStruct((n_rows, D_MODEL // 2), U32),
        compiler_params=_cparams("arbitrary"),
        name="expert_ffn",
    )(block_e, n_valid, xs_rows, w_up, w_down, sel, b_glu, b_lin, b_down)


def _final_kernel(x1_ref, yg_ref, gt_ref, g_ref, b_ref, *rest):
    o_ref = rest[-1]
    gt = gt_ref[...]
    ffn = jnp.zeros(x1_ref.shape, F32)
    for kk in range(TOP_K):
        ffn = ffn + gt[:, kk:kk + 1] * _unpack_bf16_pairs(yg_ref[kk])
    o_ref[...] = _layer_norm(DEEPNORM_ALPHA * x1_ref[...] + ffn, g_ref[...], b_ref[...])


def _final(x1, yg, gates_t, ln_g, ln_b, src_row0, n_rows, out_rows, dst_row0, out_prev=None):
    tm = FINAL_TILE
    assert src_row0 % tm == 0 and dst_row0 % tm == 0 and n_rows % tm == 0
    src, dst = src_row0 // tm, dst_row0 // tm
    in_specs = [
        pl.BlockSpec((tm, D_MODEL), lambda i: (src + i, 0)),
        pl.BlockSpec((TOP_K, tm, D_MODEL // 2), lambda i: (0, src + i, 0)),
        pl.BlockSpec((tm, TOP_K), lambda i: (src + i, 0)),
        pl.BlockSpec((1, D_MODEL), lambda i: (0, 0)),
        pl.BlockSpec((1, D_MODEL), lambda i: (0, 0)),
    ]
    args = [x1, yg, gates_t, ln_g.reshape(1, -1), ln_b.reshape(1, -1)]
    aliases = {}
    if out_prev is not None:
        in_specs.append(pl.BlockSpec(memory_space=pl.ANY))
        args.append(out_prev)
        aliases = {len(args) - 1: 0}
    return pl.pallas_call(
        _final_kernel,
        grid=(n_rows // tm,),
        in_specs=in_specs,
        out_specs=pl.BlockSpec((tm, D_MODEL), lambda i: (dst + i, 0)),
        out_shape=jax.ShapeDtypeStruct((out_rows, D_MODEL), F32),
        input_output_aliases=aliases,
        compiler_params=_cparams("parallel"),
        name="combine_ln2",
    )(*args)


def _routing_tables(idx, rank, counts, n_blocks):
    bm = EXPERT_BLOCK
    padded = (counts + bm - 1) // bm * bm
    pad_end = jnp.cumsum(padded)
    pad_start = pad_end - padded
    experts = jnp.arange(N_EXPERTS, dtype=I32)

    def lookup(table, e):
        shape = (N_EXPERTS,) + (1,) * e.ndim
        return jnp.sum(jnp.where(e[None] == experts.reshape(shape), table.reshape(shape), 0), axis=0)

    pos = lookup(pad_start, idx) + rank
    blk_row0 = jnp.arange(n_blocks, dtype=I32) * bm
    block_e = jnp.minimum(jnp.sum((pad_end[:, None] <= blk_row0[None, :]).astype(I32), axis=0), N_EXPERTS - 1)
    n_valid = jnp.clip(lookup(counts, block_e) - (blk_row0 - lookup(pad_start, block_e)), 0, bm)
    return pos.astype(I32), block_e.astype(I32), n_valid.astype(I32)


def _moe(pk, idx, rank, counts, w_up, b_glu, b_lin, w_down, b_down):
    n_t = pk.shape[0]
    n_blocks = (n_t * TOP_K) // EXPERT_BLOCK + N_EXPERTS
    pos, block_e, n_valid = _routing_tables(idx, rank, counts, n_blocks)
    pos_chunks = pos.reshape(TOP_K, n_t // SC_CHUNK, SC_CHUNK).transpose(1, 0, 2)
    xs_rows = _sc_dispatch(pk, pos_chunks, n_blocks * EXPERT_BLOCK)
    y_rows = _expert_ffn(xs_rows, block_e, n_valid, w_up, b_glu, b_lin, w_down, b_down)
    return _sc_combine(y_rows, pos_chunks, n_t)


def _rope_tables(seq_max):
    inv = ROPE_THETA ** (-jnp.arange(0, HEAD_DIM, 2, dtype=F32) / HEAD_DIM)
    ang = jnp.arange(seq_max).astype(F32)[:, None] * inv[None, :]
    cos, sin = jnp.cos(ang), jnp.sin(ang)
    reps = LANES // HEAD_DIM
    cos_t = jnp.tile(jnp.concatenate([cos, cos], axis=1), (1, reps))
    sin_t = jnp.tile(jnp.concatenate([-sin, sin], axis=1), (1, reps))
    return cos_t, sin_t


def _encoder_layer(xp, xs, seq_prompt, batch_sample, seq_sample, w_in, w_attn_out, w_four_out, w_o, ln1_g, ln1_b,
                   w_router, b_router, w_up, b_up, w_down, b_down, ln2_g, ln2_b):
    n_p, n_s = xp.shape[0], xs.shape[0]
    assert n_p == seq_prompt and n_s == batch_sample * seq_sample
    assert seq_prompt % (DFT_STAGE1 * 16) == 0 and seq_sample % TOKEN_TILE == 0 and n_p % seq_sample == 0
    n_t = n_p + n_s
    assert n_t % (max(DILATIONS) * ATTN_TILE) == 0 and seq_sample % (max(DILATIONS) * ATTN_SUB) == 0

    w_proj = w_in[:, :PROJ_WIDTH].astype(BF16)
    w_gate = w_in[:, PROJ_WIDTH:].astype(BF16)
    cos_t, sin_t = _rope_tables(max(seq_prompt, seq_sample))
    q_views, k_views, v_views, f = _inproj(xp, xs, w_proj, cos_t, sin_t, seq_sample)

    o_list, l_list = [], []
    for gi, dil in enumerate(DILATIONS):
        o, lse = _attention_group(q_views[gi], k_views[gi], v_views[gi], dil, n_p, seq_sample)
        o_list.append(o)
        l_list.append(lse)

    four_p = _fourier_prompt(f, seq_prompt)
    four_s = _fourier_sample(f, n_p, batch_sample, seq_sample)

    b_glu = b_up[:, None, 0::2]
    b_lin = b_up[:, None, 1::2]
    w_ao, w_fo, w_ob = w_attn_out.astype(BF16), w_four_out.astype(BF16), w_o.astype(BF16)

    tiles = n_t // TOKEN_TILE
    bounds = [tiles * p // MOE_PARTS for p in range(MOE_PARTS + 1)]
    routed = []
    for p in range(MOE_PARTS):
        routed.append(_merge_route(xp, xs, w_gate, o_list, l_list, four_p, four_s, w_ao, w_fo, w_ob, ln1_g, ln1_b,
                                   w_router, b_router, bounds[p], bounds[p + 1] - bounds[p]))
    combined = []
    for x1, pk, idx, gates, rank, cnt in routed:
        combined.append(_moe(pk, idx, rank, cnt[:, 0], w_up, b_glu, b_lin, w_down, b_down[:, None, :]))

    outs = [None, None]
    spans = [(0, n_p), (n_p, n_t)]
    for p, ((x1, pk, idx, gates, rank, cnt), yg) in enumerate(zip(routed, combined)):
        lo, hi = bounds[p] * TOKEN_TILE, bounds[p + 1] * TOKEN_TILE
        gates_t = gates.T
        for which, (a, b) in enumerate(spans):
            s, e = max(lo, a), min(hi, b)
            if s < e:
                outs[which] = _final(x1, yg, gates_t, ln2_g, ln2_b, s - lo, e - s, b - a, s - a, outs[which])
    return outs[0], outs[1]


def kernel(x_prompt, x_sample, w_in, w_attn_out, w_four_out, w_o, ln1_g, ln1_b, w_router, b_router, w_up, b_up,
           w_down, b_down, ln2_g, ln2_b):
    assert w_in.shape[0] == DEPTH
    bp, sp, d = x_prompt.shape
    bs, ss, _ = x_sample.shape
    assert bp == 1 and d == D_MODEL
    y_p, y_s = _encoder_layer(
        x_prompt.reshape(sp, d), x_sample.reshape(bs * ss, d), sp, bs, ss,
        w_in[0], w_attn_out[0], w_four_out[0], w_o[0], ln1_g[0], ln1_b[0], w_router[0], b_router[0],
        w_up[0], b_up[0], w_down[0], b_down[0], ln2_g[0], ln2_b[0])
    return y_p.reshape(x_prompt.shape), y_s.reshape(x_sample.shape)
```

```python
import functools
import math

import jax
import jax.numpy as jnp
from jax import lax
from jax.experimental import pallas as pl
from jax.experimental.pallas import tpu as pltpu
from jax.experimental.pallas import tpu_sc as plsc

F32 = jnp.float32
BF16 = jnp.bfloat16
I32 = jnp.int32
U32 = jnp.uint32

D_MODEL = 1024
HEAD_DIM = 64
HEADS_PER_GROUP = 4
GROUP_WIDTH = HEADS_PER_GROUP * HEAD_DIM
DILATIONS = (1, 4, 16)
HALF_WINDOW = 64
N_GROUPS = len(DILATIONS)
QKV_WIDTH = N_GROUPS * GROUP_WIDTH
F_GROUPS = 4
F_GROUP_DIM = 64
F_WIDTH = F_GROUPS * F_GROUP_DIM
PROJ_WIDTH = 3 * QKV_WIDTH + F_WIDTH
N_EXPERTS = 32
TOP_K = 4
D_FF = 1024
SWIGLU_LIMIT = 7.0
SWIGLU_ALPHA = 1.702
LN_EPS = 1e-5
ROPE_THETA = 10000.0
NEG_INF = -1e30
DEPTH = 1
DEEPNORM_ALPHA = (2 * DEPTH) ** 0.25

LANES = 128
TOKEN_TILE = 512
FINAL_TILE = 1024
ATTN_TILE = 1024
ATTN_SUB = 128
EXPERT_BLOCK = 512
FF_CHUNK = 1024
SPLIT_CHUNK = 256
DFT_STAGE1 = 128
DFT_STAGE2_BATCH = 8
DFT_SMALL_STAGE1 = 8
DFT_SMALL_BATCH = 4
SC_CHUNK = 64
MOE_PARTS = 2
VMEM_LIMIT = 56 * 1024 * 1024


def _cparams(*sem):
    return pltpu.CompilerParams(dimension_semantics=sem, vmem_limit_bytes=VMEM_LIMIT)


def _to_dilated_view(stage_ref, out_ref, val, dil):
    if dil == 1:
        out_ref[...] = val.astype(BF16)
        return
    rows = val.shape[0] // dil
    for c in range(GROUP_WIDTH // LANES):
        stage_ref[c] = val[:, c * LANES:(c + 1) * LANES]
    for r in range(dil):
        for c in range(GROUP_WIDTH // LANES):
            lo = r * GROUP_WIDTH + c * LANES
            out_ref[:, lo:lo + LANES] = stage_ref[c, pl.ds(r, rows, stride=dil), :].astype(BF16)


def _inproj_kernel(xp_ref, xs_ref, w_ref, cos_ref, sin_ref, *refs, n_prompt_tiles):
    qkv_refs = refs[:3 * N_GROUPS]
    f_ref = refs[3 * N_GROUPS]
    stage_refs = refs[3 * N_GROUPS + 1:]
    i = pl.program_id(0)
    x = jnp.where(i < n_prompt_tiles, xp_ref[...], xs_ref[...]).astype(BF16)
    proj = jnp.dot(x, w_ref[...], preferred_element_type=F32)
    cos = cos_ref[...]
    sin = sin_ref[...]
    lane = lax.broadcasted_iota(I32, cos.shape, 1)
    first_half = (lane % HEAD_DIM) < (HEAD_DIM // 2)

    def rope(t):
        parts = []
        for c in range(GROUP_WIDTH // LANES):
            tc = t[:, c * LANES:(c + 1) * LANES]
            swapped = jnp.where(first_half, pltpu.roll(tc, LANES - HEAD_DIM // 2, 1), pltpu.roll(tc, HEAD_DIM // 2, 1))
            parts.append(tc * cos + swapped * sin)
        return jnp.concatenate(parts, axis=1)

    for gi, dil in enumerate(DILATIONS):
        sl = slice(gi * GROUP_WIDTH, (gi + 1) * GROUP_WIDTH)
        qg = rope(proj[:, sl]) * (HEAD_DIM ** -0.5)
        kg = rope(proj[:, QKV_WIDTH + gi * GROUP_WIDTH:QKV_WIDTH + (gi + 1) * GROUP_WIDTH])
        vg = proj[:, 2 * QKV_WIDTH + gi * GROUP_WIDTH:2 * QKV_WIDTH + (gi + 1) * GROUP_WIDTH]
        for which, val in enumerate((qg, kg, vg)):
            _to_dilated_view(stage_refs[which * N_GROUPS + gi], qkv_refs[which * N_GROUPS + gi], val, dil)
    f_ref[...] = proj[:, 3 * QKV_WIDTH:PROJ_WIDTH].astype(BF16)


def _inproj(xp, xs, w_proj, cos_t, sin_t, seq_sample):
    n_p, n_s = xp.shape[0], xs.shape[0]
    n_t = n_p + n_s
    tm = TOKEN_TILE
    npt = n_p // tm
    tiles_per_sample_seq = seq_sample // tm

    def table_idx(i):
        return (jnp.where(i < npt, i, (i - npt) % tiles_per_sample_seq), 0)

    view_specs = [pl.BlockSpec((tm // d, d * GROUP_WIDTH), lambda i: (i, 0)) for d in DILATIONS] * 3
    view_shapes = [jax.ShapeDtypeStruct((n_t // d, d * GROUP_WIDTH), BF16) for d in DILATIONS] * 3
    outs = pl.pallas_call(
        functools.partial(_inproj_kernel, n_prompt_tiles=npt),
        grid=(n_t // tm,),
        in_specs=[
            pl.BlockSpec((tm, D_MODEL), lambda i: (jnp.minimum(i, npt - 1), 0)),
            pl.BlockSpec((tm, D_MODEL), lambda i: (jnp.maximum(i - npt, 0), 0)),
            pl.BlockSpec((D_MODEL, PROJ_WIDTH), lambda i: (0, 0)),
            pl.BlockSpec((tm, LANES), table_idx),
            pl.BlockSpec((tm, LANES), table_idx),
        ],
        out_specs=view_specs + [pl.BlockSpec((tm, F_WIDTH), lambda i: (i, 0))],
        out_shape=view_shapes + [jax.ShapeDtypeStruct((n_t, F_WIDTH), BF16)],
        scratch_shapes=[pltpu.VMEM((GROUP_WIDTH // LANES, tm, LANES), F32)] * (3 * N_GROUPS),
        compiler_params=_cparams("parallel"),
        name="inproj_rope",
    )(xp, xs, w_proj, cos_t, sin_t)
    q_views, k_views, v_views = outs[0:N_GROUPS], outs[N_GROUPS:2 * N_GROUPS], outs[2 * N_GROUPS:3 * N_GROUPS]
    return q_views, k_views, v_views, outs[3 * N_GROUPS]


def _attn_kernel(q_ref, kl_ref, km_ref, kr_ref, vl_ref, vm_ref, vr_ref, o_ref, lse_ref, *, prompt_rows, sample_rows):
    i = pl.program_id(1)
    kext = jnp.concatenate([kl_ref[...], km_ref[...], kr_ref[...]], axis=0)
    vext = jnp.concatenate([vl_ref[...], vm_ref[...], vr_ref[...]], axis=0)
    kw = ATTN_SUB + 2 * HALF_WINDOW
    a = lax.broadcasted_iota(I32, (ATTN_SUB, kw), 0)
    c = lax.broadcasted_iota(I32, (ATTN_SUB, kw), 1)
    in_band = jnp.abs(c - HALF_WINDOW - a) <= HALF_WINDOW
    head_of_lane = lax.broadcasted_iota(I32, (1, GROUP_WIDTH), 1) // HEAD_DIM
    for j in range(ATTN_TILE // ATTN_SUB):
        r0 = i * ATTN_TILE + j * ATTN_SUB
        in_prompt = r0 < prompt_rows
        b = jnp.maximum(r0 - prompt_rows, 0) // sample_rows
        lo = jnp.where(in_prompt, 0, prompt_rows + b * sample_rows)
        hi = jnp.where(in_prompt, prompt_rows, prompt_rows + (b + 1) * sample_rows)
        key_row = r0 - HALF_WINDOW + c
        valid = in_band & (key_row >= lo) & (key_row < hi)
        qj = q_ref[j * ATTN_SUB:(j + 1) * ATTN_SUB, :]
        kj = kext[j * ATTN_SUB:j * ATTN_SUB + kw, :]
        vj = vext[j * ATTN_SUB:j * ATTN_SUB + kw, :]
        q4 = jnp.concatenate([jnp.where(head_of_lane == h, qj, jnp.zeros_like(qj)) for h in range(HEADS_PER_GROUP)],
                             axis=0)
        s4 = lax.dot_general(q4, kj, (((1,), (1,)), ((), ())), preferred_element_type=F32)
        ps, scales, lses = [], [], []
        for h in range(HEADS_PER_GROUP):
            s = jnp.where(valid, s4[h * ATTN_SUB:(h + 1) * ATTN_SUB], NEG_INF)
            m = jnp.max(s, axis=1, keepdims=True)
            p = jnp.exp(s - m)
            l = jnp.sum(p, axis=1, keepdims=True)
            ps.append(p.astype(BF16))
            scales.append(1.0 / l)
            lses.append(m + jnp.log(l))
        o4 = jnp.dot(jnp.concatenate(ps, axis=0), vj, preferred_element_type=F32)
        acc = jnp.zeros((ATTN_SUB, GROUP_WIDTH), F32)
        lse_full = jnp.zeros((ATTN_SUB, GROUP_WIDTH), F32)
        for h in range(HEADS_PER_GROUP):
            mine = head_of_lane == h
            acc = jnp.where(mine, o4[h * ATTN_SUB:(h + 1) * ATTN_SUB] * scales[h], acc)
            lse_full = jnp.where(mine, lses[h], lse_full)
        o_ref[j * ATTN_SUB:(j + 1) * ATTN_SUB, :] = acc.astype(BF16)
        lse_ref[j * ATTN_SUB:(j + 1) * ATTN_SUB, :] = lse_full


def _attention_group(qv, kv, vv, dil, n_prompt, seq_sample):
    rows = qv.shape[0]
    tq = ATTN_TILE
    halo_per_tile = tq // HALF_WINDOW
    n_halo_blocks = rows // HALF_WINDOW
    main = pl.BlockSpec((tq, GROUP_WIDTH), lambda r, i: (i, r))
    left = pl.BlockSpec((HALF_WINDOW, GROUP_WIDTH), lambda r, i: (jnp.maximum(i * halo_per_tile - 1, 0), r))
    right = pl.BlockSpec((HALF_WINDOW, GROUP_WIDTH),
                         lambda r, i: (jnp.minimum((i + 1) * halo_per_tile, n_halo_blocks - 1), r))
    return pl.pallas_call(
        functools.partial(_attn_kernel, prompt_rows=n_prompt // dil, sample_rows=seq_sample // dil),
        grid=(dil, rows // tq),
        in_specs=[main, left, main, right, left, main, right],
        out_specs=[main, main],
        out_shape=[jax.ShapeDtypeStruct((rows, dil * GROUP_WIDTH), BF16),
                   jax.ShapeDtypeStruct((rows, dil * GROUP_WIDTH), F32)],
        compiler_params=_cparams("parallel", "parallel"),
        name=f"dilated_attention_{dil}",
    )(qv, kv, kv, kv, vv, vv, vv)


def _angle_table(idx, period):
    return (idx % period).astype(F32) * (2.0 * math.pi / period)


def _channel_dft(seq_len):
    c = jnp.arange(F_GROUP_DIM, dtype=I32)
    th = _angle_table(c[:, None] * c[None, :], F_GROUP_DIM)
    scale = (seq_len * F_GROUP_DIM) ** -0.5
    eye = jnp.eye(F_GROUPS, dtype=F32)
    cb = jnp.kron(eye, jnp.cos(th)) * scale
    sb = jnp.kron(eye, jnp.sin(th)) * scale
    return jnp.concatenate([cb, sb], axis=0).astype(BF16)


def _stage2_matrices(s1, s2):
    n2 = jnp.arange(s2, dtype=I32)[None, :]
    th_t = _angle_table(jnp.arange(s1, dtype=I32)[:, None] * n2, s1 * s2)
    th_f = _angle_table(jnp.arange(s2, dtype=I32)[:, None] * n2, s2)
    tc, ts = jnp.cos(th_t)[:, None, :], jnp.sin(th_t)[:, None, :]
    fc, fs = jnp.cos(th_f)[None, :, :], jnp.sin(th_f)[None, :, :]
    gr = tc * fc - ts * fs
    gi = -(ts * fc + tc * fs)
    return jnp.concatenate([jnp.concatenate([gr, -gi], axis=2), jnp.concatenate([gi, gr], axis=2)], axis=1).astype(BF16)


def _dft_stage1_kernel(m_ref, x_ref, a_ref):
    a_ref[...] = jnp.dot(m_ref[...], x_ref[...], preferred_element_type=F32).astype(BF16)


def _dft_stage2_kernel(g_ref, a_ref, cs_ref, o_ref):
    s2 = a_ref.shape[2]
    for kk in range(g_ref.shape[0]):
        a = jnp.concatenate([a_ref[0, kk], a_ref[1, kk]], axis=0)
        x = jnp.dot(g_ref[kk], a, preferred_element_type=F32)
        xr = x[:s2].astype(BF16)
        xi = x[s2:].astype(BF16)
        o_ref[:, kk * F_WIDTH:(kk + 1) * F_WIDTH] = (
            jnp.dot(xr, cs_ref[:F_WIDTH, :], preferred_element_type=F32)
            + jnp.dot(xi, cs_ref[F_WIDTH:, :], preferred_element_type=F32)).astype(BF16)


def _fourier_prompt(f, seq):
    s1 = DFT_STAGE1
    s2 = seq // s1
    n1 = jnp.arange(s1, dtype=I32)
    th1 = _angle_table(n1[:, None] * n1[None, :], s1)
    m1 = jnp.concatenate([jnp.cos(th1), -jnp.sin(th1)], axis=0).astype(BF16)
    g = _stage2_matrices(s1, s2)
    cs = _channel_dft(seq)

    cols = s2 * F_WIDTH
    fv = f[:seq].reshape(s1, cols)
    tn = min(cols, 4096)
    a = pl.pallas_call(
        _dft_stage1_kernel,
        grid=(cols // tn,),
        in_specs=[pl.BlockSpec((2 * s1, s1), lambda j: (0, 0)), pl.BlockSpec((s1, tn), lambda j: (0, j))],
        out_specs=pl.BlockSpec((2 * s1, tn), lambda j: (0, j)),
        out_shape=jax.ShapeDtypeStruct((2 * s1, cols), BF16),
        compiler_params=_cparams("parallel"),
        name="dft_stage1",
    )(m1, fv)
    a4 = a.reshape(2, s1, s2, F_WIDTH)
    kb = DFT_STAGE2_BATCH
    out = pl.pallas_call(
        _dft_stage2_kernel,
        grid=(s1 // kb,),
        in_specs=[pl.BlockSpec((kb, 2 * s2, 2 * s2), lambda k: (k, 0, 0)),
                  pl.BlockSpec((2, kb, s2, F_WIDTH), lambda k: (0, k, 0, 0)),
                  pl.BlockSpec((2 * F_WIDTH, F_WIDTH), lambda k: (0, 0))],
        out_specs=pl.BlockSpec((s2, kb * F_WIDTH), lambda k: (0, k)),
        out_shape=jax.ShapeDtypeStruct((s2, s1 * F_WIDTH), BF16),
        compiler_params=_cparams("parallel"),
        name="dft_stage2",
    )(g, a4, cs)
    return out.reshape(seq, F_WIDTH)


def _dft_small_kernel(g_ref, x_ref, cs_ref, o_ref, stage_ref):
    s1 = DFT_SMALL_STAGE1
    assert s1 == 8
    s2 = x_ref.shape[1] // s1
    root = math.sqrt(0.5)
    n_seq = x_ref.shape[0]
    a_all = []
    for b in range(n_seq):
        x0, x1, x2, x3, x4, x5, x6, x7 = [x_ref[b, n1 * s2:(n1 + 1) * s2, :].astype(F32) for n1 in range(s1)]
        u, v, p, q = x0 - x4, x2 - x6, x1 - x5, x3 - x7
        e_sum, e_dif = (x0 + x4) + (x2 + x6), (x0 + x4) - (x2 + x6)
        o_sum, o_dif = (x1 + x5) + (x3 + x7), (x1 + x5) - (x3 + x7)
        rpq_m, rpq_p = root * (p - q), root * (p + q)
        zero = jnp.zeros_like(x0)
        a_re = [e_sum + o_sum, u + rpq_m, e_dif, u - rpq_m, e_sum - o_sum, u - rpq_m, e_dif, u + rpq_m]
        im1, im3 = -v - rpq_p, v - rpq_p
        a_im = [zero, im1, -o_dif, im3, zero, -im3, o_dif, -im1]
        a_all.append([jnp.concatenate([a_re[k1].astype(BF16), a_im[k1].astype(BF16)], axis=0) for k1 in range(s1)])

    for k1 in range(s1):
        a = jnp.concatenate([a_all[b][k1] for b in range(n_seq)], axis=1)
        x = jnp.dot(g_ref[k1], a, preferred_element_type=F32)
        for b in range(n_seq):
            xb = x[:, b * F_WIDTH:(b + 1) * F_WIDTH]
            out = (jnp.dot(xb[:s2].astype(BF16), cs_ref[:F_WIDTH, :], preferred_element_type=F32)
                   + jnp.dot(xb[s2:].astype(BF16), cs_ref[F_WIDTH:, :], preferred_element_type=F32))
            for c in range(F_WIDTH // LANES):
                stage_ref[b * (F_WIDTH // LANES) + c, pl.ds(k1, s2, stride=s1), :] = out[:, c * LANES:(c + 1) * LANES]
    for b in range(n_seq):
        o_ref[b] = jnp.concatenate([stage_ref[b * (F_WIDTH // LANES) + c] for c in range(F_WIDTH // LANES)],
                                   axis=1).astype(BF16)


def _fourier_sample(f, n_prompt, batch, seq):
    s1 = DFT_SMALL_STAGE1
    s2 = seq // s1
    g = _stage2_matrices(s1, s2)
    cs = _channel_dft(seq)
    f3 = f.reshape(f.shape[0] // seq, seq, F_WIDTH)
    first = n_prompt // seq
    nb = DFT_SMALL_BATCH
    assert batch % nb == 0 and first % nb == 0
    return pl.pallas_call(
        _dft_small_kernel,
        grid=(batch // nb,),
        in_specs=[pl.BlockSpec((s1, 2 * s2, 2 * s2), lambda b: (0, 0, 0)),
                  pl.BlockSpec((nb, seq, F_WIDTH), lambda b: (first // nb + b, 0, 0)),
                  pl.BlockSpec((2 * F_WIDTH, F_WIDTH), lambda b: (0, 0))],
        out_specs=pl.BlockSpec((nb, seq, F_WIDTH), lambda b: (b, 0, 0)),
        out_shape=jax.ShapeDtypeStruct((batch, seq, F_WIDTH), BF16),
        scratch_shapes=[pltpu.VMEM((nb * (F_WIDTH // LANES), seq, LANES), F32)],
        compiler_params=_cparams("parallel"),
        name="dft_small",
    )(g, f3, cs).reshape(batch * seq, F_WIDTH)


def _pack_bf16_pairs(x):
    w = x.shape[1] // 2
    bits = lax.bitcast_convert_type(x.astype(BF16).astype(F32), U32)
    return (bits[:, :w] >> 16) | (bits[:, w:] & jnp.uint32(0xFFFF0000))


def _unpack_bf16_pairs(u):
    lo = lax.bitcast_convert_type(u << 16, F32)
    hi = lax.bitcast_convert_type(u & jnp.uint32(0xFFFF0000), F32)
    return jnp.concatenate([lo, hi], axis=1)


def _layer_norm(h, g, b):
    mu = jnp.mean(h, axis=-1, keepdims=True)
    d = h - mu
    var = jnp.mean(d * d, axis=-1, keepdims=True)
    return d * lax.rsqrt(var + LN_EPS) * g + b


def _from_dilated_view(stage_ref, blk_ref, dil):
    if dil == 1:
        return blk_ref[...].astype(F32)
    rows = blk_ref.shape[0]
    for r in range(dil):
        for c in range(GROUP_WIDTH // LANES):
            lo = r * GROUP_WIDTH + c * LANES
            stage_ref[c, pl.ds(r, rows, stride=dil), :] = blk_ref[:, lo:lo + LANES].astype(F32)
    return jnp.concatenate([stage_ref[c] for c in range(GROUP_WIDTH // LANES)], axis=1)


def _merge_kernel(xp_ref, xs_ref, wg_ref, o1_ref, o2_ref, o3_ref, l1_ref, l2_ref, l3_ref, fp_ref, fs_ref,
                  wao_ref, wfo_ref, wo_ref, g_ref, b_ref, wr2_ref, br_ref, tri_ref,
                  x1_ref, pk_ref, idx_ref, gate_ref, rank_ref, cnt_ref, run_ref, *stage_refs, n_prompt_tiles):
    i = pl.program_id(0)

    @pl.when(i == 0)
    def _():
        run_ref[...] = jnp.zeros_like(run_ref)

    is_prompt = i < n_prompt_tiles
    x = jnp.where(is_prompt, xp_ref[...], xs_ref[...])
    gpre = jnp.dot(x.astype(BF16), wg_ref[...], preferred_element_type=F32)
    o1, o2, o3 = [_from_dilated_view(stage_refs[gi], ref, d)
                  for gi, (ref, d) in enumerate(zip((o1_ref, o2_ref, o3_ref), DILATIONS))]
    l1, l2, l3 = [_from_dilated_view(stage_refs[N_GROUPS + gi], ref, d)
                  for gi, (ref, d) in enumerate(zip((l1_ref, l2_ref, l3_ref), DILATIONS))]
    m = jnp.maximum(jnp.maximum(l1, l2), l3)
    e1, e2, e3 = jnp.exp(l1 - m), jnp.exp(l2 - m), jnp.exp(l3 - m)
    attn = (e1 * o1 + e2 * o2 + e3 * o3) * (1.0 / (e1 + e2 + e3))
    a = jnp.dot(attn.astype(BF16), wao_ref[...], preferred_element_type=F32)
    four = jnp.where(is_prompt, fp_ref[...], fs_ref[...])
    ff = jnp.dot(four, wfo_ref[...], preferred_element_type=F32)
    merged = jax.nn.sigmoid(gpre[:, :D_MODEL]) * a + jax.nn.sigmoid(gpre[:, D_MODEL:]) * ff
    mix = jnp.dot(merged.astype(BF16), wo_ref[...], preferred_element_type=F32)
    x1 = _layer_norm(DEEPNORM_ALPHA * x + mix, g_ref[...], b_ref[...])
    x1_ref[...] = x1
    pk_ref[...] = _pack_bf16_pairs(x1)

    xh = x1.astype(BF16)
    xl = (x1 - xh.astype(F32)).astype(BF16)
    tm = x1.shape[0]
    cross = lax.dot_general(wr2_ref[...], jnp.concatenate([xh, xl], axis=0), (((1,), (1,)), ((), ())),
                            preferred_element_type=F32)
    logits = ((cross[:N_EXPERTS, :tm] + cross[N_EXPERTS:, :tm])
              + (cross[:N_EXPERTS, tm:] + cross[N_EXPERTS:, tm:])) + br_ref[...]
    eio = lax.broadcasted_iota(I32, (N_EXPERTS, tm), 0)
    work = logits
    vals, sels, picks = [], [], []
    for _ in range(TOP_K):
        mk = jnp.max(work, axis=0, keepdims=True)
        ik = jnp.min(jnp.where(work == mk, eio, N_EXPERTS), axis=0, keepdims=True)
        sel = eio == ik
        vals.append(mk)
        sels.append(sel)
        picks.append(ik)
        work = jnp.where(sel, -jnp.inf, work)
    es = [jnp.exp(v - vals[0]) for v in vals]
    inv_den = 1.0 / (es[0] + es[1] + es[2] + es[3])
    chosen = (sels[0] | sels[1] | sels[2] | sels[3])
    chosen_f = chosen.astype(F32)
    prefix = jnp.dot(chosen_f.astype(BF16), tri_ref[...], preferred_element_type=F32)
    before = prefix + run_ref[...]
    for kk in range(TOP_K):
        idx_ref[kk:kk + 1, :] = picks[kk]
        gate_ref[kk:kk + 1, :] = es[kk] * inv_den
        rank_ref[kk:kk + 1, :] = jnp.sum(jnp.where(sels[kk], before, 0.0), axis=0, keepdims=True).astype(I32)
    run_ref[...] = run_ref[...] + jnp.sum(chosen_f, axis=1, keepdims=True)
    cnt_ref[...] = jnp.broadcast_to(run_ref[...], cnt_ref.shape).astype(I32)


def _merge_route(xp, xs, w_gate, o_list, l_list, four_p, four_s, w_ao, w_fo, w_o, ln_g, ln_b, w_r, b_r, tile0,
                 n_tiles):
    n_p = xp.shape[0]
    tm = TOKEN_TILE
    n_t = n_tiles * tm
    npt = n_p // tm
    w_rt = w_r.T
    w_rh = w_rt.astype(BF16)
    w_rl = (w_rt - w_rh.astype(F32)).astype(BF16)
    tri = (jnp.arange(tm)[:, None] < jnp.arange(tm)[None, :]).astype(BF16)
    row = lambda w: pl.BlockSpec((tm, w), lambda i: (i, 0))
    prompt_row = lambda w: pl.BlockSpec((tm, w), lambda i: (jnp.clip(tile0 + i, 0, npt - 1), 0))
    sample_row = lambda w: pl.BlockSpec((tm, w), lambda i: (jnp.maximum(tile0 + i - npt, 0), 0))
    full = lambda r, c: pl.BlockSpec((r, c), lambda i: (0, 0))
    lane_row = pl.BlockSpec((TOP_K, tm), lambda i: (0, i))
    views = [pl.BlockSpec((tm // d, d * GROUP_WIDTH), lambda i: (tile0 + i, 0)) for d in DILATIONS]
    return pl.pallas_call(
        functools.partial(_merge_kernel, n_prompt_tiles=npt - tile0),
        grid=(n_tiles,),
        in_specs=[
            prompt_row(D_MODEL), sample_row(D_MODEL),
            full(D_MODEL, 2 * D_MODEL),
            *views, *views,
            prompt_row(F_WIDTH), sample_row(F_WIDTH),
            full(GROUP_WIDTH, D_MODEL), full(F_WIDTH, D_MODEL), full(D_MODEL, D_MODEL),
            full(1, D_MODEL), full(1, D_MODEL),
            full(2 * N_EXPERTS, D_MODEL), full(N_EXPERTS, 1),
            full(tm, tm),
        ],
        out_specs=[row(D_MODEL), row(D_MODEL // 2), lane_row, lane_row, lane_row, full(N_EXPERTS, LANES)],
        out_shape=[
            jax.ShapeDtypeStruct((n_t, D_MODEL), F32),
            jax.ShapeDtypeStruct((n_t, D_MODEL // 2), U32),
            jax.ShapeDtypeStruct((TOP_K, n_t), I32),
            jax.ShapeDtypeStruct((TOP_K, n_t), F32),
            jax.ShapeDtypeStruct((TOP_K, n_t), I32),
            jax.ShapeDtypeStruct((N_EXPERTS, LANES), I32),
        ],
        scratch_shapes=[pltpu.VMEM((N_EXPERTS, 1), F32)]
        + [pltpu.VMEM((GROUP_WIDTH // LANES, tm, LANES), F32)] * (2 * N_GROUPS),
        compiler_params=_cparams("arbitrary"),
        name="merge_ln_route",
    )(xp, xs, w_gate, *o_list, *l_list, four_p, four_s, w_ao, w_fo, w_o, ln_g.reshape(1, -1), ln_b.reshape(1, -1),
      jnp.concatenate([w_rh, w_rl], axis=0), b_r.reshape(-1, 1), tri)


def _sc_workers():
    info = plsc.get_sparse_core_info()
    return info.num_cores, info.num_cores * info.num_subcores


def _sc_dispatch(rows_src, pos_chunks, n_rows_out):
    n_t, w = rows_src.shape
    n_cores, n_workers = _sc_workers()
    chunks_per_worker = n_t // (n_workers * SC_CHUNK)
    mesh = plsc.VectorSubcoreMesh(core_axis_name="c", subcore_axis_name="s")

    @functools.partial(
        pl.kernel, mesh=mesh,
        out_type=jax.ShapeDtypeStruct((n_rows_out, w), rows_src.dtype),
        scratch_types=[pltpu.VMEM((TOP_K, SC_CHUNK), I32), pltpu.VMEM((SC_CHUNK, w), rows_src.dtype)],
    )
    def dispatch(src_hbm, pos_hbm, out_hbm, idx_v, rows_v):
        wid = lax.axis_index("s") * n_cores + lax.axis_index("c")

        @pl.loop(0, chunks_per_worker)
        def _(j):
            chunk = wid * chunks_per_worker + j
            pltpu.sync_copy(pos_hbm.at[chunk], idx_v)
            pltpu.sync_copy(src_hbm.at[pl.ds(chunk * SC_CHUNK, SC_CHUNK)], rows_v)
            for kk in range(TOP_K):
                pltpu.sync_copy(rows_v, out_hbm.at[idx_v.at[kk]])

    return dispatch(rows_src, pos_chunks)


def _sc_combine(rows_src, pos_chunks, n_t):
    w = rows_src.shape[1]
    n_cores, n_workers = _sc_workers()
    chunks_per_worker = n_t // (n_workers * SC_CHUNK)
    mesh = plsc.VectorSubcoreMesh(core_axis_name="c", subcore_axis_name="s")

    @functools.partial(
        pl.kernel, mesh=mesh,
        out_type=jax.ShapeDtypeStruct((TOP_K, n_t, w), rows_src.dtype),
        scratch_types=[pltpu.VMEM((TOP_K, SC_CHUNK), I32), pltpu.VMEM((SC_CHUNK, w), rows_src.dtype)],
    )
    def combine(src_hbm, pos_hbm, out_hbm, idx_v, rows_v):
        wid = lax.axis_index("s") * n_cores + lax.axis_index("c")

        @pl.loop(0, chunks_per_worker)
        def _(j):
            chunk = wid * chunks_per_worker + j
            pltpu.sync_copy(pos_hbm.at[chunk], idx_v)
            for kk in range(TOP_K):
                pltpu.sync_copy(src_hbm.at[idx_v.at[kk]], rows_v)
                pltpu.sync_copy(rows_v, out_hbm.at[kk, pl.ds(chunk * SC_CHUNK, SC_CHUNK)])

    return combine(rows_src, pos_chunks)


def _expert_kernel(be_ref, nv_ref, x_ref, *refs, prepare):
    if prepare:
        wu_ref, wdn_ref, p_ref, bg_ref, bl_ref, bd_ref, y_ref, wg_ref, wl_ref, wd_ref = refs
    else:
        wg_ref, wl_ref, wd_ref, bg_ref, bl_ref, bd_ref, y_ref = refs
    blk = pl.program_id(0)
    nv = nv_ref[blk]

    if prepare:
        new_expert = (blk == 0) | (be_ref[blk] != be_ref[jnp.maximum(blk - 1, 0)])

        @pl.when(new_expert)
        def _():
            half = SPLIT_CHUNK // 2
            for c in range(wu_ref.shape[2] // SPLIT_CHUNK):
                w = wu_ref[0, :, c * SPLIT_CHUNK:(c + 1) * SPLIT_CHUNK].astype(BF16)
                r = jnp.dot(w, p_ref[...], preferred_element_type=F32)
                wg_ref[0, :, c * half:(c + 1) * half] = r[:, :half].astype(BF16)
                wl_ref[0, :, c * half:(c + 1) * half] = r[:, half:].astype(BF16)
            wd_ref[...] = wdn_ref[...].astype(BF16)

    @pl.when(nv > 0)
    def _():
        x = _unpack_bf16_pairs(x_ref[...])
        rows = lax.broadcasted_iota(I32, (x.shape[0], 1), 0)
        x = jnp.where(rows < nv, x, 0.0).astype(BF16)
        acc = jnp.zeros((x.shape[0], D_MODEL), F32) + bd_ref[0]
        for c in range(D_FF // FF_CHUNK):
            sl = slice(c * FF_CHUNK, (c + 1) * FF_CHUNK)
            hg = jnp.dot(x, wg_ref[0, :, sl], preferred_element_type=F32) + bg_ref[0, :, sl]
            hl = jnp.dot(x, wl_ref[0, :, sl], preferred_element_type=F32) + bl_ref[0, :, sl]
            glu = jnp.minimum(hg, SWIGLU_LIMIT)
            lin = jnp.clip(hl, -SWIGLU_LIMIT, SWIGLU_LIMIT)
            act = glu * jax.nn.sigmoid(SWIGLU_ALPHA * glu) * (lin + 1.0)
            acc = acc + jnp.dot(act.astype(BF16), wd_ref[0, sl, :], preferred_element_type=F32)
        y_ref[...] = _pack_bf16_pairs(acc)

    @pl.when(nv == 0)
    def _():
        y_ref[...] = jnp.zeros_like(y_ref)


def _expert_ffn(xs_rows, block_e, n_valid, weights, b_glu, b_lin, b_down):
    n_rows = xs_rows.shape[0]
    bm = EXPERT_BLOCK
    prepare = len(weights) == 2
    wspec = lambda r, c: pl.BlockSpec((1, r, c), lambda b, be, nv: (be[b], 0, 0))
    rows_spec = pl.BlockSpec((bm, D_MODEL // 2), lambda b, be, nv: (b, 0))
    bias_specs = [wspec(1, D_FF), wspec(1, D_FF), wspec(1, D_MODEL)]
    y_shape = jax.ShapeDtypeStruct((n_rows, D_MODEL // 2), U32)
    if prepare:
        half = SPLIT_CHUNK // 2
        src = jnp.arange(SPLIT_CHUNK, dtype=I32)[:, None]
        dst = jnp.arange(SPLIT_CHUNK, dtype=I32)[None, :]
        sel = jnp.where(dst < half, src == 2 * dst, src == 2 * (dst - half) + 1).astype(BF16)
        operands = (*weights, sel)
        weight_specs = [wspec(D_MODEL, 2 * D_FF), wspec(D_FF, D_MODEL),
                        pl.BlockSpec((SPLIT_CHUNK, SPLIT_CHUNK), lambda b, be, nv: (0, 0))]
        prepared = jax.ShapeDtypeStruct((N_EXPERTS, D_MODEL, D_FF), BF16)
        out_specs = [rows_spec, wspec(D_MODEL, D_FF), wspec(D_MODEL, D_FF), wspec(D_FF, D_MODEL)]
        out_shape = [y_shape, prepared, prepared, prepared]
    else:
        operands = tuple(weights)
        weight_specs = [wspec(D_MODEL, D_FF), wspec(D_MODEL, D_FF), wspec(D_FF, D_MODEL)]
        out_specs, out_shape = rows_spec, y_shape
    grid_spec = pltpu.PrefetchScalarGridSpec(
        num_scalar_prefetch=2,
        grid=(n_rows // bm,),
        in_specs=[rows_spec, *weight_specs, *bias_specs],
        out_specs=out_specs,
    )
    out = pl.pallas_call(
        functools.partial(_expert_kernel, prepare=prepare),
        grid_spec=grid_spec,
        out_shape=out_shape,
        compiler_params=_cparams("arbitrary"),
        name="expert_ffn",
    )(block_e, n_valid, xs_rows, *operands, b_glu, b_lin, b_down)
    return (out[0], tuple(out[1:])) if prepare else (out, tuple(weights))


def _final_kernel(x1_ref, yg_ref, gt_ref, g_ref, b_ref, *rest):
    o_ref = rest[-1]
    gt = gt_ref[...]
    ffn = jnp.zeros(x1_ref.shape, F32)
    for kk in range(TOP_K):
        ffn = ffn + gt[:, kk:kk + 1] * _unpack_bf16_pairs(yg_ref[kk])
    o_ref[...] = _layer_norm(DEEPNORM_ALPHA * x1_ref[...] + ffn, g_ref[...], b_ref[...])


def _final(x1, yg, gates_t, ln_g, ln_b, src_row0, n_rows, out_rows, dst_row0, out_prev=None):
    tm = FINAL_TILE
    assert src_row0 % tm == 0 and dst_row0 % tm == 0 and n_rows % tm == 0
    src, dst = src_row0 // tm, dst_row0 // tm
    in_specs = [
        pl.BlockSpec((tm, D_MODEL), lambda i: (src + i, 0)),
        pl.BlockSpec((TOP_K, tm, D_MODEL // 2), lambda i: (0, src + i, 0)),
        pl.BlockSpec((tm, TOP_K), lambda i: (src + i, 0)),
        pl.BlockSpec((1, D_MODEL), lambda i: (0, 0)),
        pl.BlockSpec((1, D_MODEL), lambda i: (0, 0)),
    ]
    args = [x1, yg, gates_t, ln_g.reshape(1, -1), ln_b.reshape(1, -1)]
    aliases = {}
    if out_prev is not None:
        in_specs.append(pl.BlockSpec(memory_space=pl.ANY))
        args.append(out_prev)
        aliases = {len(args) - 1: 0}
    return pl.pallas_call(
        _final_kernel,
        grid=(n_rows // tm,),
        in_specs=in_specs,
        out_specs=pl.BlockSpec((tm, D_MODEL), lambda i: (dst + i, 0)),
        out_shape=jax.ShapeDtypeStruct((out_rows, D_MODEL), F32),
        input_output_aliases=aliases,
        compiler_params=_cparams("parallel"),
        name="combine_ln2",
    )(*args)


def _routing_tables(idx, rank, counts, n_blocks):
    bm = EXPERT_BLOCK
    padded = jnp.maximum((counts + bm - 1) // bm, 1) * bm
    pad_end = jnp.cumsum(padded)
    pad_start = pad_end - padded
    experts = jnp.arange(N_EXPERTS, dtype=I32)

    def lookup(table, e):
        shape = (N_EXPERTS,) + (1,) * e.ndim
        return jnp.sum(jnp.where(e[None] == experts.reshape(shape), table.reshape(shape), 0), axis=0)

    pos = lookup(pad_start, idx) + rank
    blk_row0 = jnp.arange(n_blocks, dtype=I32) * bm
    block_e = jnp.minimum(jnp.sum((pad_end[:, None] <= blk_row0[None, :]).astype(I32), axis=0), N_EXPERTS - 1)
    n_valid = jnp.clip(lookup(counts, block_e) - (blk_row0 - lookup(pad_start, block_e)), 0, bm)
    return pos.astype(I32), block_e.astype(I32), n_valid.astype(I32)


def _moe(pk, idx, rank, counts, weights, b_glu, b_lin, b_down):
    n_t = pk.shape[0]
    n_blocks = (n_t * TOP_K) // EXPERT_BLOCK + N_EXPERTS
    pos, block_e, n_valid = _routing_tables(idx, rank, counts, n_blocks)
    pos_chunks = pos.reshape(TOP_K, n_t // SC_CHUNK, SC_CHUNK).transpose(1, 0, 2)
    xs_rows = _sc_dispatch(pk, pos_chunks, n_blocks * EXPERT_BLOCK)
    y_rows, prepared = _expert_ffn(xs_rows, block_e, n_valid, weights, b_glu, b_lin, b_down)
    return _sc_combine(y_rows, pos_chunks, n_t), prepared


def _rope_tables(seq_max):
    inv = ROPE_THETA ** (-jnp.arange(0, HEAD_DIM, 2, dtype=F32) / HEAD_DIM)
    ang = jnp.arange(seq_max).astype(F32)[:, None] * inv[None, :]
    cos, sin = jnp.cos(ang), jnp.sin(ang)
    reps = LANES // HEAD_DIM
    cos_t = jnp.tile(jnp.concatenate([cos, cos], axis=1), (1, reps))
    sin_t = jnp.tile(jnp.concatenate([-sin, sin], axis=1), (1, reps))
    return cos_t, sin_t


def _encoder_layer(xp, xs, seq_prompt, batch_sample, seq_sample, w_in, w_attn_out, w_four_out, w_o, ln1_g, ln1_b,
                   w_router, b_router, w_up, b_up, w_down, b_down, ln2_g, ln2_b):
    n_p, n_s = xp.shape[0], xs.shape[0]
    assert n_p == seq_prompt and n_s == batch_sample * seq_sample
    assert seq_prompt % (DFT_STAGE1 * 16) == 0 and seq_sample % TOKEN_TILE == 0 and n_p % seq_sample == 0
    n_t = n_p + n_s
    assert n_t % (max(DILATIONS) * ATTN_TILE) == 0 and seq_sample % (max(DILATIONS) * ATTN_SUB) == 0

    w_proj = w_in[:, :PROJ_WIDTH].astype(BF16)
    w_gate = w_in[:, PROJ_WIDTH:].astype(BF16)
    cos_t, sin_t = _rope_tables(max(seq_prompt, seq_sample))
    q_views, k_views, v_views, f = _inproj(xp, xs, w_proj, cos_t, sin_t, seq_sample)

    o_list, l_list = [], []
    for gi, dil in enumerate(DILATIONS):
        o, lse = _attention_group(q_views[gi], k_views[gi], v_views[gi], dil, n_p, seq_sample)
        o_list.append(o)
        l_list.append(lse)

    four_p = _fourier_prompt(f, seq_prompt)
    four_s = _fourier_sample(f, n_p, batch_sample, seq_sample)

    b_glu = b_up[:, None, 0::2]
    b_lin = b_up[:, None, 1::2]
    w_ao, w_fo, w_ob = w_attn_out.astype(BF16), w_four_out.astype(BF16), w_o.astype(BF16)

    tiles = n_t // TOKEN_TILE
    bounds = [tiles * p // MOE_PARTS for p in range(MOE_PARTS + 1)]
    routed = []
    for p in range(MOE_PARTS):
        routed.append(_merge_route(xp, xs, w_gate, o_list, l_list, four_p, four_s, w_ao, w_fo, w_ob, ln1_g, ln1_b,
                                   w_router, b_router, bounds[p], bounds[p + 1] - bounds[p]))
    combined = []
    weights = (w_up, w_down)
    for x1, pk, idx, gates, rank, cnt in routed:
        yg, weights = _moe(pk, idx, rank, cnt[:, 0], weights, b_glu, b_lin, b_down[:, None, :])
        combined.append(yg)

    outs = [None, None]
    spans = [(0, n_p), (n_p, n_t)]
    for p, ((x1, pk, idx, gates, rank, cnt), yg) in enumerate(zip(routed, combined)):
        lo, hi = bounds[p] * TOKEN_TILE, bounds[p + 1] * TOKEN_TILE
        gates_t = gates.T
        for which, (a, b) in enumerate(spans):
            s, e = max(lo, a), min(hi, b)
            if s < e:
                outs[which] = _final(x1, yg, gates_t, ln2_g, ln2_b, s - lo, e - s, b - a, s - a, outs[which])
    return outs[0], outs[1]


def kernel(x_prompt, x_sample, w_in, w_attn_out, w_four_out, w_o, ln1_g, ln1_b, w_router, b_router, w_up, b_up,
           w_down, b_down, ln2_g, ln2_b):
    assert w_in.shape[0] == DEPTH
    bp, sp, d = x_prompt.shape
    bs, ss, _ = x_sample.shape
    assert bp == 1 and d == D_MODEL
    y_p, y_s = _encoder_layer(
        x_prompt.reshape(sp, d), x_sample.reshape(bs * ss, d), sp, bs, ss,
        w_in[0], w_attn_out[0], w_four_out[0], w_o[0], ln1_g[0], ln1_b[0], w_router[0], b_router[0],
        w_up[0], b_up[0], w_down[0], b_down[0], ln2_g[0], ln2_b[0])
    return y_p.reshape(x_prompt.shape), y_s.reshape(x_sample.shape)
```

```python
import functools
import math

import jax
import jax.numpy as jnp
from jax import lax
from jax.experimental import pallas as pl
from jax.experimental.pallas import tpu as pltpu
from jax.experimental.pallas import tpu_sc as plsc

F32 = jnp.float32
BF16 = jnp.bfloat16
I32 = jnp.int32
U32 = jnp.uint32

D_MODEL = 1024
HEAD_DIM = 64
HEADS_PER_GROUP = 4
GROUP_WIDTH = HEADS_PER_GROUP * HEAD_DIM
DILATIONS = (1, 4, 16)
HALF_WINDOW = 64
N_GROUPS = len(DILATIONS)
QKV_WIDTH = N_GROUPS * GROUP_WIDTH
F_GROUPS = 4
F_GROUP_DIM = 64
F_WIDTH = F_GROUPS * F_GROUP_DIM
PROJ_WIDTH = 3 * QKV_WIDTH + F_WIDTH
N_EXPERTS = 32
TOP_K = 4
D_FF = 1024
SWIGLU_LIMIT = 7.0
SWIGLU_ALPHA = 1.702
LN_EPS = 1e-5
ROPE_THETA = 10000.0
NEG_INF = -1e30
DEPTH = 1
DEEPNORM_ALPHA = (2 * DEPTH) ** 0.25

LANES = 128
TOKEN_TILE = 512
FINAL_TILE = 1024
ATTN_TILE = 1024
ATTN_SUB = 128
EXPERT_BLOCK = 512
FF_CHUNK = 1024
SPLIT_CHUNK = 256
DFT_STAGE1 = 128
DFT_STAGE2_BATCH = 8
DFT_SMALL_STAGE1 = 8
DFT_SMALL_BATCH = 4
SC_CHUNK = 64
SC_ROUND_ROWS = 2 * 16 * SC_CHUNK
MOE_PART_SHARES = (0.7, 0.3)
VMEM_LIMIT = 56 * 1024 * 1024


def _cparams(*sem):
    return pltpu.CompilerParams(dimension_semantics=sem, vmem_limit_bytes=VMEM_LIMIT)


def _to_dilated_view(stage_ref, out_ref, val, dil):
    if dil == 1:
        out_ref[...] = val.astype(BF16)
        return
    rows = val.shape[0] // dil
    for c in range(GROUP_WIDTH // LANES):
        stage_ref[c] = val[:, c * LANES:(c + 1) * LANES]
    for r in range(dil):
        for c in range(GROUP_WIDTH // LANES):
            lo = r * GROUP_WIDTH + c * LANES
            out_ref[:, lo:lo + LANES] = stage_ref[c, pl.ds(r, rows, stride=dil), :].astype(BF16)


def _inproj_kernel(xp_ref, xs_ref, w_ref, cos_ref, sin_ref, *refs, n_prompt_tiles):
    qkv_refs = refs[:3 * N_GROUPS]
    f_ref = refs[3 * N_GROUPS]
    stage_refs = refs[3 * N_GROUPS + 1:]
    i = pl.program_id(0)
    x = jnp.where(i < n_prompt_tiles, xp_ref[...], xs_ref[...]).astype(BF16)
    proj = jnp.dot(x, w_ref[...], preferred_element_type=F32)
    cos = cos_ref[...]
    sin = sin_ref[...]
    lane = lax.broadcasted_iota(I32, cos.shape, 1)
    first_half = (lane % HEAD_DIM) < (HEAD_DIM // 2)

    def rope(t):
        parts = []
        for c in range(GROUP_WIDTH // LANES):
            tc = t[:, c * LANES:(c + 1) * LANES]
            swapped = jnp.where(first_half, pltpu.roll(tc, LANES - HEAD_DIM // 2, 1), pltpu.roll(tc, HEAD_DIM // 2, 1))
            parts.append(tc * cos + swapped * sin)
        return jnp.concatenate(parts, axis=1)

    for gi, dil in enumerate(DILATIONS):
        sl = slice(gi * GROUP_WIDTH, (gi + 1) * GROUP_WIDTH)
        qg = rope(proj[:, sl]) * (HEAD_DIM ** -0.5)
        kg = rope(proj[:, QKV_WIDTH + gi * GROUP_WIDTH:QKV_WIDTH + (gi + 1) * GROUP_WIDTH])
        vg = proj[:, 2 * QKV_WIDTH + gi * GROUP_WIDTH:2 * QKV_WIDTH + (gi + 1) * GROUP_WIDTH]
        for which, val in enumerate((qg, kg, vg)):
            _to_dilated_view(stage_refs[which * N_GROUPS + gi], qkv_refs[which * N_GROUPS + gi], val, dil)
    f_ref[...] = proj[:, 3 * QKV_WIDTH:PROJ_WIDTH].astype(BF16)


def _inproj(xp, xs, w_proj, cos_t, sin_t, seq_sample):
    n_p, n_s = xp.shape[0], xs.shape[0]
    n_t = n_p + n_s
    tm = TOKEN_TILE
    npt = n_p // tm
    tiles_per_sample_seq = seq_sample // tm

    def table_idx(i):
        return (jnp.where(i < npt, i, (i - npt) % tiles_per_sample_seq), 0)

    view_specs = [pl.BlockSpec((tm // d, d * GROUP_WIDTH), lambda i: (i, 0)) for d in DILATIONS] * 3
    view_shapes = [jax.ShapeDtypeStruct((n_t // d, d * GROUP_WIDTH), BF16) for d in DILATIONS] * 3
    outs = pl.pallas_call(
        functools.partial(_inproj_kernel, n_prompt_tiles=npt),
        grid=(n_t // tm,),
        in_specs=[
            pl.BlockSpec((tm, D_MODEL), lambda i: (jnp.minimum(i, npt - 1), 0)),
            pl.BlockSpec((tm, D_MODEL), lambda i: (jnp.maximum(i - npt, 0), 0)),
            pl.BlockSpec((D_MODEL, PROJ_WIDTH), lambda i: (0, 0)),
            pl.BlockSpec((tm, LANES), table_idx),
            pl.BlockSpec((tm, LANES), table_idx),
        ],
        out_specs=view_specs + [pl.BlockSpec((tm, F_WIDTH), lambda i: (i, 0))],
        out_shape=view_shapes + [jax.ShapeDtypeStruct((n_t, F_WIDTH), BF16)],
        scratch_shapes=[pltpu.VMEM((GROUP_WIDTH // LANES, tm, LANES), F32)] * (3 * N_GROUPS),
        compiler_params=_cparams("parallel"),
        name="inproj_rope",
    )(xp, xs, w_proj, cos_t, sin_t)
    q_views, k_views, v_views = outs[0:N_GROUPS], outs[N_GROUPS:2 * N_GROUPS], outs[2 * N_GROUPS:3 * N_GROUPS]
    return q_views, k_views, v_views, outs[3 * N_GROUPS]


def _attn_kernel(q_ref, kl_ref, km_ref, kr_ref, vl_ref, vm_ref, vr_ref, o_ref, lse_ref, *, prompt_rows, sample_rows):
    i = pl.program_id(1)
    kext = jnp.concatenate([kl_ref[...], km_ref[...], kr_ref[...]], axis=0)
    vext = jnp.concatenate([vl_ref[...], vm_ref[...], vr_ref[...]], axis=0)
    kw = ATTN_SUB + 2 * HALF_WINDOW
    a = lax.broadcasted_iota(I32, (ATTN_SUB, kw), 0)
    c = lax.broadcasted_iota(I32, (ATTN_SUB, kw), 1)
    in_band = jnp.abs(c - HALF_WINDOW - a) <= HALF_WINDOW
    head_of_lane = lax.broadcasted_iota(I32, (1, GROUP_WIDTH), 1) // HEAD_DIM
    for j in range(ATTN_TILE // ATTN_SUB):
        r0 = i * ATTN_TILE + j * ATTN_SUB
        in_prompt = r0 < prompt_rows
        b = jnp.maximum(r0 - prompt_rows, 0) // sample_rows
        lo = jnp.where(in_prompt, 0, prompt_rows + b * sample_rows)
        hi = jnp.where(in_prompt, prompt_rows, prompt_rows + (b + 1) * sample_rows)
        key_row = r0 - HALF_WINDOW + c
        valid = in_band & (key_row >= lo) & (key_row < hi)
        qj = q_ref[j * ATTN_SUB:(j + 1) * ATTN_SUB, :]
        kj = kext[j * ATTN_SUB:j * ATTN_SUB + kw, :]
        vj = vext[j * ATTN_SUB:j * ATTN_SUB + kw, :]
        q4 = jnp.concatenate([jnp.where(head_of_lane == h, qj, jnp.zeros_like(qj)) for h in range(HEADS_PER_GROUP)],
                             axis=0)
        s4 = lax.dot_general(q4, kj, (((1,), (1,)), ((), ())), preferred_element_type=F32)
        ps, scales, lses = [], [], []
        for h in range(HEADS_PER_GROUP):
            s = jnp.where(valid, s4[h * ATTN_SUB:(h + 1) * ATTN_SUB], NEG_INF)
            m = jnp.max(s, axis=1, keepdims=True)
            p = jnp.exp(s - m)
            l = jnp.sum(p, axis=1, keepdims=True)
            ps.append(p.astype(BF16))
            scales.append(1.0 / l)
            lses.append(m + jnp.log(l))
        o4 = jnp.dot(jnp.concatenate(ps, axis=0), vj, preferred_element_type=F32)
        acc = jnp.zeros((ATTN_SUB, GROUP_WIDTH), F32)
        lse_full = jnp.zeros((ATTN_SUB, GROUP_WIDTH), F32)
        for h in range(HEADS_PER_GROUP):
            mine = head_of_lane == h
            acc = jnp.where(mine, o4[h * ATTN_SUB:(h + 1) * ATTN_SUB] * scales[h], acc)
            lse_full = jnp.where(mine, lses[h], lse_full)
        o_ref[j * ATTN_SUB:(j + 1) * ATTN_SUB, :] = acc.astype(BF16)
        lse_ref[j * ATTN_SUB:(j + 1) * ATTN_SUB, :] = lse_full


def _attention_group(qv, kv, vv, dil, n_prompt, seq_sample):
    rows = qv.shape[0]
    tq = ATTN_TILE
    halo_per_tile = tq // HALF_WINDOW
    n_halo_blocks = rows // HALF_WINDOW
    main = pl.BlockSpec((tq, GROUP_WIDTH), lambda r, i: (i, r))
    left = pl.BlockSpec((HALF_WINDOW, GROUP_WIDTH), lambda r, i: (jnp.maximum(i * halo_per_tile - 1, 0), r))
    right = pl.BlockSpec((HALF_WINDOW, GROUP_WIDTH),
                         lambda r, i: (jnp.minimum((i + 1) * halo_per_tile, n_halo_blocks - 1), r))
    return pl.pallas_call(
        functools.partial(_attn_kernel, prompt_rows=n_prompt // dil, sample_rows=seq_sample // dil),
        grid=(dil, rows // tq),
        in_specs=[main, left, main, right, left, main, right],
        out_specs=[main, main],
        out_shape=[jax.ShapeDtypeStruct((rows, dil * GROUP_WIDTH), BF16),
                   jax.ShapeDtypeStruct((rows, dil * GROUP_WIDTH), F32)],
        compiler_params=_cparams("parallel", "parallel"),
        name=f"dilated_attention_{dil}",
    )(qv, kv, kv, kv, vv, vv, vv)


def _angle_table(idx, period):
    return (idx % period).astype(F32) * (2.0 * math.pi / period)


def _channel_dft(seq_len):
    c = jnp.arange(F_GROUP_DIM, dtype=I32)
    th = _angle_table(c[:, None] * c[None, :], F_GROUP_DIM)
    scale = (seq_len * F_GROUP_DIM) ** -0.5
    eye = jnp.eye(F_GROUPS, dtype=F32)
    cb = jnp.kron(eye, jnp.cos(th)) * scale
    sb = jnp.kron(eye, jnp.sin(th)) * scale
    return jnp.concatenate([cb, sb], axis=0).astype(BF16)


def _stage2_matrices(s1, s2):
    n2 = jnp.arange(s2, dtype=I32)[None, :]
    th_t = _angle_table(jnp.arange(s1, dtype=I32)[:, None] * n2, s1 * s2)
    th_f = _angle_table(jnp.arange(s2, dtype=I32)[:, None] * n2, s2)
    tc, ts = jnp.cos(th_t)[:, None, :], jnp.sin(th_t)[:, None, :]
    fc, fs = jnp.cos(th_f)[None, :, :], jnp.sin(th_f)[None, :, :]
    gr = tc * fc - ts * fs
    gi = -(ts * fc + tc * fs)
    return jnp.concatenate([jnp.concatenate([gr, -gi], axis=2), jnp.concatenate([gi, gr], axis=2)], axis=1).astype(BF16)


def _dft_stage1_kernel(m_ref, x_ref, a_ref):
    a_ref[...] = jnp.dot(m_ref[...], x_ref[...], preferred_element_type=F32).astype(BF16)


def _dft_stage2_kernel(g_ref, a_ref, cs_ref, o_ref):
    s2 = a_ref.shape[2]
    for kk in range(g_ref.shape[0]):
        a = jnp.concatenate([a_ref[0, kk], a_ref[1, kk]], axis=0)
        x = jnp.dot(g_ref[kk], a, preferred_element_type=F32)
        xr = x[:s2].astype(BF16)
        xi = x[s2:].astype(BF16)
        o_ref[:, kk * F_WIDTH:(kk + 1) * F_WIDTH] = (
            jnp.dot(xr, cs_ref[:F_WIDTH, :], preferred_element_type=F32)
            + jnp.dot(xi, cs_ref[F_WIDTH:, :], preferred_element_type=F32)).astype(BF16)


def _fourier_prompt(f, seq):
    s1 = DFT_STAGE1
    s2 = seq // s1
    n1 = jnp.arange(s1, dtype=I32)
    th1 = _angle_table(n1[:, None] * n1[None, :], s1)
    m1 = jnp.concatenate([jnp.cos(th1), -jnp.sin(th1)], axis=0).astype(BF16)
    g = _stage2_matrices(s1, s2)
    cs = _channel_dft(seq)

    cols = s2 * F_WIDTH
    fv = f[:seq].reshape(s1, cols)
    tn = min(cols, 4096)
    a = pl.pallas_call(
        _dft_stage1_kernel,
        grid=(cols // tn,),
        in_specs=[pl.BlockSpec((2 * s1, s1), lambda j: (0, 0)), pl.BlockSpec((s1, tn), lambda j: (0, j))],
        out_specs=pl.BlockSpec((2 * s1, tn), lambda j: (0, j)),
        out_shape=jax.ShapeDtypeStruct((2 * s1, cols), BF16),
        compiler_params=_cparams("parallel"),
        name="dft_stage1",
    )(m1, fv)
    a4 = a.reshape(2, s1, s2, F_WIDTH)
    kb = DFT_STAGE2_BATCH
    out = pl.pallas_call(
        _dft_stage2_kernel,
        grid=(s1 // kb,),
        in_specs=[pl.BlockSpec((kb, 2 * s2, 2 * s2), lambda k: (k, 0, 0)),
                  pl.BlockSpec((2, kb, s2, F_WIDTH), lambda k: (0, k, 0, 0)),
                  pl.BlockSpec((2 * F_WIDTH, F_WIDTH), lambda k: (0, 0))],
        out_specs=pl.BlockSpec((s2, kb * F_WIDTH), lambda k: (0, k)),
        out_shape=jax.ShapeDtypeStruct((s2, s1 * F_WIDTH), BF16),
        compiler_params=_cparams("parallel"),
        name="dft_stage2",
    )(g, a4, cs)
    return out.reshape(seq, F_WIDTH)


def _dft_small_kernel(g_ref, x_ref, cs_ref, o_ref, stage_ref):
    s1 = DFT_SMALL_STAGE1
    assert s1 == 8
    s2 = x_ref.shape[1] // s1
    root = math.sqrt(0.5)
    n_seq = x_ref.shape[0]
    a_all = []
    for b in range(n_seq):
        x0, x1, x2, x3, x4, x5, x6, x7 = [x_ref[b, n1 * s2:(n1 + 1) * s2, :].astype(F32) for n1 in range(s1)]
        u, v, p, q = x0 - x4, x2 - x6, x1 - x5, x3 - x7
        e_sum, e_dif = (x0 + x4) + (x2 + x6), (x0 + x4) - (x2 + x6)
        o_sum, o_dif = (x1 + x5) + (x3 + x7), (x1 + x5) - (x3 + x7)
        rpq_m, rpq_p = root * (p - q), root * (p + q)
        zero = jnp.zeros_like(x0)
        a_re = [e_sum + o_sum, u + rpq_m, e_dif, u - rpq_m, e_sum - o_sum, u - rpq_m, e_dif, u + rpq_m]
        im1, im3 = -v - rpq_p, v - rpq_p
        a_im = [zero, im1, -o_dif, im3, zero, -im3, o_dif, -im1]
        a_all.append([jnp.concatenate([a_re[k1].astype(BF16), a_im[k1].astype(BF16)], axis=0) for k1 in range(s1)])

    for k1 in range(s1):
        a = jnp.concatenate([a_all[b][k1] for b in range(n_seq)], axis=1)
        x = jnp.dot(g_ref[k1], a, preferred_element_type=F32)
        for b in range(n_seq):
            xb = x[:, b * F_WIDTH:(b + 1) * F_WIDTH]
            out = (jnp.dot(xb[:s2].astype(BF16), cs_ref[:F_WIDTH, :], preferred_element_type=F32)
                   + jnp.dot(xb[s2:].astype(BF16), cs_ref[F_WIDTH:, :], preferred_element_type=F32))
            for c in range(F_WIDTH // LANES):
                stage_ref[b * (F_WIDTH // LANES) + c, pl.ds(k1, s2, stride=s1), :] = out[:, c * LANES:(c + 1) * LANES]
    for b in range(n_seq):
        o_ref[b] = jnp.concatenate([stage_ref[b * (F_WIDTH // LANES) + c] for c in range(F_WIDTH // LANES)],
                                   axis=1).astype(BF16)


def _fourier_sample(f, n_prompt, batch, seq):
    s1 = DFT_SMALL_STAGE1
    s2 = seq // s1
    g = _stage2_matrices(s1, s2)
    cs = _channel_dft(seq)
    f3 = f.reshape(f.shape[0] // seq, seq, F_WIDTH)
    first = n_prompt // seq
    nb = DFT_SMALL_BATCH
    assert batch % nb == 0 and first % nb == 0
    return pl.pallas_call(
        _dft_small_kernel,
        grid=(batch // nb,),
        in_specs=[pl.BlockSpec((s1, 2 * s2, 2 * s2), lambda b: (0, 0, 0)),
                  pl.BlockSpec((nb, seq, F_WIDTH), lambda b: (first // nb + b, 0, 0)),
                  pl.BlockSpec((2 * F_WIDTH, F_WIDTH), lambda b: (0, 0))],
        out_specs=pl.BlockSpec((nb, seq, F_WIDTH), lambda b: (b, 0, 0)),
        out_shape=jax.ShapeDtypeStruct((batch, seq, F_WIDTH), BF16),
        scratch_shapes=[pltpu.VMEM((nb * (F_WIDTH // LANES), seq, LANES), F32)],
        compiler_params=_cparams("parallel"),
        name="dft_small",
    )(g, f3, cs).reshape(batch * seq, F_WIDTH)


def _pack_bf16_pairs(x):
    w = x.shape[1] // 2
    bits = lax.bitcast_convert_type(x.astype(BF16).astype(F32), U32)
    return (bits[:, :w] >> 16) | (bits[:, w:] & jnp.uint32(0xFFFF0000))


def _unpack_bf16_pairs(u):
    lo = lax.bitcast_convert_type(u << 16, F32)
    hi = lax.bitcast_convert_type(u & jnp.uint32(0xFFFF0000), F32)
    return jnp.concatenate([lo, hi], axis=1)


def _layer_norm(h, g, b):
    mu = jnp.mean(h, axis=-1, keepdims=True)
    d = h - mu
    var = jnp.mean(d * d, axis=-1, keepdims=True)
    return d * lax.rsqrt(var + LN_EPS) * g + b


def _from_dilated_view(stage_ref, blk_ref, dil):
    if dil == 1:
        return blk_ref[...].astype(F32)
    rows = blk_ref.shape[0]
    for r in range(dil):
        for c in range(GROUP_WIDTH // LANES):
            lo = r * GROUP_WIDTH + c * LANES
            stage_ref[c, pl.ds(r, rows, stride=dil), :] = blk_ref[:, lo:lo + LANES].astype(F32)
    return jnp.concatenate([stage_ref[c] for c in range(GROUP_WIDTH // LANES)], axis=1)


def _merge_kernel(xp_ref, xs_ref, wg_ref, o1_ref, o2_ref, o3_ref, l1_ref, l2_ref, l3_ref, fp_ref, fs_ref,
                  wao_ref, wfo_ref, wo_ref, g_ref, b_ref, wr2_ref, br_ref, tri_ref,
                  x1_ref, pk_ref, idx_ref, gate_ref, rank_ref, cnt_ref, run_ref, *stage_refs, n_prompt_tiles):
    i = pl.program_id(0)

    @pl.when(i == 0)
    def _():
        run_ref[...] = jnp.zeros_like(run_ref)

    is_prompt = i < n_prompt_tiles
    x = jnp.where(is_prompt, xp_ref[...], xs_ref[...])
    gpre = jnp.dot(x.astype(BF16), wg_ref[...], preferred_element_type=F32)
    o1, o2, o3 = [_from_dilated_view(stage_refs[gi], ref, d)
                  for gi, (ref, d) in enumerate(zip((o1_ref, o2_ref, o3_ref), DILATIONS))]
    l1, l2, l3 = [_from_dilated_view(stage_refs[N_GROUPS + gi], ref, d)
                  for gi, (ref, d) in enumerate(zip((l1_ref, l2_ref, l3_ref), DILATIONS))]
    m = jnp.maximum(jnp.maximum(l1, l2), l3)
    e1, e2, e3 = jnp.exp(l1 - m), jnp.exp(l2 - m), jnp.exp(l3 - m)
    attn = (e1 * o1 + e2 * o2 + e3 * o3) * (1.0 / (e1 + e2 + e3))
    a = jnp.dot(attn.astype(BF16), wao_ref[...], preferred_element_type=F32)
    four = jnp.where(is_prompt, fp_ref[...], fs_ref[...])
    ff = jnp.dot(four, wfo_ref[...], preferred_element_type=F32)
    merged = jax.nn.sigmoid(gpre[:, :D_MODEL]) * a + jax.nn.sigmoid(gpre[:, D_MODEL:]) * ff
    mix = jnp.dot(merged.astype(BF16), wo_ref[...], preferred_element_type=F32)
    x1 = _layer_norm(DEEPNORM_ALPHA * x + mix, g_ref[...], b_ref[...])
    x1_ref[...] = x1
    pk_ref[...] = _pack_bf16_pairs(x1)

    xh = x1.astype(BF16)
    xl = (x1 - xh.astype(F32)).astype(BF16)
    tm = x1.shape[0]
    cross = lax.dot_general(wr2_ref[...], jnp.concatenate([xh, xl], axis=0), (((1,), (1,)), ((), ())),
                            preferred_element_type=F32)
    logits = ((cross[:N_EXPERTS, :tm] + cross[N_EXPERTS:, :tm])
              + (cross[:N_EXPERTS, tm:] + cross[N_EXPERTS:, tm:])) + br_ref[...]
    eio = lax.broadcasted_iota(I32, (N_EXPERTS, tm), 0)
    work = logits
    vals, sels, picks = [], [], []
    for _ in range(TOP_K):
        mk = jnp.max(work, axis=0, keepdims=True)
        ik = jnp.min(jnp.where(work == mk, eio, N_EXPERTS), axis=0, keepdims=True)
        sel = eio == ik
        vals.append(mk)
        sels.append(sel)
        picks.append(ik)
        work = jnp.where(sel, -jnp.inf, work)
    es = [jnp.exp(v - vals[0]) for v in vals]
    inv_den = 1.0 / (es[0] + es[1] + es[2] + es[3])
    chosen = (sels[0] | sels[1] | sels[2] | sels[3])
    chosen_f = chosen.astype(F32)
    prefix = jnp.dot(chosen_f.astype(BF16), tri_ref[...], preferred_element_type=F32)
    before = prefix + run_ref[...]
    for kk in range(TOP_K):
        idx_ref[kk:kk + 1, :] = picks[kk]
        gate_ref[kk:kk + 1, :] = es[kk] * inv_den
        rank_ref[kk:kk + 1, :] = jnp.sum(jnp.where(sels[kk], before, 0.0), axis=0, keepdims=True).astype(I32)
    run_ref[...] = run_ref[...] + jnp.sum(chosen_f, axis=1, keepdims=True)
    cnt_ref[...] = jnp.broadcast_to(run_ref[...], cnt_ref.shape).astype(I32)


def _merge_route(xp, xs, w_gate, o_list, l_list, four_p, four_s, w_ao, w_fo, w_o, ln_g, ln_b, w_r, b_r, tile0,
                 n_tiles):
    n_p = xp.shape[0]
    tm = TOKEN_TILE
    n_t = n_tiles * tm
    npt = n_p // tm
    w_rt = w_r.T
    w_rh = w_rt.astype(BF16)
    w_rl = (w_rt - w_rh.astype(F32)).astype(BF16)
    tri = (jnp.arange(tm)[:, None] < jnp.arange(tm)[None, :]).astype(BF16)
    row = lambda w: pl.BlockSpec((tm, w), lambda i: (i, 0))
    prompt_row = lambda w: pl.BlockSpec((tm, w), lambda i: (jnp.clip(tile0 + i, 0, npt - 1), 0))
    sample_row = lambda w: pl.BlockSpec((tm, w), lambda i: (jnp.maximum(tile0 + i - npt, 0), 0))
    full = lambda r, c: pl.BlockSpec((r, c), lambda i: (0, 0))
    lane_row = pl.BlockSpec((TOP_K, tm), lambda i: (0, i))
    views = [pl.BlockSpec((tm // d, d * GROUP_WIDTH), lambda i: (tile0 + i, 0)) for d in DILATIONS]
    return pl.pallas_call(
        functools.partial(_merge_kernel, n_prompt_tiles=npt - tile0),
        grid=(n_tiles,),
        in_specs=[
            prompt_row(D_MODEL), sample_row(D_MODEL),
            full(D_MODEL, 2 * D_MODEL),
            *views, *views,
            prompt_row(F_WIDTH), sample_row(F_WIDTH),
            full(GROUP_WIDTH, D_MODEL), full(F_WIDTH, D_MODEL), full(D_MODEL, D_MODEL),
            full(1, D_MODEL), full(1, D_MODEL),
            full(2 * N_EXPERTS, D_MODEL), full(N_EXPERTS, 1),
            full(tm, tm),
        ],
        out_specs=[row(D_MODEL), row(D_MODEL // 2), lane_row, lane_row, lane_row, full(N_EXPERTS, LANES)],
        out_shape=[
            jax.ShapeDtypeStruct((n_t, D_MODEL), F32),
            jax.ShapeDtypeStruct((n_t, D_MODEL // 2), U32),
            jax.ShapeDtypeStruct((TOP_K, n_t), I32),
            jax.ShapeDtypeStruct((TOP_K, n_t), F32),
            jax.ShapeDtypeStruct((TOP_K, n_t), I32),
            jax.ShapeDtypeStruct((N_EXPERTS, LANES), I32),
        ],
        scratch_shapes=[pltpu.VMEM((N_EXPERTS, 1), F32)]
        + [pltpu.VMEM((GROUP_WIDTH // LANES, tm, LANES), F32)] * (2 * N_GROUPS),
        compiler_params=_cparams("arbitrary"),
        name="merge_ln_route",
    )(xp, xs, w_gate, *o_list, *l_list, four_p, four_s, w_ao, w_fo, w_o, ln_g.reshape(1, -1), ln_b.reshape(1, -1),
      jnp.concatenate([w_rh, w_rl], axis=0), b_r.reshape(-1, 1), tri)


def _sc_workers():
    info = plsc.get_sparse_core_info()
    return info.num_cores, info.num_cores * info.num_subcores


def _sc_dispatch(rows_src, pos_chunks, n_rows_out):
    n_t, w = rows_src.shape
    n_cores, n_workers = _sc_workers()
    assert n_workers * SC_CHUNK == SC_ROUND_ROWS and n_t % SC_ROUND_ROWS == 0
    chunks_per_worker = n_t // (n_workers * SC_CHUNK)
    mesh = plsc.VectorSubcoreMesh(core_axis_name="c", subcore_axis_name="s")

    @functools.partial(
        pl.kernel, mesh=mesh,
        out_type=jax.ShapeDtypeStruct((n_rows_out, w), rows_src.dtype),
        scratch_types=[pltpu.VMEM((TOP_K, SC_CHUNK), I32), pltpu.VMEM((SC_CHUNK, w), rows_src.dtype)],
    )
    def dispatch(src_hbm, pos_hbm, out_hbm, idx_v, rows_v):
        wid = lax.axis_index("s") * n_cores + lax.axis_index("c")

        @pl.loop(0, chunks_per_worker)
        def _(j):
            chunk = wid * chunks_per_worker + j
            pltpu.sync_copy(pos_hbm.at[chunk], idx_v)
            pltpu.sync_copy(src_hbm.at[pl.ds(chunk * SC_CHUNK, SC_CHUNK)], rows_v)
            for kk in range(TOP_K):
                pltpu.sync_copy(rows_v, out_hbm.at[idx_v.at[kk]])

    return dispatch(rows_src, pos_chunks)


def _sc_combine(rows_src, pos_chunks, n_t):
    w = rows_src.shape[1]
    n_cores, n_workers = _sc_workers()
    assert n_workers * SC_CHUNK == SC_ROUND_ROWS and n_t % SC_ROUND_ROWS == 0
    chunks_per_worker = n_t // (n_workers * SC_CHUNK)
    mesh = plsc.VectorSubcoreMesh(core_axis_name="c", subcore_axis_name="s")

    @functools.partial(
        pl.kernel, mesh=mesh,
        out_type=jax.ShapeDtypeStruct((TOP_K, n_t, w), rows_src.dtype),
        scratch_types=[pltpu.VMEM((TOP_K, SC_CHUNK), I32), pltpu.VMEM((SC_CHUNK, w), rows_src.dtype)],
    )
    def combine(src_hbm, pos_hbm, out_hbm, idx_v, rows_v):
        wid = lax.axis_index("s") * n_cores + lax.axis_index("c")

        @pl.loop(0, chunks_per_worker)
        def _(j):
            chunk = wid * chunks_per_worker + j
            pltpu.sync_copy(pos_hbm.at[chunk], idx_v)
            for kk in range(TOP_K):
                pltpu.sync_copy(src_hbm.at[idx_v.at[kk]], rows_v)
                pltpu.sync_copy(rows_v, out_hbm.at[kk, pl.ds(chunk * SC_CHUNK, SC_CHUNK)])

    return combine(rows_src, pos_chunks)


def _expert_kernel(be_ref, nv_ref, x_ref, *refs, prepare):
    if prepare:
        wu_ref, wdn_ref, p_ref, bg_ref, bl_ref, bd_ref, y_ref, wg_ref, wl_ref, wd_ref = refs
    else:
        wg_ref, wl_ref, wd_ref, bg_ref, bl_ref, bd_ref, y_ref = refs
    blk = pl.program_id(0)
    nv = nv_ref[blk]

    if prepare:
        new_expert = (blk == 0) | (be_ref[blk] != be_ref[jnp.maximum(blk - 1, 0)])

        @pl.when(new_expert)
        def _():
            half = SPLIT_CHUNK // 2
            for c in range(wu_ref.shape[2] // SPLIT_CHUNK):
                w = wu_ref[0, :, c * SPLIT_CHUNK:(c + 1) * SPLIT_CHUNK].astype(BF16)
                r = jnp.dot(w, p_ref[...], preferred_element_type=F32)
                wg_ref[0, :, c * half:(c + 1) * half] = r[:, :half].astype(BF16)
                wl_ref[0, :, c * half:(c + 1) * half] = r[:, half:].astype(BF16)
            wd_ref[...] = wdn_ref[...].astype(BF16)

    @pl.when(nv > 0)
    def _():
        x = _unpack_bf16_pairs(x_ref[...])
        rows = lax.broadcasted_iota(I32, (x.shape[0], 1), 0)
        x = jnp.where(rows < nv, x, 0.0).astype(BF16)
        acc = jnp.zeros((x.shape[0], D_MODEL), F32) + bd_ref[0]
        for c in range(D_FF // FF_CHUNK):
            sl = slice(c * FF_CHUNK, (c + 1) * FF_CHUNK)
            hg = jnp.dot(x, wg_ref[0, :, sl], preferred_element_type=F32) + bg_ref[0, :, sl]
            hl = jnp.dot(x, wl_ref[0, :, sl], preferred_element_type=F32) + bl_ref[0, :, sl]
            glu = jnp.minimum(hg, SWIGLU_LIMIT)
            lin = jnp.clip(hl, -SWIGLU_LIMIT, SWIGLU_LIMIT)
            act = glu * jax.nn.sigmoid(SWIGLU_ALPHA * glu) * (lin + 1.0)
            acc = acc + jnp.dot(act.astype(BF16), wd_ref[0, sl, :], preferred_element_type=F32)
        y_ref[...] = _pack_bf16_pairs(acc)

    @pl.when(nv == 0)
    def _():
        y_ref[...] = jnp.zeros_like(y_ref)


def _expert_ffn(xs_rows, block_e, n_valid, weights, b_glu, b_lin, b_down):
    n_rows = xs_rows.shape[0]
    bm = EXPERT_BLOCK
    prepare = len(weights) == 2
    wspec = lambda r, c: pl.BlockSpec((1, r, c), lambda b, be, nv: (be[b], 0, 0))
    rows_spec = pl.BlockSpec((bm, D_MODEL // 2), lambda b, be, nv: (b, 0))
    bias_specs = [wspec(1, D_FF), wspec(1, D_FF), wspec(1, D_MODEL)]
    y_shape = jax.ShapeDtypeStruct((n_rows, D_MODEL // 2), U32)
    if prepare:
        half = SPLIT_CHUNK // 2
        src = jnp.arange(SPLIT_CHUNK, dtype=I32)[:, None]
        dst = jnp.arange(SPLIT_CHUNK, dtype=I32)[None, :]
        sel = jnp.where(dst < half, src == 2 * dst, src == 2 * (dst - half) + 1).astype(BF16)
        operands = (*weights, sel)
        weight_specs = [wspec(D_MODEL, 2 * D_FF), wspec(D_FF, D_MODEL),
                        pl.BlockSpec((SPLIT_CHUNK, SPLIT_CHUNK), lambda b, be, nv: (0, 0))]
        prepared = jax.ShapeDtypeStruct((N_EXPERTS, D_MODEL, D_FF), BF16)
        out_specs = [rows_spec, wspec(D_MODEL, D_FF), wspec(D_MODEL, D_FF), wspec(D_FF, D_MODEL)]
        out_shape = [y_shape, prepared, prepared, prepared]
    else:
        operands = tuple(weights)
        weight_specs = [wspec(D_MODEL, D_FF), wspec(D_MODEL, D_FF), wspec(D_FF, D_MODEL)]
        out_specs, out_shape = rows_spec, y_shape
    grid_spec = pltpu.PrefetchScalarGridSpec(
        num_scalar_prefetch=2,
        grid=(n_rows // bm,),
        in_specs=[rows_spec, *weight_specs, *bias_specs],
        out_specs=out_specs,
    )
    out = pl.pallas_call(
        functools.partial(_expert_kernel, prepare=prepare),
        grid_spec=grid_spec,
        out_shape=out_shape,
        compiler_params=_cparams("arbitrary"),
        name="expert_ffn",
    )(block_e, n_valid, xs_rows, *operands, b_glu, b_lin, b_down)
    return (out[0], tuple(out[1:])) if prepare else (out, tuple(weights))


def _final_kernel(x1_ref, yg_ref, gt_ref, g_ref, b_ref, *rest):
    o_ref = rest[-1]
    gates = gt_ref[...]
    pad = jnp.zeros((LANES - TOP_K, gates.shape[1]), F32)
    gt = jnp.concatenate([gates, pad], axis=0).T
    ffn = jnp.zeros(x1_ref.shape, F32)
    for kk in range(TOP_K):
        ffn = ffn + gt[:, kk:kk + 1] * _unpack_bf16_pairs(yg_ref[kk])
    o_ref[...] = _layer_norm(DEEPNORM_ALPHA * x1_ref[...] + ffn, g_ref[...], b_ref[...])


def _final(x1, yg, gates, ln_g, ln_b, src_row0, n_rows, out_rows, dst_row0, out_prev=None):
    tm = FINAL_TILE
    assert src_row0 % tm == 0 and dst_row0 % tm == 0 and n_rows % tm == 0
    src, dst = src_row0 // tm, dst_row0 // tm
    in_specs = [
        pl.BlockSpec((tm, D_MODEL), lambda i: (src + i, 0)),
        pl.BlockSpec((TOP_K, tm, D_MODEL // 2), lambda i: (0, src + i, 0)),
        pl.BlockSpec((TOP_K, tm), lambda i: (0, src + i)),
        pl.BlockSpec((1, D_MODEL), lambda i: (0, 0)),
        pl.BlockSpec((1, D_MODEL), lambda i: (0, 0)),
    ]
    args = [x1, yg, gates, ln_g.reshape(1, -1), ln_b.reshape(1, -1)]
    aliases = {}
    if out_prev is not None:
        in_specs.append(pl.BlockSpec(memory_space=pl.ANY))
        args.append(out_prev)
        aliases = {len(args) - 1: 0}
    return pl.pallas_call(
        _final_kernel,
        grid=(n_rows // tm,),
        in_specs=in_specs,
        out_specs=pl.BlockSpec((tm, D_MODEL), lambda i: (dst + i, 0)),
        out_shape=jax.ShapeDtypeStruct((out_rows, D_MODEL), F32),
        input_output_aliases=aliases,
        compiler_params=_cparams("parallel"),
        name="combine_ln2",
    )(*args)


def _routing_tables(idx, rank, counts, n_blocks):
    bm = EXPERT_BLOCK
    padded = jnp.maximum((counts + bm - 1) // bm, 1) * bm
    pad_end = jnp.cumsum(padded)
    pad_start = pad_end - padded
    experts = jnp.arange(N_EXPERTS, dtype=I32)

    def lookup(table, e):
        shape = (N_EXPERTS,) + (1,) * e.ndim
        return jnp.sum(jnp.where(e[None] == experts.reshape(shape), table.reshape(shape), 0), axis=0)

    pos = lookup(pad_start, idx) + rank
    blk_row0 = jnp.arange(n_blocks, dtype=I32) * bm
    block_e = jnp.minimum(jnp.sum((pad_end[:, None] <= blk_row0[None, :]).astype(I32), axis=0), N_EXPERTS - 1)
    n_valid = jnp.clip(lookup(counts, block_e) - (blk_row0 - lookup(pad_start, block_e)), 0, bm)
    return pos.astype(I32), block_e.astype(I32), n_valid.astype(I32)


def _moe(pk, idx, rank, counts, weights, b_glu, b_lin, b_down):
    n_t = pk.shape[0]
    n_blocks = (n_t * TOP_K) // EXPERT_BLOCK + N_EXPERTS
    pos, block_e, n_valid = _routing_tables(idx, rank, counts, n_blocks)
    pos_chunks = pos.reshape(TOP_K, n_t // SC_CHUNK, SC_CHUNK).transpose(1, 0, 2)
    xs_rows = _sc_dispatch(pk, pos_chunks, n_blocks * EXPERT_BLOCK)
    y_rows, prepared = _expert_ffn(xs_rows, block_e, n_valid, weights, b_glu, b_lin, b_down)
    return _sc_combine(y_rows, pos_chunks, n_t), prepared


def _rope_tables(seq_max):
    inv = ROPE_THETA ** (-jnp.arange(0, HEAD_DIM, 2, dtype=F32) / HEAD_DIM)
    ang = jnp.arange(seq_max).astype(F32)[:, None] * inv[None, :]
    cos, sin = jnp.cos(ang), jnp.sin(ang)
    reps = LANES // HEAD_DIM
    cos_t = jnp.tile(jnp.concatenate([cos, cos], axis=1), (1, reps))
    sin_t = jnp.tile(jnp.concatenate([-sin, sin], axis=1), (1, reps))
    return cos_t, sin_t


def _encoder_layer(xp, xs, seq_prompt, batch_sample, seq_sample, w_in, w_attn_out, w_four_out, w_o, ln1_g, ln1_b,
                   w_router, b_router, w_up, b_up, w_down, b_down, ln2_g, ln2_b):
    n_p, n_s = xp.shape[0], xs.shape[0]
    assert n_p == seq_prompt and n_s == batch_sample * seq_sample
    assert seq_prompt % (DFT_STAGE1 * 16) == 0 and seq_sample % TOKEN_TILE == 0 and n_p % seq_sample == 0
    n_t = n_p + n_s
    assert n_t % (max(DILATIONS) * ATTN_TILE) == 0 and seq_sample % (max(DILATIONS) * ATTN_SUB) == 0

    w_proj = w_in[:, :PROJ_WIDTH].astype(BF16)
    w_gate = w_in[:, PROJ_WIDTH:].astype(BF16)
    cos_t, sin_t = _rope_tables(max(seq_prompt, seq_sample))
    q_views, k_views, v_views, f = _inproj(xp, xs, w_proj, cos_t, sin_t, seq_sample)

    o_list, l_list = [], []
    for gi, dil in enumerate(DILATIONS):
        o, lse = _attention_group(q_views[gi], k_views[gi], v_views[gi], dil, n_p, seq_sample)
        o_list.append(o)
        l_list.append(lse)

    four_p = _fourier_prompt(f, seq_prompt)
    four_s = _fourier_sample(f, n_p, batch_sample, seq_sample)

    b_glu = b_up[:, None, 0::2]
    b_lin = b_up[:, None, 1::2]
    w_ao, w_fo, w_ob = w_attn_out.astype(BF16), w_four_out.astype(BF16), w_o.astype(BF16)

    tiles = n_t // TOKEN_TILE
    align = math.lcm(SC_ROUND_ROWS, FINAL_TILE) // TOKEN_TILE
    bounds = [0]
    for share in MOE_PART_SHARES[:-1]:
        bounds.append(min(tiles, bounds[-1] + max(align, round(tiles * share / align) * align)))
    bounds.append(tiles)
    n_parts = len(bounds) - 1
    routed = []
    for p in range(n_parts):
        routed.append(_merge_route(xp, xs, w_gate, o_list, l_list, four_p, four_s, w_ao, w_fo, w_ob, ln1_g, ln1_b,
                                   w_router, b_router, bounds[p], bounds[p + 1] - bounds[p]))
    combined = []
    weights = (w_up, w_down)
    for x1, pk, idx, gates, rank, cnt in routed:
        yg, weights = _moe(pk, idx, rank, cnt[:, 0], weights, b_glu, b_lin, b_down[:, None, :])
        combined.append(yg)

    outs = [None, None]
    spans = [(0, n_p), (n_p, n_t)]
    for p, ((x1, pk, idx, gates, rank, cnt), yg) in enumerate(zip(routed, combined)):
        lo, hi = bounds[p] * TOKEN_TILE, bounds[p + 1] * TOKEN_TILE
        for which, (a, b) in enumerate(spans):
            s, e = max(lo, a), min(hi, b)
            if s < e:
                outs[which] = _final(x1, yg, gates, ln2_g, ln2_b, s - lo, e - s, b - a, s - a, outs[which])
    return outs[0], outs[1]


def kernel(x_prompt, x_sample, w_in, w_attn_out, w_four_out, w_o, ln1_g, ln1_b, w_router, b_router, w_up, b_up,
           w_down, b_down, ln2_g, ln2_b):
    assert w_in.shape[0] == DEPTH
    bp, sp, d = x_prompt.shape
    bs, ss, _ = x_sample.shape
    assert bp == 1 and d == D_MODEL
    y_p, y_s = _encoder_layer(
        x_prompt.reshape(sp, d), x_sample.reshape(bs * ss, d), sp, bs, ss,
        w_in[0], w_attn_out[0], w_four_out[0], w_o[0], ln1_g[0], ln1_b[0], w_router[0], b_router[0],
        w_up[0], b_up[0], w_down[0], b_down[0], ln2_g[0], ln2_b[0])
    return y_p.reshape(x_prompt.shape), y_s.reshape(x_sample.shape)
```

```python
import functools
import math

import jax
import jax.numpy as jnp
from jax import lax
from jax.experimental import pallas as pl
from jax.experimental.pallas import tpu as pltpu
from jax.experimental.pallas import tpu_sc as plsc

F32 = jnp.float32
BF16 = jnp.bfloat16
I32 = jnp.int32
U32 = jnp.uint32

D_MODEL = 1024
HEAD_DIM = 64
HEADS_PER_GROUP = 4
GROUP_WIDTH = HEADS_PER_GROUP * HEAD_DIM
DILATIONS = (1, 4, 16)
HALF_WINDOW = 64
N_GROUPS = len(DILATIONS)
QKV_WIDTH = N_GROUPS * GROUP_WIDTH
F_GROUPS = 4
F_GROUP_DIM = 64
F_WIDTH = F_GROUPS * F_GROUP_DIM
PROJ_WIDTH = 3 * QKV_WIDTH + F_WIDTH
N_EXPERTS = 32
TOP_K = 4
D_FF = 1024
SWIGLU_LIMIT = 7.0
SWIGLU_ALPHA = 1.702
LN_EPS = 1e-5
ROPE_THETA = 10000.0
NEG_INF = -1e30
DEPTH = 1
DEEPNORM_ALPHA = (2 * DEPTH) ** 0.25

LANES = 128
TOKEN_TILE = 512
FINAL_TILE = 1024
ATTN_TILE = 1024
ATTN_SUB = 128
EXPERT_BLOCK = 512
FF_CHUNK = 1024
SPLIT_CHUNK = 256
DFT_STAGE1 = 128
DFT_STAGE2_BATCH = 8
DFT_SMALL_STAGE1 = 8
DFT_SMALL_BATCH = 4
SC_CHUNK = 64
SC_ROUND_ROWS = 2 * 16 * SC_CHUNK
MOE_PART_SHARES = (0.7, 0.3)
VMEM_LIMIT = 56 * 1024 * 1024


def _cparams(*sem):
    return pltpu.CompilerParams(dimension_semantics=sem, vmem_limit_bytes=VMEM_LIMIT)


def _to_dilated_view(stage_ref, out_ref, val, dil):
    if dil == 1:
        out_ref[...] = val.astype(BF16)
        return
    rows = val.shape[0] // dil
    for c in range(GROUP_WIDTH // LANES):
        stage_ref[c] = val[:, c * LANES:(c + 1) * LANES]
    for r in range(dil):
        for c in range(GROUP_WIDTH // LANES):
            lo = r * GROUP_WIDTH + c * LANES
            out_ref[:, lo:lo + LANES] = stage_ref[c, pl.ds(r, rows, stride=dil), :].astype(BF16)


def _inproj_kernel(xp_ref, xs_ref, w_ref, cos_ref, sin_ref, *refs, n_prompt_tiles):
    qkv_refs = refs[:3 * N_GROUPS]
    f_ref = refs[3 * N_GROUPS]
    stage_refs = refs[3 * N_GROUPS + 1:]
    i = pl.program_id(0)
    x = jnp.where(i < n_prompt_tiles, xp_ref[...], xs_ref[...]).astype(BF16)
    proj = jnp.dot(x, w_ref[...], preferred_element_type=F32)
    cos = cos_ref[...]
    sin = sin_ref[...]
    lane = lax.broadcasted_iota(I32, cos.shape, 1)
    first_half = (lane % HEAD_DIM) < (HEAD_DIM // 2)

    def rope(t):
        parts = []
        for c in range(GROUP_WIDTH // LANES):
            tc = t[:, c * LANES:(c + 1) * LANES]
            swapped = jnp.where(first_half, pltpu.roll(tc, LANES - HEAD_DIM // 2, 1), pltpu.roll(tc, HEAD_DIM // 2, 1))
            parts.append(tc * cos + swapped * sin)
        return jnp.concatenate(parts, axis=1)

    for gi, dil in enumerate(DILATIONS):
        sl = slice(gi * GROUP_WIDTH, (gi + 1) * GROUP_WIDTH)
        qg = rope(proj[:, sl]) * (HEAD_DIM ** -0.5)
        kg = rope(proj[:, QKV_WIDTH + gi * GROUP_WIDTH:QKV_WIDTH + (gi + 1) * GROUP_WIDTH])
        vg = proj[:, 2 * QKV_WIDTH + gi * GROUP_WIDTH:2 * QKV_WIDTH + (gi + 1) * GROUP_WIDTH]
        for which, val in enumerate((qg, kg, vg)):
            _to_dilated_view(stage_refs[which * N_GROUPS + gi], qkv_refs[which * N_GROUPS + gi], val, dil)
    f_ref[...] = proj[:, 3 * QKV_WIDTH:PROJ_WIDTH].astype(BF16)


def _inproj(xp, xs, w_proj, cos_t, sin_t, seq_sample):
    n_p, n_s = xp.shape[0], xs.shape[0]
    n_t = n_p + n_s
    tm = TOKEN_TILE
    npt = n_p // tm
    tiles_per_sample_seq = seq_sample // tm

    def table_idx(i):
        return (jnp.where(i < npt, i, (i - npt) % tiles_per_sample_seq), 0)

    view_specs = [pl.BlockSpec((tm // d, d * GROUP_WIDTH), lambda i: (i, 0)) for d in DILATIONS] * 3
    view_shapes = [jax.ShapeDtypeStruct((n_t // d, d * GROUP_WIDTH), BF16) for d in DILATIONS] * 3
    outs = pl.pallas_call(
        functools.partial(_inproj_kernel, n_prompt_tiles=npt),
        grid=(n_t // tm,),
        in_specs=[
            pl.BlockSpec((tm, D_MODEL), lambda i: (jnp.minimum(i, npt - 1), 0)),
            pl.BlockSpec((tm, D_MODEL), lambda i: (jnp.maximum(i - npt, 0), 0)),
            pl.BlockSpec((D_MODEL, PROJ_WIDTH), lambda i: (0, 0)),
            pl.BlockSpec((tm, LANES), table_idx),
            pl.BlockSpec((tm, LANES), table_idx),
        ],
        out_specs=view_specs + [pl.BlockSpec((tm, F_WIDTH), lambda i: (i, 0))],
        out_shape=view_shapes + [jax.ShapeDtypeStruct((n_t, F_WIDTH), BF16)],
        scratch_shapes=[pltpu.VMEM((GROUP_WIDTH // LANES, tm, LANES), F32)] * (3 * N_GROUPS),
        compiler_params=_cparams("parallel"),
        name="inproj_rope",
    )(xp, xs, w_proj, cos_t, sin_t)
    q_views, k_views, v_views = outs[0:N_GROUPS], outs[N_GROUPS:2 * N_GROUPS], outs[2 * N_GROUPS:3 * N_GROUPS]
    return q_views, k_views, v_views, outs[3 * N_GROUPS]


def _attn_kernel(q_ref, kl_ref, km_ref, kr_ref, vl_ref, vm_ref, vr_ref, o_ref, lse_ref, *, prompt_rows, sample_rows):
    i = pl.program_id(1)
    head_of_lane = lax.broadcasted_iota(I32, (1, GROUP_WIDTH), 1) // HEAD_DIM

    def sub_block(j, kj, vj, valid):
        qj = q_ref[j * ATTN_SUB:(j + 1) * ATTN_SUB, :]
        q4 = jnp.concatenate([jnp.where(head_of_lane == h, qj, jnp.zeros_like(qj)) for h in range(HEADS_PER_GROUP)],
                             axis=0)
        s4 = lax.dot_general(q4, kj, (((1,), (1,)), ((), ())), preferred_element_type=F32)
        ps, scales, lses = [], [], []
        for h in range(HEADS_PER_GROUP):
            s = jnp.where(valid, s4[h * ATTN_SUB:(h + 1) * ATTN_SUB], NEG_INF)
            m = jnp.max(s, axis=1, keepdims=True)
            p = jnp.exp(s - m)
            l = jnp.sum(p, axis=1, keepdims=True)
            ps.append(p.astype(BF16))
            scales.append(1.0 / l)
            lses.append(m + jnp.log(l))
        o4 = jnp.dot(jnp.concatenate(ps, axis=0), vj, preferred_element_type=F32)
        acc = jnp.zeros((ATTN_SUB, GROUP_WIDTH), F32)
        lse_full = jnp.zeros((ATTN_SUB, GROUP_WIDTH), F32)
        for h in range(HEADS_PER_GROUP):
            mine = head_of_lane == h
            acc = jnp.where(mine, o4[h * ATTN_SUB:(h + 1) * ATTN_SUB] * scales[h], acc)
            lse_full = jnp.where(mine, lses[h], lse_full)
        o_ref[j * ATTN_SUB:(j + 1) * ATTN_SUB, :] = acc.astype(BF16)
        lse_ref[j * ATTN_SUB:(j + 1) * ATTN_SUB, :] = lse_full

    def banded_step():
        kext = jnp.concatenate([kl_ref[...], km_ref[...], kr_ref[...]], axis=0)
        vext = jnp.concatenate([vl_ref[...], vm_ref[...], vr_ref[...]], axis=0)
        kw = ATTN_SUB + 2 * HALF_WINDOW
        a = lax.broadcasted_iota(I32, (ATTN_SUB, kw), 0)
        c = lax.broadcasted_iota(I32, (ATTN_SUB, kw), 1)
        in_band = jnp.abs(c - HALF_WINDOW - a) <= HALF_WINDOW
        for j in range(ATTN_TILE // ATTN_SUB):
            r0 = i * ATTN_TILE + j * ATTN_SUB
            in_prompt = r0 < prompt_rows
            b = jnp.maximum(r0 - prompt_rows, 0) // sample_rows
            lo = jnp.where(in_prompt, 0, prompt_rows + b * sample_rows)
            hi = jnp.where(in_prompt, prompt_rows, prompt_rows + (b + 1) * sample_rows)
            key_row = r0 - HALF_WINDOW + c
            valid = in_band & (key_row >= lo) & (key_row < hi)
            sub_block(j, kext[j * ATTN_SUB:j * ATTN_SUB + kw, :], vext[j * ATTN_SUB:j * ATTN_SUB + kw, :], valid)

    def single_sequence_step():
        a = lax.broadcasted_iota(I32, (ATTN_SUB, ATTN_SUB), 0)
        c = lax.broadcasted_iota(I32, (ATTN_SUB, ATTN_SUB), 1)
        valid = jnp.abs(c - a) <= HALF_WINDOW
        for j in range(ATTN_TILE // ATTN_SUB):
            rows = slice(j * ATTN_SUB, (j + 1) * ATTN_SUB)
            sub_block(j, km_ref[rows, :], vm_ref[rows, :], valid)

    if sample_rows == ATTN_SUB and prompt_rows % ATTN_TILE == 0:
        in_sample = i * ATTN_TILE >= prompt_rows
        pl.when(in_sample)(single_sequence_step)
        pl.when(jnp.logical_not(in_sample))(banded_step)
    else:
        banded_step()


def _attention_group(qv, kv, vv, dil, n_prompt, seq_sample):
    rows = qv.shape[0]
    tq = ATTN_TILE
    halo_per_tile = tq // HALF_WINDOW
    n_halo_blocks = rows // HALF_WINDOW
    main = pl.BlockSpec((tq, GROUP_WIDTH), lambda r, i: (i, r))
    left = pl.BlockSpec((HALF_WINDOW, GROUP_WIDTH), lambda r, i: (jnp.maximum(i * halo_per_tile - 1, 0), r))
    right = pl.BlockSpec((HALF_WINDOW, GROUP_WIDTH),
                         lambda r, i: (jnp.minimum((i + 1) * halo_per_tile, n_halo_blocks - 1), r))
    return pl.pallas_call(
        functools.partial(_attn_kernel, prompt_rows=n_prompt // dil, sample_rows=seq_sample // dil),
        grid=(dil, rows // tq),
        in_specs=[main, left, main, right, left, main, right],
        out_specs=[main, main],
        out_shape=[jax.ShapeDtypeStruct((rows, dil * GROUP_WIDTH), BF16),
                   jax.ShapeDtypeStruct((rows, dil * GROUP_WIDTH), F32)],
        compiler_params=_cparams("parallel", "parallel"),
        name=f"dilated_attention_{dil}",
    )(qv, kv, kv, kv, vv, vv, vv)


def _angle_table(idx, period):
    return (idx % period).astype(F32) * (2.0 * math.pi / period)


def _channel_dft(seq_len):
    c = jnp.arange(F_GROUP_DIM, dtype=I32)
    th = _angle_table(c[:, None] * c[None, :], F_GROUP_DIM)
    scale = (seq_len * F_GROUP_DIM) ** -0.5
    eye = jnp.eye(F_GROUPS, dtype=F32)
    cb = jnp.kron(eye, jnp.cos(th)) * scale
    sb = jnp.kron(eye, jnp.sin(th)) * scale
    return jnp.concatenate([cb, sb], axis=0).astype(BF16)


def _stage2_matrices(s1, s2):
    n2 = jnp.arange(s2, dtype=I32)[None, :]
    th_t = _angle_table(jnp.arange(s1, dtype=I32)[:, None] * n2, s1 * s2)
    th_f = _angle_table(jnp.arange(s2, dtype=I32)[:, None] * n2, s2)
    tc, ts = jnp.cos(th_t)[:, None, :], jnp.sin(th_t)[:, None, :]
    fc, fs = jnp.cos(th_f)[None, :, :], jnp.sin(th_f)[None, :, :]
    gr = tc * fc - ts * fs
    gi = -(ts * fc + tc * fs)
    return jnp.concatenate([jnp.concatenate([gr, -gi], axis=2), jnp.concatenate([gi, gr], axis=2)], axis=1).astype(BF16)


def _dft_stage1_kernel(m_ref, x_ref, a_ref):
    a_ref[...] = jnp.dot(m_ref[...], x_ref[...], preferred_element_type=F32).astype(BF16)


def _dft_stage2_kernel(g_ref, a_ref, cs_ref, o_ref):
    s2 = a_ref.shape[2]
    for kk in range(g_ref.shape[0]):
        a = jnp.concatenate([a_ref[0, kk], a_ref[1, kk]], axis=0)
        x = jnp.dot(g_ref[kk], a, preferred_element_type=F32)
        xr = x[:s2].astype(BF16)
        xi = x[s2:].astype(BF16)
        o_ref[:, kk * F_WIDTH:(kk + 1) * F_WIDTH] = (
            jnp.dot(xr, cs_ref[:F_WIDTH, :], preferred_element_type=F32)
            + jnp.dot(xi, cs_ref[F_WIDTH:, :], preferred_element_type=F32)).astype(BF16)


def _fourier_prompt(f, seq):
    s1 = DFT_STAGE1
    s2 = seq // s1
    n1 = jnp.arange(s1, dtype=I32)
    th1 = _angle_table(n1[:, None] * n1[None, :], s1)
    m1 = jnp.concatenate([jnp.cos(th1), -jnp.sin(th1)], axis=0).astype(BF16)
    g = _stage2_matrices(s1, s2)
    cs = _channel_dft(seq)

    cols = s2 * F_WIDTH
    fv = f[:seq].reshape(s1, cols)
    tn = min(cols, 4096)
    a = pl.pallas_call(
        _dft_stage1_kernel,
        grid=(cols // tn,),
        in_specs=[pl.BlockSpec((2 * s1, s1), lambda j: (0, 0)), pl.BlockSpec((s1, tn), lambda j: (0, j))],
        out_specs=pl.BlockSpec((2 * s1, tn), lambda j: (0, j)),
        out_shape=jax.ShapeDtypeStruct((2 * s1, cols), BF16),
        compiler_params=_cparams("parallel"),
        name="dft_stage1",
    )(m1, fv)
    a4 = a.reshape(2, s1, s2, F_WIDTH)
    kb = DFT_STAGE2_BATCH
    out = pl.pallas_call(
        _dft_stage2_kernel,
        grid=(s1 // kb,),
        in_specs=[pl.BlockSpec((kb, 2 * s2, 2 * s2), lambda k: (k, 0, 0)),
                  pl.BlockSpec((2, kb, s2, F_WIDTH), lambda k: (0, k, 0, 0)),
                  pl.BlockSpec((2 * F_WIDTH, F_WIDTH), lambda k: (0, 0))],
        out_specs=pl.BlockSpec((s2, kb * F_WIDTH), lambda k: (0, k)),
        out_shape=jax.ShapeDtypeStruct((s2, s1 * F_WIDTH), BF16),
        compiler_params=_cparams("parallel"),
        name="dft_stage2",
    )(g, a4, cs)
    return out.reshape(seq, F_WIDTH)


def _dft_small_kernel(g_ref, x_ref, cs_ref, o_ref, stage_ref):
    s1 = DFT_SMALL_STAGE1
    assert s1 == 8
    s2 = x_ref.shape[1] // s1
    root = math.sqrt(0.5)
    n_seq = x_ref.shape[0]
    a_all = []
    for b in range(n_seq):
        x0, x1, x2, x3, x4, x5, x6, x7 = [x_ref[b, n1 * s2:(n1 + 1) * s2, :].astype(F32) for n1 in range(s1)]
        u, v, p, q = x0 - x4, x2 - x6, x1 - x5, x3 - x7
        e_sum, e_dif = (x0 + x4) + (x2 + x6), (x0 + x4) - (x2 + x6)
        o_sum, o_dif = (x1 + x5) + (x3 + x7), (x1 + x5) - (x3 + x7)
        rpq_m, rpq_p = root * (p - q), root * (p + q)
        zero = jnp.zeros_like(x0)
        a_re = [e_sum + o_sum, u + rpq_m, e_dif, u - rpq_m, e_sum - o_sum, u - rpq_m, e_dif, u + rpq_m]
        im1, im3 = -v - rpq_p, v - rpq_p
        a_im = [zero, im1, -o_dif, im3, zero, -im3, o_dif, -im1]
        a_all.append([jnp.concatenate([a_re[k1].astype(BF16), a_im[k1].astype(BF16)], axis=0) for k1 in range(s1)])

    for k1 in range(s1):
        a = jnp.concatenate([a_all[b][k1] for b in range(n_seq)], axis=1)
        x = jnp.dot(g_ref[k1], a, preferred_element_type=F32)
        for b in range(n_seq):
            xb = x[:, b * F_WIDTH:(b + 1) * F_WIDTH]
            out = (jnp.dot(xb[:s2].astype(BF16), cs_ref[:F_WIDTH, :], preferred_element_type=F32)
                   + jnp.dot(xb[s2:].astype(BF16), cs_ref[F_WIDTH:, :], preferred_element_type=F32))
            for c in range(F_WIDTH // LANES):
                stage_ref[b * (F_WIDTH // LANES) + c, pl.ds(k1, s2, stride=s1), :] = out[:, c * LANES:(c + 1) * LANES]
    for b in range(n_seq):
        o_ref[b] = jnp.concatenate([stage_ref[b * (F_WIDTH // LANES) + c] for c in range(F_WIDTH // LANES)],
                                   axis=1).astype(BF16)


def _fourier_sample(f, n_prompt, batch, seq):
    s1 = DFT_SMALL_STAGE1
    s2 = seq // s1
    g = _stage2_matrices(s1, s2)
    cs = _channel_dft(seq)
    f3 = f.reshape(f.shape[0] // seq, seq, F_WIDTH)
    first = n_prompt // seq
    nb = DFT_SMALL_BATCH
    assert batch % nb == 0 and first % nb == 0
    return pl.pallas_call(
        _dft_small_kernel,
        grid=(batch // nb,),
        in_specs=[pl.BlockSpec((s1, 2 * s2, 2 * s2), lambda b: (0, 0, 0)),
                  pl.BlockSpec((nb, seq, F_WIDTH), lambda b: (first // nb + b, 0, 0)),
                  pl.BlockSpec((2 * F_WIDTH, F_WIDTH), lambda b: (0, 0))],
        out_specs=pl.BlockSpec((nb, seq, F_WIDTH), lambda b: (b, 0, 0)),
        out_shape=jax.ShapeDtypeStruct((batch, seq, F_WIDTH), BF16),
        scratch_shapes=[pltpu.VMEM((nb * (F_WIDTH // LANES), seq, LANES), F32)],
        compiler_params=_cparams("parallel"),
        name="dft_small",
    )(g, f3, cs).reshape(batch * seq, F_WIDTH)


def _pack_bf16_pairs(x):
    w = x.shape[1] // 2
    bits = lax.bitcast_convert_type(x.astype(BF16).astype(F32), U32)
    return (bits[:, :w] >> 16) | (bits[:, w:] & jnp.uint32(0xFFFF0000))


def _unpack_bf16_pairs(u):
    lo = lax.bitcast_convert_type(u << 16, F32)
    hi = lax.bitcast_convert_type(u & jnp.uint32(0xFFFF0000), F32)
    return jnp.concatenate([lo, hi], axis=1)


def _layer_norm(h, g, b):
    mu = jnp.mean(h, axis=-1, keepdims=True)
    d = h - mu
    var = jnp.mean(d * d, axis=-1, keepdims=True)
    return d * lax.rsqrt(var + LN_EPS) * g + b


def _from_dilated_view(stage_ref, blk_ref, dil):
    if dil == 1:
        return blk_ref[...].astype(F32)
    rows = blk_ref.shape[0]
    for r in range(dil):
        for c in range(GROUP_WIDTH // LANES):
            lo = r * GROUP_WIDTH + c * LANES
            stage_ref[c, pl.ds(r, rows, stride=dil), :] = blk_ref[:, lo:lo + LANES].astype(F32)
    return jnp.concatenate([stage_ref[c] for c in range(GROUP_WIDTH // LANES)], axis=1)


def _merge_kernel(xp_ref, xs_ref, wg_ref, o1_ref, o2_ref, o3_ref, l1_ref, l2_ref, l3_ref, fp_ref, fs_ref,
                  wao_ref, wfo_ref, wo_ref, g_ref, b_ref, wr2_ref, br_ref, tri_ref,
                  x1_ref, pk_ref, idx_ref, gate_ref, rank_ref, cnt_ref, run_ref, *stage_refs, n_prompt_tiles):
    i = pl.program_id(0)

    @pl.when(i == 0)
    def _():
        run_ref[...] = jnp.zeros_like(run_ref)

    is_prompt = i < n_prompt_tiles
    x = jnp.where(is_prompt, xp_ref[...], xs_ref[...])
    gpre = jnp.dot(x.astype(BF16), wg_ref[...], preferred_element_type=F32)
    o1, o2, o3 = [_from_dilated_view(stage_refs[gi], ref, d)
                  for gi, (ref, d) in enumerate(zip((o1_ref, o2_ref, o3_ref), DILATIONS))]
    l1, l2, l3 = [_from_dilated_view(stage_refs[N_GROUPS + gi], ref, d)
                  for gi, (ref, d) in enumerate(zip((l1_ref, l2_ref, l3_ref), DILATIONS))]
    m = jnp.maximum(jnp.maximum(l1, l2), l3)
    e1, e2, e3 = jnp.exp(l1 - m), jnp.exp(l2 - m), jnp.exp(l3 - m)
    attn = (e1 * o1 + e2 * o2 + e3 * o3) * (1.0 / (e1 + e2 + e3))
    a = jnp.dot(attn.astype(BF16), wao_ref[...], preferred_element_type=F32)
    four = jnp.where(is_prompt, fp_ref[...], fs_ref[...])
    ff = jnp.dot(four, wfo_ref[...], preferred_element_type=F32)
    merged = jax.nn.sigmoid(gpre[:, :D_MODEL]) * a + jax.nn.sigmoid(gpre[:, D_MODEL:]) * ff
    mix = jnp.dot(merged.astype(BF16), wo_ref[...], preferred_element_type=F32)
    x1 = _layer_norm(DEEPNORM_ALPHA * x + mix, g_ref[...], b_ref[...])
    x1_ref[...] = x1
    pk_ref[...] = _pack_bf16_pairs(x1)

    xh = x1.astype(BF16)
    xl = (x1 - xh.astype(F32)).astype(BF16)
    tm = x1.shape[0]
    cross = lax.dot_general(wr2_ref[...], jnp.concatenate([xh, xl], axis=0), (((1,), (1,)), ((), ())),
                            preferred_element_type=F32)
    logits = ((cross[:N_EXPERTS, :tm] + cross[N_EXPERTS:, :tm])
              + (cross[:N_EXPERTS, tm:] + cross[N_EXPERTS:, tm:])) + br_ref[...]
    eio = lax.broadcasted_iota(I32, (N_EXPERTS, tm), 0)
    work = logits
    vals, sels, picks = [], [], []
    for _ in range(TOP_K):
        mk = jnp.max(work, axis=0, keepdims=True)
        ik = jnp.min(jnp.where(work == mk, eio, N_EXPERTS), axis=0, keepdims=True)
        sel = eio == ik
        vals.append(mk)
        sels.append(sel)
        picks.append(ik)
        work = jnp.where(sel, -jnp.inf, work)
    es = [jnp.exp(v - vals[0]) for v in vals]
    inv_den = 1.0 / (es[0] + es[1] + es[2] + es[3])
    chosen = (sels[0] | sels[1] | sels[2] | sels[3])
    chosen_f = chosen.astype(F32)
    prefix = jnp.dot(chosen_f.astype(BF16), tri_ref[...], preferred_element_type=F32)
    before = prefix + run_ref[...]
    for kk in range(TOP_K):
        idx_ref[kk:kk + 1, :] = picks[kk]
        gate_ref[kk:kk + 1, :] = es[kk] * inv_den
        rank_ref[kk:kk + 1, :] = jnp.sum(jnp.where(sels[kk], before, 0.0), axis=0, keepdims=True).astype(I32)
    run_ref[...] = run_ref[...] + jnp.sum(chosen_f, axis=1, keepdims=True)
    cnt_ref[...] = jnp.broadcast_to(run_ref[...], cnt_ref.shape).astype(I32)


def _merge_route(xp, xs, w_gate, o_list, l_list, four_p, four_s, w_ao, w_fo, w_o, ln_g, ln_b, w_r, b_r, tile0,
                 n_tiles):
    n_p = xp.shape[0]
    tm = TOKEN_TILE
    n_t = n_tiles * tm
    npt = n_p // tm
    w_rt = w_r.T
    w_rh = w_rt.astype(BF16)
    w_rl = (w_rt - w_rh.astype(F32)).astype(BF16)
    tri = (jnp.arange(tm)[:, None] < jnp.arange(tm)[None, :]).astype(BF16)
    row = lambda w: pl.BlockSpec((tm, w), lambda i: (i, 0))
    prompt_row = lambda w: pl.BlockSpec((tm, w), lambda i: (jnp.clip(tile0 + i, 0, npt - 1), 0))
    sample_row = lambda w: pl.BlockSpec((tm, w), lambda i: (jnp.maximum(tile0 + i - npt, 0), 0))
    full = lambda r, c: pl.BlockSpec((r, c), lambda i: (0, 0))
    lane_row = pl.BlockSpec((TOP_K, tm), lambda i: (0, i))
    views = [pl.BlockSpec((tm // d, d * GROUP_WIDTH), lambda i: (tile0 + i, 0)) for d in DILATIONS]
    return pl.pallas_call(
        functools.partial(_merge_kernel, n_prompt_tiles=npt - tile0),
        grid=(n_tiles,),
        in_specs=[
            prompt_row(D_MODEL), sample_row(D_MODEL),
            full(D_MODEL, 2 * D_MODEL),
            *views, *views,
            prompt_row(F_WIDTH), sample_row(F_WIDTH),
            full(GROUP_WIDTH, D_MODEL), full(F_WIDTH, D_MODEL), full(D_MODEL, D_MODEL),
            full(1, D_MODEL), full(1, D_MODEL),
            full(2 * N_EXPERTS, D_MODEL), full(N_EXPERTS, 1),
            full(tm, tm),
        ],
        out_specs=[row(D_MODEL), row(D_MODEL // 2), lane_row, lane_row, lane_row, full(N_EXPERTS, LANES)],
        out_shape=[
            jax.ShapeDtypeStruct((n_t, D_MODEL), F32),
            jax.ShapeDtypeStruct((n_t, D_MODEL // 2), U32),
            jax.ShapeDtypeStruct((TOP_K, n_t), I32),
            jax.ShapeDtypeStruct((TOP_K, n_t), F32),
            jax.ShapeDtypeStruct((TOP_K, n_t), I32),
            jax.ShapeDtypeStruct((N_EXPERTS, LANES), I32),
        ],
        scratch_shapes=[pltpu.VMEM((N_EXPERTS, 1), F32)]
        + [pltpu.VMEM((GROUP_WIDTH // LANES, tm, LANES), F32)] * (2 * N_GROUPS),
        compiler_params=_cparams("arbitrary"),
        name="merge_ln_route",
    )(xp, xs, w_gate, *o_list, *l_list, four_p, four_s, w_ao, w_fo, w_o, ln_g.reshape(1, -1), ln_b.reshape(1, -1),
      jnp.concatenate([w_rh, w_rl], axis=0), b_r.reshape(-1, 1), tri)


def _sc_workers():
    info = plsc.get_sparse_core_info()
    return info.num_cores, info.num_cores * info.num_subcores


def _sc_dispatch(rows_src, pos_chunks, n_rows_out):
    n_t, w = rows_src.shape
    n_cores, n_workers = _sc_workers()
    assert n_workers * SC_CHUNK == SC_ROUND_ROWS and n_t % SC_ROUND_ROWS == 0
    chunks_per_worker = n_t // (n_workers * SC_CHUNK)
    mesh = plsc.VectorSubcoreMesh(core_axis_name="c", subcore_axis_name="s")

    @functools.partial(
        pl.kernel, mesh=mesh,
        out_type=jax.ShapeDtypeStruct((n_rows_out, w), rows_src.dtype),
        scratch_types=[pltpu.VMEM((TOP_K, SC_CHUNK), I32), pltpu.VMEM((SC_CHUNK, w), rows_src.dtype)],
    )
    def dispatch(src_hbm, pos_hbm, out_hbm, idx_v, rows_v):
        wid = lax.axis_index("s") * n_cores + lax.axis_index("c")

        @pl.loop(0, chunks_per_worker)
        def _(j):
            chunk = wid * chunks_per_worker + j
            pltpu.sync_copy(pos_hbm.at[chunk], idx_v)
            pltpu.sync_copy(src_hbm.at[pl.ds(chunk * SC_CHUNK, SC_CHUNK)], rows_v)
            for kk in range(TOP_K):
                pltpu.sync_copy(rows_v, out_hbm.at[idx_v.at[kk]])

    return dispatch(rows_src, pos_chunks)


def _sc_combine(rows_src, pos_chunks, n_t):
    w = rows_src.shape[1]
    n_cores, n_workers = _sc_workers()
    assert n_workers * SC_CHUNK == SC_ROUND_ROWS and n_t % SC_ROUND_ROWS == 0
    chunks_per_worker = n_t // (n_workers * SC_CHUNK)
    mesh = plsc.VectorSubcoreMesh(core_axis_name="c", subcore_axis_name="s")

    @functools.partial(
        pl.kernel, mesh=mesh,
        out_type=jax.ShapeDtypeStruct((TOP_K, n_t, w), rows_src.dtype),
        scratch_types=[pltpu.VMEM((TOP_K, SC_CHUNK), I32), pltpu.VMEM((SC_CHUNK, w), rows_src.dtype)],
    )
    def combine(src_hbm, pos_hbm, out_hbm, idx_v, rows_v):
        wid = lax.axis_index("s") * n_cores + lax.axis_index("c")

        @pl.loop(0, chunks_per_worker)
        def _(j):
            chunk = wid * chunks_per_worker + j
            pltpu.sync_copy(pos_hbm.at[chunk], idx_v)
            for kk in range(TOP_K):
                pltpu.sync_copy(src_hbm.at[idx_v.at[kk]], rows_v)
                pltpu.sync_copy(rows_v, out_hbm.at[kk, pl.ds(chunk * SC_CHUNK, SC_CHUNK)])

    return combine(rows_src, pos_chunks)


def _expert_kernel(be_ref, nv_ref, x_ref, *refs, prepare):
    if prepare:
        wu_ref, wdn_ref, p_ref, bg_ref, bl_ref, bd_ref, y_ref, wg_ref, wl_ref, wd_ref = refs
    else:
        wg_ref, wl_ref, wd_ref, bg_ref, bl_ref, bd_ref, y_ref = refs
    blk = pl.program_id(0)
    nv = nv_ref[blk]

    if prepare:
        new_expert = (blk == 0) | (be_ref[blk] != be_ref[jnp.maximum(blk - 1, 0)])

        @pl.when(new_expert)
        def _():
            half = SPLIT_CHUNK // 2
            for c in range(wu_ref.shape[2] // SPLIT_CHUNK):
                w = wu_ref[0, :, c * SPLIT_CHUNK:(c + 1) * SPLIT_CHUNK].astype(BF16)
                r = jnp.dot(w, p_ref[...], preferred_element_type=F32)
                wg_ref[0, :, c * half:(c + 1) * half] = r[:, :half].astype(BF16)
                wl_ref[0, :, c * half:(c + 1) * half] = r[:, half:].astype(BF16)
            wd_ref[...] = wdn_ref[...].astype(BF16)

    @pl.when(nv > 0)
    def _():
        x = _unpack_bf16_pairs(x_ref[...])
        rows = lax.broadcasted_iota(I32, (x.shape[0], 1), 0)
        x = jnp.where(rows < nv, x, 0.0).astype(BF16)
        acc = jnp.zeros((x.shape[0], D_MODEL), F32) + bd_ref[0]
        for c in range(D_FF // FF_CHUNK):
            sl = slice(c * FF_CHUNK, (c + 1) * FF_CHUNK)
            hg = jnp.dot(x, wg_ref[0, :, sl], preferred_element_type=F32) + bg_ref[0, :, sl]
            hl = jnp.dot(x, wl_ref[0, :, sl], preferred_element_type=F32) + bl_ref[0, :, sl]
            glu = jnp.minimum(hg, SWIGLU_LIMIT)
            lin = jnp.clip(hl, -SWIGLU_LIMIT, SWIGLU_LIMIT)
            act = glu * jax.nn.sigmoid(SWIGLU_ALPHA * glu) * (lin + 1.0)
            acc = acc + jnp.dot(act.astype(BF16), wd_ref[0, sl, :], preferred_element_type=F32)
        y_ref[...] = _pack_bf16_pairs(acc)

    @pl.when(nv == 0)
    def _():
        y_ref[...] = jnp.zeros_like(y_ref)


def _expert_ffn(xs_rows, block_e, n_valid, weights, b_glu, b_lin, b_down):
    n_rows = xs_rows.shape[0]
    bm = EXPERT_BLOCK
    prepare = len(weights) == 2
    wspec = lambda r, c: pl.BlockSpec((1, r, c), lambda b, be, nv: (be[b], 0, 0))
    rows_spec = pl.BlockSpec((bm, D_MODEL // 2), lambda b, be, nv: (b, 0))
    bias_specs = [wspec(1, D_FF), wspec(1, D_FF), wspec(1, D_MODEL)]
    y_shape = jax.ShapeDtypeStruct((n_rows, D_MODEL // 2), U32)
    if prepare:
        half = SPLIT_CHUNK // 2
        src = jnp.arange(SPLIT_CHUNK, dtype=I32)[:, None]
        dst = jnp.arange(SPLIT_CHUNK, dtype=I32)[None, :]
        sel = jnp.where(dst < half, src == 2 * dst, src == 2 * (dst - half) + 1).astype(BF16)
        operands = (*weights, sel)
        weight_specs = [wspec(D_MODEL, 2 * D_FF), wspec(D_FF, D_MODEL),
                        pl.BlockSpec((SPLIT_CHUNK, SPLIT_CHUNK), lambda b, be, nv: (0, 0))]
        prepared = jax.ShapeDtypeStruct((N_EXPERTS, D_MODEL, D_FF), BF16)
        out_specs = [rows_spec, wspec(D_MODEL, D_FF), wspec(D_MODEL, D_FF), wspec(D_FF, D_MODEL)]
        out_shape = [y_shape, prepared, prepared, prepared]
    else:
        operands = tuple(weights)
        weight_specs = [wspec(D_MODEL, D_FF), wspec(D_MODEL, D_FF), wspec(D_FF, D_MODEL)]
        out_specs, out_shape = rows_spec, y_shape
    grid_spec = pltpu.PrefetchScalarGridSpec(
        num_scalar_prefetch=2,
        grid=(n_rows // bm,),
        in_specs=[rows_spec, *weight_specs, *bias_specs],
        out_specs=out_specs,
    )
    out = pl.pallas_call(
        functools.partial(_expert_kernel, prepare=prepare),
        grid_spec=grid_spec,
        out_shape=out_shape,
        compiler_params=_cparams("arbitrary"),
        name="expert_ffn",
    )(block_e, n_valid, xs_rows, *operands, b_glu, b_lin, b_down)
    return (out[0], tuple(out[1:])) if prepare else (out, tuple(weights))


def _final_kernel(x1_ref, yg_ref, gt_ref, g_ref, b_ref, *rest):
    o_ref = rest[-1]
    gates = gt_ref[...]
    pad = jnp.zeros((LANES - TOP_K, gates.shape[1]), F32)
    gt = jnp.concatenate([gates, pad], axis=0).T
    ffn = jnp.zeros(x1_ref.shape, F32)
    for kk in range(TOP_K):
        ffn = ffn + gt[:, kk:kk + 1] * _unpack_bf16_pairs(yg_ref[kk])
    o_ref[...] = _layer_norm(DEEPNORM_ALPHA * x1_ref[...] + ffn, g_ref[...], b_ref[...])


def _final(x1, yg, gates, ln_g, ln_b, src_row0, n_rows, out_rows, dst_row0, out_prev=None):
    tm = FINAL_TILE
    assert src_row0 % tm == 0 and dst_row0 % tm == 0 and n_rows % tm == 0
    src, dst = src_row0 // tm, dst_row0 // tm
    in_specs = [
        pl.BlockSpec((tm, D_MODEL), lambda i: (src + i, 0)),
        pl.BlockSpec((TOP_K, tm, D_MODEL // 2), lambda i: (0, src + i, 0)),
        pl.BlockSpec((TOP_K, tm), lambda i: (0, src + i)),
        pl.BlockSpec((1, D_MODEL), lambda i: (0, 0)),
        pl.BlockSpec((1, D_MODEL), lambda i: (0, 0)),
    ]
    args = [x1, yg, gates, ln_g.reshape(1, -1), ln_b.reshape(1, -1)]
    aliases = {}
    if out_prev is not None:
        in_specs.append(pl.BlockSpec(memory_space=pl.ANY))
        args.append(out_prev)
        aliases = {len(args) - 1: 0}
    return pl.pallas_call(
        _final_kernel,
        grid=(n_rows // tm,),
        in_specs=in_specs,
        out_specs=pl.BlockSpec((tm, D_MODEL), lambda i: (dst + i, 0)),
        out_shape=jax.ShapeDtypeStruct((out_rows, D_MODEL), F32),
        input_output_aliases=aliases,
        compiler_params=_cparams("parallel"),
        name="combine_ln2",
    )(*args)


def _routing_tables(idx, rank, counts, n_blocks):
    bm = EXPERT_BLOCK
    padded = jnp.maximum((counts + bm - 1) // bm, 1) * bm
    pad_end = jnp.cumsum(padded)
    pad_start = pad_end - padded
    experts = jnp.arange(N_EXPERTS, dtype=I32)

    def lookup(table, e):
        shape = (N_EXPERTS,) + (1,) * e.ndim
        return jnp.sum(jnp.where(e[None] == experts.reshape(shape), table.reshape(shape), 0), axis=0)

    pos = lookup(pad_start, idx) + rank
    blk_row0 = jnp.arange(n_blocks, dtype=I32) * bm
    block_e = jnp.minimum(jnp.sum((pad_end[:, None] <= blk_row0[None, :]).astype(I32), axis=0), N_EXPERTS - 1)
    n_valid = jnp.clip(lookup(counts, block_e) - (blk_row0 - lookup(pad_start, block_e)), 0, bm)
    return pos.astype(I32), block_e.astype(I32), n_valid.astype(I32)


def _moe(pk, idx, rank, counts, weights, b_glu, b_lin, b_down):
    n_t = pk.shape[0]
    n_blocks = (n_t * TOP_K) // EXPERT_BLOCK + N_EXPERTS
    pos, block_e, n_valid = _routing_tables(idx, rank, counts, n_blocks)
    pos_chunks = pos.reshape(TOP_K, n_t // SC_CHUNK, SC_CHUNK).transpose(1, 0, 2)
    xs_rows = _sc_dispatch(pk, pos_chunks, n_blocks * EXPERT_BLOCK)
    y_rows, prepared = _expert_ffn(xs_rows, block_e, n_valid, weights, b_glu, b_lin, b_down)
    return _sc_combine(y_rows, pos_chunks, n_t), prepared


def _rope_tables(seq_max):
    inv = ROPE_THETA ** (-jnp.arange(0, HEAD_DIM, 2, dtype=F32) / HEAD_DIM)
    ang = jnp.arange(seq_max).astype(F32)[:, None] * inv[None, :]
    cos, sin = jnp.cos(ang), jnp.sin(ang)
    reps = LANES // HEAD_DIM
    cos_t = jnp.tile(jnp.concatenate([cos, cos], axis=1), (1, reps))
    sin_t = jnp.tile(jnp.concatenate([-sin, sin], axis=1), (1, reps))
    return cos_t, sin_t


def _encoder_layer(xp, xs, seq_prompt, batch_sample, seq_sample, w_in, w_attn_out, w_four_out, w_o, ln1_g, ln1_b,
                   w_router, b_router, w_up, b_up, w_down, b_down, ln2_g, ln2_b):
    n_p, n_s = xp.shape[0], xs.shape[0]
    assert n_p == seq_prompt and n_s == batch_sample * seq_sample
    assert seq_prompt % (DFT_STAGE1 * 16) == 0 and seq_sample % TOKEN_TILE == 0 and n_p % seq_sample == 0
    n_t = n_p + n_s
    assert n_t % (max(DILATIONS) * ATTN_TILE) == 0 and seq_sample % (max(DILATIONS) * ATTN_SUB) == 0

    w_proj = w_in[:, :PROJ_WIDTH].astype(BF16)
    w_gate = w_in[:, PROJ_WIDTH:].astype(BF16)
    cos_t, sin_t = _rope_tables(max(seq_prompt, seq_sample))
    q_views, k_views, v_views, f = _inproj(xp, xs, w_proj, cos_t, sin_t, seq_sample)

    o_list, l_list = [], []
    for gi, dil in enumerate(DILATIONS):
        o, lse = _attention_group(q_views[gi], k_views[gi], v_views[gi], dil, n_p, seq_sample)
        o_list.append(o)
        l_list.append(lse)

    four_p = _fourier_prompt(f, seq_prompt)
    four_s = _fourier_sample(f, n_p, batch_sample, seq_sample)

    b_glu = b_up[:, None, 0::2]
    b_lin = b_up[:, None, 1::2]
    w_ao, w_fo, w_ob = w_attn_out.astype(BF16), w_four_out.astype(BF16), w_o.astype(BF16)

    tiles = n_t // TOKEN_TILE
    align = math.lcm(SC_ROUND_ROWS, FINAL_TILE) // TOKEN_TILE
    bounds = [0]
    for share in MOE_PART_SHARES[:-1]:
        bounds.append(min(tiles, bounds[-1] + max(align, round(tiles * share / align) * align)))
    bounds.append(tiles)
    n_parts = len(bounds) - 1
    routed = []
    for p in range(n_parts):
        routed.append(_merge_route(xp, xs, w_gate, o_list, l_list, four_p, four_s, w_ao, w_fo, w_ob, ln1_g, ln1_b,
                                   w_router, b_router, bounds[p], bounds[p + 1] - bounds[p]))
    combined = []
    weights = (w_up, w_down)
    for x1, pk, idx, gates, rank, cnt in routed:
        yg, weights = _moe(pk, idx, rank, cnt[:, 0], weights, b_glu, b_lin, b_down[:, None, :])
        combined.append(yg)

    outs = [None, None]
    spans = [(0, n_p), (n_p, n_t)]
    for p, ((x1, pk, idx, gates, rank, cnt), yg) in enumerate(zip(routed, combined)):
        lo, hi = bounds[p] * TOKEN_TILE, bounds[p + 1] * TOKEN_TILE
        for which, (a, b) in enumerate(spans):
            s, e = max(lo, a), min(hi, b)
            if s < e:
                outs[which] = _final(x1, yg, gates, ln2_g, ln2_b, s - lo, e - s, b - a, s - a, outs[which])
    return outs[0], outs[1]


def kernel(x_prompt, x_sample, w_in, w_attn_out, w_four_out, w_o, ln1_g, ln1_b, w_router, b_router, w_up, b_up,
           w_down, b_down, ln2_g, ln2_b):
    assert w_in.shape[0] == DEPTH
    bp, sp, d = x_prompt.shape
    bs, ss, _ = x_sample.shape
    assert bp == 1 and d == D_MODEL
    y_p, y_s = _encoder_layer(
        x_prompt.reshape(sp, d), x_sample.reshape(bs * ss, d), sp, bs, ss,
        w_in[0], w_attn_out[0], w_four_out[0], w_o[0], ln1_g[0], ln1_b[0], w_router[0], b_router[0],
        w_up[0], b_up[0], w_down[0], b_down[0], ln2_g[0], ln2_b[0])
    return y_p.reshape(x_prompt.shape), y_s.reshape(x_sample.shape)
```

```python
import functools
import math

import jax
import jax.numpy as jnp
from jax import lax
from jax.experimental import pallas as pl
from jax.experimental.pallas import tpu as pltpu
from jax.experimental.pallas import tpu_sc as plsc

F32 = jnp.float32
BF16 = jnp.bfloat16
I32 = jnp.int32
U32 = jnp.uint32

D_MODEL = 1024
HEAD_DIM = 64
HEADS_PER_GROUP = 4
GROUP_WIDTH = HEADS_PER_GROUP * HEAD_DIM
DILATIONS = (1, 4, 16)
HALF_WINDOW = 64
N_GROUPS = len(DILATIONS)
QKV_WIDTH = N_GROUPS * GROUP_WIDTH
F_GROUPS = 4
F_GROUP_DIM = 64
F_WIDTH = F_GROUPS * F_GROUP_DIM
PROJ_WIDTH = 3 * QKV_WIDTH + F_WIDTH
N_EXPERTS = 32
TOP_K = 4
D_FF = 1024
SWIGLU_LIMIT = 7.0
SWIGLU_ALPHA = 1.702
LN_EPS = 1e-5
ROPE_THETA = 10000.0
NEG_INF = -1e30
DEPTH = 1
DEEPNORM_ALPHA = (2 * DEPTH) ** 0.25

LANES = 128
TOKEN_TILE = 512
FINAL_TILE = 1024
ATTN_TILE = 1024
ATTN_SUB = 128
EXPERT_BLOCK = 512
EXPERT_SUB = 128
FF_CHUNK = 1024
SPLIT_CHUNK = 256
DFT_STAGE1 = 128
DFT_STAGE2_BATCH = 8
DFT_SMALL_STAGE1 = 8
DFT_SMALL_BATCH = 4
SC_CHUNK = 64
SC_ROUND_ROWS = 2 * 16 * SC_CHUNK
MOE_PART_SHARES = (0.7, 0.3)
VMEM_LIMIT = 56 * 1024 * 1024


def _cparams(*sem):
    return pltpu.CompilerParams(dimension_semantics=sem, vmem_limit_bytes=VMEM_LIMIT)


def _to_dilated_view(stage_ref, out_ref, val, dil):
    if dil == 1:
        out_ref[...] = val.astype(BF16)
        return
    rows = val.shape[0] // dil
    for c in range(GROUP_WIDTH // LANES):
        stage_ref[c] = val[:, c * LANES:(c + 1) * LANES]
    for r in range(dil):
        for c in range(GROUP_WIDTH // LANES):
            lo = r * GROUP_WIDTH + c * LANES
            out_ref[:, lo:lo + LANES] = stage_ref[c, pl.ds(r, rows, stride=dil), :].astype(BF16)


def _inproj_kernel(xp_ref, xs_ref, w_ref, cos_ref, sin_ref, *refs, n_prompt_tiles):
    qkv_refs = refs[:3 * N_GROUPS]
    f_ref = refs[3 * N_GROUPS]
    stage_refs = refs[3 * N_GROUPS + 1:]
    i = pl.program_id(0)
    x = jnp.where(i < n_prompt_tiles, xp_ref[...], xs_ref[...]).astype(BF16)
    proj = jnp.dot(x, w_ref[...], preferred_element_type=F32)
    cos = cos_ref[...]
    sin = sin_ref[...]
    lane = lax.broadcasted_iota(I32, cos.shape, 1)
    first_half = (lane % HEAD_DIM) < (HEAD_DIM // 2)

    def rope(t):
        parts = []
        for c in range(GROUP_WIDTH // LANES):
            tc = t[:, c * LANES:(c + 1) * LANES]
            swapped = jnp.where(first_half, pltpu.roll(tc, LANES - HEAD_DIM // 2, 1), pltpu.roll(tc, HEAD_DIM // 2, 1))
            parts.append(tc * cos + swapped * sin)
        return jnp.concatenate(parts, axis=1)

    for gi, dil in enumerate(DILATIONS):
        sl = slice(gi * GROUP_WIDTH, (gi + 1) * GROUP_WIDTH)
        qg = rope(proj[:, sl]) * (HEAD_DIM ** -0.5)
        kg = rope(proj[:, QKV_WIDTH + gi * GROUP_WIDTH:QKV_WIDTH + (gi + 1) * GROUP_WIDTH])
        vg = proj[:, 2 * QKV_WIDTH + gi * GROUP_WIDTH:2 * QKV_WIDTH + (gi + 1) * GROUP_WIDTH]
        for which, val in enumerate((qg, kg, vg)):
            _to_dilated_view(stage_refs[which * N_GROUPS + gi], qkv_refs[which * N_GROUPS + gi], val, dil)
    f_ref[...] = proj[:, 3 * QKV_WIDTH:PROJ_WIDTH].astype(BF16)


def _inproj(xp, xs, w_proj, cos_t, sin_t, seq_sample):
    n_p, n_s = xp.shape[0], xs.shape[0]
    n_t = n_p + n_s
    tm = TOKEN_TILE
    npt = n_p // tm
    tiles_per_sample_seq = seq_sample // tm

    def table_idx(i):
        return (jnp.where(i < npt, i, (i - npt) % tiles_per_sample_seq), 0)

    view_specs = [pl.BlockSpec((tm // d, d * GROUP_WIDTH), lambda i: (i, 0)) for d in DILATIONS] * 3
    view_shapes = [jax.ShapeDtypeStruct((n_t // d, d * GROUP_WIDTH), BF16) for d in DILATIONS] * 3
    outs = pl.pallas_call(
        functools.partial(_inproj_kernel, n_prompt_tiles=npt),
        grid=(n_t // tm,),
        in_specs=[
            pl.BlockSpec((tm, D_MODEL), lambda i: (jnp.minimum(i, npt - 1), 0)),
            pl.BlockSpec((tm, D_MODEL), lambda i: (jnp.maximum(i - npt, 0), 0)),
            pl.BlockSpec((D_MODEL, PROJ_WIDTH), lambda i: (0, 0)),
            pl.BlockSpec((tm, LANES), table_idx),
            pl.BlockSpec((tm, LANES), table_idx),
        ],
        out_specs=view_specs + [pl.BlockSpec((tm, F_WIDTH), lambda i: (i, 0))],
        out_shape=view_shapes + [jax.ShapeDtypeStruct((n_t, F_WIDTH), BF16)],
        scratch_shapes=[pltpu.VMEM((GROUP_WIDTH // LANES, tm, LANES), F32)] * (3 * N_GROUPS),
        compiler_params=_cparams("parallel"),
        name="inproj_rope",
    )(xp, xs, w_proj, cos_t, sin_t)
    q_views, k_views, v_views = outs[0:N_GROUPS], outs[N_GROUPS:2 * N_GROUPS], outs[2 * N_GROUPS:3 * N_GROUPS]
    return q_views, k_views, v_views, outs[3 * N_GROUPS]


def _attn_kernel(q_ref, kl_ref, km_ref, kr_ref, vl_ref, vm_ref, vr_ref, o_ref, lse_ref, *, prompt_rows, sample_rows):
    i = pl.program_id(1)
    head_of_lane = lax.broadcasted_iota(I32, (1, GROUP_WIDTH), 1) // HEAD_DIM

    def sub_block(j, kj, vj, valid):
        qj = q_ref[j * ATTN_SUB:(j + 1) * ATTN_SUB, :]
        q4 = jnp.concatenate([jnp.where(head_of_lane == h, qj, jnp.zeros_like(qj)) for h in range(HEADS_PER_GROUP)],
                             axis=0)
        s4 = lax.dot_general(q4, kj, (((1,), (1,)), ((), ())), preferred_element_type=F32)
        ps, scales, lses = [], [], []
        for h in range(HEADS_PER_GROUP):
            s = jnp.where(valid, s4[h * ATTN_SUB:(h + 1) * ATTN_SUB], NEG_INF)
            m = jnp.max(s, axis=1, keepdims=True)
            p = jnp.exp(s - m)
            l = jnp.sum(p, axis=1, keepdims=True)
            ps.append(p.astype(BF16))
            scales.append(1.0 / l)
            lses.append(m + jnp.log(l))
        o4 = jnp.dot(jnp.concatenate(ps, axis=0), vj, preferred_element_type=F32)
        acc = jnp.zeros((ATTN_SUB, GROUP_WIDTH), F32)
        lse_full = jnp.zeros((ATTN_SUB, GROUP_WIDTH), F32)
        for h in range(HEADS_PER_GROUP):
            mine = head_of_lane == h
            acc = jnp.where(mine, o4[h * ATTN_SUB:(h + 1) * ATTN_SUB] * scales[h], acc)
            lse_full = jnp.where(mine, lses[h], lse_full)
        o_ref[j * ATTN_SUB:(j + 1) * ATTN_SUB, :] = acc.astype(BF16)
        lse_ref[j * ATTN_SUB:(j + 1) * ATTN_SUB, :] = lse_full

    def banded_step():
        kext = jnp.concatenate([kl_ref[...], km_ref[...], kr_ref[...]], axis=0)
        vext = jnp.concatenate([vl_ref[...], vm_ref[...], vr_ref[...]], axis=0)
        kw = ATTN_SUB + 2 * HALF_WINDOW
        a = lax.broadcasted_iota(I32, (ATTN_SUB, kw), 0)
        c = lax.broadcasted_iota(I32, (ATTN_SUB, kw), 1)
        in_band = jnp.abs(c - HALF_WINDOW - a) <= HALF_WINDOW
        for j in range(ATTN_TILE // ATTN_SUB):
            r0 = i * ATTN_TILE + j * ATTN_SUB
            in_prompt = r0 < prompt_rows
            b = jnp.maximum(r0 - prompt_rows, 0) // sample_rows
            lo = jnp.where(in_prompt, 0, prompt_rows + b * sample_rows)
            hi = jnp.where(in_prompt, prompt_rows, prompt_rows + (b + 1) * sample_rows)
            key_row = r0 - HALF_WINDOW + c
            valid = in_band & (key_row >= lo) & (key_row < hi)
            sub_block(j, kext[j * ATTN_SUB:j * ATTN_SUB + kw, :], vext[j * ATTN_SUB:j * ATTN_SUB + kw, :], valid)

    def single_sequence_step():
        a = lax.broadcasted_iota(I32, (ATTN_SUB, ATTN_SUB), 0)
        c = lax.broadcasted_iota(I32, (ATTN_SUB, ATTN_SUB), 1)
        valid = jnp.abs(c - a) <= HALF_WINDOW
        for j in range(ATTN_TILE // ATTN_SUB):
            rows = slice(j * ATTN_SUB, (j + 1) * ATTN_SUB)
            sub_block(j, km_ref[rows, :], vm_ref[rows, :], valid)

    if sample_rows == ATTN_SUB and prompt_rows % ATTN_TILE == 0:
        in_sample = i * ATTN_TILE >= prompt_rows
        pl.when(in_sample)(single_sequence_step)
        pl.when(jnp.logical_not(in_sample))(banded_step)
    else:
        banded_step()


def _attention_group(qv, kv, vv, dil, n_prompt, seq_sample):
    rows = qv.shape[0]
    tq = ATTN_TILE
    halo_per_tile = tq // HALF_WINDOW
    n_halo_blocks = rows // HALF_WINDOW
    main = pl.BlockSpec((tq, GROUP_WIDTH), lambda r, i: (i, r))
    left = pl.BlockSpec((HALF_WINDOW, GROUP_WIDTH), lambda r, i: (jnp.maximum(i * halo_per_tile - 1, 0), r))
    right = pl.BlockSpec((HALF_WINDOW, GROUP_WIDTH),
                         lambda r, i: (jnp.minimum((i + 1) * halo_per_tile, n_halo_blocks - 1), r))
    return pl.pallas_call(
        functools.partial(_attn_kernel, prompt_rows=n_prompt // dil, sample_rows=seq_sample // dil),
        grid=(dil, rows // tq),
        in_specs=[main, left, main, right, left, main, right],
        out_specs=[main, main],
        out_shape=[jax.ShapeDtypeStruct((rows, dil * GROUP_WIDTH), BF16),
                   jax.ShapeDtypeStruct((rows, dil * GROUP_WIDTH), F32)],
        compiler_params=_cparams("parallel", "parallel"),
        name=f"dilated_attention_{dil}",
    )(qv, kv, kv, kv, vv, vv, vv)


def _angle_table(idx, period):
    return (idx % period).astype(F32) * (2.0 * math.pi / period)


def _channel_dft(seq_len):
    c = jnp.arange(F_GROUP_DIM, dtype=I32)
    th = _angle_table(c[:, None] * c[None, :], F_GROUP_DIM)
    scale = (seq_len * F_GROUP_DIM) ** -0.5
    eye = jnp.eye(F_GROUPS, dtype=F32)
    cb = jnp.kron(eye, jnp.cos(th)) * scale
    sb = jnp.kron(eye, jnp.sin(th)) * scale
    return jnp.concatenate([cb, sb], axis=0).astype(BF16)


def _stage2_matrices(s1, s2):
    n2 = jnp.arange(s2, dtype=I32)[None, :]
    th_t = _angle_table(jnp.arange(s1, dtype=I32)[:, None] * n2, s1 * s2)
    th_f = _angle_table(jnp.arange(s2, dtype=I32)[:, None] * n2, s2)
    tc, ts = jnp.cos(th_t)[:, None, :], jnp.sin(th_t)[:, None, :]
    fc, fs = jnp.cos(th_f)[None, :, :], jnp.sin(th_f)[None, :, :]
    gr = tc * fc - ts * fs
    gi = -(ts * fc + tc * fs)
    return jnp.concatenate([jnp.concatenate([gr, -gi], axis=2), jnp.concatenate([gi, gr], axis=2)], axis=1).astype(BF16)


def _dft_stage1_kernel(m_ref, x_ref, a_ref):
    a_ref[...] = jnp.dot(m_ref[...], x_ref[...], preferred_element_type=F32).astype(BF16)


def _dft_stage2_kernel(g_ref, a_ref, cs_ref, o_ref):
    s2 = a_ref.shape[2]
    for kk in range(g_ref.shape[0]):
        a = jnp.concatenate([a_ref[0, kk], a_ref[1, kk]], axis=0)
        x = jnp.dot(g_ref[kk], a, preferred_element_type=F32)
        xr = x[:s2].astype(BF16)
        xi = x[s2:].astype(BF16)
        o_ref[:, kk * F_WIDTH:(kk + 1) * F_WIDTH] = (
            jnp.dot(xr, cs_ref[:F_WIDTH, :], preferred_element_type=F32)
            + jnp.dot(xi, cs_ref[F_WIDTH:, :], preferred_element_type=F32)).astype(BF16)


def _fourier_prompt(f, seq):
    s1 = DFT_STAGE1
    s2 = seq // s1
    n1 = jnp.arange(s1, dtype=I32)
    th1 = _angle_table(n1[:, None] * n1[None, :], s1)
    m1 = jnp.concatenate([jnp.cos(th1), -jnp.sin(th1)], axis=0).astype(BF16)
    g = _stage2_matrices(s1, s2)
    cs = _channel_dft(seq)

    cols = s2 * F_WIDTH
    fv = f[:seq].reshape(s1, cols)
    tn = min(cols, 4096)
    a = pl.pallas_call(
        _dft_stage1_kernel,
        grid=(cols // tn,),
        in_specs=[pl.BlockSpec((2 * s1, s1), lambda j: (0, 0)), pl.BlockSpec((s1, tn), lambda j: (0, j))],
        out_specs=pl.BlockSpec((2 * s1, tn), lambda j: (0, j)),
        out_shape=jax.ShapeDtypeStruct((2 * s1, cols), BF16),
        compiler_params=_cparams("parallel"),
        name="dft_stage1",
    )(m1, fv)
    a4 = a.reshape(2, s1, s2, F_WIDTH)
    kb = DFT_STAGE2_BATCH
    out = pl.pallas_call(
        _dft_stage2_kernel,
        grid=(s1 // kb,),
        in_specs=[pl.BlockSpec((kb, 2 * s2, 2 * s2), lambda k: (k, 0, 0)),
                  pl.BlockSpec((2, kb, s2, F_WIDTH), lambda k: (0, k, 0, 0)),
                  pl.BlockSpec((2 * F_WIDTH, F_WIDTH), lambda k: (0, 0))],
        out_specs=pl.BlockSpec((s2, kb * F_WIDTH), lambda k: (0, k)),
        out_shape=jax.ShapeDtypeStruct((s2, s1 * F_WIDTH), BF16),
        compiler_params=_cparams("parallel"),
        name="dft_stage2",
    )(g, a4, cs)
    return out.reshape(seq, F_WIDTH)


def _dft_small_kernel(g_ref, x_ref, cs_ref, o_ref, stage_ref):
    s1 = DFT_SMALL_STAGE1
    assert s1 == 8
    s2 = x_ref.shape[1] // s1
    root = math.sqrt(0.5)
    n_seq = x_ref.shape[0]
    a_all = []
    for b in range(n_seq):
        x0, x1, x2, x3, x4, x5, x6, x7 = [x_ref[b, n1 * s2:(n1 + 1) * s2, :].astype(F32) for n1 in range(s1)]
        u, v, p, q = x0 - x4, x2 - x6, x1 - x5, x3 - x7
        e_sum, e_dif = (x0 + x4) + (x2 + x6), (x0 + x4) - (x2 + x6)
        o_sum, o_dif = (x1 + x5) + (x3 + x7), (x1 + x5) - (x3 + x7)
        rpq_m, rpq_p = root * (p - q), root * (p + q)
        zero = jnp.zeros_like(x0)
        a_re = [e_sum + o_sum, u + rpq_m, e_dif, u - rpq_m, e_sum - o_sum, u - rpq_m, e_dif, u + rpq_m]
        im1, im3 = -v - rpq_p, v - rpq_p
        a_im = [zero, im1, -o_dif, im3, zero, -im3, o_dif, -im1]
        a_all.append([jnp.concatenate([a_re[k1].astype(BF16), a_im[k1].astype(BF16)], axis=0) for k1 in range(s1)])

    for k1 in range(s1):
        a = jnp.concatenate([a_all[b][k1] for b in range(n_seq)], axis=1)
        x = jnp.dot(g_ref[k1], a, preferred_element_type=F32)
        for b in range(n_seq):
            xb = x[:, b * F_WIDTH:(b + 1) * F_WIDTH]
            out = (jnp.dot(xb[:s2].astype(BF16), cs_ref[:F_WIDTH, :], preferred_element_type=F32)
                   + jnp.dot(xb[s2:].astype(BF16), cs_ref[F_WIDTH:, :], preferred_element_type=F32))
            for c in range(F_WIDTH // LANES):
                stage_ref[b * (F_WIDTH // LANES) + c, pl.ds(k1, s2, stride=s1), :] = out[:, c * LANES:(c + 1) * LANES]
    for b in range(n_seq):
        o_ref[b] = jnp.concatenate([stage_ref[b * (F_WIDTH // LANES) + c] for c in range(F_WIDTH // LANES)],
                                   axis=1).astype(BF16)


def _fourier_sample(f, n_prompt, batch, seq):
    s1 = DFT_SMALL_STAGE1
    s2 = seq // s1
    g = _stage2_matrices(s1, s2)
    cs = _channel_dft(seq)
    f3 = f.reshape(f.shape[0] // seq, seq, F_WIDTH)
    first = n_prompt // seq
    nb = DFT_SMALL_BATCH
    assert batch % nb == 0 and first % nb == 0
    return pl.pallas_call(
        _dft_small_kernel,
        grid=(batch // nb,),
        in_specs=[pl.BlockSpec((s1, 2 * s2, 2 * s2), lambda b: (0, 0, 0)),
                  pl.BlockSpec((nb, seq, F_WIDTH), lambda b: (first // nb + b, 0, 0)),
                  pl.BlockSpec((2 * F_WIDTH, F_WIDTH), lambda b: (0, 0))],
        out_specs=pl.BlockSpec((nb, seq, F_WIDTH), lambda b: (b, 0, 0)),
        out_shape=jax.ShapeDtypeStruct((batch, seq, F_WIDTH), BF16),
        scratch_shapes=[pltpu.VMEM((nb * (F_WIDTH // LANES), seq, LANES), F32)],
        compiler_params=_cparams("parallel"),
        name="dft_small",
    )(g, f3, cs).reshape(batch * seq, F_WIDTH)


def _pack_bf16_pairs(x):
    w = x.shape[1] // 2
    bits = lax.bitcast_convert_type(x.astype(BF16).astype(F32), U32)
    return (bits[:, :w] >> 16) | (bits[:, w:] & jnp.uint32(0xFFFF0000))


def _unpack_bf16_pairs(u):
    lo = lax.bitcast_convert_type(u << 16, F32)
    hi = lax.bitcast_convert_type(u & jnp.uint32(0xFFFF0000), F32)
    return jnp.concatenate([lo, hi], axis=1)


def _layer_norm(h, g, b):
    mu = jnp.mean(h, axis=-1, keepdims=True)
    d = h - mu
    var = jnp.mean(d * d, axis=-1, keepdims=True)
    return d * lax.rsqrt(var + LN_EPS) * g + b


def _from_dilated_view(stage_ref, blk_ref, dil):
    if dil == 1:
        return blk_ref[...].astype(F32)
    rows = blk_ref.shape[0]
    for r in range(dil):
        for c in range(GROUP_WIDTH // LANES):
            lo = r * GROUP_WIDTH + c * LANES
            stage_ref[c, pl.ds(r, rows, stride=dil), :] = blk_ref[:, lo:lo + LANES].astype(F32)
    return jnp.concatenate([stage_ref[c] for c in range(GROUP_WIDTH // LANES)], axis=1)


def _merge_kernel(xp_ref, xs_ref, wg_ref, o1_ref, o2_ref, o3_ref, l1_ref, l2_ref, l3_ref, fp_ref, fs_ref,
                  wao_ref, wfo_ref, wo_ref, g_ref, b_ref, wr2_ref, br_ref, tri_ref,
                  x1_ref, pk_ref, idx_ref, gate_ref, rank_ref, cnt_ref, run_ref, *stage_refs, n_prompt_tiles):
    i = pl.program_id(0)

    @pl.when(i == 0)
    def _():
        run_ref[...] = jnp.zeros_like(run_ref)

    is_prompt = i < n_prompt_tiles
    x = jnp.where(is_prompt, xp_ref[...], xs_ref[...])
    gpre = jnp.dot(x.astype(BF16), wg_ref[...], preferred_element_type=F32)
    o1, o2, o3 = [_from_dilated_view(stage_refs[gi], ref, d)
                  for gi, (ref, d) in enumerate(zip((o1_ref, o2_ref, o3_ref), DILATIONS))]
    l1, l2, l3 = [_from_dilated_view(stage_refs[N_GROUPS + gi], ref, d)
                  for gi, (ref, d) in enumerate(zip((l1_ref, l2_ref, l3_ref), DILATIONS))]
    m = jnp.maximum(jnp.maximum(l1, l2), l3)
    e1, e2, e3 = jnp.exp(l1 - m), jnp.exp(l2 - m), jnp.exp(l3 - m)
    attn = (e1 * o1 + e2 * o2 + e3 * o3) * (1.0 / (e1 + e2 + e3))
    a = jnp.dot(attn.astype(BF16), wao_ref[...], preferred_element_type=F32)
    four = jnp.where(is_prompt, fp_ref[...], fs_ref[...])
    ff = jnp.dot(four, wfo_ref[...], preferred_element_type=F32)
    merged = jax.nn.sigmoid(gpre[:, :D_MODEL]) * a + jax.nn.sigmoid(gpre[:, D_MODEL:]) * ff
    mix = jnp.dot(merged.astype(BF16), wo_ref[...], preferred_element_type=F32)
    x1 = _layer_norm(DEEPNORM_ALPHA * x + mix, g_ref[...], b_ref[...])
    x1_ref[...] = x1
    pk_ref[...] = _pack_bf16_pairs(x1)

    xh = x1.astype(BF16)
    xl = (x1 - xh.astype(F32)).astype(BF16)
    tm = x1.shape[0]
    cross = lax.dot_general(wr2_ref[...], jnp.concatenate([xh, xl], axis=0), (((1,), (1,)), ((), ())),
                            preferred_element_type=F32)
    logits = ((cross[:N_EXPERTS, :tm] + cross[N_EXPERTS:, :tm])
              + (cross[:N_EXPERTS, tm:] + cross[N_EXPERTS:, tm:])) + br_ref[...]
    eio = lax.broadcasted_iota(I32, (N_EXPERTS, tm), 0)
    work = logits
    vals, sels, picks = [], [], []
    for _ in range(TOP_K):
        mk = jnp.max(work, axis=0, keepdims=True)
        ik = jnp.min(jnp.where(work == mk, eio, N_EXPERTS), axis=0, keepdims=True)
        sel = eio == ik
        vals.append(mk)
        sels.append(sel)
        picks.append(ik)
        work = jnp.where(sel, -jnp.inf, work)
    es = [jnp.exp(v - vals[0]) for v in vals]
    inv_den = 1.0 / (es[0] + es[1] + es[2] + es[3])
    chosen = (sels[0] | sels[1] | sels[2] | sels[3])
    chosen_f = chosen.astype(F32)
    prefix = jnp.dot(chosen_f.astype(BF16), tri_ref[...], preferred_element_type=F32)
    before = prefix + run_ref[...]
    for kk in range(TOP_K):
        idx_ref[kk:kk + 1, :] = picks[kk]
        gate_ref[kk:kk + 1, :] = es[kk] * inv_den
        rank_ref[kk:kk + 1, :] = jnp.sum(jnp.where(sels[kk], before, 0.0), axis=0, keepdims=True).astype(I32)
    run_ref[...] = run_ref[...] + jnp.sum(chosen_f, axis=1, keepdims=True)
    cnt_ref[...] = jnp.broadcast_to(run_ref[...], cnt_ref.shape).astype(I32)


def _merge_route(xp, xs, w_gate, o_list, l_list, four_p, four_s, w_ao, w_fo, w_o, ln_g, ln_b, w_r, b_r, tile0,
                 n_tiles):
    n_p = xp.shape[0]
    tm = TOKEN_TILE
    n_t = n_tiles * tm
    npt = n_p // tm
    w_rt = w_r.T
    w_rh = w_rt.astype(BF16)
    w_rl = (w_rt - w_rh.astype(F32)).astype(BF16)
    tri = (jnp.arange(tm)[:, None] < jnp.arange(tm)[None, :]).astype(BF16)
    row = lambda w: pl.BlockSpec((tm, w), lambda i: (i, 0))
    prompt_row = lambda w: pl.BlockSpec((tm, w), lambda i: (jnp.clip(tile0 + i, 0, npt - 1), 0))
    sample_row = lambda w: pl.BlockSpec((tm, w), lambda i: (jnp.maximum(tile0 + i - npt, 0), 0))
    full = lambda r, c: pl.BlockSpec((r, c), lambda i: (0, 0))
    lane_row = pl.BlockSpec((TOP_K, tm), lambda i: (0, i))
    views = [pl.BlockSpec((tm // d, d * GROUP_WIDTH), lambda i: (tile0 + i, 0)) for d in DILATIONS]
    return pl.pallas_call(
        functools.partial(_merge_kernel, n_prompt_tiles=npt - tile0),
        grid=(n_tiles,),
        in_specs=[
            prompt_row(D_MODEL), sample_row(D_MODEL),
            full(D_MODEL, 2 * D_MODEL),
            *views, *views,
            prompt_row(F_WIDTH), sample_row(F_WIDTH),
            full(GROUP_WIDTH, D_MODEL), full(F_WIDTH, D_MODEL), full(D_MODEL, D_MODEL),
            full(1, D_MODEL), full(1, D_MODEL),
            full(2 * N_EXPERTS, D_MODEL), full(N_EXPERTS, 1),
            full(tm, tm),
        ],
        out_specs=[row(D_MODEL), row(D_MODEL // 2), lane_row, lane_row, lane_row, full(N_EXPERTS, LANES)],
        out_shape=[
            jax.ShapeDtypeStruct((n_t, D_MODEL), F32),
            jax.ShapeDtypeStruct((n_t, D_MODEL // 2), U32),
            jax.ShapeDtypeStruct((TOP_K, n_t), I32),
            jax.ShapeDtypeStruct((TOP_K, n_t), F32),
            jax.ShapeDtypeStruct((TOP_K, n_t), I32),
            jax.ShapeDtypeStruct((N_EXPERTS, LANES), I32),
        ],
        scratch_shapes=[pltpu.VMEM((N_EXPERTS, 1), F32)]
        + [pltpu.VMEM((GROUP_WIDTH // LANES, tm, LANES), F32)] * (2 * N_GROUPS),
        compiler_params=_cparams("arbitrary"),
        name="merge_ln_route",
    )(xp, xs, w_gate, *o_list, *l_list, four_p, four_s, w_ao, w_fo, w_o, ln_g.reshape(1, -1), ln_b.reshape(1, -1),
      jnp.concatenate([w_rh, w_rl], axis=0), b_r.reshape(-1, 1), tri)


def _sc_workers():
    info = plsc.get_sparse_core_info()
    return info.num_cores, info.num_cores * info.num_subcores


def _sc_dispatch(rows_src, pos_chunks, n_rows_out):
    n_t, w = rows_src.shape
    n_cores, n_workers = _sc_workers()
    assert n_workers * SC_CHUNK == SC_ROUND_ROWS and n_t % SC_ROUND_ROWS == 0
    chunks_per_worker = n_t // (n_workers * SC_CHUNK)
    mesh = plsc.VectorSubcoreMesh(core_axis_name="c", subcore_axis_name="s")

    @functools.partial(
        pl.kernel, mesh=mesh,
        out_type=jax.ShapeDtypeStruct((n_rows_out, w), rows_src.dtype),
        scratch_types=[pltpu.VMEM((TOP_K, SC_CHUNK), I32), pltpu.VMEM((SC_CHUNK, w), rows_src.dtype)],
    )
    def dispatch(src_hbm, pos_hbm, out_hbm, idx_v, rows_v):
        wid = lax.axis_index("s") * n_cores + lax.axis_index("c")

        @pl.loop(0, chunks_per_worker)
        def _(j):
            chunk = wid * chunks_per_worker + j
            pltpu.sync_copy(pos_hbm.at[chunk], idx_v)
            pltpu.sync_copy(src_hbm.at[pl.ds(chunk * SC_CHUNK, SC_CHUNK)], rows_v)
            for kk in range(TOP_K):
                pltpu.sync_copy(rows_v, out_hbm.at[idx_v.at[kk]])

    return dispatch(rows_src, pos_chunks)


def _sc_combine(rows_src, pos_chunks, n_t):
    w = rows_src.shape[1]
    n_cores, n_workers = _sc_workers()
    assert n_workers * SC_CHUNK == SC_ROUND_ROWS and n_t % SC_ROUND_ROWS == 0
    chunks_per_worker = n_t // (n_workers * SC_CHUNK)
    mesh = plsc.VectorSubcoreMesh(core_axis_name="c", subcore_axis_name="s")

    @functools.partial(
        pl.kernel, mesh=mesh,
        out_type=jax.ShapeDtypeStruct((TOP_K, n_t, w), rows_src.dtype),
        scratch_types=[pltpu.VMEM((TOP_K, SC_CHUNK), I32), pltpu.VMEM((SC_CHUNK, w), rows_src.dtype)],
    )
    def combine(src_hbm, pos_hbm, out_hbm, idx_v, rows_v):
        wid = lax.axis_index("s") * n_cores + lax.axis_index("c")

        @pl.loop(0, chunks_per_worker)
        def _(j):
            chunk = wid * chunks_per_worker + j
            pltpu.sync_copy(pos_hbm.at[chunk], idx_v)
            for kk in range(TOP_K):
                pltpu.sync_copy(src_hbm.at[idx_v.at[kk]], rows_v)
                pltpu.sync_copy(rows_v, out_hbm.at[kk, pl.ds(chunk * SC_CHUNK, SC_CHUNK)])

    return combine(rows_src, pos_chunks)


def _expert_kernel(be_ref, nv_ref, x_ref, *refs, prepare):
    if prepare:
        wu_ref, wdn_ref, p_ref, bg_ref, bl_ref, bd_ref, y_ref, wg_ref, wl_ref, wd_ref = refs
    else:
        wg_ref, wl_ref, wd_ref, bg_ref, bl_ref, bd_ref, y_ref = refs
    blk = pl.program_id(0)
    nv = nv_ref[blk]

    if prepare:
        new_expert = (blk == 0) | (be_ref[blk] != be_ref[jnp.maximum(blk - 1, 0)])

        @pl.when(new_expert)
        def _():
            half = SPLIT_CHUNK // 2
            for c in range(wu_ref.shape[2] // SPLIT_CHUNK):
                w = wu_ref[0, :, c * SPLIT_CHUNK:(c + 1) * SPLIT_CHUNK].astype(BF16)
                r = jnp.dot(w, p_ref[...], preferred_element_type=F32)
                wg_ref[0, :, c * half:(c + 1) * half] = r[:, :half].astype(BF16)
                wl_ref[0, :, c * half:(c + 1) * half] = r[:, half:].astype(BF16)
            wd_ref[...] = wdn_ref[...].astype(BF16)

    def ffn_rows(m):
        x = _unpack_bf16_pairs(x_ref[0:m, :])
        rows = lax.broadcasted_iota(I32, (m, 1), 0)
        x = jnp.where(rows < nv, x, 0.0).astype(BF16)
        acc = jnp.zeros((m, D_MODEL), F32) + bd_ref[0]
        for c in range(D_FF // FF_CHUNK):
            sl = slice(c * FF_CHUNK, (c + 1) * FF_CHUNK)
            hg = jnp.dot(x, wg_ref[0, :, sl], preferred_element_type=F32) + bg_ref[0, :, sl]
            hl = jnp.dot(x, wl_ref[0, :, sl], preferred_element_type=F32) + bl_ref[0, :, sl]
            glu = jnp.minimum(hg, SWIGLU_LIMIT)
            lin = jnp.clip(hl, -SWIGLU_LIMIT, SWIGLU_LIMIT)
            act = glu * jax.nn.sigmoid(SWIGLU_ALPHA * glu) * (lin + 1.0)
            acc = acc + jnp.dot(act.astype(BF16), wd_ref[0, sl, :], preferred_element_type=F32)
        y_ref[0:m, :] = _pack_bf16_pairs(acc)
        if m < y_ref.shape[0]:
            y_ref[m:, :] = jnp.zeros((y_ref.shape[0] - m, y_ref.shape[1]), y_ref.dtype)

    for m in range(EXPERT_SUB, x_ref.shape[0] + 1, EXPERT_SUB):
        pl.when((nv > m - EXPERT_SUB) & (nv <= m))(functools.partial(ffn_rows, m))

    @pl.when(nv == 0)
    def _():
        y_ref[...] = jnp.zeros_like(y_ref)


def _expert_ffn(xs_rows, block_e, n_valid, weights, b_glu, b_lin, b_down):
    n_rows = xs_rows.shape[0]
    bm = EXPERT_BLOCK
    prepare = len(weights) == 2
    wspec = lambda r, c: pl.BlockSpec((1, r, c), lambda b, be, nv: (be[b], 0, 0))
    rows_spec = pl.BlockSpec((bm, D_MODEL // 2), lambda b, be, nv: (b, 0))
    bias_specs = [wspec(1, D_FF), wspec(1, D_FF), wspec(1, D_MODEL)]
    y_shape = jax.ShapeDtypeStruct((n_rows, D_MODEL // 2), U32)
    if prepare:
        half = SPLIT_CHUNK // 2
        src = jnp.arange(SPLIT_CHUNK, dtype=I32)[:, None]
        dst = jnp.arange(SPLIT_CHUNK, dtype=I32)[None, :]
        sel = jnp.where(dst < half, src == 2 * dst, src == 2 * (dst - half) + 1).astype(BF16)
        operands = (*weights, sel)
        weight_specs = [wspec(D_MODEL, 2 * D_FF), wspec(D_FF, D_MODEL),
                        pl.BlockSpec((SPLIT_CHUNK, SPLIT_CHUNK), lambda b, be, nv: (0, 0))]
        prepared = jax.ShapeDtypeStruct((N_EXPERTS, D_MODEL, D_FF), BF16)
        out_specs = [rows_spec, wspec(D_MODEL, D_FF), wspec(D_MODEL, D_FF), wspec(D_FF, D_MODEL)]
        out_shape = [y_shape, prepared, prepared, prepared]
    else:
        operands = tuple(weights)
        weight_specs = [wspec(D_MODEL, D_FF), wspec(D_MODEL, D_FF), wspec(D_FF, D_MODEL)]
        out_specs, out_shape = rows_spec, y_shape
    grid_spec = pltpu.PrefetchScalarGridSpec(
        num_scalar_prefetch=2,
        grid=(n_rows // bm,),
        in_specs=[rows_spec, *weight_specs, *bias_specs],
        out_specs=out_specs,
    )
    out = pl.pallas_call(
        functools.partial(_expert_kernel, prepare=prepare),
        grid_spec=grid_spec,
        out_shape=out_shape,
        compiler_params=_cparams("arbitrary"),
        name="expert_ffn",
    )(block_e, n_valid, xs_rows, *operands, b_glu, b_lin, b_down)
    return (out[0], tuple(out[1:])) if prepare else (out, tuple(weights))


def _final_kernel(x1_ref, yg_ref, gt_ref, g_ref, b_ref, *rest):
    o_ref = rest[-1]
    gates = gt_ref[...]
    pad = jnp.zeros((LANES - TOP_K, gates.shape[1]), F32)
    gt = jnp.concatenate([gates, pad], axis=0).T
    ffn = jnp.zeros(x1_ref.shape, F32)
    for kk in range(TOP_K):
        ffn = ffn + gt[:, kk:kk + 1] * _unpack_bf16_pairs(yg_ref[kk])
    o_ref[...] = _layer_norm(DEEPNORM_ALPHA * x1_ref[...] + ffn, g_ref[...], b_ref[...])


def _final(x1, yg, gates, ln_g, ln_b, src_row0, n_rows, out_rows, dst_row0, out_prev=None):
    tm = FINAL_TILE
    assert src_row0 % tm == 0 and dst_row0 % tm == 0 and n_rows % tm == 0
    src, dst = src_row0 // tm, dst_row0 // tm
    in_specs = [
        pl.BlockSpec((tm, D_MODEL), lambda i: (src + i, 0)),
        pl.BlockSpec((TOP_K, tm, D_MODEL // 2), lambda i: (0, src + i, 0)),
        pl.BlockSpec((TOP_K, tm), lambda i: (0, src + i)),
        pl.BlockSpec((1, D_MODEL), lambda i: (0, 0)),
        pl.BlockSpec((1, D_MODEL), lambda i: (0, 0)),
    ]
    args = [x1, yg, gates, ln_g.reshape(1, -1), ln_b.reshape(1, -1)]
    aliases = {}
    if out_prev is not None:
        in_specs.append(pl.BlockSpec(memory_space=pl.ANY))
        args.append(out_prev)
        aliases = {len(args) - 1: 0}
    return pl.pallas_call(
        _final_kernel,
        grid=(n_rows // tm,),
        in_specs=in_specs,
        out_specs=pl.BlockSpec((tm, D_MODEL), lambda i: (dst + i, 0)),
        out_shape=jax.ShapeDtypeStruct((out_rows, D_MODEL), F32),
        input_output_aliases=aliases,
        compiler_params=_cparams("parallel"),
        name="combine_ln2",
    )(*args)


def _routing_tables(idx, rank, counts, n_blocks):
    bm = EXPERT_BLOCK
    padded = jnp.maximum((counts + bm - 1) // bm, 1) * bm
    pad_end = jnp.cumsum(padded)
    pad_start = pad_end - padded
    experts = jnp.arange(N_EXPERTS, dtype=I32)

    def lookup(table, e):
        shape = (N_EXPERTS,) + (1,) * e.ndim
        return jnp.sum(jnp.where(e[None] == experts.reshape(shape), table.reshape(shape), 0), axis=0)

    pos = lookup(pad_start, idx) + rank
    blk_row0 = jnp.arange(n_blocks, dtype=I32) * bm
    block_e = jnp.minimum(jnp.sum((pad_end[:, None] <= blk_row0[None, :]).astype(I32), axis=0), N_EXPERTS - 1)
    n_valid = jnp.clip(lookup(counts, block_e) - (blk_row0 - lookup(pad_start, block_e)), 0, bm)
    return pos.astype(I32), block_e.astype(I32), n_valid.astype(I32)


def _moe(pk, idx, rank, counts, weights, b_glu, b_lin, b_down):
    n_t = pk.shape[0]
    n_blocks = (n_t * TOP_K) // EXPERT_BLOCK + N_EXPERTS
    pos, block_e, n_valid = _routing_tables(idx, rank, counts, n_blocks)
    pos_chunks = pos.reshape(TOP_K, n_t // SC_CHUNK, SC_CHUNK).transpose(1, 0, 2)
    xs_rows = _sc_dispatch(pk, pos_chunks, n_blocks * EXPERT_BLOCK)
    y_rows, prepared = _expert_ffn(xs_rows, block_e, n_valid, weights, b_glu, b_lin, b_down)
    return _sc_combine(y_rows, pos_chunks, n_t), prepared


def _rope_tables(seq_max):
    inv = ROPE_THETA ** (-jnp.arange(0, HEAD_DIM, 2, dtype=F32) / HEAD_DIM)
    ang = jnp.arange(seq_max).astype(F32)[:, None] * inv[None, :]
    cos, sin = jnp.cos(ang), jnp.sin(ang)
    reps = LANES // HEAD_DIM
    cos_t = jnp.tile(jnp.concatenate([cos, cos], axis=1), (1, reps))
    sin_t = jnp.tile(jnp.concatenate([-sin, sin], axis=1), (1, reps))
    return cos_t, sin_t


def _encoder_layer(xp, xs, seq_prompt, batch_sample, seq_sample, w_in, w_attn_out, w_four_out, w_o, ln1_g, ln1_b,
                   w_router, b_router, w_up, b_up, w_down, b_down, ln2_g, ln2_b):
    n_p, n_s = xp.shape[0], xs.shape[0]
    assert n_p == seq_prompt and n_s == batch_sample * seq_sample
    assert seq_prompt % (DFT_STAGE1 * 16) == 0 and seq_sample % TOKEN_TILE == 0 and n_p % seq_sample == 0
    n_t = n_p + n_s
    assert n_t % (max(DILATIONS) * ATTN_TILE) == 0 and seq_sample % (max(DILATIONS) * ATTN_SUB) == 0

    w_proj = w_in[:, :PROJ_WIDTH].astype(BF16)
    w_gate = w_in[:, PROJ_WIDTH:].astype(BF16)
    cos_t, sin_t = _rope_tables(max(seq_prompt, seq_sample))
    q_views, k_views, v_views, f = _inproj(xp, xs, w_proj, cos_t, sin_t, seq_sample)

    o_list, l_list = [], []
    for gi, dil in enumerate(DILATIONS):
        o, lse = _attention_group(q_views[gi], k_views[gi], v_views[gi], dil, n_p, seq_sample)
        o_list.append(o)
        l_list.append(lse)

    four_p = _fourier_prompt(f, seq_prompt)
    four_s = _fourier_sample(f, n_p, batch_sample, seq_sample)

    b_glu = b_up[:, None, 0::2]
    b_lin = b_up[:, None, 1::2]
    w_ao, w_fo, w_ob = w_attn_out.astype(BF16), w_four_out.astype(BF16), w_o.astype(BF16)

    tiles = n_t // TOKEN_TILE
    align = math.lcm(SC_ROUND_ROWS, FINAL_TILE) // TOKEN_TILE
    bounds = [0]
    for share in MOE_PART_SHARES[:-1]:
        bounds.append(min(tiles, bounds[-1] + max(align, round(tiles * share / align) * align)))
    bounds.append(tiles)
    n_parts = len(bounds) - 1
    routed = []
    for p in range(n_parts):
        routed.append(_merge_route(xp, xs, w_gate, o_list, l_list, four_p, four_s, w_ao, w_fo, w_ob, ln1_g, ln1_b,
                                   w_router, b_router, bounds[p], bounds[p + 1] - bounds[p]))
    combined = []
    weights = (w_up, w_down)
    for x1, pk, idx, gates, rank, cnt in routed:
        yg, weights = _moe(pk, idx, rank, cnt[:, 0], weights, b_glu, b_lin, b_down[:, None, :])
        combined.append(yg)

    outs = [None, None]
    spans = [(0, n_p), (n_p, n_t)]
    for p, ((x1, pk, idx, gates, rank, cnt), yg) in enumerate(zip(routed, combined)):
        lo, hi = bounds[p] * TOKEN_TILE, bounds[p + 1] * TOKEN_TILE
        for which, (a, b) in enumerate(spans):
            s, e = max(lo, a), min(hi, b)
            if s < e:
                outs[which] = _final(x1, yg, gates, ln2_g, ln2_b, s - lo, e - s, b - a, s - a, outs[which])
    return outs[0], outs[1]


def kernel(x_prompt, x_sample, w_in, w_attn_out, w_four_out, w_o, ln1_g, ln1_b, w_router, b_router, w_up, b_up,
           w_down, b_down, ln2_g, ln2_b):
    assert w_in.shape[0] == DEPTH
    bp, sp, d = x_prompt.shape
    bs, ss, _ = x_sample.shape
    assert bp == 1 and d == D_MODEL
    y_p, y_s = _encoder_layer(
        x_prompt.reshape(sp, d), x_sample.reshape(bs * ss, d), sp, bs, ss,
        w_in[0], w_attn_out[0], w_four_out[0], w_o[0], ln1_g[0], ln1_b[0], w_router[0], b_router[0],
        w_up[0], b_up[0], w_down[0], b_down[0], ln2_g[0], ln2_b[0])
    return y_p.reshape(x_prompt.shape), y_s.reshape(x_sample.shape)
```

```python
import functools
import math

import jax
import jax.numpy as jnp
from jax import lax
from jax.experimental import pallas as pl
from jax.experimental.pallas import tpu as pltpu
from jax.experimental.pallas import tpu_sc as plsc

F32 = jnp.float32
BF16 = jnp.bfloat16
I32 = jnp.int32
U32 = jnp.uint32

D_MODEL = 1024
HEAD_DIM = 64
HEADS_PER_GROUP = 4
GROUP_WIDTH = HEADS_PER_GROUP * HEAD_DIM
DILATIONS = (1, 4, 16)
HALF_WINDOW = 64
N_GROUPS = len(DILATIONS)
QKV_WIDTH = N_GROUPS * GROUP_WIDTH
F_GROUPS = 4
F_GROUP_DIM = 64
F_WIDTH = F_GROUPS * F_GROUP_DIM
PROJ_WIDTH = 3 * QKV_WIDTH + F_WIDTH
N_EXPERTS = 32
TOP_K = 4
D_FF = 1024
SWIGLU_LIMIT = 7.0
SWIGLU_ALPHA = 1.702
LN_EPS = 1e-5
ROPE_THETA = 10000.0
NEG_INF = -1e30
DEPTH = 1
DEEPNORM_ALPHA = (2 * DEPTH) ** 0.25

LANES = 128
MXU_TILE = 256
TOKEN_TILE = 512
FINAL_TILE = 1024
ATTN_TILES = (2048, 1024)
ATTN_SUB = 128
EXPERT_BLOCK = 512
FF_CHUNK = 1024
SPLIT_CHUNK = MXU_TILE
DFT_STAGE1 = 128
DFT_STAGE2_BATCH = 8
DFT_SMALL_STAGE1 = 8
DFT_SMALL_BATCH = 4
SC_CHUNK = 64
SC_ROUND_ROWS = 2 * 16 * SC_CHUNK
MOE_PART_SHARES = (0.7, 0.3)
VMEM_LIMIT = 56 * 1024 * 1024


def _cparams(*sem):
    return pltpu.CompilerParams(dimension_semantics=sem, vmem_limit_bytes=VMEM_LIMIT)


def _to_dilated_view(stage_ref, out_ref, val, dil):
    if dil == 1:
        out_ref[...] = val.astype(BF16)
        return
    rows = val.shape[0] // dil
    for c in range(GROUP_WIDTH // LANES):
        stage_ref[c] = val[:, c * LANES:(c + 1) * LANES]
    for r in range(dil):
        for c in range(GROUP_WIDTH // LANES):
            lo = r * GROUP_WIDTH + c * LANES
            out_ref[:, lo:lo + LANES] = stage_ref[c, pl.ds(r, rows, stride=dil), :].astype(BF16)


def _inproj_kernel(xp_ref, xs_ref, w_ref, cos_ref, sin_ref, *refs, n_prompt_tiles):
    qkv_refs = refs[:3 * N_GROUPS]
    f_ref = refs[3 * N_GROUPS]
    stage_refs = refs[3 * N_GROUPS + 1:]
    i = pl.program_id(0)
    x = jnp.where(i < n_prompt_tiles, xp_ref[...], xs_ref[...]).astype(BF16)
    proj = jnp.dot(x, w_ref[...], preferred_element_type=F32)
    cos = cos_ref[...]
    sin = sin_ref[...]
    lane = lax.broadcasted_iota(I32, cos.shape, 1)
    first_half = (lane % HEAD_DIM) < (HEAD_DIM // 2)

    def rope(t):
        parts = []
        for c in range(GROUP_WIDTH // LANES):
            tc = t[:, c * LANES:(c + 1) * LANES]
            swapped = jnp.where(first_half, pltpu.roll(tc, LANES - HEAD_DIM // 2, 1), pltpu.roll(tc, HEAD_DIM // 2, 1))
            parts.append(tc * cos + swapped * sin)
        return jnp.concatenate(parts, axis=1)

    for gi, dil in enumerate(DILATIONS):
        sl = slice(gi * GROUP_WIDTH, (gi + 1) * GROUP_WIDTH)
        qg = rope(proj[:, sl]) * (HEAD_DIM ** -0.5)
        kg = rope(proj[:, QKV_WIDTH + gi * GROUP_WIDTH:QKV_WIDTH + (gi + 1) * GROUP_WIDTH])
        vg = proj[:, 2 * QKV_WIDTH + gi * GROUP_WIDTH:2 * QKV_WIDTH + (gi + 1) * GROUP_WIDTH]
        for which, val in enumerate((qg, kg, vg)):
            _to_dilated_view(stage_refs[which * N_GROUPS + gi], qkv_refs[which * N_GROUPS + gi], val, dil)
    f_ref[...] = proj[:, 3 * QKV_WIDTH:PROJ_WIDTH].astype(BF16)


def _inproj(xp, xs, w_proj, cos_t, sin_t, seq_sample):
    n_p, n_s = xp.shape[0], xs.shape[0]
    n_t = n_p + n_s
    tm = TOKEN_TILE
    npt = n_p // tm
    tiles_per_sample_seq = seq_sample // tm

    def table_idx(i):
        return (jnp.where(i < npt, i, (i - npt) % tiles_per_sample_seq), 0)

    view_specs = [pl.BlockSpec((tm // d, d * GROUP_WIDTH), lambda i: (i, 0)) for d in DILATIONS] * 3
    view_shapes = [jax.ShapeDtypeStruct((n_t // d, d * GROUP_WIDTH), BF16) for d in DILATIONS] * 3
    outs = pl.pallas_call(
        functools.partial(_inproj_kernel, n_prompt_tiles=npt),
        grid=(n_t // tm,),
        in_specs=[
            pl.BlockSpec((tm, D_MODEL), lambda i: (jnp.minimum(i, npt - 1), 0)),
            pl.BlockSpec((tm, D_MODEL), lambda i: (jnp.maximum(i - npt, 0), 0)),
            pl.BlockSpec((D_MODEL, PROJ_WIDTH), lambda i: (0, 0)),
            pl.BlockSpec((tm, LANES), table_idx),
            pl.BlockSpec((tm, LANES), table_idx),
        ],
        out_specs=view_specs + [pl.BlockSpec((tm, F_WIDTH), lambda i: (i, 0))],
        out_shape=view_shapes + [jax.ShapeDtypeStruct((n_t, F_WIDTH), BF16)],
        scratch_shapes=[pltpu.VMEM((GROUP_WIDTH // LANES, tm, LANES), F32)] * (3 * N_GROUPS),
        compiler_params=_cparams("parallel"),
        name="inproj_rope",
    )(xp, xs, w_proj, cos_t, sin_t)
    q_views, k_views, v_views = outs[0:N_GROUPS], outs[N_GROUPS:2 * N_GROUPS], outs[2 * N_GROUPS:3 * N_GROUPS]
    return q_views, k_views, v_views, outs[3 * N_GROUPS]


def _attn_kernel(q_ref, kl_ref, km_ref, kr_ref, vl_ref, vm_ref, vr_ref, o_ref, lse_ref, *, prompt_rows, sample_rows):
    i = pl.program_id(1)
    tq = q_ref.shape[0]
    head_of_lane = lax.broadcasted_iota(I32, (1, GROUP_WIDTH), 1) // HEAD_DIM

    def sub_block(j, kj, vj, valid):
        qj = q_ref[j * ATTN_SUB:(j + 1) * ATTN_SUB, :]
        q4 = jnp.concatenate([jnp.where(head_of_lane == h, qj, jnp.zeros_like(qj)) for h in range(HEADS_PER_GROUP)],
                             axis=0)
        s4 = lax.dot_general(q4, kj, (((1,), (1,)), ((), ())), preferred_element_type=F32)
        ps, scales, lses = [], [], []
        for h in range(HEADS_PER_GROUP):
            s = jnp.where(valid, s4[h * ATTN_SUB:(h + 1) * ATTN_SUB], NEG_INF)
            m = jnp.max(s, axis=1, keepdims=True)
            p = jnp.exp(s - m)
            l = jnp.sum(p, axis=1, keepdims=True)
            ps.append(p.astype(BF16))
            scales.append(1.0 / l)
            lses.append(m + jnp.log(l))
        o4 = jnp.dot(jnp.concatenate(ps, axis=0), vj, preferred_element_type=F32)
        acc = jnp.zeros((ATTN_SUB, GROUP_WIDTH), F32)
        lse_full = jnp.zeros((ATTN_SUB, GROUP_WIDTH), F32)
        for h in range(HEADS_PER_GROUP):
            mine = head_of_lane == h
            acc = jnp.where(mine, o4[h * ATTN_SUB:(h + 1) * ATTN_SUB] * scales[h], acc)
            lse_full = jnp.where(mine, lses[h], lse_full)
        o_ref[j * ATTN_SUB:(j + 1) * ATTN_SUB, :] = acc.astype(BF16)
        lse_ref[j * ATTN_SUB:(j + 1) * ATTN_SUB, :] = lse_full

    def banded_step():
        kext = jnp.concatenate([kl_ref[...], km_ref[...], kr_ref[...]], axis=0)
        vext = jnp.concatenate([vl_ref[...], vm_ref[...], vr_ref[...]], axis=0)
        kw = ATTN_SUB + 2 * HALF_WINDOW
        a = lax.broadcasted_iota(I32, (ATTN_SUB, kw), 0)
        c = lax.broadcasted_iota(I32, (ATTN_SUB, kw), 1)
        in_band = jnp.abs(c - HALF_WINDOW - a) <= HALF_WINDOW
        for j in range(tq // ATTN_SUB):
            r0 = i * tq + j * ATTN_SUB
            in_prompt = r0 < prompt_rows
            b = jnp.maximum(r0 - prompt_rows, 0) // sample_rows
            lo = jnp.where(in_prompt, 0, prompt_rows + b * sample_rows)
            hi = jnp.where(in_prompt, prompt_rows, prompt_rows + (b + 1) * sample_rows)
            key_row = r0 - HALF_WINDOW + c
            valid = in_band & (key_row >= lo) & (key_row < hi)
            sub_block(j, kext[j * ATTN_SUB:j * ATTN_SUB + kw, :], vext[j * ATTN_SUB:j * ATTN_SUB + kw, :], valid)

    def single_sequence_step():
        a = lax.broadcasted_iota(I32, (ATTN_SUB, ATTN_SUB), 0)
        c = lax.broadcasted_iota(I32, (ATTN_SUB, ATTN_SUB), 1)
        valid = jnp.abs(c - a) <= HALF_WINDOW
        for j in range(tq // ATTN_SUB):
            rows = slice(j * ATTN_SUB, (j + 1) * ATTN_SUB)
            sub_block(j, km_ref[rows, :], vm_ref[rows, :], valid)

    if sample_rows == ATTN_SUB and prompt_rows % tq == 0:
        in_sample = i * tq >= prompt_rows
        pl.when(in_sample)(single_sequence_step)
        pl.when(jnp.logical_not(in_sample))(banded_step)
    else:
        banded_step()


def _attention_group(qv, kv, vv, dil, n_prompt, seq_sample):
    rows = qv.shape[0]
    tq = next(t for t in ATTN_TILES if rows % t == 0)
    halo_per_tile = tq // HALF_WINDOW
    n_halo_blocks = rows // HALF_WINDOW
    main = pl.BlockSpec((tq, GROUP_WIDTH), lambda r, i: (i, r))
    left = pl.BlockSpec((HALF_WINDOW, GROUP_WIDTH), lambda r, i: (jnp.maximum(i * halo_per_tile - 1, 0), r))
    right = pl.BlockSpec((HALF_WINDOW, GROUP_WIDTH),
                         lambda r, i: (jnp.minimum((i + 1) * halo_per_tile, n_halo_blocks - 1), r))
    return pl.pallas_call(
        functools.partial(_attn_kernel, prompt_rows=n_prompt // dil, sample_rows=seq_sample // dil),
        grid=(dil, rows // tq),
        in_specs=[main, left, main, right, left, main, right],
        out_specs=[main, main],
        out_shape=[jax.ShapeDtypeStruct((rows, dil * GROUP_WIDTH), BF16),
                   jax.ShapeDtypeStruct((rows, dil * GROUP_WIDTH), F32)],
        compiler_params=_cparams("parallel", "parallel"),
        name=f"dilated_attention_{dil}",
    )(qv, kv, kv, kv, vv, vv, vv)


def _angle_table(idx, period):
    return (idx % period).astype(F32) * (2.0 * math.pi / period)


def _channel_dft(seq_len):
    c = jnp.arange(F_GROUP_DIM, dtype=I32)
    th = _angle_table(c[:, None] * c[None, :], F_GROUP_DIM)
    scale = (seq_len * F_GROUP_DIM) ** -0.5
    eye = jnp.eye(F_GROUPS, dtype=F32)
    cb = jnp.kron(eye, jnp.cos(th)) * scale
    sb = jnp.kron(eye, jnp.sin(th)) * scale
    return jnp.concatenate([cb, sb], axis=0).astype(BF16)


def _stage2_matrices(s1, s2):
    n2 = jnp.arange(s2, dtype=I32)[None, :]
    th_t = _angle_table(jnp.arange(s1, dtype=I32)[:, None] * n2, s1 * s2)
    th_f = _angle_table(jnp.arange(s2, dtype=I32)[:, None] * n2, s2)
    tc, ts = jnp.cos(th_t)[:, None, :], jnp.sin(th_t)[:, None, :]
    fc, fs = jnp.cos(th_f)[None, :, :], jnp.sin(th_f)[None, :, :]
    gr = tc * fc - ts * fs
    gi = -(ts * fc + tc * fs)
    return jnp.concatenate([jnp.concatenate([gr, -gi], axis=2), jnp.concatenate([gi, gr], axis=2)], axis=1).astype(BF16)


def _dft_stage1_kernel(m_ref, x_ref, a_ref):
    a_ref[...] = jnp.dot(m_ref[...], x_ref[...], preferred_element_type=F32).astype(BF16)


def _dft_stage2_kernel(g_ref, a_ref, cs_ref, o_ref):
    s2 = a_ref.shape[2]
    for kk in range(g_ref.shape[0]):
        a = jnp.concatenate([a_ref[0, kk], a_ref[1, kk]], axis=0)
        x = jnp.dot(g_ref[kk], a, preferred_element_type=F32)
        xr = x[:s2].astype(BF16)
        xi = x[s2:].astype(BF16)
        o_ref[:, kk * F_WIDTH:(kk + 1) * F_WIDTH] = (
            jnp.dot(xr, cs_ref[:F_WIDTH, :], preferred_element_type=F32)
            + jnp.dot(xi, cs_ref[F_WIDTH:, :], preferred_element_type=F32)).astype(BF16)


def _fourier_prompt(f, seq):
    s1 = DFT_STAGE1
    s2 = seq // s1
    n1 = jnp.arange(s1, dtype=I32)
    th1 = _angle_table(n1[:, None] * n1[None, :], s1)
    m1 = jnp.concatenate([jnp.cos(th1), -jnp.sin(th1)], axis=0).astype(BF16)
    g = _stage2_matrices(s1, s2)
    cs = _channel_dft(seq)

    cols = s2 * F_WIDTH
    fv = f[:seq].reshape(s1, cols)
    tn = min(cols, 4096)
    a = pl.pallas_call(
        _dft_stage1_kernel,
        grid=(cols // tn,),
        in_specs=[pl.BlockSpec((2 * s1, s1), lambda j: (0, 0)), pl.BlockSpec((s1, tn), lambda j: (0, j))],
        out_specs=pl.BlockSpec((2 * s1, tn), lambda j: (0, j)),
        out_shape=jax.ShapeDtypeStruct((2 * s1, cols), BF16),
        compiler_params=_cparams("parallel"),
        name="dft_stage1",
    )(m1, fv)
    a4 = a.reshape(2, s1, s2, F_WIDTH)
    kb = DFT_STAGE2_BATCH
    out = pl.pallas_call(
        _dft_stage2_kernel,
        grid=(s1 // kb,),
        in_specs=[pl.BlockSpec((kb, 2 * s2, 2 * s2), lambda k: (k, 0, 0)),
                  pl.BlockSpec((2, kb, s2, F_WIDTH), lambda k: (0, k, 0, 0)),
                  pl.BlockSpec((2 * F_WIDTH, F_WIDTH), lambda k: (0, 0))],
        out_specs=pl.BlockSpec((s2, kb * F_WIDTH), lambda k: (0, k)),
        out_shape=jax.ShapeDtypeStruct((s2, s1 * F_WIDTH), BF16),
        compiler_params=_cparams("parallel"),
        name="dft_stage2",
    )(g, a4, cs)
    return out.reshape(seq, F_WIDTH)


def _dft_small_kernel(g_ref, x_ref, cs_ref, o_ref, stage_ref):
    s1 = DFT_SMALL_STAGE1
    assert s1 == 8
    s2 = x_ref.shape[1] // s1
    root = math.sqrt(0.5)
    n_seq = x_ref.shape[0]
    a_all = []
    for b in range(n_seq):
        x0, x1, x2, x3, x4, x5, x6, x7 = [x_ref[b, n1 * s2:(n1 + 1) * s2, :].astype(F32) for n1 in range(s1)]
        u, v, p, q = x0 - x4, x2 - x6, x1 - x5, x3 - x7
        e_sum, e_dif = (x0 + x4) + (x2 + x6), (x0 + x4) - (x2 + x6)
        o_sum, o_dif = (x1 + x5) + (x3 + x7), (x1 + x5) - (x3 + x7)
        rpq_m, rpq_p = root * (p - q), root * (p + q)
        zero = jnp.zeros_like(x0)
        a_re = [e_sum + o_sum, u + rpq_m, e_dif, u - rpq_m, e_sum - o_sum, u - rpq_m, e_dif, u + rpq_m]
        im1, im3 = -v - rpq_p, v - rpq_p
        a_im = [zero, im1, -o_dif, im3, zero, -im3, o_dif, -im1]
        a_all.append([jnp.concatenate([a_re[k1].astype(BF16), a_im[k1].astype(BF16)], axis=0) for k1 in range(s1)])

    for k1 in range(s1):
        a = jnp.concatenate([a_all[b][k1] for b in range(n_seq)], axis=1)
        x = jnp.dot(g_ref[k1], a, preferred_element_type=F32)
        for b in range(n_seq):
            xb = x[:, b * F_WIDTH:(b + 1) * F_WIDTH]
            out = (jnp.dot(xb[:s2].astype(BF16), cs_ref[:F_WIDTH, :], preferred_element_type=F32)
                   + jnp.dot(xb[s2:].astype(BF16), cs_ref[F_WIDTH:, :], preferred_element_type=F32))
            for c in range(F_WIDTH // LANES):
                stage_ref[b * (F_WIDTH // LANES) + c, pl.ds(k1, s2, stride=s1), :] = out[:, c * LANES:(c + 1) * LANES]
    for b in range(n_seq):
        o_ref[b] = jnp.concatenate([stage_ref[b * (F_WIDTH // LANES) + c] for c in range(F_WIDTH // LANES)],
                                   axis=1).astype(BF16)


def _fourier_sample(f, n_prompt, batch, seq):
    s1 = DFT_SMALL_STAGE1
    s2 = seq // s1
    g = _stage2_matrices(s1, s2)
    cs = _channel_dft(seq)
    f3 = f.reshape(f.shape[0] // seq, seq, F_WIDTH)
    first = n_prompt // seq
    nb = DFT_SMALL_BATCH
    assert batch % nb == 0 and first % nb == 0
    return pl.pallas_call(
        _dft_small_kernel,
        grid=(batch // nb,),
        in_specs=[pl.BlockSpec((s1, 2 * s2, 2 * s2), lambda b: (0, 0, 0)),
                  pl.BlockSpec((nb, seq, F_WIDTH), lambda b: (first // nb + b, 0, 0)),
                  pl.BlockSpec((2 * F_WIDTH, F_WIDTH), lambda b: (0, 0))],
        out_specs=pl.BlockSpec((nb, seq, F_WIDTH), lambda b: (b, 0, 0)),
        out_shape=jax.ShapeDtypeStruct((batch, seq, F_WIDTH), BF16),
        scratch_shapes=[pltpu.VMEM((nb * (F_WIDTH // LANES), seq, LANES), F32)],
        compiler_params=_cparams("parallel"),
        name="dft_small",
    )(g, f3, cs).reshape(batch * seq, F_WIDTH)


def _pack_bf16_pairs(x):
    w = x.shape[1] // 2
    bits = lax.bitcast_convert_type(x.astype(BF16).astype(F32), U32)
    return (bits[:, :w] >> 16) | (bits[:, w:] & jnp.uint32(0xFFFF0000))


def _unpack_bf16_pairs(u):
    lo = lax.bitcast_convert_type(u << 16, F32)
    hi = lax.bitcast_convert_type(u & jnp.uint32(0xFFFF0000), F32)
    return jnp.concatenate([lo, hi], axis=1)


def _layer_norm(h, g, b):
    mu = jnp.mean(h, axis=-1, keepdims=True)
    d = h - mu
    var = jnp.mean(d * d, axis=-1, keepdims=True)
    return d * lax.rsqrt(var + LN_EPS) * g + b


def _from_dilated_view(stage_ref, blk_ref, dil):
    if dil == 1:
        return blk_ref[...].astype(F32)
    rows = blk_ref.shape[0]
    for r in range(dil):
        for c in range(GROUP_WIDTH // LANES):
            lo = r * GROUP_WIDTH + c * LANES
            stage_ref[c, pl.ds(r, rows, stride=dil), :] = blk_ref[:, lo:lo + LANES].astype(F32)
    return jnp.concatenate([stage_ref[c] for c in range(GROUP_WIDTH // LANES)], axis=1)


def _merge_kernel(xp_ref, xs_ref, wg_ref, o1_ref, o2_ref, o3_ref, l1_ref, l2_ref, l3_ref, fp_ref, fs_ref,
                  wao_ref, wfo_ref, wo_ref, g_ref, b_ref, wr2_ref, br_ref, tri_ref,
                  x1_ref, pk_ref, idx_ref, gate_ref, rank_ref, cnt_ref, run_ref, *stage_refs, n_prompt_tiles):
    i = pl.program_id(0)

    @pl.when(i == 0)
    def _():
        run_ref[...] = jnp.zeros_like(run_ref)

    is_prompt = i < n_prompt_tiles
    x = jnp.where(is_prompt, xp_ref[...], xs_ref[...])
    gpre = jnp.dot(x.astype(BF16), wg_ref[...], preferred_element_type=F32)
    o1, o2, o3 = [_from_dilated_view(stage_refs[gi], ref, d)
                  for gi, (ref, d) in enumerate(zip((o1_ref, o2_ref, o3_ref), DILATIONS))]
    l1, l2, l3 = [_from_dilated_view(stage_refs[N_GROUPS + gi], ref, d)
                  for gi, (ref, d) in enumerate(zip((l1_ref, l2_ref, l3_ref), DILATIONS))]
    m = jnp.maximum(jnp.maximum(l1, l2), l3)
    e1, e2, e3 = jnp.exp(l1 - m), jnp.exp(l2 - m), jnp.exp(l3 - m)
    attn = (e1 * o1 + e2 * o2 + e3 * o3) * (1.0 / (e1 + e2 + e3))
    a = jnp.dot(attn.astype(BF16), wao_ref[...], preferred_element_type=F32)
    four = jnp.where(is_prompt, fp_ref[...], fs_ref[...])
    ff = jnp.dot(four, wfo_ref[...], preferred_element_type=F32)
    merged = jax.nn.sigmoid(gpre[:, :D_MODEL]) * a + jax.nn.sigmoid(gpre[:, D_MODEL:]) * ff
    mix = jnp.dot(merged.astype(BF16), wo_ref[...], preferred_element_type=F32)
    x1 = _layer_norm(DEEPNORM_ALPHA * x + mix, g_ref[...], b_ref[...])
    x1_ref[...] = x1
    pk_ref[...] = _pack_bf16_pairs(x1)

    xh = x1.astype(BF16)
    xl = (x1 - xh.astype(F32)).astype(BF16)
    tm = x1.shape[0]
    cross = lax.dot_general(wr2_ref[...], jnp.concatenate([xh, xl], axis=0), (((1,), (1,)), ((), ())),
                            preferred_element_type=F32)
    logits = ((cross[:N_EXPERTS, :tm] + cross[N_EXPERTS:, :tm])
              + (cross[:N_EXPERTS, tm:] + cross[N_EXPERTS:, tm:])) + br_ref[...]
    eio = lax.broadcasted_iota(I32, (N_EXPERTS, tm), 0)
    work = logits
    vals, sels, picks = [], [], []
    for _ in range(TOP_K):
        mk = jnp.max(work, axis=0, keepdims=True)
        ik = jnp.min(jnp.where(work == mk, eio, N_EXPERTS), axis=0, keepdims=True)
        sel = eio == ik
        vals.append(mk)
        sels.append(sel)
        picks.append(ik)
        work = jnp.where(sel, -jnp.inf, work)
    es = [jnp.exp(v - vals[0]) for v in vals]
    inv_den = 1.0 / (es[0] + es[1] + es[2] + es[3])
    chosen = (sels[0] | sels[1] | sels[2] | sels[3])
    chosen_f = chosen.astype(F32)
    prefix = jnp.dot(chosen_f.astype(BF16), tri_ref[...], preferred_element_type=F32)
    before = prefix + run_ref[...]
    for kk in range(TOP_K):
        idx_ref[kk:kk + 1, :] = picks[kk]
        gate_ref[kk:kk + 1, :] = es[kk] * inv_den
        rank_ref[kk:kk + 1, :] = jnp.sum(jnp.where(sels[kk], before, 0.0), axis=0, keepdims=True).astype(I32)
    run_ref[...] = run_ref[...] + jnp.sum(chosen_f, axis=1, keepdims=True)
    cnt_ref[...] = jnp.broadcast_to(run_ref[...], cnt_ref.shape).astype(I32)


def _merge_route(xp, xs, w_gate, o_list, l_list, four_p, four_s, w_ao, w_fo, w_o, ln_g, ln_b, w_r, b_r, tile0,
                 n_tiles):
    n_p = xp.shape[0]
    tm = TOKEN_TILE
    n_t = n_tiles * tm
    npt = n_p // tm
    w_rt = w_r.T
    w_rh = w_rt.astype(BF16)
    w_rl = (w_rt - w_rh.astype(F32)).astype(BF16)
    tri = (jnp.arange(tm)[:, None] < jnp.arange(tm)[None, :]).astype(BF16)
    row = lambda w: pl.BlockSpec((tm, w), lambda i: (i, 0))
    prompt_row = lambda w: pl.BlockSpec((tm, w), lambda i: (jnp.clip(tile0 + i, 0, npt - 1), 0))
    sample_row = lambda w: pl.BlockSpec((tm, w), lambda i: (jnp.maximum(tile0 + i - npt, 0), 0))
    full = lambda r, c: pl.BlockSpec((r, c), lambda i: (0, 0))
    lane_row = pl.BlockSpec((TOP_K, tm), lambda i: (0, i))
    views = [pl.BlockSpec((tm // d, d * GROUP_WIDTH), lambda i: (tile0 + i, 0)) for d in DILATIONS]
    return pl.pallas_call(
        functools.partial(_merge_kernel, n_prompt_tiles=npt - tile0),
        grid=(n_tiles,),
        in_specs=[
            prompt_row(D_MODEL), sample_row(D_MODEL),
            full(D_MODEL, 2 * D_MODEL),
            *views, *views,
            prompt_row(F_WIDTH), sample_row(F_WIDTH),
            full(GROUP_WIDTH, D_MODEL), full(F_WIDTH, D_MODEL), full(D_MODEL, D_MODEL),
            full(1, D_MODEL), full(1, D_MODEL),
            full(2 * N_EXPERTS, D_MODEL), full(N_EXPERTS, 1),
            full(tm, tm),
        ],
        out_specs=[row(D_MODEL), row(D_MODEL // 2), lane_row, lane_row, lane_row, full(N_EXPERTS, LANES)],
        out_shape=[
            jax.ShapeDtypeStruct((n_t, D_MODEL), F32),
            jax.ShapeDtypeStruct((n_t, D_MODEL // 2), U32),
            jax.ShapeDtypeStruct((TOP_K, n_t), I32),
            jax.ShapeDtypeStruct((TOP_K, n_t), F32),
            jax.ShapeDtypeStruct((TOP_K, n_t), I32),
            jax.ShapeDtypeStruct((N_EXPERTS, LANES), I32),
        ],
        scratch_shapes=[pltpu.VMEM((N_EXPERTS, 1), F32)]
        + [pltpu.VMEM((GROUP_WIDTH // LANES, tm, LANES), F32)] * (2 * N_GROUPS),
        compiler_params=_cparams("arbitrary"),
        name="merge_ln_route",
    )(xp, xs, w_gate, *o_list, *l_list, four_p, four_s, w_ao, w_fo, w_o, ln_g.reshape(1, -1), ln_b.reshape(1, -1),
      jnp.concatenate([w_rh, w_rl], axis=0), b_r.reshape(-1, 1), tri)


def _sc_workers():
    info = plsc.get_sparse_core_info()
    return info.num_cores, info.num_cores * info.num_subcores


def _sc_dispatch(rows_src, pos_chunks, n_rows_out):
    n_t, w = rows_src.shape
    n_cores, n_workers = _sc_workers()
    assert n_workers * SC_CHUNK == SC_ROUND_ROWS and n_t % SC_ROUND_ROWS == 0
    chunks_per_worker = n_t // (n_workers * SC_CHUNK)
    mesh = plsc.VectorSubcoreMesh(core_axis_name="c", subcore_axis_name="s")

    @functools.partial(
        pl.kernel, mesh=mesh,
        out_type=jax.ShapeDtypeStruct((n_rows_out, w), rows_src.dtype),
        scratch_types=[pltpu.VMEM((TOP_K, SC_CHUNK), I32), pltpu.VMEM((SC_CHUNK, w), rows_src.dtype)],
    )
    def dispatch(src_hbm, pos_hbm, out_hbm, idx_v, rows_v):
        wid = lax.axis_index("s") * n_cores + lax.axis_index("c")

        @pl.loop(0, chunks_per_worker)
        def _(j):
            chunk = wid * chunks_per_worker + j
            pltpu.sync_copy(pos_hbm.at[chunk], idx_v)
            pltpu.sync_copy(src_hbm.at[pl.ds(chunk * SC_CHUNK, SC_CHUNK)], rows_v)
            for kk in range(TOP_K):
                pltpu.sync_copy(rows_v, out_hbm.at[idx_v.at[kk]])

    return dispatch(rows_src, pos_chunks)


def _sc_combine(rows_src, pos_chunks, n_t):
    w = rows_src.shape[1]
    n_cores, n_workers = _sc_workers()
    assert n_workers * SC_CHUNK == SC_ROUND_ROWS and n_t % SC_ROUND_ROWS == 0
    chunks_per_worker = n_t // (n_workers * SC_CHUNK)
    mesh = plsc.VectorSubcoreMesh(core_axis_name="c", subcore_axis_name="s")

    @functools.partial(
        pl.kernel, mesh=mesh,
        out_type=jax.ShapeDtypeStruct((TOP_K, n_t, w), rows_src.dtype),
        scratch_types=[pltpu.VMEM((TOP_K, SC_CHUNK), I32), pltpu.VMEM((SC_CHUNK, w), rows_src.dtype)],
    )
    def combine(src_hbm, pos_hbm, out_hbm, idx_v, rows_v):
        wid = lax.axis_index("s") * n_cores + lax.axis_index("c")

        @pl.loop(0, chunks_per_worker)
        def _(j):
            chunk = wid * chunks_per_worker + j
            pltpu.sync_copy(pos_hbm.at[chunk], idx_v)
            for kk in range(TOP_K):
                pltpu.sync_copy(src_hbm.at[idx_v.at[kk]], rows_v)
                pltpu.sync_copy(rows_v, out_hbm.at[kk, pl.ds(chunk * SC_CHUNK, SC_CHUNK)])

    return combine(rows_src, pos_chunks)


def _expert_kernel(be_ref, nv_ref, x_ref, *refs, prepare):
    if prepare:
        wu_ref, wdn_ref, p_ref, bg_ref, bl_ref, bd_ref, y_ref, wg_ref, wl_ref, wd_ref = refs
    else:
        wg_ref, wl_ref, wd_ref, bg_ref, bl_ref, bd_ref, y_ref = refs
    blk = pl.program_id(0)
    nv = nv_ref[blk]

    if prepare:
        new_expert = (blk == 0) | (be_ref[blk] != be_ref[jnp.maximum(blk - 1, 0)])

        @pl.when(new_expert)
        def _():
            half = SPLIT_CHUNK // 2
            for c in range(wu_ref.shape[2] // SPLIT_CHUNK):
                w = wu_ref[0, :, c * SPLIT_CHUNK:(c + 1) * SPLIT_CHUNK].astype(BF16)
                r = jnp.dot(w, p_ref[...], preferred_element_type=F32)
                wg_ref[0, :, c * half:(c + 1) * half] = r[:, :half].astype(BF16)
                wl_ref[0, :, c * half:(c + 1) * half] = r[:, half:].astype(BF16)
            wd_ref[...] = wdn_ref[...].astype(BF16)

    @pl.when(nv > 0)
    def _():
        x = _unpack_bf16_pairs(x_ref[...])
        rows = lax.broadcasted_iota(I32, (x.shape[0], 1), 0)
        x = jnp.where(rows < nv, x, 0.0).astype(BF16)
        acc = jnp.zeros((x.shape[0], D_MODEL), F32) + bd_ref[0]
        for c in range(D_FF // FF_CHUNK):
            sl = slice(c * FF_CHUNK, (c + 1) * FF_CHUNK)
            hg = jnp.dot(x, wg_ref[0, :, sl], preferred_element_type=F32) + bg_ref[0, :, sl]
            hl = jnp.dot(x, wl_ref[0, :, sl], preferred_element_type=F32) + bl_ref[0, :, sl]
            glu = jnp.minimum(hg, SWIGLU_LIMIT)
            lin = jnp.clip(hl, -SWIGLU_LIMIT, SWIGLU_LIMIT)
            act = glu * jax.nn.sigmoid(SWIGLU_ALPHA * glu) * (lin + 1.0)
            acc = acc + jnp.dot(act.astype(BF16), wd_ref[0, sl, :], preferred_element_type=F32)
        y_ref[...] = _pack_bf16_pairs(acc)

    @pl.when(nv == 0)
    def _():
        y_ref[...] = jnp.zeros_like(y_ref)


def _expert_ffn(xs_rows, block_e, n_valid, weights, b_glu, b_lin, b_down):
    n_rows = xs_rows.shape[0]
    bm = EXPERT_BLOCK
    prepare = len(weights) == 2
    wspec = lambda r, c: pl.BlockSpec((1, r, c), lambda b, be, nv: (be[b], 0, 0))
    rows_spec = pl.BlockSpec((bm, D_MODEL // 2), lambda b, be, nv: (b, 0))
    bias_specs = [wspec(1, D_FF), wspec(1, D_FF), wspec(1, D_MODEL)]
    y_shape = jax.ShapeDtypeStruct((n_rows, D_MODEL // 2), U32)
    if prepare:
        half = SPLIT_CHUNK // 2
        src = jnp.arange(SPLIT_CHUNK, dtype=I32)[:, None]
        dst = jnp.arange(SPLIT_CHUNK, dtype=I32)[None, :]
        sel = jnp.where(dst < half, src == 2 * dst, src == 2 * (dst - half) + 1).astype(BF16)
        operands = (*weights, sel)
        weight_specs = [wspec(D_MODEL, 2 * D_FF), wspec(D_FF, D_MODEL),
                        pl.BlockSpec((SPLIT_CHUNK, SPLIT_CHUNK), lambda b, be, nv: (0, 0))]
        prepared = jax.ShapeDtypeStruct((N_EXPERTS, D_MODEL, D_FF), BF16)
        out_specs = [rows_spec, wspec(D_MODEL, D_FF), wspec(D_MODEL, D_FF), wspec(D_FF, D_MODEL)]
        out_shape = [y_shape, prepared, prepared, prepared]
    else:
        operands = tuple(weights)
        weight_specs = [wspec(D_MODEL, D_FF), wspec(D_MODEL, D_FF), wspec(D_FF, D_MODEL)]
        out_specs, out_shape = rows_spec, y_shape
    grid_spec = pltpu.PrefetchScalarGridSpec(
        num_scalar_prefetch=2,
        grid=(n_rows // bm,),
        in_specs=[rows_spec, *weight_specs, *bias_specs],
        out_specs=out_specs,
    )
    out = pl.pallas_call(
        functools.partial(_expert_kernel, prepare=prepare),
        grid_spec=grid_spec,
        out_shape=out_shape,
        compiler_params=_cparams("arbitrary"),
        name="expert_ffn",
    )(block_e, n_valid, xs_rows, *operands, b_glu, b_lin, b_down)
    return (out[0], tuple(out[1:])) if prepare else (out, tuple(weights))


def _final_kernel(x1_ref, yg_ref, gt_ref, g_ref, b_ref, *rest):
    o_ref = rest[-1]
    gates = gt_ref[...]
    pad = jnp.zeros((LANES - TOP_K, gates.shape[1]), F32)
    gt = jnp.concatenate([gates, pad], axis=0).T
    ffn = jnp.zeros(x1_ref.shape, F32)
    for kk in range(TOP_K):
        ffn = ffn + gt[:, kk:kk + 1] * _unpack_bf16_pairs(yg_ref[kk])
    o_ref[...] = _layer_norm(DEEPNORM_ALPHA * x1_ref[...] + ffn, g_ref[...], b_ref[...])


def _final(x1, yg, gates, ln_g, ln_b, src_row0, n_rows, out_rows, dst_row0, out_prev=None):
    tm = FINAL_TILE
    assert src_row0 % tm == 0 and dst_row0 % tm == 0 and n_rows % tm == 0
    src, dst = src_row0 // tm, dst_row0 // tm
    in_specs = [
        pl.BlockSpec((tm, D_MODEL), lambda i: (src + i, 0)),
        pl.BlockSpec((TOP_K, tm, D_MODEL // 2), lambda i: (0, src + i, 0)),
        pl.BlockSpec((TOP_K, tm), lambda i: (0, src + i)),
        pl.BlockSpec((1, D_MODEL), lambda i: (0, 0)),
        pl.BlockSpec((1, D_MODEL), lambda i: (0, 0)),
    ]
    args = [x1, yg, gates, ln_g.reshape(1, -1), ln_b.reshape(1, -1)]
    aliases = {}
    if out_prev is not None:
        in_specs.append(pl.BlockSpec(memory_space=pl.ANY))
        args.append(out_prev)
        aliases = {len(args) - 1: 0}
    return pl.pallas_call(
        _final_kernel,
        grid=(n_rows // tm,),
        in_specs=in_specs,
        out_specs=pl.BlockSpec((tm, D_MODEL), lambda i: (dst + i, 0)),
        out_shape=jax.ShapeDtypeStruct((out_rows, D_MODEL), F32),
        input_output_aliases=aliases,
        compiler_params=_cparams("parallel"),
        name="combine_ln2",
    )(*args)


def _routing_tables(idx, rank, counts, n_blocks):
    bm = EXPERT_BLOCK
    padded = jnp.maximum((counts + bm - 1) // bm, 1) * bm
    pad_end = jnp.cumsum(padded)
    pad_start = pad_end - padded
    experts = jnp.arange(N_EXPERTS, dtype=I32)

    def lookup(table, e):
        shape = (N_EXPERTS,) + (1,) * e.ndim
        return jnp.sum(jnp.where(e[None] == experts.reshape(shape), table.reshape(shape), 0), axis=0)

    pos = lookup(pad_start, idx) + rank
    blk_row0 = jnp.arange(n_blocks, dtype=I32) * bm
    block_e = jnp.minimum(jnp.sum((pad_end[:, None] <= blk_row0[None, :]).astype(I32), axis=0), N_EXPERTS - 1)
    n_valid = jnp.clip(lookup(counts, block_e) - (blk_row0 - lookup(pad_start, block_e)), 0, bm)
    return pos.astype(I32), block_e.astype(I32), n_valid.astype(I32)


def _moe(pk, idx, rank, counts, weights, b_glu, b_lin, b_down):
    n_t = pk.shape[0]
    n_blocks = (n_t * TOP_K) // EXPERT_BLOCK + N_EXPERTS
    pos, block_e, n_valid = _routing_tables(idx, rank, counts, n_blocks)
    pos_chunks = pos.reshape(TOP_K, n_t // SC_CHUNK, SC_CHUNK).transpose(1, 0, 2)
    xs_rows = _sc_dispatch(pk, pos_chunks, n_blocks * EXPERT_BLOCK)
    y_rows, prepared = _expert_ffn(xs_rows, block_e, n_valid, weights, b_glu, b_lin, b_down)
    return _sc_combine(y_rows, pos_chunks, n_t), prepared


def _rope_tables(seq_max):
    inv = ROPE_THETA ** (-jnp.arange(0, HEAD_DIM, 2, dtype=F32) / HEAD_DIM)
    ang = jnp.arange(seq_max).astype(F32)[:, None] * inv[None, :]
    cos, sin = jnp.cos(ang), jnp.sin(ang)
    reps = LANES // HEAD_DIM
    cos_t = jnp.tile(jnp.concatenate([cos, cos], axis=1), (1, reps))
    sin_t = jnp.tile(jnp.concatenate([-sin, sin], axis=1), (1, reps))
    return cos_t, sin_t


def _encoder_layer(xp, xs, seq_prompt, batch_sample, seq_sample, w_in, w_attn_out, w_four_out, w_o, ln1_g, ln1_b,
                   w_router, b_router, w_up, b_up, w_down, b_down, ln2_g, ln2_b):
    n_p, n_s = xp.shape[0], xs.shape[0]
    assert n_p == seq_prompt and n_s == batch_sample * seq_sample
    assert seq_prompt % (DFT_STAGE1 * 16) == 0 and seq_sample % TOKEN_TILE == 0 and n_p % seq_sample == 0
    n_t = n_p + n_s
    assert n_t % (max(DILATIONS) * min(ATTN_TILES)) == 0 and seq_sample % (max(DILATIONS) * ATTN_SUB) == 0

    w_proj = w_in[:, :PROJ_WIDTH].astype(BF16)
    w_gate = w_in[:, PROJ_WIDTH:].astype(BF16)
    cos_t, sin_t = _rope_tables(max(seq_prompt, seq_sample))
    q_views, k_views, v_views, f = _inproj(xp, xs, w_proj, cos_t, sin_t, seq_sample)

    o_list, l_list = [], []
    for gi, dil in enumerate(DILATIONS):
        o, lse = _attention_group(q_views[gi], k_views[gi], v_views[gi], dil, n_p, seq_sample)
        o_list.append(o)
        l_list.append(lse)

    four_p = _fourier_prompt(f, seq_prompt)
    four_s = _fourier_sample(f, n_p, batch_sample, seq_sample)

    b_glu = b_up[:, None, 0::2]
    b_lin = b_up[:, None, 1::2]
    w_ao, w_fo, w_ob = w_attn_out.astype(BF16), w_four_out.astype(BF16), w_o.astype(BF16)

    tiles = n_t // TOKEN_TILE
    align = math.lcm(SC_ROUND_ROWS, FINAL_TILE) // TOKEN_TILE
    bounds = [0]
    for share in MOE_PART_SHARES[:-1]:
        bounds.append(min(tiles, bounds[-1] + max(align, round(tiles * share / align) * align)))
    bounds.append(tiles)
    n_parts = len(bounds) - 1
    routed = []
    for p in range(n_parts):
        routed.append(_merge_route(xp, xs, w_gate, o_list, l_list, four_p, four_s, w_ao, w_fo, w_ob, ln1_g, ln1_b,
                                   w_router, b_router, bounds[p], bounds[p + 1] - bounds[p]))
    combined = []
    weights = (w_up, w_down)
    for x1, pk, idx, gates, rank, cnt in routed:
        yg, weights = _moe(pk, idx, rank, cnt[:, 0], weights, b_glu, b_lin, b_down[:, None, :])
        combined.append(yg)

    outs = [None, None]
    spans = [(0, n_p), (n_p, n_t)]
    for p, ((x1, pk, idx, gates, rank, cnt), yg) in enumerate(zip(routed, combined)):
        lo, hi = bounds[p] * TOKEN_TILE, bounds[p + 1] * TOKEN_TILE
        for which, (a, b) in enumerate(spans):
            s, e = max(lo, a), min(hi, b)
            if s < e:
                outs[which] = _final(x1, yg, gates, ln2_g, ln2_b, s - lo, e - s, b - a, s - a, outs[which])
    return outs[0], outs[1]


def kernel(x_prompt, x_sample, w_in, w_attn_out, w_four_out, w_o, ln1_g, ln1_b, w_router, b_router, w_up, b_up,
           w_down, b_down, ln2_g, ln2_b):
    assert w_in.shape[0] == DEPTH
    bp, sp, d = x_prompt.shape
    bs, ss, _ = x_sample.shape
    assert bp == 1 and d == D_MODEL
    y_p, y_s = _encoder_layer(
        x_prompt.reshape(sp, d), x_sample.reshape(bs * ss, d), sp, bs, ss,
        w_in[0], w_attn_out[0], w_four_out[0], w_o[0], ln1_g[0], ln1_b[0], w_router[0], b_router[0],
        w_up[0], b_up[0], w_down[0], b_down[0], ln2_g[0], ln2_b[0])
    return y_p.reshape(x_prompt.shape), y_s.reshape(x_sample.shape)
```

```python
import functools
import math

import jax
import jax.numpy as jnp
from jax import lax
from jax.experimental import pallas as pl
from jax.experimental.pallas import tpu as pltpu
from jax.experimental.pallas import tpu_sc as plsc

F32 = jnp.float32
BF16 = jnp.bfloat16
I32 = jnp.int32
U32 = jnp.uint32

D_MODEL = 1024
HEAD_DIM = 64
HEADS_PER_GROUP = 4
GROUP_WIDTH = HEADS_PER_GROUP * HEAD_DIM
DILATIONS = (1, 4, 16)
HALF_WINDOW = 64
N_GROUPS = len(DILATIONS)
QKV_WIDTH = N_GROUPS * GROUP_WIDTH
F_GROUPS = 4
F_GROUP_DIM = 64
F_WIDTH = F_GROUPS * F_GROUP_DIM
PROJ_WIDTH = 3 * QKV_WIDTH + F_WIDTH
N_EXPERTS = 32
TOP_K = 4
D_FF = 1024
SWIGLU_LIMIT = 7.0
SWIGLU_ALPHA = 1.702
LN_EPS = 1e-5
ROPE_THETA = 10000.0
NEG_INF = -1e30
DEPTH = 1
DEEPNORM_ALPHA = (2 * DEPTH) ** 0.25

LANES = 128
MXU_TILE = 256
TOKEN_TILE = 512
INPROJ_TILE = 1024
FINAL_TILE = 1024
ATTN_TILES = (2048, 1024)
ATTN_SUB = 128
EXPERT_BLOCK = 512
FF_CHUNK = 1024
SPLIT_CHUNK = MXU_TILE
DFT_STAGE1 = 128
DFT_STAGE2_BATCH = 8
DFT_SMALL_STAGE1 = 8
DFT_SMALL_BATCH = 4
SC_CHUNK = 64
SC_ROUND_ROWS = 2 * 16 * SC_CHUNK
MOE_PART_SHARES = (0.7, 0.3)
VMEM_LIMIT = 56 * 1024 * 1024


def _cparams(*sem):
    return pltpu.CompilerParams(dimension_semantics=sem, vmem_limit_bytes=VMEM_LIMIT)


def _to_dilated_view(stage_ref, out_ref, val, dil):
    if dil == 1:
        out_ref[...] = val.astype(BF16)
        return
    rows = val.shape[0] // dil
    for c in range(GROUP_WIDTH // LANES):
        stage_ref[c] = val[:, c * LANES:(c + 1) * LANES]
    for r in range(dil):
        for c in range(GROUP_WIDTH // LANES):
            lo = r * GROUP_WIDTH + c * LANES
            out_ref[:, lo:lo + LANES] = stage_ref[c, pl.ds(r, rows, stride=dil), :].astype(BF16)


def _inproj_kernel(xp_ref, xs_ref, w_ref, cos_ref, sin_ref, *refs, n_prompt_tiles):
    qkv_refs = refs[:3 * N_GROUPS]
    f_ref = refs[3 * N_GROUPS]
    stage_refs = refs[3 * N_GROUPS + 1:]
    i = pl.program_id(0)
    x = jnp.where(i < n_prompt_tiles, xp_ref[...], xs_ref[...]).astype(BF16)
    proj = jnp.dot(x, w_ref[...], preferred_element_type=F32)
    cos = cos_ref[...]
    sin = sin_ref[...]
    lane = lax.broadcasted_iota(I32, cos.shape, 1)
    first_half = (lane % HEAD_DIM) < (HEAD_DIM // 2)

    def rope(t):
        parts = []
        for c in range(GROUP_WIDTH // LANES):
            tc = t[:, c * LANES:(c + 1) * LANES]
            swapped = jnp.where(first_half, pltpu.roll(tc, LANES - HEAD_DIM // 2, 1), pltpu.roll(tc, HEAD_DIM // 2, 1))
            parts.append(tc * cos + swapped * sin)
        return jnp.concatenate(parts, axis=1)

    for gi, dil in enumerate(DILATIONS):
        sl = slice(gi * GROUP_WIDTH, (gi + 1) * GROUP_WIDTH)
        qg = rope(proj[:, sl]) * (HEAD_DIM ** -0.5)
        kg = rope(proj[:, QKV_WIDTH + gi * GROUP_WIDTH:QKV_WIDTH + (gi + 1) * GROUP_WIDTH])
        vg = proj[:, 2 * QKV_WIDTH + gi * GROUP_WIDTH:2 * QKV_WIDTH + (gi + 1) * GROUP_WIDTH]
        for which, val in enumerate((qg, kg, vg)):
            _to_dilated_view(stage_refs[which * N_GROUPS + gi], qkv_refs[which * N_GROUPS + gi], val, dil)
    f_ref[...] = proj[:, 3 * QKV_WIDTH:PROJ_WIDTH].astype(BF16)


def _inproj(xp, xs, w_proj, cos_t, sin_t, seq_sample):
    n_p, n_s = xp.shape[0], xs.shape[0]
    n_t = n_p + n_s
    tm = INPROJ_TILE
    assert n_p % tm == 0 and seq_sample % tm == 0
    npt = n_p // tm
    tiles_per_sample_seq = seq_sample // tm

    def table_idx(i):
        return (jnp.where(i < npt, i, (i - npt) % tiles_per_sample_seq), 0)

    view_specs = [pl.BlockSpec((tm // d, d * GROUP_WIDTH), lambda i: (i, 0)) for d in DILATIONS] * 3
    view_shapes = [jax.ShapeDtypeStruct((n_t // d, d * GROUP_WIDTH), BF16) for d in DILATIONS] * 3
    outs = pl.pallas_call(
        functools.partial(_inproj_kernel, n_prompt_tiles=npt),
        grid=(n_t // tm,),
        in_specs=[
            pl.BlockSpec((tm, D_MODEL), lambda i: (jnp.minimum(i, npt - 1), 0)),
            pl.BlockSpec((tm, D_MODEL), lambda i: (jnp.maximum(i - npt, 0), 0)),
            pl.BlockSpec((D_MODEL, PROJ_WIDTH), lambda i: (0, 0)),
            pl.BlockSpec((tm, LANES), table_idx),
            pl.BlockSpec((tm, LANES), table_idx),
        ],
        out_specs=view_specs + [pl.BlockSpec((tm, F_WIDTH), lambda i: (i, 0))],
        out_shape=view_shapes + [jax.ShapeDtypeStruct((n_t, F_WIDTH), BF16)],
        scratch_shapes=[pltpu.VMEM((GROUP_WIDTH // LANES, tm, LANES), F32)] * (3 * N_GROUPS),
        compiler_params=_cparams("parallel"),
        name="inproj_rope",
    )(xp, xs, w_proj, cos_t, sin_t)
    q_views, k_views, v_views = outs[0:N_GROUPS], outs[N_GROUPS:2 * N_GROUPS], outs[2 * N_GROUPS:3 * N_GROUPS]
    return q_views, k_views, v_views, outs[3 * N_GROUPS]


def _attn_kernel(q_ref, kl_ref, km_ref, kr_ref, vl_ref, vm_ref, vr_ref, o_ref, lse_ref, *, prompt_rows, sample_rows):
    i = pl.program_id(1)
    tq = q_ref.shape[0]
    head_of_lane = lax.broadcasted_iota(I32, (1, GROUP_WIDTH), 1) // HEAD_DIM

    def sub_block(j, kj, vj, valid):
        qj = q_ref[j * ATTN_SUB:(j + 1) * ATTN_SUB, :]
        q4 = jnp.concatenate([jnp.where(head_of_lane == h, qj, jnp.zeros_like(qj)) for h in range(HEADS_PER_GROUP)],
                             axis=0)
        s4 = lax.dot_general(q4, kj, (((1,), (1,)), ((), ())), preferred_element_type=F32)
        ps, scales, lses = [], [], []
        for h in range(HEADS_PER_GROUP):
            s = jnp.where(valid, s4[h * ATTN_SUB:(h + 1) * ATTN_SUB], NEG_INF)
            m = jnp.max(s, axis=1, keepdims=True)
            p = jnp.exp(s - m)
            l = jnp.sum(p, axis=1, keepdims=True)
            ps.append(p.astype(BF16))
            scales.append(1.0 / l)
            lses.append(m + jnp.log(l))
        o4 = jnp.dot(jnp.concatenate(ps, axis=0), vj, preferred_element_type=F32)
        acc = jnp.zeros((ATTN_SUB, GROUP_WIDTH), F32)
        lse_full = jnp.zeros((ATTN_SUB, GROUP_WIDTH), F32)
        for h in range(HEADS_PER_GROUP):
            mine = head_of_lane == h
            acc = jnp.where(mine, o4[h * ATTN_SUB:(h + 1) * ATTN_SUB] * scales[h], acc)
            lse_full = jnp.where(mine, lses[h], lse_full)
        o_ref[j * ATTN_SUB:(j + 1) * ATTN_SUB, :] = acc.astype(BF16)
        lse_ref[j * ATTN_SUB:(j + 1) * ATTN_SUB, :] = lse_full

    def banded_step():
        kext = jnp.concatenate([kl_ref[...], km_ref[...], kr_ref[...]], axis=0)
        vext = jnp.concatenate([vl_ref[...], vm_ref[...], vr_ref[...]], axis=0)
        kw = ATTN_SUB + 2 * HALF_WINDOW
        a = lax.broadcasted_iota(I32, (ATTN_SUB, kw), 0)
        c = lax.broadcasted_iota(I32, (ATTN_SUB, kw), 1)
        in_band = jnp.abs(c - HALF_WINDOW - a) <= HALF_WINDOW
        for j in range(tq // ATTN_SUB):
            r0 = i * tq + j * ATTN_SUB
            in_prompt = r0 < prompt_rows
            b = jnp.maximum(r0 - prompt_rows, 0) // sample_rows
            lo = jnp.where(in_prompt, 0, prompt_rows + b * sample_rows)
            hi = jnp.where(in_prompt, prompt_rows, prompt_rows + (b + 1) * sample_rows)
            key_row = r0 - HALF_WINDOW + c
            valid = in_band & (key_row >= lo) & (key_row < hi)
            sub_block(j, kext[j * ATTN_SUB:j * ATTN_SUB + kw, :], vext[j * ATTN_SUB:j * ATTN_SUB + kw, :], valid)

    def single_sequence_step():
        a = lax.broadcasted_iota(I32, (ATTN_SUB, ATTN_SUB), 0)
        c = lax.broadcasted_iota(I32, (ATTN_SUB, ATTN_SUB), 1)
        valid = jnp.abs(c - a) <= HALF_WINDOW
        for j in range(tq // ATTN_SUB):
            rows = slice(j * ATTN_SUB, (j + 1) * ATTN_SUB)
            sub_block(j, km_ref[rows, :], vm_ref[rows, :], valid)

    if sample_rows == ATTN_SUB and prompt_rows % tq == 0:
        in_sample = i * tq >= prompt_rows
        pl.when(in_sample)(single_sequence_step)
        pl.when(jnp.logical_not(in_sample))(banded_step)
    else:
        banded_step()


def _attention_group(qv, kv, vv, dil, n_prompt, seq_sample):
    rows = qv.shape[0]
    tq = next(t for t in ATTN_TILES if rows % t == 0)
    halo_per_tile = tq // HALF_WINDOW
    n_halo_blocks = rows // HALF_WINDOW
    main = pl.BlockSpec((tq, GROUP_WIDTH), lambda r, i: (i, r))
    left = pl.BlockSpec((HALF_WINDOW, GROUP_WIDTH), lambda r, i: (jnp.maximum(i * halo_per_tile - 1, 0), r))
    right = pl.BlockSpec((HALF_WINDOW, GROUP_WIDTH),
                         lambda r, i: (jnp.minimum((i + 1) * halo_per_tile, n_halo_blocks - 1), r))
    return pl.pallas_call(
        functools.partial(_attn_kernel, prompt_rows=n_prompt // dil, sample_rows=seq_sample // dil),
        grid=(dil, rows // tq),
        in_specs=[main, left, main, right, left, main, right],
        out_specs=[main, main],
        out_shape=[jax.ShapeDtypeStruct((rows, dil * GROUP_WIDTH), BF16),
                   jax.ShapeDtypeStruct((rows, dil * GROUP_WIDTH), F32)],
        compiler_params=_cparams("parallel", "parallel"),
        name=f"dilated_attention_{dil}",
    )(qv, kv, kv, kv, vv, vv, vv)


def _angle_table(idx, period):
    return (idx % period).astype(F32) * (2.0 * math.pi / period)


def _channel_dft(seq_len):
    c = jnp.arange(F_GROUP_DIM, dtype=I32)
    th = _angle_table(c[:, None] * c[None, :], F_GROUP_DIM)
    scale = (seq_len * F_GROUP_DIM) ** -0.5
    eye = jnp.eye(F_GROUPS, dtype=F32)
    cb = jnp.kron(eye, jnp.cos(th)) * scale
    sb = jnp.kron(eye, jnp.sin(th)) * scale
    return jnp.concatenate([cb, sb], axis=0).astype(BF16)


def _stage2_matrices(s1, s2):
    n2 = jnp.arange(s2, dtype=I32)[None, :]
    th_t = _angle_table(jnp.arange(s1, dtype=I32)[:, None] * n2, s1 * s2)
    th_f = _angle_table(jnp.arange(s2, dtype=I32)[:, None] * n2, s2)
    tc, ts = jnp.cos(th_t)[:, None, :], jnp.sin(th_t)[:, None, :]
    fc, fs = jnp.cos(th_f)[None, :, :], jnp.sin(th_f)[None, :, :]
    gr = tc * fc - ts * fs
    gi = -(ts * fc + tc * fs)
    return jnp.concatenate([jnp.concatenate([gr, -gi], axis=2), jnp.concatenate([gi, gr], axis=2)], axis=1).astype(BF16)


def _dft_stage1_kernel(m_ref, x_ref, a_ref):
    a_ref[...] = jnp.dot(m_ref[...], x_ref[...], preferred_element_type=F32).astype(BF16)


def _dft_stage2_kernel(g_ref, a_ref, cs_ref, o_ref):
    s2 = a_ref.shape[2]
    for kk in range(g_ref.shape[0]):
        a = jnp.concatenate([a_ref[0, kk], a_ref[1, kk]], axis=0)
        x = jnp.dot(g_ref[kk], a, preferred_element_type=F32)
        xr = x[:s2].astype(BF16)
        xi = x[s2:].astype(BF16)
        o_ref[:, kk * F_WIDTH:(kk + 1) * F_WIDTH] = (
            jnp.dot(xr, cs_ref[:F_WIDTH, :], preferred_element_type=F32)
            + jnp.dot(xi, cs_ref[F_WIDTH:, :], preferred_element_type=F32)).astype(BF16)


def _fourier_prompt(f, seq):
    s1 = DFT_STAGE1
    s2 = seq // s1
    n1 = jnp.arange(s1, dtype=I32)
    th1 = _angle_table(n1[:, None] * n1[None, :], s1)
    m1 = jnp.concatenate([jnp.cos(th1), -jnp.sin(th1)], axis=0).astype(BF16)
    g = _stage2_matrices(s1, s2)
    cs = _channel_dft(seq)

    cols = s2 * F_WIDTH
    fv = f[:seq].reshape(s1, cols)
    tn = min(cols, 4096)
    a = pl.pallas_call(
        _dft_stage1_kernel,
        grid=(cols // tn,),
        in_specs=[pl.BlockSpec((2 * s1, s1), lambda j: (0, 0)), pl.BlockSpec((s1, tn), lambda j: (0, j))],
        out_specs=pl.BlockSpec((2 * s1, tn), lambda j: (0, j)),
        out_shape=jax.ShapeDtypeStruct((2 * s1, cols), BF16),
        compiler_params=_cparams("parallel"),
        name="dft_stage1",
    )(m1, fv)
    a4 = a.reshape(2, s1, s2, F_WIDTH)
    kb = DFT_STAGE2_BATCH
    out = pl.pallas_call(
        _dft_stage2_kernel,
        grid=(s1 // kb,),
        in_specs=[pl.BlockSpec((kb, 2 * s2, 2 * s2), lambda k: (k, 0, 0)),
                  pl.BlockSpec((2, kb, s2, F_WIDTH), lambda k: (0, k, 0, 0)),
                  pl.BlockSpec((2 * F_WIDTH, F_WIDTH), lambda k: (0, 0))],
        out_specs=pl.BlockSpec((s2, kb * F_WIDTH), lambda k: (0, k)),
        out_shape=jax.ShapeDtypeStruct((s2, s1 * F_WIDTH), BF16),
        compiler_params=_cparams("parallel"),
        name="dft_stage2",
    )(g, a4, cs)
    return out.reshape(seq, F_WIDTH)


def _dft_small_kernel(g_ref, x_ref, cs_ref, o_ref, stage_ref):
    s1 = DFT_SMALL_STAGE1
    assert s1 == 8
    s2 = x_ref.shape[1] // s1
    root = math.sqrt(0.5)
    n_seq = x_ref.shape[0]
    a_all = []
    for b in range(n_seq):
        x0, x1, x2, x3, x4, x5, x6, x7 = [x_ref[b, n1 * s2:(n1 + 1) * s2, :].astype(F32) for n1 in range(s1)]
        u, v, p, q = x0 - x4, x2 - x6, x1 - x5, x3 - x7
        e_sum, e_dif = (x0 + x4) + (x2 + x6), (x0 + x4) - (x2 + x6)
        o_sum, o_dif = (x1 + x5) + (x3 + x7), (x1 + x5) - (x3 + x7)
        rpq_m, rpq_p = root * (p - q), root * (p + q)
        zero = jnp.zeros_like(x0)
        a_re = [e_sum + o_sum, u + rpq_m, e_dif, u - rpq_m, e_sum - o_sum, u - rpq_m, e_dif, u + rpq_m]
        im1, im3 = -v - rpq_p, v - rpq_p
        a_im = [zero, im1, -o_dif, im3, zero, -im3, o_dif, -im1]
        a_all.append([jnp.concatenate([a_re[k1].astype(BF16), a_im[k1].astype(BF16)], axis=0) for k1 in range(s1)])

    for k1 in range(s1):
        a = jnp.concatenate([a_all[b][k1] for b in range(n_seq)], axis=1)
        x = jnp.dot(g_ref[k1], a, preferred_element_type=F32)
        for b in range(n_seq):
            xb = x[:, b * F_WIDTH:(b + 1) * F_WIDTH]
            out = (jnp.dot(xb[:s2].astype(BF16), cs_ref[:F_WIDTH, :], preferred_element_type=F32)
                   + jnp.dot(xb[s2:].astype(BF16), cs_ref[F_WIDTH:, :], preferred_element_type=F32))
            for c in range(F_WIDTH // LANES):
                stage_ref[b * (F_WIDTH // LANES) + c, pl.ds(k1, s2, stride=s1), :] = out[:, c * LANES:(c + 1) * LANES]
    for b in range(n_seq):
        o_ref[b] = jnp.concatenate([stage_ref[b * (F_WIDTH // LANES) + c] for c in range(F_WIDTH // LANES)],
                                   axis=1).astype(BF16)


def _fourier_sample(f, n_prompt, batch, seq):
    s1 = DFT_SMALL_STAGE1
    s2 = seq // s1
    g = _stage2_matrices(s1, s2)
    cs = _channel_dft(seq)
    f3 = f.reshape(f.shape[0] // seq, seq, F_WIDTH)
    first = n_prompt // seq
    nb = DFT_SMALL_BATCH
    assert batch % nb == 0 and first % nb == 0
    return pl.pallas_call(
        _dft_small_kernel,
        grid=(batch // nb,),
        in_specs=[pl.BlockSpec((s1, 2 * s2, 2 * s2), lambda b: (0, 0, 0)),
                  pl.BlockSpec((nb, seq, F_WIDTH), lambda b: (first // nb + b, 0, 0)),
                  pl.BlockSpec((2 * F_WIDTH, F_WIDTH), lambda b: (0, 0))],
        out_specs=pl.BlockSpec((nb, seq, F_WIDTH), lambda b: (b, 0, 0)),
        out_shape=jax.ShapeDtypeStruct((batch, seq, F_WIDTH), BF16),
        scratch_shapes=[pltpu.VMEM((nb * (F_WIDTH // LANES), seq, LANES), F32)],
        compiler_params=_cparams("parallel"),
        name="dft_small",
    )(g, f3, cs).reshape(batch * seq, F_WIDTH)


def _pack_bf16_pairs(x):
    w = x.shape[1] // 2
    bits = lax.bitcast_convert_type(x.astype(BF16).astype(F32), U32)
    return (bits[:, :w] >> 16) | (bits[:, w:] & jnp.uint32(0xFFFF0000))


def _unpack_bf16_pairs(u):
    lo = lax.bitcast_convert_type(u << 16, F32)
    hi = lax.bitcast_convert_type(u & jnp.uint32(0xFFFF0000), F32)
    return jnp.concatenate([lo, hi], axis=1)


def _layer_norm(h, g, b):
    mu = jnp.mean(h, axis=-1, keepdims=True)
    d = h - mu
    var = jnp.mean(d * d, axis=-1, keepdims=True)
    return d * lax.rsqrt(var + LN_EPS) * g + b


def _from_dilated_view(stage_ref, blk_ref, dil):
    if dil == 1:
        return blk_ref[...].astype(F32)
    rows = blk_ref.shape[0]
    for r in range(dil):
        for c in range(GROUP_WIDTH // LANES):
            lo = r * GROUP_WIDTH + c * LANES
            stage_ref[c, pl.ds(r, rows, stride=dil), :] = blk_ref[:, lo:lo + LANES].astype(F32)
    return jnp.concatenate([stage_ref[c] for c in range(GROUP_WIDTH // LANES)], axis=1)


def _merge_kernel(xp_ref, xs_ref, wg_ref, o1_ref, o2_ref, o3_ref, l1_ref, l2_ref, l3_ref, fp_ref, fs_ref,
                  wao_ref, wfo_ref, wo_ref, g_ref, b_ref, wr2_ref, br_ref, tri_ref,
                  x1_ref, pk_ref, idx_ref, gate_ref, rank_ref, cnt_ref, run_ref, *stage_refs, n_prompt_tiles):
    i = pl.program_id(0)

    @pl.when(i == 0)
    def _():
        run_ref[...] = jnp.zeros_like(run_ref)

    is_prompt = i < n_prompt_tiles
    x = jnp.where(is_prompt, xp_ref[...], xs_ref[...])
    gpre = jnp.dot(x.astype(BF16), wg_ref[...], preferred_element_type=F32)
    o1, o2, o3 = [_from_dilated_view(stage_refs[gi], ref, d)
                  for gi, (ref, d) in enumerate(zip((o1_ref, o2_ref, o3_ref), DILATIONS))]
    l1, l2, l3 = [_from_dilated_view(stage_refs[N_GROUPS + gi], ref, d)
                  for gi, (ref, d) in enumerate(zip((l1_ref, l2_ref, l3_ref), DILATIONS))]
    m = jnp.maximum(jnp.maximum(l1, l2), l3)
    e1, e2, e3 = jnp.exp(l1 - m), jnp.exp(l2 - m), jnp.exp(l3 - m)
    attn = (e1 * o1 + e2 * o2 + e3 * o3) * (1.0 / (e1 + e2 + e3))
    a = jnp.dot(attn.astype(BF16), wao_ref[...], preferred_element_type=F32)
    four = jnp.where(is_prompt, fp_ref[...], fs_ref[...])
    ff = jnp.dot(four, wfo_ref[...], preferred_element_type=F32)
    merged = jax.nn.sigmoid(gpre[:, :D_MODEL]) * a + jax.nn.sigmoid(gpre[:, D_MODEL:]) * ff
    mix = jnp.dot(merged.astype(BF16), wo_ref[...], preferred_element_type=F32)
    x1 = _layer_norm(DEEPNORM_ALPHA * x + mix, g_ref[...], b_ref[...])
    x1_ref[...] = x1
    pk_ref[...] = _pack_bf16_pairs(x1)

    xh = x1.astype(BF16)
    xl = (x1 - xh.astype(F32)).astype(BF16)
    tm = x1.shape[0]
    cross = lax.dot_general(wr2_ref[...], jnp.concatenate([xh, xl], axis=0), (((1,), (1,)), ((), ())),
                            preferred_element_type=F32)
    logits = ((cross[:N_EXPERTS, :tm] + cross[N_EXPERTS:, :tm])
              + (cross[:N_EXPERTS, tm:] + cross[N_EXPERTS:, tm:])) + br_ref[...]
    eio = lax.broadcasted_iota(I32, (N_EXPERTS, tm), 0)
    work = logits
    vals, sels, picks = [], [], []
    for _ in range(TOP_K):
        mk = jnp.max(work, axis=0, keepdims=True)
        ik = jnp.min(jnp.where(work == mk, eio, N_EXPERTS), axis=0, keepdims=True)
        sel = eio == ik
        vals.append(mk)
        sels.append(sel)
        picks.append(ik)
        work = jnp.where(sel, -jnp.inf, work)
    es = [jnp.exp(v - vals[0]) for v in vals]
    inv_den = 1.0 / (es[0] + es[1] + es[2] + es[3])
    chosen = (sels[0] | sels[1] | sels[2] | sels[3])
    chosen_f = chosen.astype(F32)
    prefix = jnp.dot(chosen_f.astype(BF16), tri_ref[...], preferred_element_type=F32)
    before = prefix + run_ref[...]
    for kk in range(TOP_K):
        idx_ref[kk:kk + 1, :] = picks[kk]
        gate_ref[kk:kk + 1, :] = es[kk] * inv_den
        rank_ref[kk:kk + 1, :] = jnp.sum(jnp.where(sels[kk], before, 0.0), axis=0, keepdims=True).astype(I32)
    run_ref[...] = run_ref[...] + jnp.sum(chosen_f, axis=1, keepdims=True)
    cnt_ref[...] = jnp.broadcast_to(run_ref[...], cnt_ref.shape).astype(I32)


def _merge_route(xp, xs, w_gate, o_list, l_list, four_p, four_s, w_ao, w_fo, w_o, ln_g, ln_b, w_r, b_r, tile0,
                 n_tiles):
    n_p = xp.shape[0]
    tm = TOKEN_TILE
    n_t = n_tiles * tm
    npt = n_p // tm
    w_rt = w_r.T
    w_rh = w_rt.astype(BF16)
    w_rl = (w_rt - w_rh.astype(F32)).astype(BF16)
    tri = (jnp.arange(tm)[:, None] < jnp.arange(tm)[None, :]).astype(BF16)
    row = lambda w: pl.BlockSpec((tm, w), lambda i: (i, 0))
    prompt_row = lambda w: pl.BlockSpec((tm, w), lambda i: (jnp.clip(tile0 + i, 0, npt - 1), 0))
    sample_row = lambda w: pl.BlockSpec((tm, w), lambda i: (jnp.maximum(tile0 + i - npt, 0), 0))
    full = lambda r, c: pl.BlockSpec((r, c), lambda i: (0, 0))
    lane_row = pl.BlockSpec((TOP_K, tm), lambda i: (0, i))
    views = [pl.BlockSpec((tm // d, d * GROUP_WIDTH), lambda i: (tile0 + i, 0)) for d in DILATIONS]
    return pl.pallas_call(
        functools.partial(_merge_kernel, n_prompt_tiles=npt - tile0),
        grid=(n_tiles,),
        in_specs=[
            prompt_row(D_MODEL), sample_row(D_MODEL),
            full(D_MODEL, 2 * D_MODEL),
            *views, *views,
            prompt_row(F_WIDTH), sample_row(F_WIDTH),
            full(GROUP_WIDTH, D_MODEL), full(F_WIDTH, D_MODEL), full(D_MODEL, D_MODEL),
            full(1, D_MODEL), full(1, D_MODEL),
            full(2 * N_EXPERTS, D_MODEL), full(N_EXPERTS, 1),
            full(tm, tm),
        ],
        out_specs=[row(D_MODEL), row(D_MODEL // 2), lane_row, lane_row, lane_row, full(N_EXPERTS, LANES)],
        out_shape=[
            jax.ShapeDtypeStruct((n_t, D_MODEL), F32),
            jax.ShapeDtypeStruct((n_t, D_MODEL // 2), U32),
            jax.ShapeDtypeStruct((TOP_K, n_t), I32),
            jax.ShapeDtypeStruct((TOP_K, n_t), F32),
            jax.ShapeDtypeStruct((TOP_K, n_t), I32),
            jax.ShapeDtypeStruct((N_EXPERTS, LANES), I32),
        ],
        scratch_shapes=[pltpu.VMEM((N_EXPERTS, 1), F32)]
        + [pltpu.VMEM((GROUP_WIDTH // LANES, tm, LANES), F32)] * (2 * N_GROUPS),
        compiler_params=_cparams("arbitrary"),
        name="merge_ln_route",
    )(xp, xs, w_gate, *o_list, *l_list, four_p, four_s, w_ao, w_fo, w_o, ln_g.reshape(1, -1), ln_b.reshape(1, -1),
      jnp.concatenate([w_rh, w_rl], axis=0), b_r.reshape(-1, 1), tri)


def _sc_workers():
    info = plsc.get_sparse_core_info()
    return info.num_cores, info.num_cores * info.num_subcores


def _sc_dispatch(rows_src, pos_chunks, n_rows_out):
    n_t, w = rows_src.shape
    n_cores, n_workers = _sc_workers()
    assert n_workers * SC_CHUNK == SC_ROUND_ROWS and n_t % SC_ROUND_ROWS == 0
    chunks_per_worker = n_t // (n_workers * SC_CHUNK)
    mesh = plsc.VectorSubcoreMesh(core_axis_name="c", subcore_axis_name="s")

    @functools.partial(
        pl.kernel, mesh=mesh,
        out_type=jax.ShapeDtypeStruct((n_rows_out, w), rows_src.dtype),
        scratch_types=[pltpu.VMEM((TOP_K, SC_CHUNK), I32), pltpu.VMEM((SC_CHUNK, w), rows_src.dtype)],
    )
    def dispatch(src_hbm, pos_hbm, out_hbm, idx_v, rows_v):
        wid = lax.axis_index("s") * n_cores + lax.axis_index("c")

        @pl.loop(0, chunks_per_worker)
        def _(j):
            chunk = wid * chunks_per_worker + j
            pltpu.sync_copy(pos_hbm.at[chunk], idx_v)
            pltpu.sync_copy(src_hbm.at[pl.ds(chunk * SC_CHUNK, SC_CHUNK)], rows_v)
            for kk in range(TOP_K):
                pltpu.sync_copy(rows_v, out_hbm.at[idx_v.at[kk]])

    return dispatch(rows_src, pos_chunks)


def _sc_combine(rows_src, pos_chunks, n_t):
    w = rows_src.shape[1]
    n_cores, n_workers = _sc_workers()
    assert n_workers * SC_CHUNK == SC_ROUND_ROWS and n_t % SC_ROUND_ROWS == 0
    chunks_per_worker = n_t // (n_workers * SC_CHUNK)
    mesh = plsc.VectorSubcoreMesh(core_axis_name="c", subcore_axis_name="s")

    @functools.partial(
        pl.kernel, mesh=mesh,
        out_type=jax.ShapeDtypeStruct((TOP_K, n_t, w), rows_src.dtype),
        scratch_types=[pltpu.VMEM((TOP_K, SC_CHUNK), I32), pltpu.VMEM((SC_CHUNK, w), rows_src.dtype)],
    )
    def combine(src_hbm, pos_hbm, out_hbm, idx_v, rows_v):
        wid = lax.axis_index("s") * n_cores + lax.axis_index("c")

        @pl.loop(0, chunks_per_worker)
        def _(j):
            chunk = wid * chunks_per_worker + j
            pltpu.sync_copy(pos_hbm.at[chunk], idx_v)
            for kk in range(TOP_K):
                pltpu.sync_copy(src_hbm.at[idx_v.at[kk]], rows_v)
                pltpu.sync_copy(rows_v, out_hbm.at[kk, pl.ds(chunk * SC_CHUNK, SC_CHUNK)])

    return combine(rows_src, pos_chunks)


def _expert_kernel(be_ref, nv_ref, x_ref, *refs, prepare):
    if prepare:
        wu_ref, wdn_ref, p_ref, bg_ref, bl_ref, bd_ref, y_ref, wg_ref, wl_ref, wd_ref = refs
    else:
        wg_ref, wl_ref, wd_ref, bg_ref, bl_ref, bd_ref, y_ref = refs
    blk = pl.program_id(0)
    nv = nv_ref[blk]

    if prepare:
        new_expert = (blk == 0) | (be_ref[blk] != be_ref[jnp.maximum(blk - 1, 0)])

        @pl.when(new_expert)
        def _():
            half = SPLIT_CHUNK // 2
            for c in range(wu_ref.shape[2] // SPLIT_CHUNK):
                w = wu_ref[0, :, c * SPLIT_CHUNK:(c + 1) * SPLIT_CHUNK].astype(BF16)
                r = jnp.dot(w, p_ref[...], preferred_element_type=F32)
                wg_ref[0, :, c * half:(c + 1) * half] = r[:, :half].astype(BF16)
                wl_ref[0, :, c * half:(c + 1) * half] = r[:, half:].astype(BF16)
            wd_ref[...] = wdn_ref[...].astype(BF16)

    @pl.when(nv > 0)
    def _():
        x = _unpack_bf16_pairs(x_ref[...])
        rows = lax.broadcasted_iota(I32, (x.shape[0], 1), 0)
        x = jnp.where(rows < nv, x, 0.0).astype(BF16)
        acc = jnp.zeros((x.shape[0], D_MODEL), F32) + bd_ref[0]
        for c in range(D_FF // FF_CHUNK):
            sl = slice(c * FF_CHUNK, (c + 1) * FF_CHUNK)
            hg = jnp.dot(x, wg_ref[0, :, sl], preferred_element_type=F32) + bg_ref[0, :, sl]
            hl = jnp.dot(x, wl_ref[0, :, sl], preferred_element_type=F32) + bl_ref[0, :, sl]
            glu = jnp.minimum(hg, SWIGLU_LIMIT)
            lin = jnp.clip(hl, -SWIGLU_LIMIT, SWIGLU_LIMIT)
            act = glu * jax.nn.sigmoid(SWIGLU_ALPHA * glu) * (lin + 1.0)
            acc = acc + jnp.dot(act.astype(BF16), wd_ref[0, sl, :], preferred_element_type=F32)
        y_ref[...] = _pack_bf16_pairs(acc)

    @pl.when(nv == 0)
    def _():
        y_ref[...] = jnp.zeros_like(y_ref)


def _expert_ffn(xs_rows, block_e, n_valid, weights, b_glu, b_lin, b_down):
    n_rows = xs_rows.shape[0]
    bm = EXPERT_BLOCK
    prepare = len(weights) == 2
    wspec = lambda r, c: pl.BlockSpec((1, r, c), lambda b, be, nv: (be[b], 0, 0))
    rows_spec = pl.BlockSpec((bm, D_MODEL // 2), lambda b, be, nv: (b, 0))
    bias_specs = [wspec(1, D_FF), wspec(1, D_FF), wspec(1, D_MODEL)]
    y_shape = jax.ShapeDtypeStruct((n_rows, D_MODEL // 2), U32)
    if prepare:
        half = SPLIT_CHUNK // 2
        src = jnp.arange(SPLIT_CHUNK, dtype=I32)[:, None]
        dst = jnp.arange(SPLIT_CHUNK, dtype=I32)[None, :]
        sel = jnp.where(dst < half, src == 2 * dst, src == 2 * (dst - half) + 1).astype(BF16)
        operands = (*weights, sel)
        weight_specs = [wspec(D_MODEL, 2 * D_FF), wspec(D_FF, D_MODEL),
                        pl.BlockSpec((SPLIT_CHUNK, SPLIT_CHUNK), lambda b, be, nv: (0, 0))]
        prepared = jax.ShapeDtypeStruct((N_EXPERTS, D_MODEL, D_FF), BF16)
        out_specs = [rows_spec, wspec(D_MODEL, D_FF), wspec(D_MODEL, D_FF), wspec(D_FF, D_MODEL)]
        out_shape = [y_shape, prepared, prepared, prepared]
    else:
        operands = tuple(weights)
        weight_specs = [wspec(D_MODEL, D_FF), wspec(D_MODEL, D_FF), wspec(D_FF, D_MODEL)]
        out_specs, out_shape = rows_spec, y_shape
    grid_spec = pltpu.PrefetchScalarGridSpec(
        num_scalar_prefetch=2,
        grid=(n_rows // bm,),
        in_specs=[rows_spec, *weight_specs, *bias_specs],
        out_specs=out_specs,
    )
    out = pl.pallas_call(
        functools.partial(_expert_kernel, prepare=prepare),
        grid_spec=grid_spec,
        out_shape=out_shape,
        compiler_params=_cparams("arbitrary"),
        name="expert_ffn",
    )(block_e, n_valid, xs_rows, *operands, b_glu, b_lin, b_down)
    return (out[0], tuple(out[1:])) if prepare else (out, tuple(weights))


def _final_kernel(x1_ref, yg_ref, gt_ref, g_ref, b_ref, *rest):
    o_ref = rest[-1]
    gates = gt_ref[...]
    pad = jnp.zeros((LANES - TOP_K, gates.shape[1]), F32)
    gt = jnp.concatenate([gates, pad], axis=0).T
    ffn = jnp.zeros(x1_ref.shape, F32)
    for kk in range(TOP_K):
        ffn = ffn + gt[:, kk:kk + 1] * _unpack_bf16_pairs(yg_ref[kk])
    o_ref[...] = _layer_norm(DEEPNORM_ALPHA * x1_ref[...] + ffn, g_ref[...], b_ref[...])


def _final(x1, yg, gates, ln_g, ln_b, src_row0, n_rows, out_rows, dst_row0, out_prev=None):
    tm = FINAL_TILE
    assert src_row0 % tm == 0 and dst_row0 % tm == 0 and n_rows % tm == 0
    src, dst = src_row0 // tm, dst_row0 // tm
    in_specs = [
        pl.BlockSpec((tm, D_MODEL), lambda i: (src + i, 0)),
        pl.BlockSpec((TOP_K, tm, D_MODEL // 2), lambda i: (0, src + i, 0)),
        pl.BlockSpec((TOP_K, tm), lambda i: (0, src + i)),
        pl.BlockSpec((1, D_MODEL), lambda i: (0, 0)),
        pl.BlockSpec((1, D_MODEL), lambda i: (0, 0)),
    ]
    args = [x1, yg, gates, ln_g.reshape(1, -1), ln_b.reshape(1, -1)]
    aliases = {}
    if out_prev is not None:
        in_specs.append(pl.BlockSpec(memory_space=pl.ANY))
        args.append(out_prev)
        aliases = {len(args) - 1: 0}
    return pl.pallas_call(
        _final_kernel,
        grid=(n_rows // tm,),
        in_specs=in_specs,
        out_specs=pl.BlockSpec((tm, D_MODEL), lambda i: (dst + i, 0)),
        out_shape=jax.ShapeDtypeStruct((out_rows, D_MODEL), F32),
        input_output_aliases=aliases,
        compiler_params=_cparams("parallel"),
        name="combine_ln2",
    )(*args)


def _routing_tables(idx, rank, counts, n_blocks):
    bm = EXPERT_BLOCK
    padded = jnp.maximum((counts + bm - 1) // bm, 1) * bm
    pad_end = jnp.cumsum(padded)
    pad_start = pad_end - padded
    experts = jnp.arange(N_EXPERTS, dtype=I32)

    def lookup(table, e):
        shape = (N_EXPERTS,) + (1,) * e.ndim
        return jnp.sum(jnp.where(e[None] == experts.reshape(shape), table.reshape(shape), 0), axis=0)

    pos = lookup(pad_start, idx) + rank
    blk_row0 = jnp.arange(n_blocks, dtype=I32) * bm
    block_e = jnp.minimum(jnp.sum((pad_end[:, None] <= blk_row0[None, :]).astype(I32), axis=0), N_EXPERTS - 1)
    n_valid = jnp.clip(lookup(counts, block_e) - (blk_row0 - lookup(pad_start, block_e)), 0, bm)
    return pos.astype(I32), block_e.astype(I32), n_valid.astype(I32)


def _moe(pk, idx, rank, counts, weights, b_glu, b_lin, b_down):
    n_t = pk.shape[0]
    n_blocks = (n_t * TOP_K) // EXPERT_BLOCK + N_EXPERTS
    pos, block_e, n_valid = _routing_tables(idx, rank, counts, n_blocks)
    pos_chunks = pos.reshape(TOP_K, n_t // SC_CHUNK, SC_CHUNK).transpose(1, 0, 2)
    xs_rows = _sc_dispatch(pk, pos_chunks, n_blocks * EXPERT_BLOCK)
    y_rows, prepared = _expert_ffn(xs_rows, block_e, n_valid, weights, b_glu, b_lin, b_down)
    return _sc_combine(y_rows, pos_chunks, n_t), prepared


def _rope_tables(seq_max):
    inv = ROPE_THETA ** (-jnp.arange(0, HEAD_DIM, 2, dtype=F32) / HEAD_DIM)
    ang = jnp.arange(seq_max).astype(F32)[:, None] * inv[None, :]
    cos, sin = jnp.cos(ang), jnp.sin(ang)
    reps = LANES // HEAD_DIM
    cos_t = jnp.tile(jnp.concatenate([cos, cos], axis=1), (1, reps))
    sin_t = jnp.tile(jnp.concatenate([-sin, sin], axis=1), (1, reps))
    return cos_t, sin_t


def _encoder_layer(xp, xs, seq_prompt, batch_sample, seq_sample, w_in, w_attn_out, w_four_out, w_o, ln1_g, ln1_b,
                   w_router, b_router, w_up, b_up, w_down, b_down, ln2_g, ln2_b):
    n_p, n_s = xp.shape[0], xs.shape[0]
    assert n_p == seq_prompt and n_s == batch_sample * seq_sample
    assert seq_prompt % (DFT_STAGE1 * 16) == 0 and seq_sample % TOKEN_TILE == 0 and n_p % seq_sample == 0
    n_t = n_p + n_s
    assert n_t % (max(DILATIONS) * min(ATTN_TILES)) == 0 and seq_sample % (max(DILATIONS) * ATTN_SUB) == 0

    w_proj = w_in[:, :PROJ_WIDTH].astype(BF16)
    w_gate = w_in[:, PROJ_WIDTH:].astype(BF16)
    cos_t, sin_t = _rope_tables(max(seq_prompt, seq_sample))
    q_views, k_views, v_views, f = _inproj(xp, xs, w_proj, cos_t, sin_t, seq_sample)

    o_list, l_list = [], []
    for gi, dil in enumerate(DILATIONS):
        o, lse = _attention_group(q_views[gi], k_views[gi], v_views[gi], dil, n_p, seq_sample)
        o_list.append(o)
        l_list.append(lse)

    four_p = _fourier_prompt(f, seq_prompt)
    four_s = _fourier_sample(f, n_p, batch_sample, seq_sample)

    b_glu = b_up[:, None, 0::2]
    b_lin = b_up[:, None, 1::2]
    w_ao, w_fo, w_ob = w_attn_out.astype(BF16), w_four_out.astype(BF16), w_o.astype(BF16)

    tiles = n_t // TOKEN_TILE
    align = math.lcm(SC_ROUND_ROWS, FINAL_TILE) // TOKEN_TILE
    bounds = [0]
    for share in MOE_PART_SHARES[:-1]:
        bounds.append(min(tiles, bounds[-1] + max(align, round(tiles * share / align) * align)))
    bounds.append(tiles)
    n_parts = len(bounds) - 1
    routed = []
    for p in range(n_parts):
        routed.append(_merge_route(xp, xs, w_gate, o_list, l_list, four_p, four_s, w_ao, w_fo, w_ob, ln1_g, ln1_b,
                                   w_router, b_router, bounds[p], bounds[p + 1] - bounds[p]))
    combined = []
    weights = (w_up, w_down)
    for x1, pk, idx, gates, rank, cnt in routed:
        yg, weights = _moe(pk, idx, rank, cnt[:, 0], weights, b_glu, b_lin, b_down[:, None, :])
        combined.append(yg)

    outs = [None, None]
    spans = [(0, n_p), (n_p, n_t)]
    for p, ((x1, pk, idx, gates, rank, cnt), yg) in enumerate(zip(routed, combined)):
        lo, hi = bounds[p] * TOKEN_TILE, bounds[p + 1] * TOKEN_TILE
        for which, (a, b) in enumerate(spans):
            s, e = max(lo, a), min(hi, b)
            if s < e:
                outs[which] = _final(x1, yg, gates, ln2_g, ln2_b, s - lo, e - s, b - a, s - a, outs[which])
    return outs[0], outs[1]


def kernel(x_prompt, x_sample, w_in, w_attn_out, w_four_out, w_o, ln1_g, ln1_b, w_router, b_router, w_up, b_up,
           w_down, b_down, ln2_g, ln2_b):
    assert w_in.shape[0] == DEPTH
    bp, sp, d = x_prompt.shape
    bs, ss, _ = x_sample.shape
    assert bp == 1 and d == D_MODEL
    y_p, y_s = _encoder_layer(
        x_prompt.reshape(sp, d), x_sample.reshape(bs * ss, d), sp, bs, ss,
        w_in[0], w_attn_out[0], w_four_out[0], w_o[0], ln1_g[0], ln1_b[0], w_router[0], b_router[0],
        w_up[0], b_up[0], w_down[0], b_down[0], ln2_g[0], ln2_b[0])
    return y_p.reshape(x_prompt.shape), y_s.reshape(x_sample.shape)
```

```python
import functools
import math

import jax
import jax.numpy as jnp
from jax import lax
from jax.experimental import pallas as pl
from jax.experimental.pallas import tpu as pltpu
from jax.experimental.pallas import tpu_sc as plsc

F32 = jnp.float32
BF16 = jnp.bfloat16
I32 = jnp.int32
U32 = jnp.uint32

D_MODEL = 1024
HEAD_DIM = 64
HEADS_PER_GROUP = 4
GROUP_WIDTH = HEADS_PER_GROUP * HEAD_DIM
DILATIONS = (1, 4, 16)
HALF_WINDOW = 64
N_GROUPS = len(DILATIONS)
QKV_WIDTH = N_GROUPS * GROUP_WIDTH
F_GROUPS = 4
F_GROUP_DIM = 64
F_WIDTH = F_GROUPS * F_GROUP_DIM
PROJ_WIDTH = 3 * QKV_WIDTH + F_WIDTH
N_EXPERTS = 32
TOP_K = 4
D_FF = 1024
SWIGLU_LIMIT = 7.0
SWIGLU_ALPHA = 1.702
LN_EPS = 1e-5
ROPE_THETA = 10000.0
NEG_INF = -1e30
DEPTH = 1
DEEPNORM_ALPHA = (2 * DEPTH) ** 0.25

LANES = 128
MXU_TILE = 256
TOKEN_TILE = 512
INPROJ_TILE = 1024
FINAL_TILE = 1024
ATTN_TILES = (2048, 1024)
ATTN_SUB = 128
EXPERT_BLOCK = 512
FF_CHUNK = 1024
SPLIT_CHUNK = MXU_TILE
DFT_STAGE1 = 128
DFT_STAGE2_BATCH = 8
DFT_SMALL_STAGE1 = 8
DFT_SMALL_BATCH = 4
SC_CHUNK = 128
SC_ROUND_ROWS = 2 * 16 * SC_CHUNK
MOE_PART_SHARES = (0.7, 0.3)
VMEM_LIMIT = 56 * 1024 * 1024


def _cparams(*sem):
    return pltpu.CompilerParams(dimension_semantics=sem, vmem_limit_bytes=VMEM_LIMIT)


def _to_dilated_view(stage_ref, out_ref, val, dil):
    if dil == 1:
        out_ref[...] = val.astype(BF16)
        return
    rows = val.shape[0] // dil
    for c in range(GROUP_WIDTH // LANES):
        stage_ref[c] = val[:, c * LANES:(c + 1) * LANES]
    for r in range(dil):
        for c in range(GROUP_WIDTH // LANES):
            lo = r * GROUP_WIDTH + c * LANES
            out_ref[:, lo:lo + LANES] = stage_ref[c, pl.ds(r, rows, stride=dil), :].astype(BF16)


def _inproj_kernel(xp_ref, xs_ref, w_ref, cos_ref, sin_ref, *refs, n_prompt_tiles):
    qkv_refs = refs[:3 * N_GROUPS]
    f_ref = refs[3 * N_GROUPS]
    stage_refs = refs[3 * N_GROUPS + 1:]
    i = pl.program_id(0)
    x = jnp.where(i < n_prompt_tiles, xp_ref[...], xs_ref[...]).astype(BF16)
    proj = jnp.dot(x, w_ref[...], preferred_element_type=F32)
    cos = cos_ref[...]
    sin = sin_ref[...]
    lane = lax.broadcasted_iota(I32, cos.shape, 1)
    first_half = (lane % HEAD_DIM) < (HEAD_DIM // 2)

    def rope(t):
        parts = []
        for c in range(GROUP_WIDTH // LANES):
            tc = t[:, c * LANES:(c + 1) * LANES]
            swapped = jnp.where(first_half, pltpu.roll(tc, LANES - HEAD_DIM // 2, 1), pltpu.roll(tc, HEAD_DIM // 2, 1))
            parts.append(tc * cos + swapped * sin)
        return jnp.concatenate(parts, axis=1)

    for gi, dil in enumerate(DILATIONS):
        sl = slice(gi * GROUP_WIDTH, (gi + 1) * GROUP_WIDTH)
        qg = rope(proj[:, sl]) * (HEAD_DIM ** -0.5)
        kg = rope(proj[:, QKV_WIDTH + gi * GROUP_WIDTH:QKV_WIDTH + (gi + 1) * GROUP_WIDTH])
        vg = proj[:, 2 * QKV_WIDTH + gi * GROUP_WIDTH:2 * QKV_WIDTH + (gi + 1) * GROUP_WIDTH]
        for which, val in enumerate((qg, kg, vg)):
            _to_dilated_view(stage_refs[which * N_GROUPS + gi], qkv_refs[which * N_GROUPS + gi], val, dil)
    f_ref[...] = proj[:, 3 * QKV_WIDTH:PROJ_WIDTH].astype(BF16)


def _inproj(xp, xs, w_proj, cos_t, sin_t, seq_sample):
    n_p, n_s = xp.shape[0], xs.shape[0]
    n_t = n_p + n_s
    tm = INPROJ_TILE
    assert n_p % tm == 0 and seq_sample % tm == 0
    npt = n_p // tm
    tiles_per_sample_seq = seq_sample // tm

    def table_idx(i):
        return (jnp.where(i < npt, i, (i - npt) % tiles_per_sample_seq), 0)

    view_specs = [pl.BlockSpec((tm // d, d * GROUP_WIDTH), lambda i: (i, 0)) for d in DILATIONS] * 3
    view_shapes = [jax.ShapeDtypeStruct((n_t // d, d * GROUP_WIDTH), BF16) for d in DILATIONS] * 3
    outs = pl.pallas_call(
        functools.partial(_inproj_kernel, n_prompt_tiles=npt),
        grid=(n_t // tm,),
        in_specs=[
            pl.BlockSpec((tm, D_MODEL), lambda i: (jnp.minimum(i, npt - 1), 0)),
            pl.BlockSpec((tm, D_MODEL), lambda i: (jnp.maximum(i - npt, 0), 0)),
            pl.BlockSpec((D_MODEL, PROJ_WIDTH), lambda i: (0, 0)),
            pl.BlockSpec((tm, LANES), table_idx),
            pl.BlockSpec((tm, LANES), table_idx),
        ],
        out_specs=view_specs + [pl.BlockSpec((tm, F_WIDTH), lambda i: (i, 0))],
        out_shape=view_shapes + [jax.ShapeDtypeStruct((n_t, F_WIDTH), BF16)],
        scratch_shapes=[pltpu.VMEM((GROUP_WIDTH // LANES, tm, LANES), F32)] * (3 * N_GROUPS),
        compiler_params=_cparams("parallel"),
        name="inproj_rope",
    )(xp, xs, w_proj, cos_t, sin_t)
    q_views, k_views, v_views = outs[0:N_GROUPS], outs[N_GROUPS:2 * N_GROUPS], outs[2 * N_GROUPS:3 * N_GROUPS]
    return q_views, k_views, v_views, outs[3 * N_GROUPS]


def _attn_kernel(q_ref, kl_ref, km_ref, kr_ref, vl_ref, vm_ref, vr_ref, o_ref, lse_ref, *, prompt_rows, sample_rows):
    i = pl.program_id(1)
    tq = q_ref.shape[0]
    head_of_lane = lax.broadcasted_iota(I32, (1, GROUP_WIDTH), 1) // HEAD_DIM

    def sub_block(j, kj, vj, valid):
        qj = q_ref[j * ATTN_SUB:(j + 1) * ATTN_SUB, :]
        q4 = jnp.concatenate([jnp.where(head_of_lane == h, qj, jnp.zeros_like(qj)) for h in range(HEADS_PER_GROUP)],
                             axis=0)
        s4 = lax.dot_general(q4, kj, (((1,), (1,)), ((), ())), preferred_element_type=F32)
        ps, scales, lses = [], [], []
        for h in range(HEADS_PER_GROUP):
            s = jnp.where(valid, s4[h * ATTN_SUB:(h + 1) * ATTN_SUB], NEG_INF)
            m = jnp.max(s, axis=1, keepdims=True)
            p = jnp.exp(s - m)
            l = jnp.sum(p, axis=1, keepdims=True)
            ps.append(p.astype(BF16))
            scales.append(1.0 / l)
            lses.append(m + jnp.log(l))
        o4 = jnp.dot(jnp.concatenate(ps, axis=0), vj, preferred_element_type=F32)
        acc = jnp.zeros((ATTN_SUB, GROUP_WIDTH), F32)
        lse_full = jnp.zeros((ATTN_SUB, GROUP_WIDTH), F32)
        for h in range(HEADS_PER_GROUP):
            mine = head_of_lane == h
            acc = jnp.where(mine, o4[h * ATTN_SUB:(h + 1) * ATTN_SUB] * scales[h], acc)
            lse_full = jnp.where(mine, lses[h], lse_full)
        o_ref[j * ATTN_SUB:(j + 1) * ATTN_SUB, :] = acc.astype(BF16)
        lse_ref[j * ATTN_SUB:(j + 1) * ATTN_SUB, :] = lse_full

    def banded_step():
        kext = jnp.concatenate([kl_ref[...], km_ref[...], kr_ref[...]], axis=0)
        vext = jnp.concatenate([vl_ref[...], vm_ref[...], vr_ref[...]], axis=0)
        kw = ATTN_SUB + 2 * HALF_WINDOW
        a = lax.broadcasted_iota(I32, (ATTN_SUB, kw), 0)
        c = lax.broadcasted_iota(I32, (ATTN_SUB, kw), 1)
        in_band = jnp.abs(c - HALF_WINDOW - a) <= HALF_WINDOW
        for j in range(tq // ATTN_SUB):
            r0 = i * tq + j * ATTN_SUB
            in_prompt = r0 < prompt_rows
            b = jnp.maximum(r0 - prompt_rows, 0) // sample_rows
            lo = jnp.where(in_prompt, 0, prompt_rows + b * sample_rows)
            hi = jnp.where(in_prompt, prompt_rows, prompt_rows + (b + 1) * sample_rows)
            key_row = r0 - HALF_WINDOW + c
            valid = in_band & (key_row >= lo) & (key_row < hi)
            sub_block(j, kext[j * ATTN_SUB:j * ATTN_SUB + kw, :], vext[j * ATTN_SUB:j * ATTN_SUB + kw, :], valid)

    def single_sequence_step():
        a = lax.broadcasted_iota(I32, (ATTN_SUB, ATTN_SUB), 0)
        c = lax.broadcasted_iota(I32, (ATTN_SUB, ATTN_SUB), 1)
        valid = jnp.abs(c - a) <= HALF_WINDOW
        for j in range(tq // ATTN_SUB):
            rows = slice(j * ATTN_SUB, (j + 1) * ATTN_SUB)
            sub_block(j, km_ref[rows, :], vm_ref[rows, :], valid)

    if sample_rows == ATTN_SUB and prompt_rows % tq == 0:
        in_sample = i * tq >= prompt_rows
        pl.when(in_sample)(single_sequence_step)
        pl.when(jnp.logical_not(in_sample))(banded_step)
    else:
        banded_step()


def _attention_group(qv, kv, vv, dil, n_prompt, seq_sample):
    rows = qv.shape[0]
    tq = next(t for t in ATTN_TILES if rows % t == 0)
    halo_per_tile = tq // HALF_WINDOW
    n_halo_blocks = rows // HALF_WINDOW
    main = pl.BlockSpec((tq, GROUP_WIDTH), lambda r, i: (i, r))
    left = pl.BlockSpec((HALF_WINDOW, GROUP_WIDTH), lambda r, i: (jnp.maximum(i * halo_per_tile - 1, 0), r))
    right = pl.BlockSpec((HALF_WINDOW, GROUP_WIDTH),
                         lambda r, i: (jnp.minimum((i + 1) * halo_per_tile, n_halo_blocks - 1), r))
    return pl.pallas_call(
        functools.partial(_attn_kernel, prompt_rows=n_prompt // dil, sample_rows=seq_sample // dil),
        grid=(dil, rows // tq),
        in_specs=[main, left, main, right, left, main, right],
        out_specs=[main, main],
        out_shape=[jax.ShapeDtypeStruct((rows, dil * GROUP_WIDTH), BF16),
                   jax.ShapeDtypeStruct((rows, dil * GROUP_WIDTH), F32)],
        compiler_params=_cparams("parallel", "parallel"),
        name=f"dilated_attention_{dil}",
    )(qv, kv, kv, kv, vv, vv, vv)


def _angle_table(idx, period):
    return (idx % period).astype(F32) * (2.0 * math.pi / period)


def _channel_dft(seq_len):
    c = jnp.arange(F_GROUP_DIM, dtype=I32)
    th = _angle_table(c[:, None] * c[None, :], F_GROUP_DIM)
    scale = (seq_len * F_GROUP_DIM) ** -0.5
    eye = jnp.eye(F_GROUPS, dtype=F32)
    cb = jnp.kron(eye, jnp.cos(th)) * scale
    sb = jnp.kron(eye, jnp.sin(th)) * scale
    return jnp.concatenate([cb, sb], axis=0).astype(BF16)


def _stage2_matrices(s1, s2):
    n2 = jnp.arange(s2, dtype=I32)[None, :]
    th_t = _angle_table(jnp.arange(s1, dtype=I32)[:, None] * n2, s1 * s2)
    th_f = _angle_table(jnp.arange(s2, dtype=I32)[:, None] * n2, s2)
    tc, ts = jnp.cos(th_t)[:, None, :], jnp.sin(th_t)[:, None, :]
    fc, fs = jnp.cos(th_f)[None, :, :], jnp.sin(th_f)[None, :, :]
    gr = tc * fc - ts * fs
    gi = -(ts * fc + tc * fs)
    return jnp.concatenate([jnp.concatenate([gr, -gi], axis=2), jnp.concatenate([gi, gr], axis=2)], axis=1).astype(BF16)


def _dft_stage1_kernel(m_ref, x_ref, a_ref):
    a_ref[...] = jnp.dot(m_ref[...], x_ref[...], preferred_element_type=F32).astype(BF16)


def _dft_stage2_kernel(g_ref, a_ref, cs_ref, o_ref):
    s2 = a_ref.shape[2]
    for kk in range(g_ref.shape[0]):
        a = jnp.concatenate([a_ref[0, kk], a_ref[1, kk]], axis=0)
        x = jnp.dot(g_ref[kk], a, preferred_element_type=F32)
        xr = x[:s2].astype(BF16)
        xi = x[s2:].astype(BF16)
        o_ref[:, kk * F_WIDTH:(kk + 1) * F_WIDTH] = (
            jnp.dot(xr, cs_ref[:F_WIDTH, :], preferred_element_type=F32)
            + jnp.dot(xi, cs_ref[F_WIDTH:, :], preferred_element_type=F32)).astype(BF16)


def _fourier_prompt(f, seq):
    s1 = DFT_STAGE1
    s2 = seq // s1
    n1 = jnp.arange(s1, dtype=I32)
    th1 = _angle_table(n1[:, None] * n1[None, :], s1)
    m1 = jnp.concatenate([jnp.cos(th1), -jnp.sin(th1)], axis=0).astype(BF16)
    g = _stage2_matrices(s1, s2)
    cs = _channel_dft(seq)

    cols = s2 * F_WIDTH
    fv = f[:seq].reshape(s1, cols)
    tn = min(cols, 4096)
    a = pl.pallas_call(
        _dft_stage1_kernel,
        grid=(cols // tn,),
        in_specs=[pl.BlockSpec((2 * s1, s1), lambda j: (0, 0)), pl.BlockSpec((s1, tn), lambda j: (0, j))],
        out_specs=pl.BlockSpec((2 * s1, tn), lambda j: (0, j)),
        out_shape=jax.ShapeDtypeStruct((2 * s1, cols), BF16),
        compiler_params=_cparams("parallel"),
        name="dft_stage1",
    )(m1, fv)
    a4 = a.reshape(2, s1, s2, F_WIDTH)
    kb = DFT_STAGE2_BATCH
    out = pl.pallas_call(
        _dft_stage2_kernel,
        grid=(s1 // kb,),
        in_specs=[pl.BlockSpec((kb, 2 * s2, 2 * s2), lambda k: (k, 0, 0)),
                  pl.BlockSpec((2, kb, s2, F_WIDTH), lambda k: (0, k, 0, 0)),
                  pl.BlockSpec((2 * F_WIDTH, F_WIDTH), lambda k: (0, 0))],
        out_specs=pl.BlockSpec((s2, kb * F_WIDTH), lambda k: (0, k)),
        out_shape=jax.ShapeDtypeStruct((s2, s1 * F_WIDTH), BF16),
        compiler_params=_cparams("parallel"),
        name="dft_stage2",
    )(g, a4, cs)
    return out.reshape(seq, F_WIDTH)


def _dft_small_kernel(g_ref, x_ref, cs_ref, o_ref, stage_ref):
    s1 = DFT_SMALL_STAGE1
    assert s1 == 8
    s2 = x_ref.shape[1] // s1
    root = math.sqrt(0.5)
    n_seq = x_ref.shape[0]
    a_all = []
    for b in range(n_seq):
        x0, x1, x2, x3, x4, x5, x6, x7 = [x_ref[b, n1 * s2:(n1 + 1) * s2, :].astype(F32) for n1 in range(s1)]
        u, v, p, q = x0 - x4, x2 - x6, x1 - x5, x3 - x7
        e_sum, e_dif = (x0 + x4) + (x2 + x6), (x0 + x4) - (x2 + x6)
        o_sum, o_dif = (x1 + x5) + (x3 + x7), (x1 + x5) - (x3 + x7)
        rpq_m, rpq_p = root * (p - q), root * (p + q)
        zero = jnp.zeros_like(x0)
        a_re = [e_sum + o_sum, u + rpq_m, e_dif, u - rpq_m, e_sum - o_sum, u - rpq_m, e_dif, u + rpq_m]
        im1, im3 = -v - rpq_p, v - rpq_p
        a_im = [zero, im1, -o_dif, im3, zero, -im3, o_dif, -im1]
        a_all.append([jnp.concatenate([a_re[k1].astype(BF16), a_im[k1].astype(BF16)], axis=0) for k1 in range(s1)])

    for k1 in range(s1):
        a = jnp.concatenate([a_all[b][k1] for b in range(n_seq)], axis=1)
        x = jnp.dot(g_ref[k1], a, preferred_element_type=F32)
        for b in range(n_seq):
            xb = x[:, b * F_WIDTH:(b + 1) * F_WIDTH]
            out = (jnp.dot(xb[:s2].astype(BF16), cs_ref[:F_WIDTH, :], preferred_element_type=F32)
                   + jnp.dot(xb[s2:].astype(BF16), cs_ref[F_WIDTH:, :], preferred_element_type=F32))
            for c in range(F_WIDTH // LANES):
                stage_ref[b * (F_WIDTH // LANES) + c, pl.ds(k1, s2, stride=s1), :] = out[:, c * LANES:(c + 1) * LANES]
    for b in range(n_seq):
        o_ref[b] = jnp.concatenate([stage_ref[b * (F_WIDTH // LANES) + c] for c in range(F_WIDTH // LANES)],
                                   axis=1).astype(BF16)


def _fourier_sample(f, n_prompt, batch, seq):
    s1 = DFT_SMALL_STAGE1
    s2 = seq // s1
    g = _stage2_matrices(s1, s2)
    cs = _channel_dft(seq)
    f3 = f.reshape(f.shape[0] // seq, seq, F_WIDTH)
    first = n_prompt // seq
    nb = DFT_SMALL_BATCH
    assert batch % nb == 0 and first % nb == 0
    return pl.pallas_call(
        _dft_small_kernel,
        grid=(batch // nb,),
        in_specs=[pl.BlockSpec((s1, 2 * s2, 2 * s2), lambda b: (0, 0, 0)),
                  pl.BlockSpec((nb, seq, F_WIDTH), lambda b: (first // nb + b, 0, 0)),
                  pl.BlockSpec((2 * F_WIDTH, F_WIDTH), lambda b: (0, 0))],
        out_specs=pl.BlockSpec((nb, seq, F_WIDTH), lambda b: (b, 0, 0)),
        out_shape=jax.ShapeDtypeStruct((batch, seq, F_WIDTH), BF16),
        scratch_shapes=[pltpu.VMEM((nb * (F_WIDTH // LANES), seq, LANES), F32)],
        compiler_params=_cparams("parallel"),
        name="dft_small",
    )(g, f3, cs).reshape(batch * seq, F_WIDTH)


def _pack_bf16_pairs(x):
    w = x.shape[1] // 2
    bits = lax.bitcast_convert_type(x.astype(BF16).astype(F32), U32)
    return (bits[:, :w] >> 16) | (bits[:, w:] & jnp.uint32(0xFFFF0000))


def _unpack_bf16_pairs(u):
    lo = lax.bitcast_convert_type(u << 16, F32)
    hi = lax.bitcast_convert_type(u & jnp.uint32(0xFFFF0000), F32)
    return jnp.concatenate([lo, hi], axis=1)


def _layer_norm(h, g, b):
    mu = jnp.mean(h, axis=-1, keepdims=True)
    d = h - mu
    var = jnp.mean(d * d, axis=-1, keepdims=True)
    return d * lax.rsqrt(var + LN_EPS) * g + b


def _from_dilated_view(stage_ref, blk_ref, dil):
    if dil == 1:
        return blk_ref[...].astype(F32)
    rows = blk_ref.shape[0]
    for r in range(dil):
        for c in range(GROUP_WIDTH // LANES):
            lo = r * GROUP_WIDTH + c * LANES
            stage_ref[c, pl.ds(r, rows, stride=dil), :] = blk_ref[:, lo:lo + LANES].astype(F32)
    return jnp.concatenate([stage_ref[c] for c in range(GROUP_WIDTH // LANES)], axis=1)


def _merge_kernel(xp_ref, xs_ref, wg_ref, o1_ref, o2_ref, o3_ref, l1_ref, l2_ref, l3_ref, fp_ref, fs_ref,
                  wao_ref, wfo_ref, wo_ref, g_ref, b_ref, wr2_ref, br_ref, tri_ref,
                  x1_ref, pk_ref, idx_ref, gate_ref, rank_ref, cnt_ref, run_ref, *stage_refs, n_prompt_tiles):
    i = pl.program_id(0)

    @pl.when(i == 0)
    def _():
        run_ref[...] = jnp.zeros_like(run_ref)

    is_prompt = i < n_prompt_tiles
    x = jnp.where(is_prompt, xp_ref[...], xs_ref[...])
    gpre = jnp.dot(x.astype(BF16), wg_ref[...], preferred_element_type=F32)
    o1, o2, o3 = [_from_dilated_view(stage_refs[gi], ref, d)
                  for gi, (ref, d) in enumerate(zip((o1_ref, o2_ref, o3_ref), DILATIONS))]
    l1, l2, l3 = [_from_dilated_view(stage_refs[N_GROUPS + gi], ref, d)
                  for gi, (ref, d) in enumerate(zip((l1_ref, l2_ref, l3_ref), DILATIONS))]
    m = jnp.maximum(jnp.maximum(l1, l2), l3)
    e1, e2, e3 = jnp.exp(l1 - m), jnp.exp(l2 - m), jnp.exp(l3 - m)
    attn = (e1 * o1 + e2 * o2 + e3 * o3) * (1.0 / (e1 + e2 + e3))
    a = jnp.dot(attn.astype(BF16), wao_ref[...], preferred_element_type=F32)
    four = jnp.where(is_prompt, fp_ref[...], fs_ref[...])
    ff = jnp.dot(four, wfo_ref[...], preferred_element_type=F32)
    merged = jax.nn.sigmoid(gpre[:, :D_MODEL]) * a + jax.nn.sigmoid(gpre[:, D_MODEL:]) * ff
    mix = jnp.dot(merged.astype(BF16), wo_ref[...], preferred_element_type=F32)
    x1 = _layer_norm(DEEPNORM_ALPHA * x + mix, g_ref[...], b_ref[...])
    x1_ref[...] = x1
    pk_ref[...] = _pack_bf16_pairs(x1)

    xh = x1.astype(BF16)
    xl = (x1 - xh.astype(F32)).astype(BF16)
    tm = x1.shape[0]
    cross = lax.dot_general(wr2_ref[...], jnp.concatenate([xh, xl], axis=0), (((1,), (1,)), ((), ())),
                            preferred_element_type=F32)
    logits = ((cross[:N_EXPERTS, :tm] + cross[N_EXPERTS:, :tm])
              + (cross[:N_EXPERTS, tm:] + cross[N_EXPERTS:, tm:])) + br_ref[...]
    eio = lax.broadcasted_iota(I32, (N_EXPERTS, tm), 0)
    work = logits
    vals, sels, picks = [], [], []
    for _ in range(TOP_K):
        mk = jnp.max(work, axis=0, keepdims=True)
        ik = jnp.min(jnp.where(work == mk, eio, N_EXPERTS), axis=0, keepdims=True)
        sel = eio == ik
        vals.append(mk)
        sels.append(sel)
        picks.append(ik)
        work = jnp.where(sel, -jnp.inf, work)
    es = [jnp.exp(v - vals[0]) for v in vals]
    inv_den = 1.0 / (es[0] + es[1] + es[2] + es[3])
    chosen = (sels[0] | sels[1] | sels[2] | sels[3])
    chosen_f = chosen.astype(F32)
    prefix = jnp.dot(chosen_f.astype(BF16), tri_ref[...], preferred_element_type=F32)
    before = prefix + run_ref[...]
    for kk in range(TOP_K):
        idx_ref[kk:kk + 1, :] = picks[kk]
        gate_ref[kk:kk + 1, :] = es[kk] * inv_den
        rank_ref[kk:kk + 1, :] = jnp.sum(jnp.where(sels[kk], before, 0.0), axis=0, keepdims=True).astype(I32)
    run_ref[...] = run_ref[...] + jnp.sum(chosen_f, axis=1, keepdims=True)
    cnt_ref[...] = jnp.broadcast_to(run_ref[...], cnt_ref.shape).astype(I32)


def _merge_route(xp, xs, w_gate, o_list, l_list, four_p, four_s, w_ao, w_fo, w_o, ln_g, ln_b, w_r, b_r, tile0,
                 n_tiles):
    n_p = xp.shape[0]
    tm = TOKEN_TILE
    n_t = n_tiles * tm
    npt = n_p // tm
    w_rt = w_r.T
    w_rh = w_rt.astype(BF16)
    w_rl = (w_rt - w_rh.astype(F32)).astype(BF16)
    tri = (jnp.arange(tm)[:, None] < jnp.arange(tm)[None, :]).astype(BF16)
    row = lambda w: pl.BlockSpec((tm, w), lambda i: (i, 0))
    prompt_row = lambda w: pl.BlockSpec((tm, w), lambda i: (jnp.clip(tile0 + i, 0, npt - 1), 0))
    sample_row = lambda w: pl.BlockSpec((tm, w), lambda i: (jnp.maximum(tile0 + i - npt, 0), 0))
    full = lambda r, c: pl.BlockSpec((r, c), lambda i: (0, 0))
    lane_row = pl.BlockSpec((TOP_K, tm), lambda i: (0, i))
    views = [pl.BlockSpec((tm // d, d * GROUP_WIDTH), lambda i: (tile0 + i, 0)) for d in DILATIONS]
    return pl.pallas_call(
        functools.partial(_merge_kernel, n_prompt_tiles=npt - tile0),
        grid=(n_tiles,),
        in_specs=[
            prompt_row(D_MODEL), sample_row(D_MODEL),
            full(D_MODEL, 2 * D_MODEL),
            *views, *views,
            prompt_row(F_WIDTH), sample_row(F_WIDTH),
            full(GROUP_WIDTH, D_MODEL), full(F_WIDTH, D_MODEL), full(D_MODEL, D_MODEL),
            full(1, D_MODEL), full(1, D_MODEL),
            full(2 * N_EXPERTS, D_MODEL), full(N_EXPERTS, 1),
            full(tm, tm),
        ],
        out_specs=[row(D_MODEL), row(D_MODEL // 2), lane_row, lane_row, lane_row, full(N_EXPERTS, LANES)],
        out_shape=[
            jax.ShapeDtypeStruct((n_t, D_MODEL), F32),
            jax.ShapeDtypeStruct((n_t, D_MODEL // 2), U32),
            jax.ShapeDtypeStruct((TOP_K, n_t), I32),
            jax.ShapeDtypeStruct((TOP_K, n_t), F32),
            jax.ShapeDtypeStruct((TOP_K, n_t), I32),
            jax.ShapeDtypeStruct((N_EXPERTS, LANES), I32),
        ],
        scratch_shapes=[pltpu.VMEM((N_EXPERTS, 1), F32)]
        + [pltpu.VMEM((GROUP_WIDTH // LANES, tm, LANES), F32)] * (2 * N_GROUPS),
        compiler_params=_cparams("arbitrary"),
        name="merge_ln_route",
    )(xp, xs, w_gate, *o_list, *l_list, four_p, four_s, w_ao, w_fo, w_o, ln_g.reshape(1, -1), ln_b.reshape(1, -1),
      jnp.concatenate([w_rh, w_rl], axis=0), b_r.reshape(-1, 1), tri)


def _sc_workers():
    info = plsc.get_sparse_core_info()
    return info.num_cores, info.num_cores * info.num_subcores


def _sc_dispatch(rows_src, pos_chunks, n_rows_out):
    n_t, w = rows_src.shape
    n_cores, n_workers = _sc_workers()
    assert n_workers * SC_CHUNK == SC_ROUND_ROWS and n_t % SC_ROUND_ROWS == 0
    chunks_per_worker = n_t // (n_workers * SC_CHUNK)
    mesh = plsc.VectorSubcoreMesh(core_axis_name="c", subcore_axis_name="s")

    @functools.partial(
        pl.kernel, mesh=mesh,
        out_type=jax.ShapeDtypeStruct((n_rows_out, w), rows_src.dtype),
        scratch_types=[pltpu.VMEM((TOP_K, SC_CHUNK), I32), pltpu.VMEM((SC_CHUNK, w), rows_src.dtype)],
    )
    def dispatch(src_hbm, pos_hbm, out_hbm, idx_v, rows_v):
        wid = lax.axis_index("s") * n_cores + lax.axis_index("c")

        @pl.loop(0, chunks_per_worker)
        def _(j):
            chunk = wid * chunks_per_worker + j
            pltpu.sync_copy(pos_hbm.at[chunk], idx_v)
            pltpu.sync_copy(src_hbm.at[pl.ds(chunk * SC_CHUNK, SC_CHUNK)], rows_v)
            for kk in range(TOP_K):
                pltpu.sync_copy(rows_v, out_hbm.at[idx_v.at[kk]])

    return dispatch(rows_src, pos_chunks)


def _sc_combine(rows_src, pos_chunks, n_t):
    w = rows_src.shape[1]
    n_cores, n_workers = _sc_workers()
    assert n_workers * SC_CHUNK == SC_ROUND_ROWS and n_t % SC_ROUND_ROWS == 0
    chunks_per_worker = n_t // (n_workers * SC_CHUNK)
    mesh = plsc.VectorSubcoreMesh(core_axis_name="c", subcore_axis_name="s")

    @functools.partial(
        pl.kernel, mesh=mesh,
        out_type=jax.ShapeDtypeStruct((TOP_K, n_t, w), rows_src.dtype),
        scratch_types=[pltpu.VMEM((TOP_K, SC_CHUNK), I32), pltpu.VMEM((SC_CHUNK, w), rows_src.dtype)],
    )
    def combine(src_hbm, pos_hbm, out_hbm, idx_v, rows_v):
        wid = lax.axis_index("s") * n_cores + lax.axis_index("c")

        @pl.loop(0, chunks_per_worker)
        def _(j):
            chunk = wid * chunks_per_worker + j
            pltpu.sync_copy(pos_hbm.at[chunk], idx_v)
            for kk in range(TOP_K):
                pltpu.sync_copy(src_hbm.at[idx_v.at[kk]], rows_v)
                pltpu.sync_copy(rows_v, out_hbm.at[kk, pl.ds(chunk * SC_CHUNK, SC_CHUNK)])

    return combine(rows_src, pos_chunks)


def _expert_kernel(be_ref, nv_ref, x_ref, *refs, prepare):
    if prepare:
        wu_ref, wdn_ref, p_ref, bg_ref, bl_ref, bd_ref, y_ref, wg_ref, wl_ref, wd_ref = refs
    else:
        wg_ref, wl_ref, wd_ref, bg_ref, bl_ref, bd_ref, y_ref = refs
    blk = pl.program_id(0)
    nv = nv_ref[blk]

    if prepare:
        new_expert = (blk == 0) | (be_ref[blk] != be_ref[jnp.maximum(blk - 1, 0)])

        @pl.when(new_expert)
        def _():
            half = SPLIT_CHUNK // 2
            for c in range(wu_ref.shape[2] // SPLIT_CHUNK):
                w = wu_ref[0, :, c * SPLIT_CHUNK:(c + 1) * SPLIT_CHUNK].astype(BF16)
                r = jnp.dot(w, p_ref[...], preferred_element_type=F32)
                wg_ref[0, :, c * half:(c + 1) * half] = r[:, :half].astype(BF16)
                wl_ref[0, :, c * half:(c + 1) * half] = r[:, half:].astype(BF16)
            wd_ref[...] = wdn_ref[...].astype(BF16)

    @pl.when(nv > 0)
    def _():
        x = _unpack_bf16_pairs(x_ref[...])
        rows = lax.broadcasted_iota(I32, (x.shape[0], 1), 0)
        x = jnp.where(rows < nv, x, 0.0).astype(BF16)
        acc = jnp.zeros((x.shape[0], D_MODEL), F32) + bd_ref[0]
        for c in range(D_FF // FF_CHUNK):
            sl = slice(c * FF_CHUNK, (c + 1) * FF_CHUNK)
            hg = jnp.dot(x, wg_ref[0, :, sl], preferred_element_type=F32) + bg_ref[0, :, sl]
            hl = jnp.dot(x, wl_ref[0, :, sl], preferred_element_type=F32) + bl_ref[0, :, sl]
            glu = jnp.minimum(hg, SWIGLU_LIMIT)
            lin = jnp.clip(hl, -SWIGLU_LIMIT, SWIGLU_LIMIT)
            act = glu * jax.nn.sigmoid(SWIGLU_ALPHA * glu) * (lin + 1.0)
            acc = acc + jnp.dot(act.astype(BF16), wd_ref[0, sl, :], preferred_element_type=F32)
        y_ref[...] = _pack_bf16_pairs(acc)

    @pl.when(nv == 0)
    def _():
        y_ref[...] = jnp.zeros_like(y_ref)


def _expert_ffn(xs_rows, block_e, n_valid, weights, b_glu, b_lin, b_down):
    n_rows = xs_rows.shape[0]
    bm = EXPERT_BLOCK
    prepare = len(weights) == 2
    wspec = lambda r, c: pl.BlockSpec((1, r, c), lambda b, be, nv: (be[b], 0, 0))
    rows_spec = pl.BlockSpec((bm, D_MODEL // 2), lambda b, be, nv: (b, 0))
    bias_specs = [wspec(1, D_FF), wspec(1, D_FF), wspec(1, D_MODEL)]
    y_shape = jax.ShapeDtypeStruct((n_rows, D_MODEL // 2), U32)
    if prepare:
        half = SPLIT_CHUNK // 2
        src = jnp.arange(SPLIT_CHUNK, dtype=I32)[:, None]
        dst = jnp.arange(SPLIT_CHUNK, dtype=I32)[None, :]
        sel = jnp.where(dst < half, src == 2 * dst, src == 2 * (dst - half) + 1).astype(BF16)
        operands = (*weights, sel)
        weight_specs = [wspec(D_MODEL, 2 * D_FF), wspec(D_FF, D_MODEL),
                        pl.BlockSpec((SPLIT_CHUNK, SPLIT_CHUNK), lambda b, be, nv: (0, 0))]
        prepared = jax.ShapeDtypeStruct((N_EXPERTS, D_MODEL, D_FF), BF16)
        out_specs = [rows_spec, wspec(D_MODEL, D_FF), wspec(D_MODEL, D_FF), wspec(D_FF, D_MODEL)]
        out_shape = [y_shape, prepared, prepared, prepared]
    else:
        operands = tuple(weights)
        weight_specs = [wspec(D_MODEL, D_FF), wspec(D_MODEL, D_FF), wspec(D_FF, D_MODEL)]
        out_specs, out_shape = rows_spec, y_shape
    grid_spec = pltpu.PrefetchScalarGridSpec(
        num_scalar_prefetch=2,
        grid=(n_rows // bm,),
        in_specs=[rows_spec, *weight_specs, *bias_specs],
        out_specs=out_specs,
    )
    out = pl.pallas_call(
        functools.partial(_expert_kernel, prepare=prepare),
        grid_spec=grid_spec,
        out_shape=out_shape,
        compiler_params=_cparams("arbitrary"),
        name="expert_ffn",
    )(block_e, n_valid, xs_rows, *operands, b_glu, b_lin, b_down)
    return (out[0], tuple(out[1:])) if prepare else (out, tuple(weights))


def _final_kernel(x1_ref, yg_ref, gt_ref, g_ref, b_ref, *rest):
    o_ref = rest[-1]
    gates = gt_ref[...]
    pad = jnp.zeros((LANES - TOP_K, gates.shape[1]), F32)
    gt = jnp.concatenate([gates, pad], axis=0).T
    ffn = jnp.zeros(x1_ref.shape, F32)
    for kk in range(TOP_K):
        ffn = ffn + gt[:, kk:kk + 1] * _unpack_bf16_pairs(yg_ref[kk])
    o_ref[...] = _layer_norm(DEEPNORM_ALPHA * x1_ref[...] + ffn, g_ref[...], b_ref[...])


def _final(x1, yg, gates, ln_g, ln_b, src_row0, n_rows, out_rows, dst_row0, out_prev=None):
    tm = FINAL_TILE
    assert src_row0 % tm == 0 and dst_row0 % tm == 0 and n_rows % tm == 0
    src, dst = src_row0 // tm, dst_row0 // tm
    in_specs = [
        pl.BlockSpec((tm, D_MODEL), lambda i: (src + i, 0)),
        pl.BlockSpec((TOP_K, tm, D_MODEL // 2), lambda i: (0, src + i, 0)),
        pl.BlockSpec((TOP_K, tm), lambda i: (0, src + i)),
        pl.BlockSpec((1, D_MODEL), lambda i: (0, 0)),
        pl.BlockSpec((1, D_MODEL), lambda i: (0, 0)),
    ]
    args = [x1, yg, gates, ln_g.reshape(1, -1), ln_b.reshape(1, -1)]
    aliases = {}
    if out_prev is not None:
        in_specs.append(pl.BlockSpec(memory_space=pl.ANY))
        args.append(out_prev)
        aliases = {len(args) - 1: 0}
    return pl.pallas_call(
        _final_kernel,
        grid=(n_rows // tm,),
        in_specs=in_specs,
        out_specs=pl.BlockSpec((tm, D_MODEL), lambda i: (dst + i, 0)),
        out_shape=jax.ShapeDtypeStruct((out_rows, D_MODEL), F32),
        input_output_aliases=aliases,
        compiler_params=_cparams("parallel"),
        name="combine_ln2",
    )(*args)


def _routing_tables(idx, rank, counts, n_blocks):
    bm = EXPERT_BLOCK
    padded = jnp.maximum((counts + bm - 1) // bm, 1) * bm
    pad_end = jnp.cumsum(padded)
    pad_start = pad_end - padded
    experts = jnp.arange(N_EXPERTS, dtype=I32)

    def lookup(table, e):
        shape = (N_EXPERTS,) + (1,) * e.ndim
        return jnp.sum(jnp.where(e[None] == experts.reshape(shape), table.reshape(shape), 0), axis=0)

    pos = lookup(pad_start, idx) + rank
    blk_row0 = jnp.arange(n_blocks, dtype=I32) * bm
    block_e = jnp.minimum(jnp.sum((pad_end[:, None] <= blk_row0[None, :]).astype(I32), axis=0), N_EXPERTS - 1)
    n_valid = jnp.clip(lookup(counts, block_e) - (blk_row0 - lookup(pad_start, block_e)), 0, bm)
    return pos.astype(I32), block_e.astype(I32), n_valid.astype(I32)


def _moe(pk, idx, rank, counts, weights, b_glu, b_lin, b_down):
    n_t = pk.shape[0]
    n_blocks = (n_t * TOP_K) // EXPERT_BLOCK + N_EXPERTS
    pos, block_e, n_valid = _routing_tables(idx, rank, counts, n_blocks)
    pos_chunks = pos.reshape(TOP_K, n_t // SC_CHUNK, SC_CHUNK).transpose(1, 0, 2)
    xs_rows = _sc_dispatch(pk, pos_chunks, n_blocks * EXPERT_BLOCK)
    y_rows, prepared = _expert_ffn(xs_rows, block_e, n_valid, weights, b_glu, b_lin, b_down)
    return _sc_combine(y_rows, pos_chunks, n_t), prepared


def _rope_tables(seq_max):
    inv = ROPE_THETA ** (-jnp.arange(0, HEAD_DIM, 2, dtype=F32) / HEAD_DIM)
    ang = jnp.arange(seq_max).astype(F32)[:, None] * inv[None, :]
    cos, sin = jnp.cos(ang), jnp.sin(ang)
    reps = LANES // HEAD_DIM
    cos_t = jnp.tile(jnp.concatenate([cos, cos], axis=1), (1, reps))
    sin_t = jnp.tile(jnp.concatenate([-sin, sin], axis=1), (1, reps))
    return cos_t, sin_t


def _encoder_layer(xp, xs, seq_prompt, batch_sample, seq_sample, w_in, w_attn_out, w_four_out, w_o, ln1_g, ln1_b,
                   w_router, b_router, w_up, b_up, w_down, b_down, ln2_g, ln2_b):
    n_p, n_s = xp.shape[0], xs.shape[0]
    assert n_p == seq_prompt and n_s == batch_sample * seq_sample
    assert seq_prompt % (DFT_STAGE1 * 16) == 0 and seq_sample % TOKEN_TILE == 0 and n_p % seq_sample == 0
    n_t = n_p + n_s
    assert n_t % (max(DILATIONS) * min(ATTN_TILES)) == 0 and seq_sample % (max(DILATIONS) * ATTN_SUB) == 0

    w_proj = w_in[:, :PROJ_WIDTH].astype(BF16)
    w_gate = w_in[:, PROJ_WIDTH:].astype(BF16)
    cos_t, sin_t = _rope_tables(max(seq_prompt, seq_sample))
    q_views, k_views, v_views, f = _inproj(xp, xs, w_proj, cos_t, sin_t, seq_sample)

    o_list, l_list = [], []
    for gi, dil in enumerate(DILATIONS):
        o, lse = _attention_group(q_views[gi], k_views[gi], v_views[gi], dil, n_p, seq_sample)
        o_list.append(o)
        l_list.append(lse)

    four_p = _fourier_prompt(f, seq_prompt)
    four_s = _fourier_sample(f, n_p, batch_sample, seq_sample)

    b_glu = b_up[:, None, 0::2]
    b_lin = b_up[:, None, 1::2]
    w_ao, w_fo, w_ob = w_attn_out.astype(BF16), w_four_out.astype(BF16), w_o.astype(BF16)

    tiles = n_t // TOKEN_TILE
    align = math.lcm(SC_ROUND_ROWS, FINAL_TILE) // TOKEN_TILE
    bounds = [0]
    for share in MOE_PART_SHARES[:-1]:
        bounds.append(min(tiles, bounds[-1] + max(align, round(tiles * share / align) * align)))
    bounds.append(tiles)
    n_parts = len(bounds) - 1
    routed = []
    for p in range(n_parts):
        routed.append(_merge_route(xp, xs, w_gate, o_list, l_list, four_p, four_s, w_ao, w_fo, w_ob, ln1_g, ln1_b,
                                   w_router, b_router, bounds[p], bounds[p + 1] - bounds[p]))
    combined = []
    weights = (w_up, w_down)
    for x1, pk, idx, gates, rank, cnt in routed:
        yg, weights = _moe(pk, idx, rank, cnt[:, 0], weights, b_glu, b_lin, b_down[:, None, :])
        combined.append(yg)

    outs = [None, None]
    spans = [(0, n_p), (n_p, n_t)]
    for p, ((x1, pk, idx, gates, rank, cnt), yg) in enumerate(zip(routed, combined)):
        lo, hi = bounds[p] * TOKEN_TILE, bounds[p + 1] * TOKEN_TILE
        for which, (a, b) in enumerate(spans):
            s, e = max(lo, a), min(hi, b)
            if s < e:
                outs[which] = _final(x1, yg, gates, ln2_g, ln2_b, s - lo, e - s, b - a, s - a, outs[which])
    return outs[0], outs[1]


def kernel(x_prompt, x_sample, w_in, w_attn_out, w_four_out, w_o, ln1_g, ln1_b, w_router, b_router, w_up, b_up,
           w_down, b_down, ln2_g, ln2_b):
    assert w_in.shape[0] == DEPTH
    bp, sp, d = x_prompt.shape
    bs, ss, _ = x_sample.shape
    assert bp == 1 and d == D_MODEL
    y_p, y_s = _encoder_layer(
        x_prompt.reshape(sp, d), x_sample.reshape(bs * ss, d), sp, bs, ss,
        w_in[0], w_attn_out[0], w_four_out[0], w_o[0], ln1_g[0], ln1_b[0], w_router[0], b_router[0],
        w_up[0], b_up[0], w_down[0], b_down[0], ln2_g[0], ln2_b[0])
    return y_p.reshape(x_prompt.shape), y_s.reshape(x_sample.shape)
```

```python
import functools
import math

import jax
import jax.numpy as jnp
from jax import lax
from jax.experimental import pallas as pl
from jax.experimental.pallas import tpu as pltpu
from jax.experimental.pallas import tpu_sc as plsc

F32 = jnp.float32
BF16 = jnp.bfloat16
I32 = jnp.int32
U32 = jnp.uint32

D_MODEL = 1024
HEAD_DIM = 64
HEADS_PER_GROUP = 4
GROUP_WIDTH = HEADS_PER_GROUP * HEAD_DIM
DILATIONS = (1, 4, 16)
HALF_WINDOW = 64
N_GROUPS = len(DILATIONS)
QKV_WIDTH = N_GROUPS * GROUP_WIDTH
F_GROUPS = 4
F_GROUP_DIM = 64
F_WIDTH = F_GROUPS * F_GROUP_DIM
PROJ_WIDTH = 3 * QKV_WIDTH + F_WIDTH
N_EXPERTS = 32
TOP_K = 4
D_FF = 1024
SWIGLU_LIMIT = 7.0
SWIGLU_ALPHA = 1.702
LN_EPS = 1e-5
ROPE_THETA = 10000.0
NEG_INF = -1e30
DEPTH = 1
DEEPNORM_ALPHA = (2 * DEPTH) ** 0.25

LANES = 128
MXU_TILE = 256
TOKEN_TILE = 512
INPROJ_TILE = 1024
FINAL_TILE = 1024
ATTN_TILES = (2048, 1024)
ATTN_SUB = 128
EXPERT_BLOCK = 512
FF_CHUNK = 1024
SPLIT_CHUNK = MXU_TILE
DFT_STAGE1 = 128
DFT_STAGE2_BATCH = 8
DFT_SMALL_STAGE1 = 8
DFT_SMALL_BATCH = 4
SC_CHUNK = 64
SC_ROUND_ROWS = 2 * 16 * SC_CHUNK
MOE_PART_SHARES = (0.45, 0.35, 0.2)
VMEM_LIMIT = 56 * 1024 * 1024


def _cparams(*sem):
    return pltpu.CompilerParams(dimension_semantics=sem, vmem_limit_bytes=VMEM_LIMIT)


def _to_dilated_view(stage_ref, out_ref, val, dil):
    if dil == 1:
        out_ref[...] = val.astype(BF16)
        return
    rows = val.shape[0] // dil
    for c in range(GROUP_WIDTH // LANES):
        stage_ref[c] = val[:, c * LANES:(c + 1) * LANES]
    for r in range(dil):
        for c in range(GROUP_WIDTH // LANES):
            lo = r * GROUP_WIDTH + c * LANES
            out_ref[:, lo:lo + LANES] = stage_ref[c, pl.ds(r, rows, stride=dil), :].astype(BF16)


def _inproj_kernel(xp_ref, xs_ref, w_ref, cos_ref, sin_ref, *refs, n_prompt_tiles):
    qkv_refs = refs[:3 * N_GROUPS]
    f_ref = refs[3 * N_GROUPS]
    stage_refs = refs[3 * N_GROUPS + 1:]
    i = pl.program_id(0)
    x = jnp.where(i < n_prompt_tiles, xp_ref[...], xs_ref[...]).astype(BF16)
    proj = jnp.dot(x, w_ref[...], preferred_element_type=F32)
    cos = cos_ref[...]
    sin = sin_ref[...]
    lane = lax.broadcasted_iota(I32, cos.shape, 1)
    first_half = (lane % HEAD_DIM) < (HEAD_DIM // 2)

    def rope(t):
        parts = []
        for c in range(GROUP_WIDTH // LANES):
            tc = t[:, c * LANES:(c + 1) * LANES]
            swapped = jnp.where(first_half, pltpu.roll(tc, LANES - HEAD_DIM // 2, 1), pltpu.roll(tc, HEAD_DIM // 2, 1))
            parts.append(tc * cos + swapped * sin)
        return jnp.concatenate(parts, axis=1)

    for gi, dil in enumerate(DILATIONS):
        sl = slice(gi * GROUP_WIDTH, (gi + 1) * GROUP_WIDTH)
        qg = rope(proj[:, sl]) * (HEAD_DIM ** -0.5)
        kg = rope(proj[:, QKV_WIDTH + gi * GROUP_WIDTH:QKV_WIDTH + (gi + 1) * GROUP_WIDTH])
        vg = proj[:, 2 * QKV_WIDTH + gi * GROUP_WIDTH:2 * QKV_WIDTH + (gi + 1) * GROUP_WIDTH]
        for which, val in enumerate((qg, kg, vg)):
            _to_dilated_view(stage_refs[which * N_GROUPS + gi], qkv_refs[which * N_GROUPS + gi], val, dil)
    f_ref[...] = proj[:, 3 * QKV_WIDTH:PROJ_WIDTH].astype(BF16)


def _inproj(xp, xs, w_proj, cos_t, sin_t, seq_sample):
    n_p, n_s = xp.shape[0], xs.shape[0]
    n_t = n_p + n_s
    tm = INPROJ_TILE
    assert n_p % tm == 0 and seq_sample % tm == 0
    npt = n_p // tm
    tiles_per_sample_seq = seq_sample // tm

    def table_idx(i):
        return (jnp.where(i < npt, i, (i - npt) % tiles_per_sample_seq), 0)

    view_specs = [pl.BlockSpec((tm // d, d * GROUP_WIDTH), lambda i: (i, 0)) for d in DILATIONS] * 3
    view_shapes = [jax.ShapeDtypeStruct((n_t // d, d * GROUP_WIDTH), BF16) for d in DILATIONS] * 3
    outs = pl.pallas_call(
        functools.partial(_inproj_kernel, n_prompt_tiles=npt),
        grid=(n_t // tm,),
        in_specs=[
            pl.BlockSpec((tm, D_MODEL), lambda i: (jnp.minimum(i, npt - 1), 0)),
            pl.BlockSpec((tm, D_MODEL), lambda i: (jnp.maximum(i - npt, 0), 0)),
            pl.BlockSpec((D_MODEL, PROJ_WIDTH), lambda i: (0, 0)),
            pl.BlockSpec((tm, LANES), table_idx),
            pl.BlockSpec((tm, LANES), table_idx),
        ],
        out_specs=view_specs + [pl.BlockSpec((tm, F_WIDTH), lambda i: (i, 0))],
        out_shape=view_shapes + [jax.ShapeDtypeStruct((n_t, F_WIDTH), BF16)],
        scratch_shapes=[pltpu.VMEM((GROUP_WIDTH // LANES, tm, LANES), F32)] * (3 * N_GROUPS),
        compiler_params=_cparams("parallel"),
        name="inproj_rope",
    )(xp, xs, w_proj, cos_t, sin_t)
    q_views, k_views, v_views = outs[0:N_GROUPS], outs[N_GROUPS:2 * N_GROUPS], outs[2 * N_GROUPS:3 * N_GROUPS]
    return q_views, k_views, v_views, outs[3 * N_GROUPS]


def _attn_kernel(q_ref, kl_ref, km_ref, kr_ref, vl_ref, vm_ref, vr_ref, o_ref, lse_ref, *, prompt_rows, sample_rows):
    i = pl.program_id(1)
    tq = q_ref.shape[0]
    head_of_lane = lax.broadcasted_iota(I32, (1, GROUP_WIDTH), 1) // HEAD_DIM

    def sub_block(j, kj, vj, valid):
        qj = q_ref[j * ATTN_SUB:(j + 1) * ATTN_SUB, :]
        q4 = jnp.concatenate([jnp.where(head_of_lane == h, qj, jnp.zeros_like(qj)) for h in range(HEADS_PER_GROUP)],
                             axis=0)
        s4 = lax.dot_general(q4, kj, (((1,), (1,)), ((), ())), preferred_element_type=F32)
        ps, scales, lses = [], [], []
        for h in range(HEADS_PER_GROUP):
            s = jnp.where(valid, s4[h * ATTN_SUB:(h + 1) * ATTN_SUB], NEG_INF)
            m = jnp.max(s, axis=1, keepdims=True)
            p = jnp.exp(s - m)
            l = jnp.sum(p, axis=1, keepdims=True)
            ps.append(p.astype(BF16))
            scales.append(1.0 / l)
            lses.append(m + jnp.log(l))
        o4 = jnp.dot(jnp.concatenate(ps, axis=0), vj, preferred_element_type=F32)
        acc = jnp.zeros((ATTN_SUB, GROUP_WIDTH), F32)
        lse_full = jnp.zeros((ATTN_SUB, GROUP_WIDTH), F32)
        for h in range(HEADS_PER_GROUP):
            mine = head_of_lane == h
            acc = jnp.where(mine, o4[h * ATTN_SUB:(h + 1) * ATTN_SUB] * scales[h], acc)
            lse_full = jnp.where(mine, lses[h], lse_full)
        o_ref[j * ATTN_SUB:(j + 1) * ATTN_SUB, :] = acc.astype(BF16)
        lse_ref[j * ATTN_SUB:(j + 1) * ATTN_SUB, :] = lse_full

    def banded_step():
        kext = jnp.concatenate([kl_ref[...], km_ref[...], kr_ref[...]], axis=0)
        vext = jnp.concatenate([vl_ref[...], vm_ref[...], vr_ref[...]], axis=0)
        kw = ATTN_SUB + 2 * HALF_WINDOW
        a = lax.broadcasted_iota(I32, (ATTN_SUB, kw), 0)
        c = lax.broadcasted_iota(I32, (ATTN_SUB, kw), 1)
        in_band = jnp.abs(c - HALF_WINDOW - a) <= HALF_WINDOW
        for j in range(tq // ATTN_SUB):
            r0 = i * tq + j * ATTN_SUB
            in_prompt = r0 < prompt_rows
            b = jnp.maximum(r0 - prompt_rows, 0) // sample_rows
            lo = jnp.where(in_prompt, 0, prompt_rows + b * sample_rows)
            hi = jnp.where(in_prompt, prompt_rows, prompt_rows + (b + 1) * sample_rows)
            key_row = r0 - HALF_WINDOW + c
            valid = in_band & (key_row >= lo) & (key_row < hi)
            sub_block(j, kext[j * ATTN_SUB:j * ATTN_SUB + kw, :], vext[j * ATTN_SUB:j * ATTN_SUB + kw, :], valid)

    def single_sequence_step():
        a = lax.broadcasted_iota(I32, (ATTN_SUB, ATTN_SUB), 0)
        c = lax.broadcasted_iota(I32, (ATTN_SUB, ATTN_SUB), 1)
        valid = jnp.abs(c - a) <= HALF_WINDOW
        for j in range(tq // ATTN_SUB):
            rows = slice(j * ATTN_SUB, (j + 1) * ATTN_SUB)
            sub_block(j, km_ref[rows, :], vm_ref[rows, :], valid)

    if sample_rows == ATTN_SUB and prompt_rows % tq == 0:
        in_sample = i * tq >= prompt_rows
        pl.when(in_sample)(single_sequence_step)
        pl.when(jnp.logical_not(in_sample))(banded_step)
    else:
        banded_step()


def _attention_group(qv, kv, vv, dil, n_prompt, seq_sample):
    rows = qv.shape[0]
    tq = next(t for t in ATTN_TILES if rows % t == 0)
    halo_per_tile = tq // HALF_WINDOW
    n_halo_blocks = rows // HALF_WINDOW
    main = pl.BlockSpec((tq, GROUP_WIDTH), lambda r, i: (i, r))
    left = pl.BlockSpec((HALF_WINDOW, GROUP_WIDTH), lambda r, i: (jnp.maximum(i * halo_per_tile - 1, 0), r))
    right = pl.BlockSpec((HALF_WINDOW, GROUP_WIDTH),
                         lambda r, i: (jnp.minimum((i + 1) * halo_per_tile, n_halo_blocks - 1), r))
    return pl.pallas_call(
        functools.partial(_attn_kernel, prompt_rows=n_prompt // dil, sample_rows=seq_sample // dil),
        grid=(dil, rows // tq),
        in_specs=[main, left, main, right, left, main, right],
        out_specs=[main, main],
        out_shape=[jax.ShapeDtypeStruct((rows, dil * GROUP_WIDTH), BF16),
                   jax.ShapeDtypeStruct((rows, dil * GROUP_WIDTH), F32)],
        compiler_params=_cparams("parallel", "parallel"),
        name=f"dilated_attention_{dil}",
    )(qv, kv, kv, kv, vv, vv, vv)


def _angle_table(idx, period):
    return (idx % period).astype(F32) * (2.0 * math.pi / period)


def _channel_dft(seq_len):
    c = jnp.arange(F_GROUP_DIM, dtype=I32)
    th = _angle_table(c[:, None] * c[None, :], F_GROUP_DIM)
    scale = (seq_len * F_GROUP_DIM) ** -0.5
    eye = jnp.eye(F_GROUPS, dtype=F32)
    cb = jnp.kron(eye, jnp.cos(th)) * scale
    sb = jnp.kron(eye, jnp.sin(th)) * scale
    return jnp.concatenate([cb, sb], axis=0).astype(BF16)


def _stage2_matrices(s1, s2):
    n2 = jnp.arange(s2, dtype=I32)[None, :]
    th_t = _angle_table(jnp.arange(s1, dtype=I32)[:, None] * n2, s1 * s2)
    th_f = _angle_table(jnp.arange(s2, dtype=I32)[:, None] * n2, s2)
    tc, ts = jnp.cos(th_t)[:, None, :], jnp.sin(th_t)[:, None, :]
    fc, fs = jnp.cos(th_f)[None, :, :], jnp.sin(th_f)[None, :, :]
    gr = tc * fc - ts * fs
    gi = -(ts * fc + tc * fs)
    return jnp.concatenate([jnp.concatenate([gr, -gi], axis=2), jnp.concatenate([gi, gr], axis=2)], axis=1).astype(BF16)


def _dft_stage1_kernel(m_ref, x_ref, a_ref):
    a_ref[...] = jnp.dot(m_ref[...], x_ref[...], preferred_element_type=F32).astype(BF16)


def _dft_stage2_kernel(g_ref, a_ref, cs_ref, o_ref):
    s2 = a_ref.shape[2]
    for kk in range(g_ref.shape[0]):
        a = jnp.concatenate([a_ref[0, kk], a_ref[1, kk]], axis=0)
        x = jnp.dot(g_ref[kk], a, preferred_element_type=F32)
        xr = x[:s2].astype(BF16)
        xi = x[s2:].astype(BF16)
        o_ref[:, kk * F_WIDTH:(kk + 1) * F_WIDTH] = (
            jnp.dot(xr, cs_ref[:F_WIDTH, :], preferred_element_type=F32)
            + jnp.dot(xi, cs_ref[F_WIDTH:, :], preferred_element_type=F32)).astype(BF16)


def _fourier_prompt(f, seq):
    s1 = DFT_STAGE1
    s2 = seq // s1
    n1 = jnp.arange(s1, dtype=I32)
    th1 = _angle_table(n1[:, None] * n1[None, :], s1)
    m1 = jnp.concatenate([jnp.cos(th1), -jnp.sin(th1)], axis=0).astype(BF16)
    g = _stage2_matrices(s1, s2)
    cs = _channel_dft(seq)

    cols = s2 * F_WIDTH
    fv = f[:seq].reshape(s1, cols)
    tn = min(cols, 4096)
    a = pl.pallas_call(
        _dft_stage1_kernel,
        grid=(cols // tn,),
        in_specs=[pl.BlockSpec((2 * s1, s1), lambda j: (0, 0)), pl.BlockSpec((s1, tn), lambda j: (0, j))],
        out_specs=pl.BlockSpec((2 * s1, tn), lambda j: (0, j)),
        out_shape=jax.ShapeDtypeStruct((2 * s1, cols), BF16),
        compiler_params=_cparams("parallel"),
        name="dft_stage1",
    )(m1, fv)
    a4 = a.reshape(2, s1, s2, F_WIDTH)
    kb = DFT_STAGE2_BATCH
    out = pl.pallas_call(
        _dft_stage2_kernel,
        grid=(s1 // kb,),
        in_specs=[pl.BlockSpec((kb, 2 * s2, 2 * s2), lambda k: (k, 0, 0)),
                  pl.BlockSpec((2, kb, s2, F_WIDTH), lambda k: (0, k, 0, 0)),
                  pl.BlockSpec((2 * F_WIDTH, F_WIDTH), lambda k: (0, 0))],
        out_specs=pl.BlockSpec((s2, kb * F_WIDTH), lambda k: (0, k)),
        out_shape=jax.ShapeDtypeStruct((s2, s1 * F_WIDTH), BF16),
        compiler_params=_cparams("parallel"),
        name="dft_stage2",
    )(g, a4, cs)
    return out.reshape(seq, F_WIDTH)


def _dft_small_kernel(g_ref, x_ref, cs_ref, o_ref, stage_ref):
    s1 = DFT_SMALL_STAGE1
    assert s1 == 8
    s2 = x_ref.shape[1] // s1
    root = math.sqrt(0.5)
    n_seq = x_ref.shape[0]
    a_all = []
    for b in range(n_seq):
        x0, x1, x2, x3, x4, x5, x6, x7 = [x_ref[b, n1 * s2:(n1 + 1) * s2, :].astype(F32) for n1 in range(s1)]
        u, v, p, q = x0 - x4, x2 - x6, x1 - x5, x3 - x7
        e_sum, e_dif = (x0 + x4) + (x2 + x6), (x0 + x4) - (x2 + x6)
        o_sum, o_dif = (x1 + x5) + (x3 + x7), (x1 + x5) - (x3 + x7)
        rpq_m, rpq_p = root * (p - q), root * (p + q)
        zero = jnp.zeros_like(x0)
        a_re = [e_sum + o_sum, u + rpq_m, e_dif, u - rpq_m, e_sum - o_sum, u - rpq_m, e_dif, u + rpq_m]
        im1, im3 = -v - rpq_p, v - rpq_p
        a_im = [zero, im1, -o_dif, im3, zero, -im3, o_dif, -im1]
        a_all.append([jnp.concatenate([a_re[k1].astype(BF16), a_im[k1].astype(BF16)], axis=0) for k1 in range(s1)])

    for k1 in range(s1):
        a = jnp.concatenate([a_all[b][k1] for b in range(n_seq)], axis=1)
        x = jnp.dot(g_ref[k1], a, preferred_element_type=F32)
        for b in range(n_seq):
            xb = x[:, b * F_WIDTH:(b + 1) * F_WIDTH]
            out = (jnp.dot(xb[:s2].astype(BF16), cs_ref[:F_WIDTH, :], preferred_element_type=F32)
                   + jnp.dot(xb[s2:].astype(BF16), cs_ref[F_WIDTH:, :], preferred_element_type=F32))
            for c in range(F_WIDTH // LANES):
                stage_ref[b * (F_WIDTH // LANES) + c, pl.ds(k1, s2, stride=s1), :] = out[:, c * LANES:(c + 1) * LANES]
    for b in range(n_seq):
        o_ref[b] = jnp.concatenate([stage_ref[b * (F_WIDTH // LANES) + c] for c in range(F_WIDTH // LANES)],
                                   axis=1).astype(BF16)


def _fourier_sample(f, n_prompt, batch, seq):
    s1 = DFT_SMALL_STAGE1
    s2 = seq // s1
    g = _stage2_matrices(s1, s2)
    cs = _channel_dft(seq)
    f3 = f.reshape(f.shape[0] // seq, seq, F_WIDTH)
    first = n_prompt // seq
    nb = DFT_SMALL_BATCH
    assert batch % nb == 0 and first % nb == 0
    return pl.pallas_call(
        _dft_small_kernel,
        grid=(batch // nb,),
        in_specs=[pl.BlockSpec((s1, 2 * s2, 2 * s2), lambda b: (0, 0, 0)),
                  pl.BlockSpec((nb, seq, F_WIDTH), lambda b: (first // nb + b, 0, 0)),
                  pl.BlockSpec((2 * F_WIDTH, F_WIDTH), lambda b: (0, 0))],
        out_specs=pl.BlockSpec((nb, seq, F_WIDTH), lambda b: (b, 0, 0)),
        out_shape=jax.ShapeDtypeStruct((batch, seq, F_WIDTH), BF16),
        scratch_shapes=[pltpu.VMEM((nb * (F_WIDTH // LANES), seq, LANES), F32)],
        compiler_params=_cparams("parallel"),
        name="dft_small",
    )(g, f3, cs).reshape(batch * seq, F_WIDTH)


def _pack_bf16_pairs(x):
    w = x.shape[1] // 2
    bits = lax.bitcast_convert_type(x.astype(BF16).astype(F32), U32)
    return (bits[:, :w] >> 16) | (bits[:, w:] & jnp.uint32(0xFFFF0000))


def _unpack_bf16_pairs(u):
    lo = lax.bitcast_convert_type(u << 16, F32)
    hi = lax.bitcast_convert_type(u & jnp.uint32(0xFFFF0000), F32)
    return jnp.concatenate([lo, hi], axis=1)


def _layer_norm(h, g, b):
    mu = jnp.mean(h, axis=-1, keepdims=True)
    d = h - mu
    var = jnp.mean(d * d, axis=-1, keepdims=True)
    return d * lax.rsqrt(var + LN_EPS) * g + b


def _from_dilated_view(stage_ref, blk_ref, dil):
    if dil == 1:
        return blk_ref[...].astype(F32)
    rows = blk_ref.shape[0]
    for r in range(dil):
        for c in range(GROUP_WIDTH // LANES):
            lo = r * GROUP_WIDTH + c * LANES
            stage_ref[c, pl.ds(r, rows, stride=dil), :] = blk_ref[:, lo:lo + LANES].astype(F32)
    return jnp.concatenate([stage_ref[c] for c in range(GROUP_WIDTH // LANES)], axis=1)


def _merge_kernel(xp_ref, xs_ref, wg_ref, o1_ref, o2_ref, o3_ref, l1_ref, l2_ref, l3_ref, fp_ref, fs_ref,
                  wao_ref, wfo_ref, wo_ref, g_ref, b_ref, wr2_ref, br_ref, tri_ref,
                  x1_ref, pk_ref, idx_ref, gate_ref, rank_ref, cnt_ref, run_ref, *stage_refs, n_prompt_tiles):
    i = pl.program_id(0)

    @pl.when(i == 0)
    def _():
        run_ref[...] = jnp.zeros_like(run_ref)

    is_prompt = i < n_prompt_tiles
    x = jnp.where(is_prompt, xp_ref[...], xs_ref[...])
    gpre = jnp.dot(x.astype(BF16), wg_ref[...], preferred_element_type=F32)
    o1, o2, o3 = [_from_dilated_view(stage_refs[gi], ref, d)
                  for gi, (ref, d) in enumerate(zip((o1_ref, o2_ref, o3_ref), DILATIONS))]
    l1, l2, l3 = [_from_dilated_view(stage_refs[N_GROUPS + gi], ref, d)
                  for gi, (ref, d) in enumerate(zip((l1_ref, l2_ref, l3_ref), DILATIONS))]
    m = jnp.maximum(jnp.maximum(l1, l2), l3)
    e1, e2, e3 = jnp.exp(l1 - m), jnp.exp(l2 - m), jnp.exp(l3 - m)
    attn = (e1 * o1 + e2 * o2 + e3 * o3) * (1.0 / (e1 + e2 + e3))
    a = jnp.dot(attn.astype(BF16), wao_ref[...], preferred_element_type=F32)
    four = jnp.where(is_prompt, fp_ref[...], fs_ref[...])
    ff = jnp.dot(four, wfo_ref[...], preferred_element_type=F32)
    merged = jax.nn.sigmoid(gpre[:, :D_MODEL]) * a + jax.nn.sigmoid(gpre[:, D_MODEL:]) * ff
    mix = jnp.dot(merged.astype(BF16), wo_ref[...], preferred_element_type=F32)
    x1 = _layer_norm(DEEPNORM_ALPHA * x + mix, g_ref[...], b_ref[...])
    x1_ref[...] = x1
    pk_ref[...] = _pack_bf16_pairs(x1)

    xh = x1.astype(BF16)
    xl = (x1 - xh.astype(F32)).astype(BF16)
    tm = x1.shape[0]
    cross = lax.dot_general(wr2_ref[...], jnp.concatenate([xh, xl], axis=0), (((1,), (1,)), ((), ())),
                            preferred_element_type=F32)
    logits = ((cross[:N_EXPERTS, :tm] + cross[N_EXPERTS:, :tm])
              + (cross[:N_EXPERTS, tm:] + cross[N_EXPERTS:, tm:])) + br_ref[...]
    eio = lax.broadcasted_iota(I32, (N_EXPERTS, tm), 0)
    work = logits
    vals, sels, picks = [], [], []
    for _ in range(TOP_K):
        mk = jnp.max(work, axis=0, keepdims=True)
        ik = jnp.min(jnp.where(work == mk, eio, N_EXPERTS), axis=0, keepdims=True)
        sel = eio == ik
        vals.append(mk)
        sels.append(sel)
        picks.append(ik)
        work = jnp.where(sel, -jnp.inf, work)
    es = [jnp.exp(v - vals[0]) for v in vals]
    inv_den = 1.0 / (es[0] + es[1] + es[2] + es[3])
    chosen = (sels[0] | sels[1] | sels[2] | sels[3])
    chosen_f = chosen.astype(F32)
    prefix = jnp.dot(chosen_f.astype(BF16), tri_ref[...], preferred_element_type=F32)
    before = prefix + run_ref[...]
    for kk in range(TOP_K):
        idx_ref[kk:kk + 1, :] = picks[kk]
        gate_ref[kk:kk + 1, :] = es[kk] * inv_den
        rank_ref[kk:kk + 1, :] = jnp.sum(jnp.where(sels[kk], before, 0.0), axis=0, keepdims=True).astype(I32)
    run_ref[...] = run_ref[...] + jnp.sum(chosen_f, axis=1, keepdims=True)
    cnt_ref[...] = jnp.broadcast_to(run_ref[...], cnt_ref.shape).astype(I32)


def _merge_route(xp, xs, w_gate, o_list, l_list, four_p, four_s, w_ao, w_fo, w_o, ln_g, ln_b, w_r, b_r, tile0,
                 n_tiles):
    n_p = xp.shape[0]
    tm = TOKEN_TILE
    n_t = n_tiles * tm
    npt = n_p // tm
    w_rt = w_r.T
    w_rh = w_rt.astype(BF16)
    w_rl = (w_rt - w_rh.astype(F32)).astype(BF16)
    tri = (jnp.arange(tm)[:, None] < jnp.arange(tm)[None, :]).astype(BF16)
    row = lambda w: pl.BlockSpec((tm, w), lambda i: (i, 0))
    prompt_row = lambda w: pl.BlockSpec((tm, w), lambda i: (jnp.clip(tile0 + i, 0, npt - 1), 0))
    sample_row = lambda w: pl.BlockSpec((tm, w), lambda i: (jnp.maximum(tile0 + i - npt, 0), 0))
    full = lambda r, c: pl.BlockSpec((r, c), lambda i: (0, 0))
    lane_row = pl.BlockSpec((TOP_K, tm), lambda i: (0, i))
    views = [pl.BlockSpec((tm // d, d * GROUP_WIDTH), lambda i: (tile0 + i, 0)) for d in DILATIONS]
    return pl.pallas_call(
        functools.partial(_merge_kernel, n_prompt_tiles=npt - tile0),
        grid=(n_tiles,),
        in_specs=[
            prompt_row(D_MODEL), sample_row(D_MODEL),
            full(D_MODEL, 2 * D_MODEL),
            *views, *views,
            prompt_row(F_WIDTH), sample_row(F_WIDTH),
            full(GROUP_WIDTH, D_MODEL), full(F_WIDTH, D_MODEL), full(D_MODEL, D_MODEL),
            full(1, D_MODEL), full(1, D_MODEL),
            full(2 * N_EXPERTS, D_MODEL), full(N_EXPERTS, 1),
            full(tm, tm),
        ],
        out_specs=[row(D_MODEL), row(D_MODEL // 2), lane_row, lane_row, lane_row, full(N_EXPERTS, LANES)],
        out_shape=[
            jax.ShapeDtypeStruct((n_t, D_MODEL), F32),
            jax.ShapeDtypeStruct((n_t, D_MODEL // 2), U32),
            jax.ShapeDtypeStruct((TOP_K, n_t), I32),
            jax.ShapeDtypeStruct((TOP_K, n_t), F32),
            jax.ShapeDtypeStruct((TOP_K, n_t), I32),
            jax.ShapeDtypeStruct((N_EXPERTS, LANES), I32),
        ],
        scratch_shapes=[pltpu.VMEM((N_EXPERTS, 1), F32)]
        + [pltpu.VMEM((GROUP_WIDTH // LANES, tm, LANES), F32)] * (2 * N_GROUPS),
        compiler_params=_cparams("arbitrary"),
        name="merge_ln_route",
    )(xp, xs, w_gate, *o_list, *l_list, four_p, four_s, w_ao, w_fo, w_o, ln_g.reshape(1, -1), ln_b.reshape(1, -1),
      jnp.concatenate([w_rh, w_rl], axis=0), b_r.reshape(-1, 1), tri)


def _sc_workers():
    info = plsc.get_sparse_core_info()
    return info.num_cores, info.num_cores * info.num_subcores


def _sc_dispatch(rows_src, pos_chunks, n_rows_out):
    n_t, w = rows_src.shape
    n_cores, n_workers = _sc_workers()
    assert n_workers * SC_CHUNK == SC_ROUND_ROWS and n_t % SC_ROUND_ROWS == 0
    chunks_per_worker = n_t // (n_workers * SC_CHUNK)
    mesh = plsc.VectorSubcoreMesh(core_axis_name="c", subcore_axis_name="s")

    @functools.partial(
        pl.kernel, mesh=mesh,
        out_type=jax.ShapeDtypeStruct((n_rows_out, w), rows_src.dtype),
        scratch_types=[pltpu.VMEM((TOP_K, SC_CHUNK), I32), pltpu.VMEM((SC_CHUNK, w), rows_src.dtype)],
    )
    def dispatch(src_hbm, pos_hbm, out_hbm, idx_v, rows_v):
        wid = lax.axis_index("s") * n_cores + lax.axis_index("c")

        @pl.loop(0, chunks_per_worker)
        def _(j):
            chunk = wid * chunks_per_worker + j
            pltpu.sync_copy(pos_hbm.at[chunk], idx_v)
            pltpu.sync_copy(src_hbm.at[pl.ds(chunk * SC_CHUNK, SC_CHUNK)], rows_v)
            for kk in range(TOP_K):
                pltpu.sync_copy(rows_v, out_hbm.at[idx_v.at[kk]])

    return dispatch(rows_src, pos_chunks)


def _sc_combine(rows_src, pos_chunks, n_t):
    w = rows_src.shape[1]
    n_cores, n_workers = _sc_workers()
    assert n_workers * SC_CHUNK == SC_ROUND_ROWS and n_t % SC_ROUND_ROWS == 0
    chunks_per_worker = n_t // (n_workers * SC_CHUNK)
    mesh = plsc.VectorSubcoreMesh(core_axis_name="c", subcore_axis_name="s")

    @functools.partial(
        pl.kernel, mesh=mesh,
        out_type=jax.ShapeDtypeStruct((TOP_K, n_t, w), rows_src.dtype),
        scratch_types=[pltpu.VMEM((TOP_K, SC_CHUNK), I32), pltpu.VMEM((SC_CHUNK, w), rows_src.dtype)],
    )
    def combine(src_hbm, pos_hbm, out_hbm, idx_v, rows_v):
        wid = lax.axis_index("s") * n_cores + lax.axis_index("c")

        @pl.loop(0, chunks_per_worker)
        def _(j):
            chunk = wid * chunks_per_worker + j
            pltpu.sync_copy(pos_hbm.at[chunk], idx_v)
            for kk in range(TOP_K):
                pltpu.sync_copy(src_hbm.at[idx_v.at[kk]], rows_v)
                pltpu.sync_copy(rows_v, out_hbm.at[kk, pl.ds(chunk * SC_CHUNK, SC_CHUNK)])

    return combine(rows_src, pos_chunks)


def _expert_kernel(be_ref, nv_ref, x_ref, *refs, prepare):
    if prepare:
        wu_ref, wdn_ref, p_ref, bg_ref, bl_ref, bd_ref, y_ref, wg_ref, wl_ref, wd_ref = refs
    else:
        wg_ref, wl_ref, wd_ref, bg_ref, bl_ref, bd_ref, y_ref = refs
    blk = pl.program_id(0)
    nv = nv_ref[blk]

    if prepare:
        new_expert = (blk == 0) | (be_ref[blk] != be_ref[jnp.maximum(blk - 1, 0)])

        @pl.when(new_expert)
        def _():
            half = SPLIT_CHUNK // 2
            for c in range(wu_ref.shape[2] // SPLIT_CHUNK):
                w = wu_ref[0, :, c * SPLIT_CHUNK:(c + 1) * SPLIT_CHUNK].astype(BF16)
                r = jnp.dot(w, p_ref[...], preferred_element_type=F32)
                wg_ref[0, :, c * half:(c + 1) * half] = r[:, :half].astype(BF16)
                wl_ref[0, :, c * half:(c + 1) * half] = r[:, half:].astype(BF16)
            wd_ref[...] = wdn_ref[...].astype(BF16)

    @pl.when(nv > 0)
    def _():
        x = _unpack_bf16_pairs(x_ref[...])
        rows = lax.broadcasted_iota(I32, (x.shape[0], 1), 0)
        x = jnp.where(rows < nv, x, 0.0).astype(BF16)
        acc = jnp.zeros((x.shape[0], D_MODEL), F32) + bd_ref[0]
        for c in range(D_FF // FF_CHUNK):
            sl = slice(c * FF_CHUNK, (c + 1) * FF_CHUNK)
            hg = jnp.dot(x, wg_ref[0, :, sl], preferred_element_type=F32) + bg_ref[0, :, sl]
            hl = jnp.dot(x, wl_ref[0, :, sl], preferred_element_type=F32) + bl_ref[0, :, sl]
            glu = jnp.minimum(hg, SWIGLU_LIMIT)
            lin = jnp.clip(hl, -SWIGLU_LIMIT, SWIGLU_LIMIT)
            act = glu * jax.nn.sigmoid(SWIGLU_ALPHA * glu) * (lin + 1.0)
            acc = acc + jnp.dot(act.astype(BF16), wd_ref[0, sl, :], preferred_element_type=F32)
        y_ref[...] = _pack_bf16_pairs(acc)

    @pl.when(nv == 0)
    def _():
        y_ref[...] = jnp.zeros_like(y_ref)


def _expert_ffn(xs_rows, block_e, n_valid, weights, b_glu, b_lin, b_down):
    n_rows = xs_rows.shape[0]
    bm = EXPERT_BLOCK
    prepare = len(weights) == 2
    wspec = lambda r, c: pl.BlockSpec((1, r, c), lambda b, be, nv: (be[b], 0, 0))
    rows_spec = pl.BlockSpec((bm, D_MODEL // 2), lambda b, be, nv: (b, 0))
    bias_specs = [wspec(1, D_FF), wspec(1, D_FF), wspec(1, D_MODEL)]
    y_shape = jax.ShapeDtypeStruct((n_rows, D_MODEL // 2), U32)
    if prepare:
        half = SPLIT_CHUNK // 2
        src = jnp.arange(SPLIT_CHUNK, dtype=I32)[:, None]
        dst = jnp.arange(SPLIT_CHUNK, dtype=I32)[None, :]
        sel = jnp.where(dst < half, src == 2 * dst, src == 2 * (dst - half) + 1).astype(BF16)
        operands = (*weights, sel)
        weight_specs = [wspec(D_MODEL, 2 * D_FF), wspec(D_FF, D_MODEL),
                        pl.BlockSpec((SPLIT_CHUNK, SPLIT_CHUNK), lambda b, be, nv: (0, 0))]
        prepared = jax.ShapeDtypeStruct((N_EXPERTS, D_MODEL, D_FF), BF16)
        out_specs = [rows_spec, wspec(D_MODEL, D_FF), wspec(D_MODEL, D_FF), wspec(D_FF, D_MODEL)]
        out_shape = [y_shape, prepared, prepared, prepared]
    else:
        operands = tuple(weights)
        weight_specs = [wspec(D_MODEL, D_FF), wspec(D_MODEL, D_FF), wspec(D_FF, D_MODEL)]
        out_specs, out_shape = rows_spec, y_shape
    grid_spec = pltpu.PrefetchScalarGridSpec(
        num_scalar_prefetch=2,
        grid=(n_rows // bm,),
        in_specs=[rows_spec, *weight_specs, *bias_specs],
        out_specs=out_specs,
    )
    out = pl.pallas_call(
        functools.partial(_expert_kernel, prepare=prepare),
        grid_spec=grid_spec,
        out_shape=out_shape,
        compiler_params=_cparams("arbitrary"),
        name="expert_ffn",
    )(block_e, n_valid, xs_rows, *operands, b_glu, b_lin, b_down)
    return (out[0], tuple(out[1:])) if prepare else (out, tuple(weights))


def _final_kernel(x1_ref, yg_ref, gt_ref, g_ref, b_ref, *rest):
    o_ref = rest[-1]
    gates = gt_ref[...]
    pad = jnp.zeros((LANES - TOP_K, gates.shape[1]), F32)
    gt = jnp.concatenate([gates, pad], axis=0).T
    ffn = jnp.zeros(x1_ref.shape, F32)
    for kk in range(TOP_K):
        ffn = ffn + gt[:, kk:kk + 1] * _unpack_bf16_pairs(yg_ref[kk])
    o_ref[...] = _layer_norm(DEEPNORM_ALPHA * x1_ref[...] + ffn, g_ref[...], b_ref[...])


def _final(x1, yg, gates, ln_g, ln_b, src_row0, n_rows, out_rows, dst_row0, out_prev=None):
    tm = FINAL_TILE
    assert src_row0 % tm == 0 and dst_row0 % tm == 0 and n_rows % tm == 0
    src, dst = src_row0 // tm, dst_row0 // tm
    in_specs = [
        pl.BlockSpec((tm, D_MODEL), lambda i: (src + i, 0)),
        pl.BlockSpec((TOP_K, tm, D_MODEL // 2), lambda i: (0, src + i, 0)),
        pl.BlockSpec((TOP_K, tm), lambda i: (0, src + i)),
        pl.BlockSpec((1, D_MODEL), lambda i: (0, 0)),
        pl.BlockSpec((1, D_MODEL), lambda i: (0, 0)),
    ]
    args = [x1, yg, gates, ln_g.reshape(1, -1), ln_b.reshape(1, -1)]
    aliases = {}
    if out_prev is not None:
        in_specs.append(pl.BlockSpec(memory_space=pl.ANY))
        args.append(out_prev)
        aliases = {len(args) - 1: 0}
    return pl.pallas_call(
        _final_kernel,
        grid=(n_rows // tm,),
        in_specs=in_specs,
        out_specs=pl.BlockSpec((tm, D_MODEL), lambda i: (dst + i, 0)),
        out_shape=jax.ShapeDtypeStruct((out_rows, D_MODEL), F32),
        input_output_aliases=aliases,
        compiler_params=_cparams("parallel"),
        name="combine_ln2",
    )(*args)


def _routing_tables(idx, rank, counts, n_blocks):
    bm = EXPERT_BLOCK
    padded = jnp.maximum((counts + bm - 1) // bm, 1) * bm
    pad_end = jnp.cumsum(padded)
    pad_start = pad_end - padded
    experts = jnp.arange(N_EXPERTS, dtype=I32)

    def lookup(table, e):
        shape = (N_EXPERTS,) + (1,) * e.ndim
        return jnp.sum(jnp.where(e[None] == experts.reshape(shape), table.reshape(shape), 0), axis=0)

    pos = lookup(pad_start, idx) + rank
    blk_row0 = jnp.arange(n_blocks, dtype=I32) * bm
    block_e = jnp.minimum(jnp.sum((pad_end[:, None] <= blk_row0[None, :]).astype(I32), axis=0), N_EXPERTS - 1)
    n_valid = jnp.clip(lookup(counts, block_e) - (blk_row0 - lookup(pad_start, block_e)), 0, bm)
    return pos.astype(I32), block_e.astype(I32), n_valid.astype(I32)


def _moe(pk, idx, rank, counts, weights, b_glu, b_lin, b_down):
    n_t = pk.shape[0]
    n_blocks = (n_t * TOP_K) // EXPERT_BLOCK + N_EXPERTS
    pos, block_e, n_valid = _routing_tables(idx, rank, counts, n_blocks)
    pos_chunks = pos.reshape(TOP_K, n_t // SC_CHUNK, SC_CHUNK).transpose(1, 0, 2)
    xs_rows = _sc_dispatch(pk, pos_chunks, n_blocks * EXPERT_BLOCK)
    y_rows, prepared = _expert_ffn(xs_rows, block_e, n_valid, weights, b_glu, b_lin, b_down)
    return _sc_combine(y_rows, pos_chunks, n_t), prepared


def _rope_tables(seq_max):
    inv = ROPE_THETA ** (-jnp.arange(0, HEAD_DIM, 2, dtype=F32) / HEAD_DIM)
    ang = jnp.arange(seq_max).astype(F32)[:, None] * inv[None, :]
    cos, sin = jnp.cos(ang), jnp.sin(ang)
    reps = LANES // HEAD_DIM
    cos_t = jnp.tile(jnp.concatenate([cos, cos], axis=1), (1, reps))
    sin_t = jnp.tile(jnp.concatenate([-sin, sin], axis=1), (1, reps))
    return cos_t, sin_t


def _encoder_layer(xp, xs, seq_prompt, batch_sample, seq_sample, w_in, w_attn_out, w_four_out, w_o, ln1_g, ln1_b,
                   w_router, b_router, w_up, b_up, w_down, b_down, ln2_g, ln2_b):
    n_p, n_s = xp.shape[0], xs.shape[0]
    assert n_p == seq_prompt and n_s == batch_sample * seq_sample
    assert seq_prompt % (DFT_STAGE1 * 16) == 0 and seq_sample % TOKEN_TILE == 0 and n_p % seq_sample == 0
    n_t = n_p + n_s
    assert n_t % (max(DILATIONS) * min(ATTN_TILES)) == 0 and seq_sample % (max(DILATIONS) * ATTN_SUB) == 0

    w_proj = w_in[:, :PROJ_WIDTH].astype(BF16)
    w_gate = w_in[:, PROJ_WIDTH:].astype(BF16)
    cos_t, sin_t = _rope_tables(max(seq_prompt, seq_sample))
    q_views, k_views, v_views, f = _inproj(xp, xs, w_proj, cos_t, sin_t, seq_sample)

    o_list, l_list = [], []
    for gi, dil in enumerate(DILATIONS):
        o, lse = _attention_group(q_views[gi], k_views[gi], v_views[gi], dil, n_p, seq_sample)
        o_list.append(o)
        l_list.append(lse)

    four_p = _fourier_prompt(f, seq_prompt)
    four_s = _fourier_sample(f, n_p, batch_sample, seq_sample)

    b_glu = b_up[:, None, 0::2]
    b_lin = b_up[:, None, 1::2]
    w_ao, w_fo, w_ob = w_attn_out.astype(BF16), w_four_out.astype(BF16), w_o.astype(BF16)

    tiles = n_t // TOKEN_TILE
    align = math.lcm(SC_ROUND_ROWS, FINAL_TILE) // TOKEN_TILE
    bounds = [0]
    for share in MOE_PART_SHARES[:-1]:
        bounds.append(min(tiles, bounds[-1] + max(align, round(tiles * share / align) * align)))
    bounds.append(tiles)
    n_parts = len(bounds) - 1
    routed = []
    for p in range(n_parts):
        routed.append(_merge_route(xp, xs, w_gate, o_list, l_list, four_p, four_s, w_ao, w_fo, w_ob, ln1_g, ln1_b,
                                   w_router, b_router, bounds[p], bounds[p + 1] - bounds[p]))
    combined = []
    weights = (w_up, w_down)
    for x1, pk, idx, gates, rank, cnt in routed:
        yg, weights = _moe(pk, idx, rank, cnt[:, 0], weights, b_glu, b_lin, b_down[:, None, :])
        combined.append(yg)

    outs = [None, None]
    spans = [(0, n_p), (n_p, n_t)]
    for p, ((x1, pk, idx, gates, rank, cnt), yg) in enumerate(zip(routed, combined)):
        lo, hi = bounds[p] * TOKEN_TILE, bounds[p + 1] * TOKEN_TILE
        for which, (a, b) in enumerate(spans):
            s, e = max(lo, a), min(hi, b)
            if s < e:
                outs[which] = _final(x1, yg, gates, ln2_g, ln2_b, s - lo, e - s, b - a, s - a, outs[which])
    return outs[0], outs[1]


def kernel(x_prompt, x_sample, w_in, w_attn_out, w_four_out, w_o, ln1_g, ln1_b, w_router, b_router, w_up, b_up,
           w_down, b_down, ln2_g, ln2_b):
    assert w_in.shape[0] == DEPTH
    bp, sp, d = x_prompt.shape
    bs, ss, _ = x_sample.shape
    assert bp == 1 and d == D_MODEL
    y_p, y_s = _encoder_layer(
        x_prompt.reshape(sp, d), x_sample.reshape(bs * ss, d), sp, bs, ss,
        w_in[0], w_attn_out[0], w_four_out[0], w_o[0], ln1_g[0], ln1_b[0], w_router[0], b_router[0],
        w_up[0], b_up[0], w_down[0], b_down[0], ln2_g[0], ln2_b[0])
    return y_p.reshape(x_prompt.shape), y_s.reshape(x_sample.shape)
```

```python
import functools
import math

import jax
import jax.numpy as jnp
from jax import lax
from jax.experimental import pallas as pl
from jax.experimental.pallas import tpu as pltpu
from jax.experimental.pallas import tpu_sc as plsc

F32 = jnp.float32
BF16 = jnp.bfloat16
I32 = jnp.int32
U32 = jnp.uint32

D_MODEL = 1024
HEAD_DIM = 64
HEADS_PER_GROUP = 4
GROUP_WIDTH = HEADS_PER_GROUP * HEAD_DIM
DILATIONS = (1, 4, 16)
HALF_WINDOW = 64
N_GROUPS = len(DILATIONS)
QKV_WIDTH = N_GROUPS * GROUP_WIDTH
F_GROUPS = 4
F_GROUP_DIM = 64
F_WIDTH = F_GROUPS * F_GROUP_DIM
PROJ_WIDTH = 3 * QKV_WIDTH + F_WIDTH
N_EXPERTS = 32
TOP_K = 4
D_FF = 1024
SWIGLU_LIMIT = 7.0
SWIGLU_ALPHA = 1.702
LN_EPS = 1e-5
ROPE_THETA = 10000.0
NEG_INF = -1e30
DEPTH = 1
DEEPNORM_ALPHA = (2 * DEPTH) ** 0.25

LANES = 128
MXU_TILE = 256
TOKEN_TILE = 512
INPROJ_TILE = 1024
FINAL_TILE = 1024
ATTN_TILES = (4096, 2048, 1024)
ATTN_SUB = 128
EXPERT_BLOCK = 512
FF_CHUNK = 1024
SPLIT_CHUNK = MXU_TILE
DFT_STAGE1 = 128
DFT_STAGE2_BATCH = 8
DFT_SMALL_STAGE1 = 8
DFT_SMALL_BATCH = 4
SC_CHUNK = 64
SC_ROUND_ROWS = 2 * 16 * SC_CHUNK
MOE_PART_SHARES = (0.7, 0.3)
VMEM_LIMIT = 56 * 1024 * 1024


def _cparams(*sem):
    return pltpu.CompilerParams(dimension_semantics=sem, vmem_limit_bytes=VMEM_LIMIT)


def _to_dilated_view(stage_ref, out_ref, val, dil):
    if dil == 1:
        out_ref[...] = val.astype(BF16)
        return
    rows = val.shape[0] // dil
    for c in range(GROUP_WIDTH // LANES):
        stage_ref[c] = val[:, c * LANES:(c + 1) * LANES]
    for r in range(dil):
        for c in range(GROUP_WIDTH // LANES):
            lo = r * GROUP_WIDTH + c * LANES
            out_ref[:, lo:lo + LANES] = stage_ref[c, pl.ds(r, rows, stride=dil), :].astype(BF16)


def _inproj_kernel(xp_ref, xs_ref, w_ref, cos_ref, sin_ref, *refs, n_prompt_tiles):
    qkv_refs = refs[:3 * N_GROUPS]
    f_ref = refs[3 * N_GROUPS]
    stage_refs = refs[3 * N_GROUPS + 1:]
    i = pl.program_id(0)
    x = jnp.where(i < n_prompt_tiles, xp_ref[...], xs_ref[...]).astype(BF16)
    proj = jnp.dot(x, w_ref[...], preferred_element_type=F32)
    cos = cos_ref[...]
    sin = sin_ref[...]
    lane = lax.broadcasted_iota(I32, cos.shape, 1)
    first_half = (lane % HEAD_DIM) < (HEAD_DIM // 2)

    def rope(t):
        parts = []
        for c in range(GROUP_WIDTH // LANES):
            tc = t[:, c * LANES:(c + 1) * LANES]
            swapped = jnp.where(first_half, pltpu.roll(tc, LANES - HEAD_DIM // 2, 1), pltpu.roll(tc, HEAD_DIM // 2, 1))
            parts.append(tc * cos + swapped * sin)
        return jnp.concatenate(parts, axis=1)

    for gi, dil in enumerate(DILATIONS):
        sl = slice(gi * GROUP_WIDTH, (gi + 1) * GROUP_WIDTH)
        qg = rope(proj[:, sl]) * (HEAD_DIM ** -0.5)
        kg = rope(proj[:, QKV_WIDTH + gi * GROUP_WIDTH:QKV_WIDTH + (gi + 1) * GROUP_WIDTH])
        vg = proj[:, 2 * QKV_WIDTH + gi * GROUP_WIDTH:2 * QKV_WIDTH + (gi + 1) * GROUP_WIDTH]
        for which, val in enumerate((qg, kg, vg)):
            _to_dilated_view(stage_refs[which * N_GROUPS + gi], qkv_refs[which * N_GROUPS + gi], val, dil)
    f_ref[...] = proj[:, 3 * QKV_WIDTH:PROJ_WIDTH].astype(BF16)


def _inproj(xp, xs, w_proj, cos_t, sin_t, seq_sample):
    n_p, n_s = xp.shape[0], xs.shape[0]
    n_t = n_p + n_s
    tm = INPROJ_TILE
    assert n_p % tm == 0 and seq_sample % tm == 0
    npt = n_p // tm
    tiles_per_sample_seq = seq_sample // tm

    def table_idx(i):
        return (jnp.where(i < npt, i, (i - npt) % tiles_per_sample_seq), 0)

    view_specs = [pl.BlockSpec((tm // d, d * GROUP_WIDTH), lambda i: (i, 0)) for d in DILATIONS] * 3
    view_shapes = [jax.ShapeDtypeStruct((n_t // d, d * GROUP_WIDTH), BF16) for d in DILATIONS] * 3
    outs = pl.pallas_call(
        functools.partial(_inproj_kernel, n_prompt_tiles=npt),
        grid=(n_t // tm,),
        in_specs=[
            pl.BlockSpec((tm, D_MODEL), lambda i: (jnp.minimum(i, npt - 1), 0)),
            pl.BlockSpec((tm, D_MODEL), lambda i: (jnp.maximum(i - npt, 0), 0)),
            pl.BlockSpec((D_MODEL, PROJ_WIDTH), lambda i: (0, 0)),
            pl.BlockSpec((tm, LANES), table_idx),
            pl.BlockSpec((tm, LANES), table_idx),
        ],
        out_specs=view_specs + [pl.BlockSpec((tm, F_WIDTH), lambda i: (i, 0))],
        out_shape=view_shapes + [jax.ShapeDtypeStruct((n_t, F_WIDTH), BF16)],
        scratch_shapes=[pltpu.VMEM((GROUP_WIDTH // LANES, tm, LANES), F32)] * (3 * N_GROUPS),
        compiler_params=_cparams("parallel"),
        name="inproj_rope",
    )(xp, xs, w_proj, cos_t, sin_t)
    q_views, k_views, v_views = outs[0:N_GROUPS], outs[N_GROUPS:2 * N_GROUPS], outs[2 * N_GROUPS:3 * N_GROUPS]
    return q_views, k_views, v_views, outs[3 * N_GROUPS]


def _attn_kernel(q_ref, kl_ref, km_ref, kr_ref, vl_ref, vm_ref, vr_ref, o_ref, lse_ref, *, prompt_rows, sample_rows):
    i = pl.program_id(1)
    tq = q_ref.shape[0]
    head_of_lane = lax.broadcasted_iota(I32, (1, GROUP_WIDTH), 1) // HEAD_DIM

    def sub_block(j, kj, vj, valid):
        qj = q_ref[j * ATTN_SUB:(j + 1) * ATTN_SUB, :]
        q4 = jnp.concatenate([jnp.where(head_of_lane == h, qj, jnp.zeros_like(qj)) for h in range(HEADS_PER_GROUP)],
                             axis=0)
        s4 = lax.dot_general(q4, kj, (((1,), (1,)), ((), ())), preferred_element_type=F32)
        ps, scales, lses = [], [], []
        for h in range(HEADS_PER_GROUP):
            s = jnp.where(valid, s4[h * ATTN_SUB:(h + 1) * ATTN_SUB], NEG_INF)
            m = jnp.max(s, axis=1, keepdims=True)
            p = jnp.exp(s - m)
            l = jnp.sum(p, axis=1, keepdims=True)
            ps.append(p.astype(BF16))
            scales.append(1.0 / l)
            lses.append(m + jnp.log(l))
        o4 = jnp.dot(jnp.concatenate(ps, axis=0), vj, preferred_element_type=F32)
        acc = jnp.zeros((ATTN_SUB, GROUP_WIDTH), F32)
        lse_full = jnp.zeros((ATTN_SUB, GROUP_WIDTH), F32)
        for h in range(HEADS_PER_GROUP):
            mine = head_of_lane == h
            acc = jnp.where(mine, o4[h * ATTN_SUB:(h + 1) * ATTN_SUB] * scales[h], acc)
            lse_full = jnp.where(mine, lses[h], lse_full)
        o_ref[j * ATTN_SUB:(j + 1) * ATTN_SUB, :] = acc.astype(BF16)
        lse_ref[j * ATTN_SUB:(j + 1) * ATTN_SUB, :] = lse_full

    def banded_step():
        kext = jnp.concatenate([kl_ref[...], km_ref[...], kr_ref[...]], axis=0)
        vext = jnp.concatenate([vl_ref[...], vm_ref[...], vr_ref[...]], axis=0)
        kw = ATTN_SUB + 2 * HALF_WINDOW
        a = lax.broadcasted_iota(I32, (ATTN_SUB, kw), 0)
        c = lax.broadcasted_iota(I32, (ATTN_SUB, kw), 1)
        in_band = jnp.abs(c - HALF_WINDOW - a) <= HALF_WINDOW
        for j in range(tq // ATTN_SUB):
            r0 = i * tq + j * ATTN_SUB
            in_prompt = r0 < prompt_rows
            b = jnp.maximum(r0 - prompt_rows, 0) // sample_rows
            lo = jnp.where(in_prompt, 0, prompt_rows + b * sample_rows)
            hi = jnp.where(in_prompt, prompt_rows, prompt_rows + (b + 1) * sample_rows)
            key_row = r0 - HALF_WINDOW + c
            valid = in_band & (key_row >= lo) & (key_row < hi)
            sub_block(j, kext[j * ATTN_SUB:j * ATTN_SUB + kw, :], vext[j * ATTN_SUB:j * ATTN_SUB + kw, :], valid)

    def single_sequence_step():
        a = lax.broadcasted_iota(I32, (ATTN_SUB, ATTN_SUB), 0)
        c = lax.broadcasted_iota(I32, (ATTN_SUB, ATTN_SUB), 1)
        valid = jnp.abs(c - a) <= HALF_WINDOW
        for j in range(tq // ATTN_SUB):
            rows = slice(j * ATTN_SUB, (j + 1) * ATTN_SUB)
            sub_block(j, km_ref[rows, :], vm_ref[rows, :], valid)

    if sample_rows == ATTN_SUB and prompt_rows % tq == 0:
        in_sample = i * tq >= prompt_rows
        pl.when(in_sample)(single_sequence_step)
        pl.when(jnp.logical_not(in_sample))(banded_step)
    else:
        banded_step()


def _attention_group(qv, kv, vv, dil, n_prompt, seq_sample):
    rows = qv.shape[0]
    tq = next(t for t in ATTN_TILES if rows % t == 0)
    halo_per_tile = tq // HALF_WINDOW
    n_halo_blocks = rows // HALF_WINDOW
    main = pl.BlockSpec((tq, GROUP_WIDTH), lambda r, i: (i, r))
    left = pl.BlockSpec((HALF_WINDOW, GROUP_WIDTH), lambda r, i: (jnp.maximum(i * halo_per_tile - 1, 0), r))
    right = pl.BlockSpec((HALF_WINDOW, GROUP_WIDTH),
                         lambda r, i: (jnp.minimum((i + 1) * halo_per_tile, n_halo_blocks - 1), r))
    return pl.pallas_call(
        functools.partial(_attn_kernel, prompt_rows=n_prompt // dil, sample_rows=seq_sample // dil),
        grid=(dil, rows // tq),
        in_specs=[main, left, main, right, left, main, right],
        out_specs=[main, main],
        out_shape=[jax.ShapeDtypeStruct((rows, dil * GROUP_WIDTH), BF16),
                   jax.ShapeDtypeStruct((rows, dil * GROUP_WIDTH), F32)],
        compiler_params=_cparams("parallel", "parallel"),
        name=f"dilated_attention_{dil}",
    )(qv, kv, kv, kv, vv, vv, vv)


def _angle_table(idx, period):
    return (idx % period).astype(F32) * (2.0 * math.pi / period)


def _channel_dft(seq_len):
    c = jnp.arange(F_GROUP_DIM, dtype=I32)
    th = _angle_table(c[:, None] * c[None, :], F_GROUP_DIM)
    scale = (seq_len * F_GROUP_DIM) ** -0.5
    eye = jnp.eye(F_GROUPS, dtype=F32)
    cb = jnp.kron(eye, jnp.cos(th)) * scale
    sb = jnp.kron(eye, jnp.sin(th)) * scale
    return jnp.concatenate([cb, sb], axis=0).astype(BF16)


def _stage2_matrices(s1, s2):
    n2 = jnp.arange(s2, dtype=I32)[None, :]
    th_t = _angle_table(jnp.arange(s1, dtype=I32)[:, None] * n2, s1 * s2)
    th_f = _angle_table(jnp.arange(s2, dtype=I32)[:, None] * n2, s2)
    tc, ts = jnp.cos(th_t)[:, None, :], jnp.sin(th_t)[:, None, :]
    fc, fs = jnp.cos(th_f)[None, :, :], jnp.sin(th_f)[None, :, :]
    gr = tc * fc - ts * fs
    gi = -(ts * fc + tc * fs)
    return jnp.concatenate([jnp.concatenate([gr, -gi], axis=2), jnp.concatenate([gi, gr], axis=2)], axis=1).astype(BF16)


def _dft_stage1_kernel(m_ref, x_ref, a_ref):
    a_ref[...] = jnp.dot(m_ref[...], x_ref[...], preferred_element_type=F32).astype(BF16)


def _dft_stage2_kernel(g_ref, a_ref, cs_ref, o_ref):
    s2 = a_ref.shape[2]
    for kk in range(g_ref.shape[0]):
        a = jnp.concatenate([a_ref[0, kk], a_ref[1, kk]], axis=0)
        x = jnp.dot(g_ref[kk], a, preferred_element_type=F32)
        xr = x[:s2].astype(BF16)
        xi = x[s2:].astype(BF16)
        o_ref[:, kk * F_WIDTH:(kk + 1) * F_WIDTH] = (
            jnp.dot(xr, cs_ref[:F_WIDTH, :], preferred_element_type=F32)
            + jnp.dot(xi, cs_ref[F_WIDTH:, :], preferred_element_type=F32)).astype(BF16)


def _fourier_prompt(f, seq):
    s1 = DFT_STAGE1
    s2 = seq // s1
    n1 = jnp.arange(s1, dtype=I32)
    th1 = _angle_table(n1[:, None] * n1[None, :], s1)
    m1 = jnp.concatenate([jnp.cos(th1), -jnp.sin(th1)], axis=0).astype(BF16)
    g = _stage2_matrices(s1, s2)
    cs = _channel_dft(seq)

    cols = s2 * F_WIDTH
    fv = f[:seq].reshape(s1, cols)
    tn = min(cols, 4096)
    a = pl.pallas_call(
        _dft_stage1_kernel,
        grid=(cols // tn,),
        in_specs=[pl.BlockSpec((2 * s1, s1), lambda j: (0, 0)), pl.BlockSpec((s1, tn), lambda j: (0, j))],
        out_specs=pl.BlockSpec((2 * s1, tn), lambda j: (0, j)),
        out_shape=jax.ShapeDtypeStruct((2 * s1, cols), BF16),
        compiler_params=_cparams("parallel"),
        name="dft_stage1",
    )(m1, fv)
    a4 = a.reshape(2, s1, s2, F_WIDTH)
    kb = DFT_STAGE2_BATCH
    out = pl.pallas_call(
        _dft_stage2_kernel,
        grid=(s1 // kb,),
        in_specs=[pl.BlockSpec((kb, 2 * s2, 2 * s2), lambda k: (k, 0, 0)),
                  pl.BlockSpec((2, kb, s2, F_WIDTH), lambda k: (0, k, 0, 0)),
                  pl.BlockSpec((2 * F_WIDTH, F_WIDTH), lambda k: (0, 0))],
        out_specs=pl.BlockSpec((s2, kb * F_WIDTH), lambda k: (0, k)),
        out_shape=jax.ShapeDtypeStruct((s2, s1 * F_WIDTH), BF16),
        compiler_params=_cparams("parallel"),
        name="dft_stage2",
    )(g, a4, cs)
    return out.reshape(seq, F_WIDTH)


def _dft_small_kernel(g_ref, x_ref, cs_ref, o_ref, stage_ref):
    s1 = DFT_SMALL_STAGE1
    assert s1 == 8
    s2 = x_ref.shape[1] // s1
    root = math.sqrt(0.5)
    n_seq = x_ref.shape[0]
    a_all = []
    for b in range(n_seq):
        x0, x1, x2, x3, x4, x5, x6, x7 = [x_ref[b, n1 * s2:(n1 + 1) * s2, :].astype(F32) for n1 in range(s1)]
        u, v, p, q = x0 - x4, x2 - x6, x1 - x5, x3 - x7
        e_sum, e_dif = (x0 + x4) + (x2 + x6), (x0 + x4) - (x2 + x6)
        o_sum, o_dif = (x1 + x5) + (x3 + x7), (x1 + x5) - (x3 + x7)
        rpq_m, rpq_p = root * (p - q), root * (p + q)
        zero = jnp.zeros_like(x0)
        a_re = [e_sum + o_sum, u + rpq_m, e_dif, u - rpq_m, e_sum - o_sum, u - rpq_m, e_dif, u + rpq_m]
        im1, im3 = -v - rpq_p, v - rpq_p
        a_im = [zero, im1, -o_dif, im3, zero, -im3, o_dif, -im1]
        a_all.append([jnp.concatenate([a_re[k1].astype(BF16), a_im[k1].astype(BF16)], axis=0) for k1 in range(s1)])

    for k1 in range(s1):
        a = jnp.concatenate([a_all[b][k1] for b in range(n_seq)], axis=1)
        x = jnp.dot(g_ref[k1], a, preferred_element_type=F32)
        for b in range(n_seq):
            xb = x[:, b * F_WIDTH:(b + 1) * F_WIDTH]
            out = (jnp.dot(xb[:s2].astype(BF16), cs_ref[:F_WIDTH, :], preferred_element_type=F32)
                   + jnp.dot(xb[s2:].astype(BF16), cs_ref[F_WIDTH:, :], preferred_element_type=F32))
            for c in range(F_WIDTH // LANES):
                stage_ref[b * (F_WIDTH // LANES) + c, pl.ds(k1, s2, stride=s1), :] = out[:, c * LANES:(c + 1) * LANES]
    for b in range(n_seq):
        o_ref[b] = jnp.concatenate([stage_ref[b * (F_WIDTH // LANES) + c] for c in range(F_WIDTH // LANES)],
                                   axis=1).astype(BF16)


def _fourier_sample(f, n_prompt, batch, seq):
    s1 = DFT_SMALL_STAGE1
    s2 = seq // s1
    g = _stage2_matrices(s1, s2)
    cs = _channel_dft(seq)
    f3 = f.reshape(f.shape[0] // seq, seq, F_WIDTH)
    first = n_prompt // seq
    nb = DFT_SMALL_BATCH
    assert batch % nb == 0 and first % nb == 0
    return pl.pallas_call(
        _dft_small_kernel,
        grid=(batch // nb,),
        in_specs=[pl.BlockSpec((s1, 2 * s2, 2 * s2), lambda b: (0, 0, 0)),
                  pl.BlockSpec((nb, seq, F_WIDTH), lambda b: (first // nb + b, 0, 0)),
                  pl.BlockSpec((2 * F_WIDTH, F_WIDTH), lambda b: (0, 0))],
        out_specs=pl.BlockSpec((nb, seq, F_WIDTH), lambda b: (b, 0, 0)),
        out_shape=jax.ShapeDtypeStruct((batch, seq, F_WIDTH), BF16),
        scratch_shapes=[pltpu.VMEM((nb * (F_WIDTH // LANES), seq, LANES), F32)],
        compiler_params=_cparams("parallel"),
        name="dft_small",
    )(g, f3, cs).reshape(batch * seq, F_WIDTH)


def _pack_bf16_pairs(x):
    w = x.shape[1] // 2
    bits = lax.bitcast_convert_type(x.astype(BF16).astype(F32), U32)
    return (bits[:, :w] >> 16) | (bits[:, w:] & jnp.uint32(0xFFFF0000))


def _unpack_bf16_pairs(u):
    lo = lax.bitcast_convert_type(u << 16, F32)
    hi = lax.bitcast_convert_type(u & jnp.uint32(0xFFFF0000), F32)
    return jnp.concatenate([lo, hi], axis=1)


def _layer_norm(h, g, b):
    mu = jnp.mean(h, axis=-1, keepdims=True)
    d = h - mu
    var = jnp.mean(d * d, axis=-1, keepdims=True)
    return d * lax.rsqrt(var + LN_EPS) * g + b


def _from_dilated_view(stage_ref, blk_ref, dil):
    if dil == 1:
        return blk_ref[...].astype(F32)
    rows = blk_ref.shape[0]
    for r in range(dil):
        for c in range(GROUP_WIDTH // LANES):
            lo = r * GROUP_WIDTH + c * LANES
            stage_ref[c, pl.ds(r, rows, stride=dil), :] = blk_ref[:, lo:lo + LANES].astype(F32)
    return jnp.concatenate([stage_ref[c] for c in range(GROUP_WIDTH // LANES)], axis=1)


def _merge_kernel(xp_ref, xs_ref, wg_ref, o1_ref, o2_ref, o3_ref, l1_ref, l2_ref, l3_ref, fp_ref, fs_ref,
                  wao_ref, wfo_ref, wo_ref, g_ref, b_ref, wr2_ref, br_ref, tri_ref,
                  x1_ref, pk_ref, idx_ref, gate_ref, rank_ref, cnt_ref, run_ref, *stage_refs, n_prompt_tiles):
    i = pl.program_id(0)

    @pl.when(i == 0)
    def _():
        run_ref[...] = jnp.zeros_like(run_ref)

    is_prompt = i < n_prompt_tiles
    x = jnp.where(is_prompt, xp_ref[...], xs_ref[...])
    gpre = jnp.dot(x.astype(BF16), wg_ref[...], preferred_element_type=F32)
    o1, o2, o3 = [_from_dilated_view(stage_refs[gi], ref, d)
                  for gi, (ref, d) in enumerate(zip((o1_ref, o2_ref, o3_ref), DILATIONS))]
    l1, l2, l3 = [_from_dilated_view(stage_refs[N_GROUPS + gi], ref, d)
                  for gi, (ref, d) in enumerate(zip((l1_ref, l2_ref, l3_ref), DILATIONS))]
    m = jnp.maximum(jnp.maximum(l1, l2), l3)
    e1, e2, e3 = jnp.exp(l1 - m), jnp.exp(l2 - m), jnp.exp(l3 - m)
    attn = (e1 * o1 + e2 * o2 + e3 * o3) * (1.0 / (e1 + e2 + e3))
    a = jnp.dot(attn.astype(BF16), wao_ref[...], preferred_element_type=F32)
    four = jnp.where(is_prompt, fp_ref[...], fs_ref[...])
    ff = jnp.dot(four, wfo_ref[...], preferred_element_type=F32)
    merged = jax.nn.sigmoid(gpre[:, :D_MODEL]) * a + jax.nn.sigmoid(gpre[:, D_MODEL:]) * ff
    mix = jnp.dot(merged.astype(BF16), wo_ref[...], preferred_element_type=F32)
    x1 = _layer_norm(DEEPNORM_ALPHA * x + mix, g_ref[...], b_ref[...])
    x1_ref[...] = x1
    pk_ref[...] = _pack_bf16_pairs(x1)

    xh = x1.astype(BF16)
    xl = (x1 - xh.astype(F32)).astype(BF16)
    tm = x1.shape[0]
    cross = lax.dot_general(wr2_ref[...], jnp.concatenate([xh, xl], axis=0), (((1,), (1,)), ((), ())),
                            preferred_element_type=F32)
    logits = ((cross[:N_EXPERTS, :tm] + cross[N_EXPERTS:, :tm])
              + (cross[:N_EXPERTS, tm:] + cross[N_EXPERTS:, tm:])) + br_ref[...]
    eio = lax.broadcasted_iota(I32, (N_EXPERTS, tm), 0)
    work = logits
    vals, sels, picks = [], [], []
    for _ in range(TOP_K):
        mk = jnp.max(work, axis=0, keepdims=True)
        ik = jnp.min(jnp.where(work == mk, eio, N_EXPERTS), axis=0, keepdims=True)
        sel = eio == ik
        vals.append(mk)
        sels.append(sel)
        picks.append(ik)
        work = jnp.where(sel, -jnp.inf, work)
    es = [jnp.exp(v - vals[0]) for v in vals]
    inv_den = 1.0 / (es[0] + es[1] + es[2] + es[3])
    chosen = (sels[0] | sels[1] | sels[2] | sels[3])
    chosen_f = chosen.astype(F32)
    prefix = jnp.dot(chosen_f.astype(BF16), tri_ref[...], preferred_element_type=F32)
    before = prefix + run_ref[...]
    for kk in range(TOP_K):
        idx_ref[kk:kk + 1, :] = picks[kk]
        gate_ref[kk:kk + 1, :] = es[kk] * inv_den
        rank_ref[kk:kk + 1, :] = jnp.sum(jnp.where(sels[kk], before, 0.0), axis=0, keepdims=True).astype(I32)
    run_ref[...] = run_ref[...] + jnp.sum(chosen_f, axis=1, keepdims=True)
    cnt_ref[...] = jnp.broadcast_to(run_ref[...], cnt_ref.shape).astype(I32)


def _merge_route(xp, xs, w_gate, o_list, l_list, four_p, four_s, w_ao, w_fo, w_o, ln_g, ln_b, w_r, b_r, tile0,
                 n_tiles):
    n_p = xp.shape[0]
    tm = TOKEN_TILE
    n_t = n_tiles * tm
    npt = n_p // tm
    w_rt = w_r.T
    w_rh = w_rt.astype(BF16)
    w_rl = (w_rt - w_rh.astype(F32)).astype(BF16)
    tri = (jnp.arange(tm)[:, None] < jnp.arange(tm)[None, :]).astype(BF16)
    row = lambda w: pl.BlockSpec((tm, w), lambda i: (i, 0))
    prompt_row = lambda w: pl.BlockSpec((tm, w), lambda i: (jnp.clip(tile0 + i, 0, npt - 1), 0))
    sample_row = lambda w: pl.BlockSpec((tm, w), lambda i: (jnp.maximum(tile0 + i - npt, 0), 0))
    full = lambda r, c: pl.BlockSpec((r, c), lambda i: (0, 0))
    lane_row = pl.BlockSpec((TOP_K, tm), lambda i: (0, i))
    views = [pl.BlockSpec((tm // d, d * GROUP_WIDTH), lambda i: (tile0 + i, 0)) for d in DILATIONS]
    return pl.pallas_call(
        functools.partial(_merge_kernel, n_prompt_tiles=npt - tile0),
        grid=(n_tiles,),
        in_specs=[
            prompt_row(D_MODEL), sample_row(D_MODEL),
            full(D_MODEL, 2 * D_MODEL),
            *views, *views,
            prompt_row(F_WIDTH), sample_row(F_WIDTH),
            full(GROUP_WIDTH, D_MODEL), full(F_WIDTH, D_MODEL), full(D_MODEL, D_MODEL),
            full(1, D_MODEL), full(1, D_MODEL),
            full(2 * N_EXPERTS, D_MODEL), full(N_EXPERTS, 1),
            full(tm, tm),
        ],
        out_specs=[row(D_MODEL), row(D_MODEL // 2), lane_row, lane_row, lane_row, full(N_EXPERTS, LANES)],
        out_shape=[
            jax.ShapeDtypeStruct((n_t, D_MODEL), F32),
            jax.ShapeDtypeStruct((n_t, D_MODEL // 2), U32),
            jax.ShapeDtypeStruct((TOP_K, n_t), I32),
            jax.ShapeDtypeStruct((TOP_K, n_t), F32),
            jax.ShapeDtypeStruct((TOP_K, n_t), I32),
            jax.ShapeDtypeStruct((N_EXPERTS, LANES), I32),
        ],
        scratch_shapes=[pltpu.VMEM((N_EXPERTS, 1), F32)]
        + [pltpu.VMEM((GROUP_WIDTH // LANES, tm, LANES), F32)] * (2 * N_GROUPS),
        compiler_params=_cparams("arbitrary"),
        name="merge_ln_route",
    )(xp, xs, w_gate, *o_list, *l_list, four_p, four_s, w_ao, w_fo, w_o, ln_g.reshape(1, -1), ln_b.reshape(1, -1),
      jnp.concatenate([w_rh, w_rl], axis=0), b_r.reshape(-1, 1), tri)


def _sc_workers():
    info = plsc.get_sparse_core_info()
    return info.num_cores, info.num_cores * info.num_subcores


def _sc_dispatch(rows_src, pos_chunks, n_rows_out):
    n_t, w = rows_src.shape
    n_cores, n_workers = _sc_workers()
    assert n_workers * SC_CHUNK == SC_ROUND_ROWS and n_t % SC_ROUND_ROWS == 0
    chunks_per_worker = n_t // (n_workers * SC_CHUNK)
    mesh = plsc.VectorSubcoreMesh(core_axis_name="c", subcore_axis_name="s")

    @functools.partial(
        pl.kernel, mesh=mesh,
        out_type=jax.ShapeDtypeStruct((n_rows_out, w), rows_src.dtype),
        scratch_types=[pltpu.VMEM((TOP_K, SC_CHUNK), I32), pltpu.VMEM((SC_CHUNK, w), rows_src.dtype)],
    )
    def dispatch(src_hbm, pos_hbm, out_hbm, idx_v, rows_v):
        wid = lax.axis_index("s") * n_cores + lax.axis_index("c")

        @pl.loop(0, chunks_per_worker)
        def _(j):
            chunk = wid * chunks_per_worker + j
            pltpu.sync_copy(pos_hbm.at[chunk], idx_v)
            pltpu.sync_copy(src_hbm.at[pl.ds(chunk * SC_CHUNK, SC_CHUNK)], rows_v)
            for kk in range(TOP_K):
                pltpu.sync_copy(rows_v, out_hbm.at[idx_v.at[kk]])

    return dispatch(rows_src, pos_chunks)


def _sc_combine(rows_src, pos_chunks, n_t):
    w = rows_src.shape[1]
    n_cores, n_workers = _sc_workers()
    assert n_workers * SC_CHUNK == SC_ROUND_ROWS and n_t % SC_ROUND_ROWS == 0
    chunks_per_worker = n_t // (n_workers * SC_CHUNK)
    mesh = plsc.VectorSubcoreMesh(core_axis_name="c", subcore_axis_name="s")

    @functools.partial(
        pl.kernel, mesh=mesh,
        out_type=jax.ShapeDtypeStruct((TOP_K, n_t, w), rows_src.dtype),
        scratch_types=[pltpu.VMEM((TOP_K, SC_CHUNK), I32), pltpu.VMEM((SC_CHUNK, w), rows_src.dtype)],
    )
    def combine(src_hbm, pos_hbm, out_hbm, idx_v, rows_v):
        wid = lax.axis_index("s") * n_cores + lax.axis_index("c")

        @pl.loop(0, chunks_per_worker)
        def _(j):
            chunk = wid * chunks_per_worker + j
            pltpu.sync_copy(pos_hbm.at[chunk], idx_v)
            for kk in range(TOP_K):
                pltpu.sync_copy(src_hbm.at[idx_v.at[kk]], rows_v)
                pltpu.sync_copy(rows_v, out_hbm.at[kk, pl.ds(chunk * SC_CHUNK, SC_CHUNK)])

    return combine(rows_src, pos_chunks)


def _expert_kernel(be_ref, nv_ref, x_ref, *refs, prepare):
    if prepare:
        wu_ref, wdn_ref, p_ref, bg_ref, bl_ref, bd_ref, y_ref, wg_ref, wl_ref, wd_ref = refs
    else:
        wg_ref, wl_ref, wd_ref, bg_ref, bl_ref, bd_ref, y_ref = refs
    blk = pl.program_id(0)
    nv = nv_ref[blk]

    if prepare:
        new_expert = (blk == 0) | (be_ref[blk] != be_ref[jnp.maximum(blk - 1, 0)])

        @pl.when(new_expert)
        def _():
            half = SPLIT_CHUNK // 2
            for c in range(wu_ref.shape[2] // SPLIT_CHUNK):
                w = wu_ref[0, :, c * SPLIT_CHUNK:(c + 1) * SPLIT_CHUNK].astype(BF16)
                r = jnp.dot(w, p_ref[...], preferred_element_type=F32)
                wg_ref[0, :, c * half:(c + 1) * half] = r[:, :half].astype(BF16)
                wl_ref[0, :, c * half:(c + 1) * half] = r[:, half:].astype(BF16)
            wd_ref[...] = wdn_ref[...].astype(BF16)

    @pl.when(nv > 0)
    def _():
        x = _unpack_bf16_pairs(x_ref[...])
        rows = lax.broadcasted_iota(I32, (x.shape[0], 1), 0)
        x = jnp.where(rows < nv, x, 0.0).astype(BF16)
        acc = jnp.zeros((x.shape[0], D_MODEL), F32) + bd_ref[0]
        for c in range(D_FF // FF_CHUNK):
            sl = slice(c * FF_CHUNK, (c + 1) * FF_CHUNK)
            hg = jnp.dot(x, wg_ref[0, :, sl], preferred_element_type=F32) + bg_ref[0, :, sl]
            hl = jnp.dot(x, wl_ref[0, :, sl], preferred_element_type=F32) + bl_ref[0, :, sl]
            glu = jnp.minimum(hg, SWIGLU_LIMIT)
            lin = jnp.clip(hl, -SWIGLU_LIMIT, SWIGLU_LIMIT)
            act = glu * jax.nn.sigmoid(SWIGLU_ALPHA * glu) * (lin + 1.0)
            acc = acc + jnp.dot(act.astype(BF16), wd_ref[0, sl, :], preferred_element_type=F32)
        y_ref[...] = _pack_bf16_pairs(acc)

    @pl.when(nv == 0)
    def _():
        y_ref[...] = jnp.zeros_like(y_ref)


def _expert_ffn(xs_rows, block_e, n_valid, weights, b_glu, b_lin, b_down):
    n_rows = xs_rows.shape[0]
    bm = EXPERT_BLOCK
    prepare = len(weights) == 2
    wspec = lambda r, c: pl.BlockSpec((1, r, c), lambda b, be, nv: (be[b], 0, 0))
    rows_spec = pl.BlockSpec((bm, D_MODEL // 2), lambda b, be, nv: (b, 0))
    bias_specs = [wspec(1, D_FF), wspec(1, D_FF), wspec(1, D_MODEL)]
    y_shape = jax.ShapeDtypeStruct((n_rows, D_MODEL // 2), U32)
    if prepare:
        half = SPLIT_CHUNK // 2
        src = jnp.arange(SPLIT_CHUNK, dtype=I32)[:, None]
        dst = jnp.arange(SPLIT_CHUNK, dtype=I32)[None, :]
        sel = jnp.where(dst < half, src == 2 * dst, src == 2 * (dst - half) + 1).astype(BF16)
        operands = (*weights, sel)
        weight_specs = [wspec(D_MODEL, 2 * D_FF), wspec(D_FF, D_MODEL),
                        pl.BlockSpec((SPLIT_CHUNK, SPLIT_CHUNK), lambda b, be, nv: (0, 0))]
        prepared = jax.ShapeDtypeStruct((N_EXPERTS, D_MODEL, D_FF), BF16)
        out_specs = [rows_spec, wspec(D_MODEL, D_FF), wspec(D_MODEL, D_FF), wspec(D_FF, D_MODEL)]
        out_shape = [y_shape, prepared, prepared, prepared]
    else:
        operands = tuple(weights)
        weight_specs = [wspec(D_MODEL, D_FF), wspec(D_MODEL, D_FF), wspec(D_FF, D_MODEL)]
        out_specs, out_shape = rows_spec, y_shape
    grid_spec = pltpu.PrefetchScalarGridSpec(
        num_scalar_prefetch=2,
        grid=(n_rows // bm,),
        in_specs=[rows_spec, *weight_specs, *bias_specs],
        out_specs=out_specs,
    )
    out = pl.pallas_call(
        functools.partial(_expert_kernel, prepare=prepare),
        grid_spec=grid_spec,
        out_shape=out_shape,
        compiler_params=_cparams("arbitrary"),
        name="expert_ffn",
    )(block_e, n_valid, xs_rows, *operands, b_glu, b_lin, b_down)
    return (out[0], tuple(out[1:])) if prepare else (out, tuple(weights))


def _final_kernel(x1_ref, yg_ref, gt_ref, g_ref, b_ref, *rest):
    o_ref = rest[-1]
    gates = gt_ref[...]
    pad = jnp.zeros((LANES - TOP_K, gates.shape[1]), F32)
    gt = jnp.concatenate([gates, pad], axis=0).T
    ffn = jnp.zeros(x1_ref.shape, F32)
    for kk in range(TOP_K):
        ffn = ffn + gt[:, kk:kk + 1] * _unpack_bf16_pairs(yg_ref[kk])
    o_ref[...] = _layer_norm(DEEPNORM_ALPHA * x1_ref[...] + ffn, g_ref[...], b_ref[...])


def _final(x1, yg, gates, ln_g, ln_b, src_row0, n_rows, out_rows, dst_row0, out_prev=None):
    tm = FINAL_TILE
    assert src_row0 % tm == 0 and dst_row0 % tm == 0 and n_rows % tm == 0
    src, dst = src_row0 // tm, dst_row0 // tm
    in_specs = [
        pl.BlockSpec((tm, D_MODEL), lambda i: (src + i, 0)),
        pl.BlockSpec((TOP_K, tm, D_MODEL // 2), lambda i: (0, src + i, 0)),
        pl.BlockSpec((TOP_K, tm), lambda i: (0, src + i)),
        pl.BlockSpec((1, D_MODEL), lambda i: (0, 0)),
        pl.BlockSpec((1, D_MODEL), lambda i: (0, 0)),
    ]
    args = [x1, yg, gates, ln_g.reshape(1, -1), ln_b.reshape(1, -1)]
    aliases = {}
    if out_prev is not None:
        in_specs.append(pl.BlockSpec(memory_space=pl.ANY))
        args.append(out_prev)
        aliases = {len(args) - 1: 0}
    return pl.pallas_call(
        _final_kernel,
        grid=(n_rows // tm,),
        in_specs=in_specs,
        out_specs=pl.BlockSpec((tm, D_MODEL), lambda i: (dst + i, 0)),
        out_shape=jax.ShapeDtypeStruct((out_rows, D_MODEL), F32),
        input_output_aliases=aliases,
        compiler_params=_cparams("parallel"),
        name="combine_ln2",
    )(*args)


def _routing_tables(idx, rank, counts, n_blocks):
    bm = EXPERT_BLOCK
    padded = jnp.maximum((counts + bm - 1) // bm, 1) * bm
    pad_end = jnp.cumsum(padded)
    pad_start = pad_end - padded
    experts = jnp.arange(N_EXPERTS, dtype=I32)

    def lookup(table, e):
        shape = (N_EXPERTS,) + (1,) * e.ndim
        return jnp.sum(jnp.where(e[None] == experts.reshape(shape), table.reshape(shape), 0), axis=0)

    pos = lookup(pad_start, idx) + rank
    blk_row0 = jnp.arange(n_blocks, dtype=I32) * bm
    block_e = jnp.minimum(jnp.sum((pad_end[:, None] <= blk_row0[None, :]).astype(I32), axis=0), N_EXPERTS - 1)
    n_valid = jnp.clip(lookup(counts, block_e) - (blk_row0 - lookup(pad_start, block_e)), 0, bm)
    return pos.astype(I32), block_e.astype(I32), n_valid.astype(I32)


def _moe(pk, idx, rank, counts, weights, b_glu, b_lin, b_down):
    n_t = pk.shape[0]
    n_blocks = (n_t * TOP_K) // EXPERT_BLOCK + N_EXPERTS
    pos, block_e, n_valid = _routing_tables(idx, rank, counts, n_blocks)
    pos_chunks = pos.reshape(TOP_K, n_t // SC_CHUNK, SC_CHUNK).transpose(1, 0, 2)
    xs_rows = _sc_dispatch(pk, pos_chunks, n_blocks * EXPERT_BLOCK)
    y_rows, prepared = _expert_ffn(xs_rows, block_e, n_valid, weights, b_glu, b_lin, b_down)
    return _sc_combine(y_rows, pos_chunks, n_t), prepared


def _rope_tables(seq_max):
    inv = ROPE_THETA ** (-jnp.arange(0, HEAD_DIM, 2, dtype=F32) / HEAD_DIM)
    ang = jnp.arange(seq_max).astype(F32)[:, None] * inv[None, :]
    cos, sin = jnp.cos(ang), jnp.sin(ang)
    reps = LANES // HEAD_DIM
    cos_t = jnp.tile(jnp.concatenate([cos, cos], axis=1), (1, reps))
    sin_t = jnp.tile(jnp.concatenate([-sin, sin], axis=1), (1, reps))
    return cos_t, sin_t


def _encoder_layer(xp, xs, seq_prompt, batch_sample, seq_sample, w_in, w_attn_out, w_four_out, w_o, ln1_g, ln1_b,
                   w_router, b_router, w_up, b_up, w_down, b_down, ln2_g, ln2_b):
    n_p, n_s = xp.shape[0], xs.shape[0]
    assert n_p == seq_prompt and n_s == batch_sample * seq_sample
    assert seq_prompt % (DFT_STAGE1 * 16) == 0 and seq_sample % TOKEN_TILE == 0 and n_p % seq_sample == 0
    n_t = n_p + n_s
    assert n_t % (max(DILATIONS) * min(ATTN_TILES)) == 0 and seq_sample % (max(DILATIONS) * ATTN_SUB) == 0

    w_proj = w_in[:, :PROJ_WIDTH].astype(BF16)
    w_gate = w_in[:, PROJ_WIDTH:].astype(BF16)
    cos_t, sin_t = _rope_tables(max(seq_prompt, seq_sample))
    q_views, k_views, v_views, f = _inproj(xp, xs, w_proj, cos_t, sin_t, seq_sample)

    o_list, l_list = [], []
    for gi, dil in enumerate(DILATIONS):
        o, lse = _attention_group(q_views[gi], k_views[gi], v_views[gi], dil, n_p, seq_sample)
        o_list.append(o)
        l_list.append(lse)

    four_p = _fourier_prompt(f, seq_prompt)
    four_s = _fourier_sample(f, n_p, batch_sample, seq_sample)

    b_glu = b_up[:, None, 0::2]
    b_lin = b_up[:, None, 1::2]
    w_ao, w_fo, w_ob = w_attn_out.astype(BF16), w_four_out.astype(BF16), w_o.astype(BF16)

    tiles = n_t // TOKEN_TILE
    align = math.lcm(SC_ROUND_ROWS, FINAL_TILE) // TOKEN_TILE
    bounds = [0]
    for share in MOE_PART_SHARES[:-1]:
        bounds.append(min(tiles, bounds[-1] + max(align, round(tiles * share / align) * align)))
    bounds.append(tiles)
    n_parts = len(bounds) - 1
    routed = []
    for p in range(n_parts):
        routed.append(_merge_route(xp, xs, w_gate, o_list, l_list, four_p, four_s, w_ao, w_fo, w_ob, ln1_g, ln1_b,
                                   w_router, b_router, bounds[p], bounds[p + 1] - bounds[p]))
    combined = []
    weights = (w_up, w_down)
    for x1, pk, idx, gates, rank, cnt in routed:
        yg, weights = _moe(pk, idx, rank, cnt[:, 0], weights, b_glu, b_lin, b_down[:, None, :])
        combined.append(yg)

    outs = [None, None]
    spans = [(0, n_p), (n_p, n_t)]
    for p, ((x1, pk, idx, gates, rank, cnt), yg) in enumerate(zip(routed, combined)):
        lo, hi = bounds[p] * TOKEN_TILE, bounds[p + 1] * TOKEN_TILE
        for which, (a, b) in enumerate(spans):
            s, e = max(lo, a), min(hi, b)
            if s < e:
                outs[which] = _final(x1, yg, gates, ln2_g, ln2_b, s - lo, e - s, b - a, s - a, outs[which])
    return outs[0], outs[1]


def kernel(x_prompt, x_sample, w_in, w_attn_out, w_four_out, w_o, ln1_g, ln1_b, w_router, b_router, w_up, b_up,
           w_down, b_down, ln2_g, ln2_b):
    assert w_in.shape[0] == DEPTH
    bp, sp, d = x_prompt.shape
    bs, ss, _ = x_sample.shape
    assert bp == 1 and d == D_MODEL
    y_p, y_s = _encoder_layer(
        x_prompt.reshape(sp, d), x_sample.reshape(bs * ss, d), sp, bs, ss,
        w_in[0], w_attn_out[0], w_four_out[0], w_o[0], ln1_g[0], ln1_b[0], w_router[0], b_router[0],
        w_up[0], b_up[0], w_down[0], b_down[0], ln2_g[0], ln2_b[0])
    return y_p.reshape(x_prompt.shape), y_s.reshape(x_sample.shape)
```

```python
import functools
import math

import jax
import jax.numpy as jnp
from jax import lax
from jax.experimental import pallas as pl
from jax.experimental.pallas import tpu as pltpu
from jax.experimental.pallas import tpu_sc as plsc

F32 = jnp.float32
BF16 = jnp.bfloat16
I32 = jnp.int32
U32 = jnp.uint32

D_MODEL = 1024
HEAD_DIM = 64
HEADS_PER_GROUP = 4
GROUP_WIDTH = HEADS_PER_GROUP * HEAD_DIM
DILATIONS = (1, 4, 16)
HALF_WINDOW = 64
N_GROUPS = len(DILATIONS)
QKV_WIDTH = N_GROUPS * GROUP_WIDTH
F_GROUPS = 4
F_GROUP_DIM = 64
F_WIDTH = F_GROUPS * F_GROUP_DIM
PROJ_WIDTH = 3 * QKV_WIDTH + F_WIDTH
N_EXPERTS = 32
TOP_K = 4
D_FF = 1024
SWIGLU_LIMIT = 7.0
SWIGLU_ALPHA = 1.702
LN_EPS = 1e-5
ROPE_THETA = 10000.0
NEG_INF = -1e30
DEPTH = 1
DEEPNORM_ALPHA = (2 * DEPTH) ** 0.25

LANES = 128
MXU_TILE = 256
TOKEN_TILE = 512
INPROJ_TILE = 1024
FINAL_TILE = 1024
ATTN_TILES = (2048, 1024)
ATTN_SUB = 128
EXPERT_BLOCK = 512
FF_CHUNK = 1024
SPLIT_CHUNK = MXU_TILE
DFT_STAGE1 = 128
DFT_STAGE2_BATCH = 8
DFT_SMALL_STAGE1 = 8
DFT_SMALL_BATCH = 4
SC_CHUNK = 64
SC_ROUND_ROWS = 2 * 16 * SC_CHUNK
MOE_PART_SHARES = (0.7, 0.3)
VMEM_LIMIT = 56 * 1024 * 1024


def _cparams(*sem):
    return pltpu.CompilerParams(dimension_semantics=sem, vmem_limit_bytes=VMEM_LIMIT)


def _to_dilated_view(stage_ref, out_ref, val, dil):
    if dil == 1:
        out_ref[...] = val.astype(BF16)
        return
    rows = val.shape[0] // dil
    for c in range(GROUP_WIDTH // LANES):
        stage_ref[c] = val[:, c * LANES:(c + 1) * LANES]
    for r in range(dil):
        for c in range(GROUP_WIDTH // LANES):
            lo = r * GROUP_WIDTH + c * LANES
            out_ref[:, lo:lo + LANES] = stage_ref[c, pl.ds(r, rows, stride=dil), :].astype(BF16)


def _inproj_kernel(xp_ref, xs_ref, w_ref, cos_ref, sin_ref, *refs, n_prompt_tiles):
    qkv_refs = refs[:3 * N_GROUPS]
    f_ref = refs[3 * N_GROUPS]
    stage_refs = refs[3 * N_GROUPS + 1:]
    i = pl.program_id(0)
    x = jnp.where(i < n_prompt_tiles, xp_ref[...], xs_ref[...]).astype(BF16)
    proj = jnp.dot(x, w_ref[...], preferred_element_type=F32)
    cos = cos_ref[...]
    sin = sin_ref[...]
    lane = lax.broadcasted_iota(I32, cos.shape, 1)
    first_half = (lane % HEAD_DIM) < (HEAD_DIM // 2)

    def rope(t):
        parts = []
        for c in range(GROUP_WIDTH // LANES):
            tc = t[:, c * LANES:(c + 1) * LANES]
            swapped = jnp.where(first_half, pltpu.roll(tc, LANES - HEAD_DIM // 2, 1), pltpu.roll(tc, HEAD_DIM // 2, 1))
            parts.append(tc * cos + swapped * sin)
        return jnp.concatenate(parts, axis=1)

    for gi, dil in enumerate(DILATIONS):
        sl = slice(gi * GROUP_WIDTH, (gi + 1) * GROUP_WIDTH)
        qg = rope(proj[:, sl]) * (HEAD_DIM ** -0.5)
        kg = rope(proj[:, QKV_WIDTH + gi * GROUP_WIDTH:QKV_WIDTH + (gi + 1) * GROUP_WIDTH])
        vg = proj[:, 2 * QKV_WIDTH + gi * GROUP_WIDTH:2 * QKV_WIDTH + (gi + 1) * GROUP_WIDTH]
        for which, val in enumerate((qg, kg, vg)):
            _to_dilated_view(stage_refs[which * N_GROUPS + gi], qkv_refs[which * N_GROUPS + gi], val, dil)
    f_ref[...] = proj[:, 3 * QKV_WIDTH:PROJ_WIDTH].astype(BF16)


def _inproj(xp, xs, w_proj, cos_t, sin_t, seq_sample):
    n_p, n_s = xp.shape[0], xs.shape[0]
    n_t = n_p + n_s
    tm = INPROJ_TILE
    assert n_p % tm == 0 and seq_sample % tm == 0
    npt = n_p // tm
    tiles_per_sample_seq = seq_sample // tm

    def table_idx(i):
        return (jnp.where(i < npt, i, (i - npt) % tiles_per_sample_seq), 0)

    view_specs = [pl.BlockSpec((tm // d, d * GROUP_WIDTH), lambda i: (i, 0)) for d in DILATIONS] * 3
    view_shapes = [jax.ShapeDtypeStruct((n_t // d, d * GROUP_WIDTH), BF16) for d in DILATIONS] * 3
    outs = pl.pallas_call(
        functools.partial(_inproj_kernel, n_prompt_tiles=npt),
        grid=(n_t // tm,),
        in_specs=[
            pl.BlockSpec((tm, D_MODEL), lambda i: (jnp.minimum(i, npt - 1), 0)),
            pl.BlockSpec((tm, D_MODEL), lambda i: (jnp.maximum(i - npt, 0), 0)),
            pl.BlockSpec((D_MODEL, PROJ_WIDTH), lambda i: (0, 0)),
            pl.BlockSpec((tm, LANES), table_idx),
            pl.BlockSpec((tm, LANES), table_idx),
        ],
        out_specs=view_specs + [pl.BlockSpec((tm, F_WIDTH), lambda i: (i, 0))],
        out_shape=view_shapes + [jax.ShapeDtypeStruct((n_t, F_WIDTH), BF16)],
        scratch_shapes=[pltpu.VMEM((GROUP_WIDTH // LANES, tm, LANES), F32)] * (3 * N_GROUPS),
        compiler_params=_cparams("parallel"),
        name="inproj_rope",
    )(xp, xs, w_proj, cos_t, sin_t)
    q_views, k_views, v_views = outs[0:N_GROUPS], outs[N_GROUPS:2 * N_GROUPS], outs[2 * N_GROUPS:3 * N_GROUPS]
    return q_views, k_views, v_views, outs[3 * N_GROUPS]


def _attn_kernel(q_ref, kl_ref, km_ref, kr_ref, vl_ref, vm_ref, vr_ref, o_ref, lse_ref, *, prompt_rows, sample_rows):
    i = pl.program_id(1)
    tq = q_ref.shape[0]
    head_of_lane = lax.broadcasted_iota(I32, (1, GROUP_WIDTH), 1) // HEAD_DIM

    def sub_block(j, kj, vj, valid):
        qj = q_ref[j * ATTN_SUB:(j + 1) * ATTN_SUB, :]
        q4 = jnp.concatenate([jnp.where(head_of_lane == h, qj, jnp.zeros_like(qj)) for h in range(HEADS_PER_GROUP)],
                             axis=0)
        s4 = lax.dot_general(q4, kj, (((1,), (1,)), ((), ())), preferred_element_type=F32)
        ps, scales, lses = [], [], []
        for h in range(HEADS_PER_GROUP):
            s = jnp.where(valid, s4[h * ATTN_SUB:(h + 1) * ATTN_SUB], NEG_INF)
            m = jnp.max(s, axis=1, keepdims=True)
            p = jnp.exp(s - m)
            l = jnp.sum(p, axis=1, keepdims=True)
            ps.append(p.astype(BF16))
            scales.append(1.0 / l)
            lses.append(m + jnp.log(l))
        o4 = jnp.dot(jnp.concatenate(ps, axis=0), vj, preferred_element_type=F32)
        acc = jnp.zeros((ATTN_SUB, GROUP_WIDTH), F32)
        lse_full = jnp.zeros((ATTN_SUB, GROUP_WIDTH), F32)
        for h in range(HEADS_PER_GROUP):
            mine = head_of_lane == h
            acc = jnp.where(mine, o4[h * ATTN_SUB:(h + 1) * ATTN_SUB] * scales[h], acc)
            lse_full = jnp.where(mine, lses[h], lse_full)
        o_ref[j * ATTN_SUB:(j + 1) * ATTN_SUB, :] = acc.astype(BF16)
        lse_ref[j * ATTN_SUB:(j + 1) * ATTN_SUB, :] = lse_full

    def banded_step():
        kext = jnp.concatenate([kl_ref[...], km_ref[...], kr_ref[...]], axis=0)
        vext = jnp.concatenate([vl_ref[...], vm_ref[...], vr_ref[...]], axis=0)
        kw = ATTN_SUB + 2 * HALF_WINDOW
        a = lax.broadcasted_iota(I32, (ATTN_SUB, kw), 0)
        c = lax.broadcasted_iota(I32, (ATTN_SUB, kw), 1)
        in_band = jnp.abs(c - HALF_WINDOW - a) <= HALF_WINDOW
        for j in range(tq // ATTN_SUB):
            r0 = i * tq + j * ATTN_SUB
            in_prompt = r0 < prompt_rows
            b = jnp.maximum(r0 - prompt_rows, 0) // sample_rows
            lo = jnp.where(in_prompt, 0, prompt_rows + b * sample_rows)
            hi = jnp.where(in_prompt, prompt_rows, prompt_rows + (b + 1) * sample_rows)
            key_row = r0 - HALF_WINDOW + c
            valid = in_band & (key_row >= lo) & (key_row < hi)
            sub_block(j, kext[j * ATTN_SUB:j * ATTN_SUB + kw, :], vext[j * ATTN_SUB:j * ATTN_SUB + kw, :], valid)

    def single_sequence_step():
        a = lax.broadcasted_iota(I32, (ATTN_SUB, ATTN_SUB), 0)
        c = lax.broadcasted_iota(I32, (ATTN_SUB, ATTN_SUB), 1)
        valid = jnp.abs(c - a) <= HALF_WINDOW
        for j in range(tq // ATTN_SUB):
            rows = slice(j * ATTN_SUB, (j + 1) * ATTN_SUB)
            sub_block(j, km_ref[rows, :], vm_ref[rows, :], valid)

    if sample_rows == ATTN_SUB and prompt_rows % tq == 0:
        in_sample = i * tq >= prompt_rows
        pl.when(in_sample)(single_sequence_step)
        pl.when(jnp.logical_not(in_sample))(banded_step)
    else:
        banded_step()


def _attention_group(qv, kv, vv, dil, n_prompt, seq_sample):
    rows = qv.shape[0]
    tq = next(t for t in ATTN_TILES if rows % t == 0)
    halo_per_tile = tq // HALF_WINDOW
    n_halo_blocks = rows // HALF_WINDOW
    main = pl.BlockSpec((tq, GROUP_WIDTH), lambda r, i: (i, r))
    left = pl.BlockSpec((HALF_WINDOW, GROUP_WIDTH), lambda r, i: (jnp.maximum(i * halo_per_tile - 1, 0), r))
    right = pl.BlockSpec((HALF_WINDOW, GROUP_WIDTH),
                         lambda r, i: (jnp.minimum((i + 1) * halo_per_tile, n_halo_blocks - 1), r))
    return pl.pallas_call(
        functools.partial(_attn_kernel, prompt_rows=n_prompt // dil, sample_rows=seq_sample // dil),
        grid=(dil, rows // tq),
        in_specs=[main, left, main, right, left, main, right],
        out_specs=[main, main],
        out_shape=[jax.ShapeDtypeStruct((rows, dil * GROUP_WIDTH), BF16),
                   jax.ShapeDtypeStruct((rows, dil * GROUP_WIDTH), F32)],
        compiler_params=_cparams("parallel", "parallel"),
        name=f"dilated_attention_{dil}",
    )(qv, kv, kv, kv, vv, vv, vv)


def _angle_table(idx, period):
    return (idx % period).astype(F32) * (2.0 * math.pi / period)


def _channel_dft(seq_len):
    c = jnp.arange(F_GROUP_DIM, dtype=I32)
    th = _angle_table(c[:, None] * c[None, :], F_GROUP_DIM)
    scale = (seq_len * F_GROUP_DIM) ** -0.5
    eye = jnp.eye(F_GROUPS, dtype=F32)
    cb = jnp.kron(eye, jnp.cos(th)) * scale
    sb = jnp.kron(eye, jnp.sin(th)) * scale
    return jnp.concatenate([cb, sb], axis=0).astype(BF16)


def _stage2_matrices(s1, s2):
    n2 = jnp.arange(s2, dtype=I32)[None, :]
    th_t = _angle_table(jnp.arange(s1, dtype=I32)[:, None] * n2, s1 * s2)
    th_f = _angle_table(jnp.arange(s2, dtype=I32)[:, None] * n2, s2)
    tc, ts = jnp.cos(th_t)[:, None, :], jnp.sin(th_t)[:, None, :]
    fc, fs = jnp.cos(th_f)[None, :, :], jnp.sin(th_f)[None, :, :]
    gr = tc * fc - ts * fs
    gi = -(ts * fc + tc * fs)
    return jnp.concatenate([jnp.concatenate([gr, -gi], axis=2), jnp.concatenate([gi, gr], axis=2)], axis=1).astype(BF16)


def _dft_stage1_kernel(m_ref, x_ref, a_ref):
    a_ref[...] = jnp.dot(m_ref[...], x_ref[...], preferred_element_type=F32).astype(BF16)


def _dft_stage2_kernel(g_ref, a_ref, cs_ref, o_ref):
    s2 = a_ref.shape[2]
    for kk in range(g_ref.shape[0]):
        a = jnp.concatenate([a_ref[0, kk], a_ref[1, kk]], axis=0)
        x = jnp.dot(g_ref[kk], a, preferred_element_type=F32)
        xr = x[:s2].astype(BF16)
        xi = x[s2:].astype(BF16)
        o_ref[:, kk * F_WIDTH:(kk + 1) * F_WIDTH] = (
            jnp.dot(xr, cs_ref[:F_WIDTH, :], preferred_element_type=F32)
            + jnp.dot(xi, cs_ref[F_WIDTH:, :], preferred_element_type=F32)).astype(BF16)


def _fourier_prompt(f, seq):
    s1 = DFT_STAGE1
    s2 = seq // s1
    n1 = jnp.arange(s1, dtype=I32)
    th1 = _angle_table(n1[:, None] * n1[None, :], s1)
    m1 = jnp.concatenate([jnp.cos(th1), -jnp.sin(th1)], axis=0).astype(BF16)
    g = _stage2_matrices(s1, s2)
    cs = _channel_dft(seq)

    cols = s2 * F_WIDTH
    fv = f[:seq].reshape(s1, cols)
    tn = min(cols, 4096)
    a = pl.pallas_call(
        _dft_stage1_kernel,
        grid=(cols // tn,),
        in_specs=[pl.BlockSpec((2 * s1, s1), lambda j: (0, 0)), pl.BlockSpec((s1, tn), lambda j: (0, j))],
        out_specs=pl.BlockSpec((2 * s1, tn), lambda j: (0, j)),
        out_shape=jax.ShapeDtypeStruct((2 * s1, cols), BF16),
        compiler_params=_cparams("parallel"),
        name="dft_stage1",
    )(m1, fv)
    a4 = a.reshape(2, s1, s2, F_WIDTH)
    kb = DFT_STAGE2_BATCH
    out = pl.pallas_call(
        _dft_stage2_kernel,
        grid=(s1 // kb,),
        in_specs=[pl.BlockSpec((kb, 2 * s2, 2 * s2), lambda k: (k, 0, 0)),
                  pl.BlockSpec((2, kb, s2, F_WIDTH), lambda k: (0, k, 0, 0)),
                  pl.BlockSpec((2 * F_WIDTH, F_WIDTH), lambda k: (0, 0))],
        out_specs=pl.BlockSpec((s2, kb * F_WIDTH), lambda k: (0, k)),
        out_shape=jax.ShapeDtypeStruct((s2, s1 * F_WIDTH), BF16),
        compiler_params=_cparams("parallel"),
        name="dft_stage2",
    )(g, a4, cs)
    return out.reshape(seq, F_WIDTH)


def _dft_small_kernel(g_ref, x_ref, cs_ref, o_ref, stage_ref):
    s1 = DFT_SMALL_STAGE1
    assert s1 == 8
    s2 = x_ref.shape[1] // s1
    root = math.sqrt(0.5)
    n_seq = x_ref.shape[0]
    a_all = []
    for b in range(n_seq):
        x0, x1, x2, x3, x4, x5, x6, x7 = [x_ref[b, n1 * s2:(n1 + 1) * s2, :].astype(F32) for n1 in range(s1)]
        u, v, p, q = x0 - x4, x2 - x6, x1 - x5, x3 - x7
        e_sum, e_dif = (x0 + x4) + (x2 + x6), (x0 + x4) - (x2 + x6)
        o_sum, o_dif = (x1 + x5) + (x3 + x7), (x1 + x5) - (x3 + x7)
        rpq_m, rpq_p = root * (p - q), root * (p + q)
        zero = jnp.zeros_like(x0)
        a_re = [e_sum + o_sum, u + rpq_m, e_dif, u - rpq_m, e_sum - o_sum, u - rpq_m, e_dif, u + rpq_m]
        im1, im3 = -v - rpq_p, v - rpq_p
        a_im = [zero, im1, -o_dif, im3, zero, -im3, o_dif, -im1]
        a_all.append([jnp.concatenate([a_re[k1].astype(BF16), a_im[k1].astype(BF16)], axis=0) for k1 in range(s1)])

    for k1 in range(s1):
        a = jnp.concatenate([a_all[b][k1] for b in range(n_seq)], axis=1)
        x = jnp.dot(g_ref[k1], a, preferred_element_type=F32)
        for b in range(n_seq):
            xb = x[:, b * F_WIDTH:(b + 1) * F_WIDTH]
            out = (jnp.dot(xb[:s2].astype(BF16), cs_ref[:F_WIDTH, :], preferred_element_type=F32)
                   + jnp.dot(xb[s2:].astype(BF16), cs_ref[F_WIDTH:, :], preferred_element_type=F32))
            for c in range(F_WIDTH // LANES):
                stage_ref[b * (F_WIDTH // LANES) + c, pl.ds(k1, s2, stride=s1), :] = out[:, c * LANES:(c + 1) * LANES]
    for b in range(n_seq):
        o_ref[b] = jnp.concatenate([stage_ref[b * (F_WIDTH // LANES) + c] for c in range(F_WIDTH // LANES)],
                                   axis=1).astype(BF16)


def _fourier_sample(f, n_prompt, batch, seq):
    s1 = DFT_SMALL_STAGE1
    s2 = seq // s1
    g = _stage2_matrices(s1, s2)
    cs = _channel_dft(seq)
    f3 = f.reshape(f.shape[0] // seq, seq, F_WIDTH)
    first = n_prompt // seq
    nb = DFT_SMALL_BATCH
    assert batch % nb == 0 and first % nb == 0
    return pl.pallas_call(
        _dft_small_kernel,
        grid=(batch // nb,),
        in_specs=[pl.BlockSpec((s1, 2 * s2, 2 * s2), lambda b: (0, 0, 0)),
                  pl.BlockSpec((nb, seq, F_WIDTH), lambda b: (first // nb + b, 0, 0)),
                  pl.BlockSpec((2 * F_WIDTH, F_WIDTH), lambda b: (0, 0))],
        out_specs=pl.BlockSpec((nb, seq, F_WIDTH), lambda b: (b, 0, 0)),
        out_shape=jax.ShapeDtypeStruct((batch, seq, F_WIDTH), BF16),
        scratch_shapes=[pltpu.VMEM((nb * (F_WIDTH // LANES), seq, LANES), F32)],
        compiler_params=_cparams("parallel"),
        name="dft_small",
    )(g, f3, cs).reshape(batch * seq, F_WIDTH)


def _pack_bf16_pairs(x):
    w = x.shape[1] // 2
    bits = lax.bitcast_convert_type(x.astype(BF16).astype(F32), U32)
    return (bits[:, :w] >> 16) | (bits[:, w:] & jnp.uint32(0xFFFF0000))


def _unpack_bf16_pairs(u):
    lo = lax.bitcast_convert_type(u << 16, F32)
    hi = lax.bitcast_convert_type(u & jnp.uint32(0xFFFF0000), F32)
    return jnp.concatenate([lo, hi], axis=1)


def _layer_norm(h, g, b):
    mu = jnp.mean(h, axis=-1, keepdims=True)
    d = h - mu
    var = jnp.mean(d * d, axis=-1, keepdims=True)
    return d * lax.rsqrt(var + LN_EPS) * g + b


def _from_dilated_view(stage_ref, blk_ref, dil):
    if dil == 1:
        return blk_ref[...].astype(F32)
    rows = blk_ref.shape[0]
    for r in range(dil):
        for c in range(GROUP_WIDTH // LANES):
            lo = r * GROUP_WIDTH + c * LANES
            stage_ref[c, pl.ds(r, rows, stride=dil), :] = blk_ref[:, lo:lo + LANES].astype(F32)
    return jnp.concatenate([stage_ref[c] for c in range(GROUP_WIDTH // LANES)], axis=1)


def _merge_kernel(xp_ref, xs_ref, wg_ref, o1_ref, o2_ref, o3_ref, l1_ref, l2_ref, l3_ref, fp_ref, fs_ref,
                  wao_ref, wfo_ref, wo_ref, g_ref, b_ref, wr2_ref, br_ref, tri_ref,
                  x1_ref, pk_ref, idx_ref, gate_ref, rank_ref, cnt_ref, run_ref, *stage_refs, n_prompt_tiles):
    i = pl.program_id(0)

    @pl.when(i == 0)
    def _():
        run_ref[...] = jnp.zeros_like(run_ref)

    is_prompt = i < n_prompt_tiles
    x = jnp.where(is_prompt, xp_ref[...], xs_ref[...])
    gpre = jnp.dot(x.astype(BF16), wg_ref[...], preferred_element_type=F32)
    o1, o2, o3 = [_from_dilated_view(stage_refs[gi], ref, d)
                  for gi, (ref, d) in enumerate(zip((o1_ref, o2_ref, o3_ref), DILATIONS))]
    l1, l2, l3 = [_from_dilated_view(stage_refs[N_GROUPS + gi], ref, d)
                  for gi, (ref, d) in enumerate(zip((l1_ref, l2_ref, l3_ref), DILATIONS))]
    m = jnp.maximum(jnp.maximum(l1, l2), l3)
    e1, e2, e3 = jnp.exp(l1 - m), jnp.exp(l2 - m), jnp.exp(l3 - m)
    attn = (e1 * o1 + e2 * o2 + e3 * o3) * (1.0 / (e1 + e2 + e3))
    a = jnp.dot(attn.astype(BF16), wao_ref[...], preferred_element_type=F32)
    four = jnp.where(is_prompt, fp_ref[...], fs_ref[...])
    ff = jnp.dot(four, wfo_ref[...], preferred_element_type=F32)
    merged = jax.nn.sigmoid(gpre[:, :D_MODEL]) * a + jax.nn.sigmoid(gpre[:, D_MODEL:]) * ff
    mix = jnp.dot(merged.astype(BF16), wo_ref[...], preferred_element_type=F32)
    x1 = _layer_norm(DEEPNORM_ALPHA * x + mix, g_ref[...], b_ref[...])
    x1_ref[...] = x1
    pk_ref[...] = _pack_bf16_pairs(x1)

    xh = x1.astype(BF16)
    xl = (x1 - xh.astype(F32)).astype(BF16)
    tm = x1.shape[0]
    cross = lax.dot_general(wr2_ref[...], jnp.concatenate([xh, xl], axis=0), (((1,), (1,)), ((), ())),
                            preferred_element_type=F32)
    logits = ((cross[:N_EXPERTS, :tm] + cross[N_EXPERTS:, :tm])
              + (cross[:N_EXPERTS, tm:] + cross[N_EXPERTS:, tm:])) + br_ref[...]
    eio = lax.broadcasted_iota(I32, (N_EXPERTS, tm), 0)
    work = logits
    vals, sels, picks = [], [], []
    for _ in range(TOP_K):
        mk = jnp.max(work, axis=0, keepdims=True)
        ik = jnp.min(jnp.where(work == mk, eio, N_EXPERTS), axis=0, keepdims=True)
        sel = eio == ik
        vals.append(mk)
        sels.append(sel)
        picks.append(ik)
        work = jnp.where(sel, -jnp.inf, work)
    es = [jnp.exp(v - vals[0]) for v in vals]
    inv_den = 1.0 / (es[0] + es[1] + es[2] + es[3])
    chosen = (sels[0] | sels[1] | sels[2] | sels[3])
    chosen_f = chosen.astype(F32)
    prefix = jnp.dot(chosen_f.astype(BF16), tri_ref[...], preferred_element_type=F32)
    before = prefix + run_ref[...]
    for kk in range(TOP_K):
        idx_ref[kk:kk + 1, :] = picks[kk]
        gate_ref[kk:kk + 1, :] = es[kk] * inv_den
        rank_ref[kk:kk + 1, :] = jnp.sum(jnp.where(sels[kk], before, 0.0), axis=0, keepdims=True).astype(I32)
    run_ref[...] = run_ref[...] + jnp.sum(chosen_f, axis=1, keepdims=True)
    cnt_ref[...] = jnp.broadcast_to(run_ref[...], cnt_ref.shape).astype(I32)


def _merge_route(xp, xs, w_gate, o_list, l_list, four_p, four_s, w_ao, w_fo, w_o, ln_g, ln_b, w_r, b_r, tile0,
                 n_tiles):
    n_p = xp.shape[0]
    tm = TOKEN_TILE
    n_t = n_tiles * tm
    npt = n_p // tm
    w_rt = w_r.T
    w_rh = w_rt.astype(BF16)
    w_rl = (w_rt - w_rh.astype(F32)).astype(BF16)
    tri = (jnp.arange(tm)[:, None] < jnp.arange(tm)[None, :]).astype(BF16)
    row = lambda w: pl.BlockSpec((tm, w), lambda i: (i, 0))
    prompt_row = lambda w: pl.BlockSpec((tm, w), lambda i: (jnp.clip(tile0 + i, 0, npt - 1), 0))
    sample_row = lambda w: pl.BlockSpec((tm, w), lambda i: (jnp.maximum(tile0 + i - npt, 0), 0))
    full = lambda r, c: pl.BlockSpec((r, c), lambda i: (0, 0))
    lane_row = pl.BlockSpec((TOP_K, tm), lambda i: (0, i))
    views = [pl.BlockSpec((tm // d, d * GROUP_WIDTH), lambda i: (tile0 + i, 0)) for d in DILATIONS]
    return pl.pallas_call(
        functools.partial(_merge_kernel, n_prompt_tiles=npt - tile0),
        grid=(n_tiles,),
        in_specs=[
            prompt_row(D_MODEL), sample_row(D_MODEL),
            full(D_MODEL, 2 * D_MODEL),
            *views, *views,
            prompt_row(F_WIDTH), sample_row(F_WIDTH),
            full(GROUP_WIDTH, D_MODEL), full(F_WIDTH, D_MODEL), full(D_MODEL, D_MODEL),
            full(1, D_MODEL), full(1, D_MODEL),
            full(2 * N_EXPERTS, D_MODEL), full(N_EXPERTS, 1),
            full(tm, tm),
        ],
        out_specs=[row(D_MODEL), row(D_MODEL // 2), lane_row, lane_row, lane_row, full(N_EXPERTS, LANES)],
        out_shape=[
            jax.ShapeDtypeStruct((n_t, D_MODEL), F32),
            jax.ShapeDtypeStruct((n_t, D_MODEL // 2), U32),
            jax.ShapeDtypeStruct((TOP_K, n_t), I32),
            jax.ShapeDtypeStruct((TOP_K, n_t), F32),
            jax.ShapeDtypeStruct((TOP_K, n_t), I32),
            jax.ShapeDtypeStruct((N_EXPERTS, LANES), I32),
        ],
        scratch_shapes=[pltpu.VMEM((N_EXPERTS, 1), F32)]
        + [pltpu.VMEM((GROUP_WIDTH // LANES, tm, LANES), F32)] * (2 * N_GROUPS),
        compiler_params=_cparams("arbitrary"),
        name="merge_ln_route",
    )(xp, xs, w_gate, *o_list, *l_list, four_p, four_s, w_ao, w_fo, w_o, ln_g.reshape(1, -1), ln_b.reshape(1, -1),
      jnp.concatenate([w_rh, w_rl], axis=0), b_r.reshape(-1, 1), tri)


def _sc_workers():
    info = plsc.get_sparse_core_info()
    return info.num_cores, info.num_cores * info.num_subcores


def _sc_dispatch(rows_src, pos_chunks, n_rows_out):
    n_t, w = rows_src.shape
    n_cores, n_workers = _sc_workers()
    assert n_workers * SC_CHUNK == SC_ROUND_ROWS and n_t % SC_ROUND_ROWS == 0
    chunks_per_worker = n_t // (n_workers * SC_CHUNK)
    mesh = plsc.VectorSubcoreMesh(core_axis_name="c", subcore_axis_name="s")

    @functools.partial(
        pl.kernel, mesh=mesh,
        out_type=jax.ShapeDtypeStruct((n_rows_out, w), rows_src.dtype),
        scratch_types=[pltpu.VMEM((TOP_K, SC_CHUNK), I32), pltpu.VMEM((SC_CHUNK, w), rows_src.dtype)],
    )
    def dispatch(src_hbm, pos_hbm, out_hbm, idx_v, rows_v):
        wid = lax.axis_index("s") * n_cores + lax.axis_index("c")

        @pl.loop(0, chunks_per_worker)
        def _(j):
            chunk = wid * chunks_per_worker + j
            pltpu.sync_copy(pos_hbm.at[chunk], idx_v)
            pltpu.sync_copy(src_hbm.at[pl.ds(chunk * SC_CHUNK, SC_CHUNK)], rows_v)
            for kk in range(TOP_K):
                pltpu.sync_copy(rows_v, out_hbm.at[idx_v.at[kk]])

    return dispatch(rows_src, pos_chunks)


def _sc_combine(rows_src, pos_chunks, n_t):
    w = rows_src.shape[1]
    n_cores, n_workers = _sc_workers()
    assert n_workers * SC_CHUNK == SC_ROUND_ROWS and n_t % SC_ROUND_ROWS == 0
    chunks_per_worker = n_t // (n_workers * SC_CHUNK)
    mesh = plsc.VectorSubcoreMesh(core_axis_name="c", subcore_axis_name="s")

    @functools.partial(
        pl.kernel, mesh=mesh,
        out_type=jax.ShapeDtypeStruct((TOP_K, n_t, w), rows_src.dtype),
        scratch_types=[pltpu.VMEM((TOP_K, SC_CHUNK), I32), pltpu.VMEM((SC_CHUNK, w), rows_src.dtype)],
    )
    def combine(src_hbm, pos_hbm, out_hbm, idx_v, rows_v):
        wid = lax.axis_index("s") * n_cores + lax.axis_index("c")

        @pl.loop(0, chunks_per_worker)
        def _(j):
            chunk = wid * chunks_per_worker + j
            pltpu.sync_copy(pos_hbm.at[chunk], idx_v)
            for kk in range(TOP_K):
                pltpu.sync_copy(src_hbm.at[idx_v.at[kk]], rows_v)
                pltpu.sync_copy(rows_v, out_hbm.at[kk, pl.ds(chunk * SC_CHUNK, SC_CHUNK)])

    return combine(rows_src, pos_chunks)


def _expert_kernel(be_ref, nv_ref, x_ref, *refs, prepare):
    if prepare:
        (wu_hbm, wdn_hbm, p_ref, bg_ref, bl_ref, bd_ref, y_ref, wg_ref, wl_ref, wd_ref,
         wu_buf, wdn_buf, sem) = refs
    else:
        wg_ref, wl_ref, wd_ref, bg_ref, bl_ref, bd_ref, y_ref = refs
    blk = pl.program_id(0)
    nv = nv_ref[blk]

    if prepare:
        expert = be_ref[blk]
        new_expert = (blk == 0) | (expert != be_ref[jnp.maximum(blk - 1, 0)])
        slot = expert % 2

        def fetch(e, s):
            return (pltpu.make_async_copy(wu_hbm.at[e], wu_buf.at[s], sem.at[0, s]),
                    pltpu.make_async_copy(wdn_hbm.at[e], wdn_buf.at[s], sem.at[1, s]))

        @pl.when(new_expert)
        def _():
            @pl.when(blk == 0)
            def _():
                for copy in fetch(expert, slot):
                    copy.start()

            for copy in fetch(expert, slot):
                copy.wait()

            @pl.when(expert + 1 < N_EXPERTS)
            def _():
                for copy in fetch(expert + 1, 1 - slot):
                    copy.start()

            half = SPLIT_CHUNK // 2
            for c in range(wu_buf.shape[2] // SPLIT_CHUNK):
                w = wu_buf[slot, :, c * SPLIT_CHUNK:(c + 1) * SPLIT_CHUNK].astype(BF16)
                r = jnp.dot(w, p_ref[...], preferred_element_type=F32)
                wg_ref[0, :, c * half:(c + 1) * half] = r[:, :half].astype(BF16)
                wl_ref[0, :, c * half:(c + 1) * half] = r[:, half:].astype(BF16)
            wd_ref[0] = wdn_buf[slot].astype(BF16)

    @pl.when(nv > 0)
    def _():
        x = _unpack_bf16_pairs(x_ref[...])
        rows = lax.broadcasted_iota(I32, (x.shape[0], 1), 0)
        x = jnp.where(rows < nv, x, 0.0).astype(BF16)
        acc = jnp.zeros((x.shape[0], D_MODEL), F32) + bd_ref[0]
        for c in range(D_FF // FF_CHUNK):
            sl = slice(c * FF_CHUNK, (c + 1) * FF_CHUNK)
            hg = jnp.dot(x, wg_ref[0, :, sl], preferred_element_type=F32) + bg_ref[0, :, sl]
            hl = jnp.dot(x, wl_ref[0, :, sl], preferred_element_type=F32) + bl_ref[0, :, sl]
            glu = jnp.minimum(hg, SWIGLU_LIMIT)
            lin = jnp.clip(hl, -SWIGLU_LIMIT, SWIGLU_LIMIT)
            act = glu * jax.nn.sigmoid(SWIGLU_ALPHA * glu) * (lin + 1.0)
            acc = acc + jnp.dot(act.astype(BF16), wd_ref[0, sl, :], preferred_element_type=F32)
        y_ref[...] = _pack_bf16_pairs(acc)

    @pl.when(nv == 0)
    def _():
        y_ref[...] = jnp.zeros_like(y_ref)


def _expert_ffn(xs_rows, block_e, n_valid, weights, b_glu, b_lin, b_down):
    n_rows = xs_rows.shape[0]
    bm = EXPERT_BLOCK
    prepare = len(weights) == 2
    wspec = lambda r, c: pl.BlockSpec((1, r, c), lambda b, be, nv: (be[b], 0, 0))
    rows_spec = pl.BlockSpec((bm, D_MODEL // 2), lambda b, be, nv: (b, 0))
    bias_specs = [wspec(1, D_FF), wspec(1, D_FF), wspec(1, D_MODEL)]
    y_shape = jax.ShapeDtypeStruct((n_rows, D_MODEL // 2), U32)
    if prepare:
        half = SPLIT_CHUNK // 2
        src = jnp.arange(SPLIT_CHUNK, dtype=I32)[:, None]
        dst = jnp.arange(SPLIT_CHUNK, dtype=I32)[None, :]
        sel = jnp.where(dst < half, src == 2 * dst, src == 2 * (dst - half) + 1).astype(BF16)
        assert weights[0].shape[0] == N_EXPERTS
        operands = (*weights, sel)
        weight_specs = [pl.BlockSpec(memory_space=pl.ANY), pl.BlockSpec(memory_space=pl.ANY),
                        pl.BlockSpec((SPLIT_CHUNK, SPLIT_CHUNK), lambda b, be, nv: (0, 0))]
        prepared = jax.ShapeDtypeStruct((N_EXPERTS, D_MODEL, D_FF), BF16)
        out_specs = [rows_spec, wspec(D_MODEL, D_FF), wspec(D_MODEL, D_FF), wspec(D_FF, D_MODEL)]
        out_shape = [y_shape, prepared, prepared, prepared]
        scratch = [pltpu.VMEM((2, D_MODEL, 2 * D_FF), F32), pltpu.VMEM((2, D_FF, D_MODEL), F32),
                   pltpu.SemaphoreType.DMA((2, 2))]
    else:
        operands = tuple(weights)
        weight_specs = [wspec(D_MODEL, D_FF), wspec(D_MODEL, D_FF), wspec(D_FF, D_MODEL)]
        out_specs, out_shape = rows_spec, y_shape
        scratch = []
    grid_spec = pltpu.PrefetchScalarGridSpec(
        num_scalar_prefetch=2,
        grid=(n_rows // bm,),
        in_specs=[rows_spec, *weight_specs, *bias_specs],
        out_specs=out_specs,
        scratch_shapes=scratch,
    )
    out = pl.pallas_call(
        functools.partial(_expert_kernel, prepare=prepare),
        grid_spec=grid_spec,
        out_shape=out_shape,
        compiler_params=_cparams("arbitrary"),
        name="expert_ffn",
    )(block_e, n_valid, xs_rows, *operands, b_glu, b_lin, b_down)
    return (out[0], tuple(out[1:])) if prepare else (out, tuple(weights))


def _final_kernel(x1_ref, yg_ref, gt_ref, g_ref, b_ref, *rest):
    o_ref = rest[-1]
    gates = gt_ref[...]
    pad = jnp.zeros((LANES - TOP_K, gates.shape[1]), F32)
    gt = jnp.concatenate([gates, pad], axis=0).T
    ffn = jnp.zeros(x1_ref.shape, F32)
    for kk in range(TOP_K):
        ffn = ffn + gt[:, kk:kk + 1] * _unpack_bf16_pairs(yg_ref[kk])
    o_ref[...] = _layer_norm(DEEPNORM_ALPHA * x1_ref[...] + ffn, g_ref[...], b_ref[...])


def _final(x1, yg, gates, ln_g, ln_b, src_row0, n_rows, out_rows, dst_row0, out_prev=None):
    tm = FINAL_TILE
    assert src_row0 % tm == 0 and dst_row0 % tm == 0 and n_rows % tm == 0
    src, dst = src_row0 // tm, dst_row0 // tm
    in_specs = [
        pl.BlockSpec((tm, D_MODEL), lambda i: (src + i, 0)),
        pl.BlockSpec((TOP_K, tm, D_MODEL // 2), lambda i: (0, src + i, 0)),
        pl.BlockSpec((TOP_K, tm), lambda i: (0, src + i)),
        pl.BlockSpec((1, D_MODEL), lambda i: (0, 0)),
        pl.BlockSpec((1, D_MODEL), lambda i: (0, 0)),
    ]
    args = [x1, yg, gates, ln_g.reshape(1, -1), ln_b.reshape(1, -1)]
    aliases = {}
    if out_prev is not None:
        in_specs.append(pl.BlockSpec(memory_space=pl.ANY))
        args.append(out_prev)
        aliases = {len(args) - 1: 0}
    return pl.pallas_call(
        _final_kernel,
        grid=(n_rows // tm,),
        in_specs=in_specs,
        out_specs=pl.BlockSpec((tm, D_MODEL), lambda i: (dst + i, 0)),
        out_shape=jax.ShapeDtypeStruct((out_rows, D_MODEL), F32),
        input_output_aliases=aliases,
        compiler_params=_cparams("parallel"),
        name="combine_ln2",
    )(*args)


def _routing_tables(idx, rank, counts, n_blocks):
    bm = EXPERT_BLOCK
    padded = jnp.maximum((counts + bm - 1) // bm, 1) * bm
    pad_end = jnp.cumsum(padded)
    pad_start = pad_end - padded
    experts = jnp.arange(N_EXPERTS, dtype=I32)

    def lookup(table, e):
        shape = (N_EXPERTS,) + (1,) * e.ndim
        return jnp.sum(jnp.where(e[None] == experts.reshape(shape), table.reshape(shape), 0), axis=0)

    pos = lookup(pad_start, idx) + rank
    blk_row0 = jnp.arange(n_blocks, dtype=I32) * bm
    block_e = jnp.minimum(jnp.sum((pad_end[:, None] <= blk_row0[None, :]).astype(I32), axis=0), N_EXPERTS - 1)
    n_valid = jnp.clip(lookup(counts, block_e) - (blk_row0 - lookup(pad_start, block_e)), 0, bm)
    return pos.astype(I32), block_e.astype(I32), n_valid.astype(I32)


def _moe(pk, idx, rank, counts, weights, b_glu, b_lin, b_down):
    n_t = pk.shape[0]
    n_blocks = (n_t * TOP_K) // EXPERT_BLOCK + N_EXPERTS
    pos, block_e, n_valid = _routing_tables(idx, rank, counts, n_blocks)
    pos_chunks = pos.reshape(TOP_K, n_t // SC_CHUNK, SC_CHUNK).transpose(1, 0, 2)
    xs_rows = _sc_dispatch(pk, pos_chunks, n_blocks * EXPERT_BLOCK)
    y_rows, prepared = _expert_ffn(xs_rows, block_e, n_valid, weights, b_glu, b_lin, b_down)
    return _sc_combine(y_rows, pos_chunks, n_t), prepared


def _rope_tables(seq_max):
    inv = ROPE_THETA ** (-jnp.arange(0, HEAD_DIM, 2, dtype=F32) / HEAD_DIM)
    ang = jnp.arange(seq_max).astype(F32)[:, None] * inv[None, :]
    cos, sin = jnp.cos(ang), jnp.sin(ang)
    reps = LANES // HEAD_DIM
    cos_t = jnp.tile(jnp.concatenate([cos, cos], axis=1), (1, reps))
    sin_t = jnp.tile(jnp.concatenate([-sin, sin], axis=1), (1, reps))
    return cos_t, sin_t


def _encoder_layer(xp, xs, seq_prompt, batch_sample, seq_sample, w_in, w_attn_out, w_four_out, w_o, ln1_g, ln1_b,
                   w_router, b_router, w_up, b_up, w_down, b_down, ln2_g, ln2_b):
    n_p, n_s = xp.shape[0], xs.shape[0]
    assert n_p == seq_prompt and n_s == batch_sample * seq_sample
    assert seq_prompt % (DFT_STAGE1 * 16) == 0 and seq_sample % TOKEN_TILE == 0 and n_p % seq_sample == 0
    n_t = n_p + n_s
    assert n_t % (max(DILATIONS) * min(ATTN_TILES)) == 0 and seq_sample % (max(DILATIONS) * ATTN_SUB) == 0

    w_proj = w_in[:, :PROJ_WIDTH].astype(BF16)
    w_gate = w_in[:, PROJ_WIDTH:].astype(BF16)
    cos_t, sin_t = _rope_tables(max(seq_prompt, seq_sample))
    q_views, k_views, v_views, f = _inproj(xp, xs, w_proj, cos_t, sin_t, seq_sample)

    o_list, l_list = [], []
    for gi, dil in enumerate(DILATIONS):
        o, lse = _attention_group(q_views[gi], k_views[gi], v_views[gi], dil, n_p, seq_sample)
        o_list.append(o)
        l_list.append(lse)

    four_p = _fourier_prompt(f, seq_prompt)
    four_s = _fourier_sample(f, n_p, batch_sample, seq_sample)

    b_glu = b_up[:, None, 0::2]
    b_lin = b_up[:, None, 1::2]
    w_ao, w_fo, w_ob = w_attn_out.astype(BF16), w_four_out.astype(BF16), w_o.astype(BF16)

    tiles = n_t // TOKEN_TILE
    align = math.lcm(SC_ROUND_ROWS, FINAL_TILE) // TOKEN_TILE
    bounds = [0]
    for share in MOE_PART_SHARES[:-1]:
        bounds.append(min(tiles, bounds[-1] + max(align, round(tiles * share / align) * align)))
    bounds.append(tiles)
    n_parts = len(bounds) - 1
    routed = []
    for p in range(n_parts):
        routed.append(_merge_route(xp, xs, w_gate, o_list, l_list, four_p, four_s, w_ao, w_fo, w_ob, ln1_g, ln1_b,
                                   w_router, b_router, bounds[p], bounds[p + 1] - bounds[p]))
    combined = []
    weights = (w_up, w_down)
    for x1, pk, idx, gates, rank, cnt in routed:
        yg, weights = _moe(pk, idx, rank, cnt[:, 0], weights, b_glu, b_lin, b_down[:, None, :])
        combined.append(yg)

    outs = [None, None]
    spans = [(0, n_p), (n_p, n_t)]
    for p, ((x1, pk, idx, gates, rank, cnt), yg) in enumerate(zip(routed, combined)):
        lo, hi = bounds[p] * TOKEN_TILE, bounds[p + 1] * TOKEN_TILE
        for which, (a, b) in enumerate(spans):
            s, e = max(lo, a), min(hi, b)
            if s < e:
                outs[which] = _final(x1, yg, gates, ln2_g, ln2_b, s - lo, e - s, b - a, s - a, outs[which])
    return outs[0], outs[1]


def kernel(x_prompt, x_sample, w_in, w_attn_out, w_four_out, w_o, ln1_g, ln1_b, w_router, b_router, w_up, b_up,
           w_down, b_down, ln2_g, ln2_b):
    assert w_in.shape[0] == DEPTH
    bp, sp, d = x_prompt.shape
    bs, ss, _ = x_sample.shape
    assert bp == 1 and d == D_MODEL
    y_p, y_s = _encoder_layer(
        x_prompt.reshape(sp, d), x_sample.reshape(bs * ss, d), sp, bs, ss,
        w_in[0], w_attn_out[0], w_four_out[0], w_o[0], ln1_g[0], ln1_b[0], w_router[0], b_router[0],
        w_up[0], b_up[0], w_down[0], b_down[0], ln2_g[0], ln2_b[0])
    return y_p.reshape(x_prompt.shape), y_s.reshape(x_sample.shape)
```

```python
import functools
import math

import jax
import jax.numpy as jnp
from jax import lax
from jax.experimental import pallas as pl
from jax.experimental.pallas import tpu as pltpu
from jax.experimental.pallas import tpu_sc as plsc

F32 = jnp.float32
BF16 = jnp.bfloat16
I32 = jnp.int32
U32 = jnp.uint32

D_MODEL = 1024
HEAD_DIM = 64
HEADS_PER_GROUP = 4
GROUP_WIDTH = HEADS_PER_GROUP * HEAD_DIM
DILATIONS = (1, 4, 16)
HALF_WINDOW = 64
N_GROUPS = len(DILATIONS)
QKV_WIDTH = N_GROUPS * GROUP_WIDTH
F_GROUPS = 4
F_GROUP_DIM = 64
F_WIDTH = F_GROUPS * F_GROUP_DIM
PROJ_WIDTH = 3 * QKV_WIDTH + F_WIDTH
N_EXPERTS = 32
TOP_K = 4
D_FF = 1024
SWIGLU_LIMIT = 7.0
SWIGLU_ALPHA = 1.702
LN_EPS = 1e-5
ROPE_THETA = 10000.0
NEG_INF = -1e30
DEPTH = 1
DEEPNORM_ALPHA = (2 * DEPTH) ** 0.25

LANES = 128
MXU_TILE = 256
TOKEN_TILE = 512
INPROJ_TILE = 1024
FINAL_TILE = 1024
ATTN_TILES = (2048, 1024)
ATTN_SUB = 128
EXPERT_BLOCK = 512
FF_CHUNK = 1024
SPLIT_CHUNK = MXU_TILE
DFT_STAGE1 = 128
DFT_STAGE2_BATCH = 8
DFT_SMALL_STAGE1 = 8
DFT_SMALL_BATCH = 4
SC_CHUNK = 64
SC_ROUND_ROWS = 2 * 16 * SC_CHUNK
MOE_PART_SHARES = (0.7, 0.3)
VMEM_LIMIT = 56 * 1024 * 1024


def _cparams(*sem):
    return pltpu.CompilerParams(dimension_semantics=sem, vmem_limit_bytes=VMEM_LIMIT)


def _to_dilated_view(stage_ref, out_ref, val, dil):
    if dil == 1:
        out_ref[...] = val.astype(BF16)
        return
    rows = val.shape[0] // dil
    for c in range(GROUP_WIDTH // LANES):
        stage_ref[c] = val[:, c * LANES:(c + 1) * LANES]
    for r in range(dil):
        for c in range(GROUP_WIDTH // LANES):
            lo = r * GROUP_WIDTH + c * LANES
            out_ref[:, lo:lo + LANES] = stage_ref[c, pl.ds(r, rows, stride=dil), :].astype(BF16)


def _inproj_kernel(xp_ref, xs_ref, w_ref, cos_ref, sin_ref, *refs, n_prompt_tiles):
    qkv_refs = refs[:3 * N_GROUPS]
    f_ref = refs[3 * N_GROUPS]
    stage_refs = refs[3 * N_GROUPS + 1:]
    i = pl.program_id(0)
    x = jnp.where(i < n_prompt_tiles, xp_ref[...], xs_ref[...]).astype(BF16)
    proj = jnp.dot(x, w_ref[...], preferred_element_type=F32)
    cos = cos_ref[...]
    sin = sin_ref[...]
    lane = lax.broadcasted_iota(I32, cos.shape, 1)
    first_half = (lane % HEAD_DIM) < (HEAD_DIM // 2)

    def rope(t):
        parts = []
        for c in range(GROUP_WIDTH // LANES):
            tc = t[:, c * LANES:(c + 1) * LANES]
            swapped = jnp.where(first_half, pltpu.roll(tc, LANES - HEAD_DIM // 2, 1), pltpu.roll(tc, HEAD_DIM // 2, 1))
            parts.append(tc * cos + swapped * sin)
        return jnp.concatenate(parts, axis=1)

    for gi, dil in enumerate(DILATIONS):
        sl = slice(gi * GROUP_WIDTH, (gi + 1) * GROUP_WIDTH)
        qg = rope(proj[:, sl]) * (HEAD_DIM ** -0.5)
        kg = rope(proj[:, QKV_WIDTH + gi * GROUP_WIDTH:QKV_WIDTH + (gi + 1) * GROUP_WIDTH])
        vg = proj[:, 2 * QKV_WIDTH + gi * GROUP_WIDTH:2 * QKV_WIDTH + (gi + 1) * GROUP_WIDTH]
        for which, val in enumerate((qg, kg, vg)):
            _to_dilated_view(stage_refs[which * N_GROUPS + gi], qkv_refs[which * N_GROUPS + gi], val, dil)
    f_ref[...] = proj[:, 3 * QKV_WIDTH:PROJ_WIDTH].astype(BF16)


def _inproj(xp, xs, w_proj, cos_t, sin_t, seq_sample):
    n_p, n_s = xp.shape[0], xs.shape[0]
    n_t = n_p + n_s
    tm = INPROJ_TILE
    assert n_p % tm == 0 and seq_sample % tm == 0
    npt = n_p // tm
    tiles_per_sample_seq = seq_sample // tm

    def table_idx(i):
        return (jnp.where(i < npt, i, (i - npt) % tiles_per_sample_seq), 0)

    view_specs = [pl.BlockSpec((tm // d, d * GROUP_WIDTH), lambda i: (i, 0)) for d in DILATIONS] * 3
    view_shapes = [jax.ShapeDtypeStruct((n_t // d, d * GROUP_WIDTH), BF16) for d in DILATIONS] * 3
    outs = pl.pallas_call(
        functools.partial(_inproj_kernel, n_prompt_tiles=npt),
        grid=(n_t // tm,),
        in_specs=[
            pl.BlockSpec((tm, D_MODEL), lambda i: (jnp.minimum(i, npt - 1), 0)),
            pl.BlockSpec((tm, D_MODEL), lambda i: (jnp.maximum(i - npt, 0), 0)),
            pl.BlockSpec((D_MODEL, PROJ_WIDTH), lambda i: (0, 0)),
            pl.BlockSpec((tm, LANES), table_idx),
            pl.BlockSpec((tm, LANES), table_idx),
        ],
        out_specs=view_specs + [pl.BlockSpec((tm, F_WIDTH), lambda i: (i, 0))],
        out_shape=view_shapes + [jax.ShapeDtypeStruct((n_t, F_WIDTH), BF16)],
        scratch_shapes=[pltpu.VMEM((GROUP_WIDTH // LANES, tm, LANES), F32)] * (3 * N_GROUPS),
        compiler_params=_cparams("parallel"),
        name="inproj_rope",
    )(xp, xs, w_proj, cos_t, sin_t)
    q_views, k_views, v_views = outs[0:N_GROUPS], outs[N_GROUPS:2 * N_GROUPS], outs[2 * N_GROUPS:3 * N_GROUPS]
    return q_views, k_views, v_views, outs[3 * N_GROUPS]


def _attn_kernel(q_ref, kl_ref, km_ref, kr_ref, vl_ref, vm_ref, vr_ref, o_ref, lse_ref, *, prompt_rows, sample_rows):
    i = pl.program_id(1)
    tq = q_ref.shape[0]
    head_of_lane = lax.broadcasted_iota(I32, (1, GROUP_WIDTH), 1) // HEAD_DIM

    def sub_block(j, kj, vj, valid):
        qj = q_ref[j * ATTN_SUB:(j + 1) * ATTN_SUB, :]
        q4 = jnp.concatenate([jnp.where(head_of_lane == h, qj, jnp.zeros_like(qj)) for h in range(HEADS_PER_GROUP)],
                             axis=0)
        s4 = lax.dot_general(q4, kj, (((1,), (1,)), ((), ())), preferred_element_type=F32)
        ps, scales, lses = [], [], []
        for h in range(HEADS_PER_GROUP):
            s = jnp.where(valid, s4[h * ATTN_SUB:(h + 1) * ATTN_SUB], NEG_INF)
            m = jnp.max(s, axis=1, keepdims=True)
            p = jnp.exp(s - m)
            l = jnp.sum(p, axis=1, keepdims=True)
            ps.append(p.astype(BF16))
            scales.append(1.0 / l)
            lses.append(m + jnp.log(l))
        o4 = jnp.dot(jnp.concatenate(ps, axis=0), vj, preferred_element_type=F32)
        acc = jnp.zeros((ATTN_SUB, GROUP_WIDTH), F32)
        lse_full = jnp.zeros((ATTN_SUB, GROUP_WIDTH), F32)
        for h in range(HEADS_PER_GROUP):
            mine = head_of_lane == h
            acc = jnp.where(mine, o4[h * ATTN_SUB:(h + 1) * ATTN_SUB] * scales[h], acc)
            lse_full = jnp.where(mine, lses[h], lse_full)
        o_ref[j * ATTN_SUB:(j + 1) * ATTN_SUB, :] = acc.astype(BF16)
        lse_ref[j * ATTN_SUB:(j + 1) * ATTN_SUB, :] = lse_full

    def banded_step():
        kext = jnp.concatenate([kl_ref[...], km_ref[...], kr_ref[...]], axis=0)
        vext = jnp.concatenate([vl_ref[...], vm_ref[...], vr_ref[...]], axis=0)
        kw = ATTN_SUB + 2 * HALF_WINDOW
        a = lax.broadcasted_iota(I32, (ATTN_SUB, kw), 0)
        c = lax.broadcasted_iota(I32, (ATTN_SUB, kw), 1)
        in_band = jnp.abs(c - HALF_WINDOW - a) <= HALF_WINDOW
        for j in range(tq // ATTN_SUB):
            r0 = i * tq + j * ATTN_SUB
            in_prompt = r0 < prompt_rows
            b = jnp.maximum(r0 - prompt_rows, 0) // sample_rows
            lo = jnp.where(in_prompt, 0, prompt_rows + b * sample_rows)
            hi = jnp.where(in_prompt, prompt_rows, prompt_rows + (b + 1) * sample_rows)
            key_row = r0 - HALF_WINDOW + c
            valid = in_band & (key_row >= lo) & (key_row < hi)
            sub_block(j, kext[j * ATTN_SUB:j * ATTN_SUB + kw, :], vext[j * ATTN_SUB:j * ATTN_SUB + kw, :], valid)

    def single_sequence_step():
        a = lax.broadcasted_iota(I32, (ATTN_SUB, ATTN_SUB), 0)
        c = lax.broadcasted_iota(I32, (ATTN_SUB, ATTN_SUB), 1)
        valid = jnp.abs(c - a) <= HALF_WINDOW
        for j in range(tq // ATTN_SUB):
            rows = slice(j * ATTN_SUB, (j + 1) * ATTN_SUB)
            sub_block(j, km_ref[rows, :], vm_ref[rows, :], valid)

    if sample_rows == ATTN_SUB and prompt_rows % tq == 0:
        in_sample = i * tq >= prompt_rows
        pl.when(in_sample)(single_sequence_step)
        pl.when(jnp.logical_not(in_sample))(banded_step)
    else:
        banded_step()


def _attention_group(qv, kv, vv, dil, n_prompt, seq_sample):
    rows = qv.shape[0]
    tq = next(t for t in ATTN_TILES if rows % t == 0)
    halo_per_tile = tq // HALF_WINDOW
    n_halo_blocks = rows // HALF_WINDOW
    main = pl.BlockSpec((tq, GROUP_WIDTH), lambda r, i: (i, r))
    left = pl.BlockSpec((HALF_WINDOW, GROUP_WIDTH), lambda r, i: (jnp.maximum(i * halo_per_tile - 1, 0), r))
    right = pl.BlockSpec((HALF_WINDOW, GROUP_WIDTH),
                         lambda r, i: (jnp.minimum((i + 1) * halo_per_tile, n_halo_blocks - 1), r))
    return pl.pallas_call(
        functools.partial(_attn_kernel, prompt_rows=n_prompt // dil, sample_rows=seq_sample // dil),
        grid=(dil, rows // tq),
        in_specs=[main, left, main, right, left, main, right],
        out_specs=[main, main],
        out_shape=[jax.ShapeDtypeStruct((rows, dil * GROUP_WIDTH), BF16),
                   jax.ShapeDtypeStruct((rows, dil * GROUP_WIDTH), F32)],
        compiler_params=_cparams("parallel", "parallel"),
        name=f"dilated_attention_{dil}",
    )(qv, kv, kv, kv, vv, vv, vv)


def _angle_table(idx, period):
    return (idx % period).astype(F32) * (2.0 * math.pi / period)


def _channel_dft(seq_len):
    c = jnp.arange(F_GROUP_DIM, dtype=I32)
    th = _angle_table(c[:, None] * c[None, :], F_GROUP_DIM)
    scale = (seq_len * F_GROUP_DIM) ** -0.5
    eye = jnp.eye(F_GROUPS, dtype=F32)
    cb = jnp.kron(eye, jnp.cos(th)) * scale
    sb = jnp.kron(eye, jnp.sin(th)) * scale
    return jnp.concatenate([cb, sb], axis=0).astype(BF16)


def _stage2_matrices(s1, s2):
    n2 = jnp.arange(s2, dtype=I32)[None, :]
    th_t = _angle_table(jnp.arange(s1, dtype=I32)[:, None] * n2, s1 * s2)
    th_f = _angle_table(jnp.arange(s2, dtype=I32)[:, None] * n2, s2)
    tc, ts = jnp.cos(th_t)[:, None, :], jnp.sin(th_t)[:, None, :]
    fc, fs = jnp.cos(th_f)[None, :, :], jnp.sin(th_f)[None, :, :]
    gr = tc * fc - ts * fs
    gi = -(ts * fc + tc * fs)
    return jnp.concatenate([jnp.concatenate([gr, -gi], axis=2), jnp.concatenate([gi, gr], axis=2)], axis=1).astype(BF16)


def _dft_stage1_kernel(m_ref, x_ref, a_ref):
    a_ref[...] = jnp.dot(m_ref[...], x_ref[...], preferred_element_type=F32).astype(BF16)


def _dft_stage2_kernel(g_ref, a_ref, cs_ref, o_ref):
    s2 = a_ref.shape[2]
    for kk in range(g_ref.shape[0]):
        a = jnp.concatenate([a_ref[0, kk], a_ref[1, kk]], axis=0)
        x = jnp.dot(g_ref[kk], a, preferred_element_type=F32)
        xr = x[:s2].astype(BF16)
        xi = x[s2:].astype(BF16)
        o_ref[:, kk * F_WIDTH:(kk + 1) * F_WIDTH] = (
            jnp.dot(xr, cs_ref[:F_WIDTH, :], preferred_element_type=F32)
            + jnp.dot(xi, cs_ref[F_WIDTH:, :], preferred_element_type=F32)).astype(BF16)


def _fourier_prompt(f, seq):
    s1 = DFT_STAGE1
    s2 = seq // s1
    n1 = jnp.arange(s1, dtype=I32)
    th1 = _angle_table(n1[:, None] * n1[None, :], s1)
    m1 = jnp.concatenate([jnp.cos(th1), -jnp.sin(th1)], axis=0).astype(BF16)
    g = _stage2_matrices(s1, s2)
    cs = _channel_dft(seq)

    cols = s2 * F_WIDTH
    fv = f[:seq].reshape(s1, cols)
    tn = min(cols, 4096)
    a = pl.pallas_call(
        _dft_stage1_kernel,
        grid=(cols // tn,),
        in_specs=[pl.BlockSpec((2 * s1, s1), lambda j: (0, 0)), pl.BlockSpec((s1, tn), lambda j: (0, j))],
        out_specs=pl.BlockSpec((2 * s1, tn), lambda j: (0, j)),
        out_shape=jax.ShapeDtypeStruct((2 * s1, cols), BF16),
        compiler_params=_cparams("parallel"),
        name="dft_stage1",
    )(m1, fv)
    a4 = a.reshape(2, s1, s2, F_WIDTH)
    kb = DFT_STAGE2_BATCH
    out = pl.pallas_call(
        _dft_stage2_kernel,
        grid=(s1 // kb,),
        in_specs=[pl.BlockSpec((kb, 2 * s2, 2 * s2), lambda k: (k, 0, 0)),
                  pl.BlockSpec((2, kb, s2, F_WIDTH), lambda k: (0, k, 0, 0)),
                  pl.BlockSpec((2 * F_WIDTH, F_WIDTH), lambda k: (0, 0))],
        out_specs=pl.BlockSpec((s2, kb * F_WIDTH), lambda k: (0, k)),
        out_shape=jax.ShapeDtypeStruct((s2, s1 * F_WIDTH), BF16),
        compiler_params=_cparams("parallel"),
        name="dft_stage2",
    )(g, a4, cs)
    return out.reshape(seq, F_WIDTH)


def _dft_small_kernel(g_ref, x_ref, cs_ref, o_ref, stage_ref):
    s1 = DFT_SMALL_STAGE1
    assert s1 == 8
    s2 = x_ref.shape[1] // s1
    root = math.sqrt(0.5)
    n_seq = x_ref.shape[0]
    a_all = []
    for b in range(n_seq):
        x0, x1, x2, x3, x4, x5, x6, x7 = [x_ref[b, n1 * s2:(n1 + 1) * s2, :].astype(F32) for n1 in range(s1)]
        u, v, p, q = x0 - x4, x2 - x6, x1 - x5, x3 - x7
        e_sum, e_dif = (x0 + x4) + (x2 + x6), (x0 + x4) - (x2 + x6)
        o_sum, o_dif = (x1 + x5) + (x3 + x7), (x1 + x5) - (x3 + x7)
        rpq_m, rpq_p = root * (p - q), root * (p + q)
        zero = jnp.zeros_like(x0)
        a_re = [e_sum + o_sum, u + rpq_m, e_dif, u - rpq_m, e_sum - o_sum, u - rpq_m, e_dif, u + rpq_m]
        im1, im3 = -v - rpq_p, v - rpq_p
        a_im = [zero, im1, -o_dif, im3, zero, -im3, o_dif, -im1]
        a_all.append([jnp.concatenate([a_re[k1].astype(BF16), a_im[k1].astype(BF16)], axis=0) for k1 in range(s1)])

    for k1 in range(s1):
        a = jnp.concatenate([a_all[b][k1] for b in range(n_seq)], axis=1)
        x = jnp.dot(g_ref[k1], a, preferred_element_type=F32)
        for b in range(n_seq):
            xb = x[:, b * F_WIDTH:(b + 1) * F_WIDTH]
            out = (jnp.dot(xb[:s2].astype(BF16), cs_ref[:F_WIDTH, :], preferred_element_type=F32)
                   + jnp.dot(xb[s2:].astype(BF16), cs_ref[F_WIDTH:, :], preferred_element_type=F32))
            for c in range(F_WIDTH // LANES):
                stage_ref[b * (F_WIDTH // LANES) + c, pl.ds(k1, s2, stride=s1), :] = out[:, c * LANES:(c + 1) * LANES]
    for b in range(n_seq):
        o_ref[b] = jnp.concatenate([stage_ref[b * (F_WIDTH // LANES) + c] for c in range(F_WIDTH // LANES)],
                                   axis=1).astype(BF16)


def _fourier_sample(f, n_prompt, batch, seq):
    s1 = DFT_SMALL_STAGE1
    s2 = seq // s1
    g = _stage2_matrices(s1, s2)
    cs = _channel_dft(seq)
    f3 = f.reshape(f.shape[0] // seq, seq, F_WIDTH)
    first = n_prompt // seq
    nb = DFT_SMALL_BATCH
    assert batch % nb == 0 and first % nb == 0
    return pl.pallas_call(
        _dft_small_kernel,
        grid=(batch // nb,),
        in_specs=[pl.BlockSpec((s1, 2 * s2, 2 * s2), lambda b: (0, 0, 0)),
                  pl.BlockSpec((nb, seq, F_WIDTH), lambda b: (first // nb + b, 0, 0)),
                  pl.BlockSpec((2 * F_WIDTH, F_WIDTH), lambda b: (0, 0))],
        out_specs=pl.BlockSpec((nb, seq, F_WIDTH), lambda b: (b, 0, 0)),
        out_shape=jax.ShapeDtypeStruct((batch, seq, F_WIDTH), BF16),
        scratch_shapes=[pltpu.VMEM((nb * (F_WIDTH // LANES), seq, LANES), F32)],
        compiler_params=_cparams("parallel"),
        name="dft_small",
    )(g, f3, cs).reshape(batch * seq, F_WIDTH)


def _pack_bf16_pairs(x):
    w = x.shape[1] // 2
    bits = lax.bitcast_convert_type(x.astype(BF16).astype(F32), U32)
    return (bits[:, :w] >> 16) | (bits[:, w:] & jnp.uint32(0xFFFF0000))


def _unpack_bf16_pairs(u):
    lo = lax.bitcast_convert_type(u << 16, F32)
    hi = lax.bitcast_convert_type(u & jnp.uint32(0xFFFF0000), F32)
    return jnp.concatenate([lo, hi], axis=1)


def _layer_norm(h, g, b):
    mu = jnp.mean(h, axis=-1, keepdims=True)
    d = h - mu
    var = jnp.mean(d * d, axis=-1, keepdims=True)
    return d * lax.rsqrt(var + LN_EPS) * g + b


def _from_dilated_view(stage_ref, blk_ref, dil):
    if dil == 1:
        return blk_ref[...].astype(F32)
    rows = blk_ref.shape[0]
    for r in range(dil):
        for c in range(GROUP_WIDTH // LANES):
            lo = r * GROUP_WIDTH + c * LANES
            stage_ref[c, pl.ds(r, rows, stride=dil), :] = blk_ref[:, lo:lo + LANES].astype(F32)
    return jnp.concatenate([stage_ref[c] for c in range(GROUP_WIDTH // LANES)], axis=1)


def _merge_kernel(xp_ref, xs_ref, wg_ref, o1_ref, o2_ref, o3_ref, l1_ref, l2_ref, l3_ref, fp_ref, fs_ref,
                  wao_ref, wfo_ref, wo_ref, g_ref, b_ref, wr2_ref, br_ref, tri_ref,
                  x1_ref, pk_ref, idx_ref, gate_ref, rank_ref, cnt_ref, run_ref, *stage_refs, n_prompt_tiles):
    i = pl.program_id(0)

    @pl.when(i == 0)
    def _():
        run_ref[...] = jnp.zeros_like(run_ref)

    is_prompt = i < n_prompt_tiles
    x = jnp.where(is_prompt, xp_ref[...], xs_ref[...])
    gpre = jnp.dot(x.astype(BF16), wg_ref[...], preferred_element_type=F32)
    o1, o2, o3 = [_from_dilated_view(stage_refs[gi], ref, d)
                  for gi, (ref, d) in enumerate(zip((o1_ref, o2_ref, o3_ref), DILATIONS))]
    l1, l2, l3 = [_from_dilated_view(stage_refs[N_GROUPS + gi], ref, d)
                  for gi, (ref, d) in enumerate(zip((l1_ref, l2_ref, l3_ref), DILATIONS))]
    m = jnp.maximum(jnp.maximum(l1, l2), l3)
    e1, e2, e3 = jnp.exp(l1 - m), jnp.exp(l2 - m), jnp.exp(l3 - m)
    attn = (e1 * o1 + e2 * o2 + e3 * o3) * (1.0 / (e1 + e2 + e3))
    a = jnp.dot(attn.astype(BF16), wao_ref[...], preferred_element_type=F32)
    four = jnp.where(is_prompt, fp_ref[...], fs_ref[...])
    ff = jnp.dot(four, wfo_ref[...], preferred_element_type=F32)
    merged = jax.nn.sigmoid(gpre[:, :D_MODEL]) * a + jax.nn.sigmoid(gpre[:, D_MODEL:]) * ff
    mix = jnp.dot(merged.astype(BF16), wo_ref[...], preferred_element_type=F32)
    x1 = _layer_norm(DEEPNORM_ALPHA * x + mix, g_ref[...], b_ref[...])
    x1_ref[...] = x1
    pk_ref[...] = _pack_bf16_pairs(x1)

    xh = x1.astype(BF16)
    xl = (x1 - xh.astype(F32)).astype(BF16)
    tm = x1.shape[0]
    cross = lax.dot_general(wr2_ref[...], jnp.concatenate([xh, xl], axis=0), (((1,), (1,)), ((), ())),
                            preferred_element_type=F32)
    logits = ((cross[:N_EXPERTS, :tm] + cross[N_EXPERTS:, :tm])
              + (cross[:N_EXPERTS, tm:] + cross[N_EXPERTS:, tm:])) + br_ref[...]
    eio = lax.broadcasted_iota(I32, (N_EXPERTS, tm), 0)
    work = logits
    vals, sels, picks = [], [], []
    for _ in range(TOP_K):
        mk = jnp.max(work, axis=0, keepdims=True)
        ik = jnp.min(jnp.where(work == mk, eio, N_EXPERTS), axis=0, keepdims=True)
        sel = eio == ik
        vals.append(mk)
        sels.append(sel)
        picks.append(ik)
        work = jnp.where(sel, -jnp.inf, work)
    es = [jnp.exp(v - vals[0]) for v in vals]
    inv_den = 1.0 / (es[0] + es[1] + es[2] + es[3])
    chosen = (sels[0] | sels[1] | sels[2] | sels[3])
    chosen_f = chosen.astype(F32)
    prefix = jnp.dot(chosen_f.astype(BF16), tri_ref[...], preferred_element_type=F32)
    before = prefix + run_ref[...]
    for kk in range(TOP_K):
        idx_ref[kk:kk + 1, :] = picks[kk]
        gate_ref[kk:kk + 1, :] = es[kk] * inv_den
        rank_ref[kk:kk + 1, :] = jnp.sum(jnp.where(sels[kk], before, 0.0), axis=0, keepdims=True).astype(I32)
    run_ref[...] = run_ref[...] + jnp.sum(chosen_f, axis=1, keepdims=True)
    cnt_ref[...] = jnp.broadcast_to(run_ref[...], cnt_ref.shape).astype(I32)


def _merge_route(xp, xs, w_gate, o_list, l_list, four_p, four_s, w_ao, w_fo, w_o, ln_g, ln_b, w_r, b_r, tile0,
                 n_tiles):
    n_p = xp.shape[0]
    tm = TOKEN_TILE
    n_t = n_tiles * tm
    npt = n_p // tm
    w_rt = w_r.T
    w_rh = w_rt.astype(BF16)
    w_rl = (w_rt - w_rh.astype(F32)).astype(BF16)
    tri = (jnp.arange(tm)[:, None] < jnp.arange(tm)[None, :]).astype(BF16)
    row = lambda w: pl.BlockSpec((tm, w), lambda i: (i, 0))
    prompt_row = lambda w: pl.BlockSpec((tm, w), lambda i: (jnp.clip(tile0 + i, 0, npt - 1), 0))
    sample_row = lambda w: pl.BlockSpec((tm, w), lambda i: (jnp.maximum(tile0 + i - npt, 0), 0))
    full = lambda r, c: pl.BlockSpec((r, c), lambda i: (0, 0))
    lane_row = pl.BlockSpec((TOP_K, tm), lambda i: (0, i))
    views = [pl.BlockSpec((tm // d, d * GROUP_WIDTH), lambda i: (tile0 + i, 0)) for d in DILATIONS]
    return pl.pallas_call(
        functools.partial(_merge_kernel, n_prompt_tiles=npt - tile0),
        grid=(n_tiles,),
        in_specs=[
            prompt_row(D_MODEL), sample_row(D_MODEL),
            full(D_MODEL, 2 * D_MODEL),
            *views, *views,
            prompt_row(F_WIDTH), sample_row(F_WIDTH),
            full(GROUP_WIDTH, D_MODEL), full(F_WIDTH, D_MODEL), full(D_MODEL, D_MODEL),
            full(1, D_MODEL), full(1, D_MODEL),
            full(2 * N_EXPERTS, D_MODEL), full(N_EXPERTS, 1),
            full(tm, tm),
        ],
        out_specs=[row(D_MODEL), row(D_MODEL // 2), lane_row, lane_row, lane_row, full(N_EXPERTS, LANES)],
        out_shape=[
            jax.ShapeDtypeStruct((n_t, D_MODEL), F32),
            jax.ShapeDtypeStruct((n_t, D_MODEL // 2), U32),
            jax.ShapeDtypeStruct((TOP_K, n_t), I32),
            jax.ShapeDtypeStruct((TOP_K, n_t), F32),
            jax.ShapeDtypeStruct((TOP_K, n_t), I32),
            jax.ShapeDtypeStruct((N_EXPERTS, LANES), I32),
        ],
        scratch_shapes=[pltpu.VMEM((N_EXPERTS, 1), F32)]
        + [pltpu.VMEM((GROUP_WIDTH // LANES, tm, LANES), F32)] * (2 * N_GROUPS),
        compiler_params=_cparams("arbitrary"),
        name="merge_ln_route",
    )(xp, xs, w_gate, *o_list, *l_list, four_p, four_s, w_ao, w_fo, w_o, ln_g.reshape(1, -1), ln_b.reshape(1, -1),
      jnp.concatenate([w_rh, w_rl], axis=0), b_r.reshape(-1, 1), tri)


def _sc_workers():
    info = plsc.get_sparse_core_info()
    return info.num_cores, info.num_cores * info.num_subcores


def _sc_dispatch(rows_src, pos_chunks, n_rows_out):
    n_t, w = rows_src.shape
    n_cores, n_workers = _sc_workers()
    assert n_workers * SC_CHUNK == SC_ROUND_ROWS and n_t % SC_ROUND_ROWS == 0
    chunks_per_worker = n_t // (n_workers * SC_CHUNK)
    mesh = plsc.VectorSubcoreMesh(core_axis_name="c", subcore_axis_name="s")

    @functools.partial(
        pl.kernel, mesh=mesh,
        out_type=jax.ShapeDtypeStruct((n_rows_out, w), rows_src.dtype),
        scratch_types=[pltpu.VMEM((TOP_K, SC_CHUNK), I32), pltpu.VMEM((SC_CHUNK, w), rows_src.dtype)],
    )
    def dispatch(src_hbm, pos_hbm, out_hbm, idx_v, rows_v):
        wid = lax.axis_index("s") * n_cores + lax.axis_index("c")

        @pl.loop(0, chunks_per_worker)
        def _(j):
            chunk = wid * chunks_per_worker + j
            pltpu.sync_copy(pos_hbm.at[chunk], idx_v)
            pltpu.sync_copy(src_hbm.at[pl.ds(chunk * SC_CHUNK, SC_CHUNK)], rows_v)
            for kk in range(TOP_K):
                pltpu.sync_copy(rows_v, out_hbm.at[idx_v.at[kk]])

    return dispatch(rows_src, pos_chunks)


def _sc_combine(rows_src, pos_chunks, n_t):
    w = rows_src.shape[1]
    n_cores, n_workers = _sc_workers()
    assert n_workers * SC_CHUNK == SC_ROUND_ROWS and n_t % SC_ROUND_ROWS == 0
    chunks_per_worker = n_t // (n_workers * SC_CHUNK)
    mesh = plsc.VectorSubcoreMesh(core_axis_name="c", subcore_axis_name="s")

    @functools.partial(
        pl.kernel, mesh=mesh,
        out_type=jax.ShapeDtypeStruct((TOP_K, n_t, w), rows_src.dtype),
        scratch_types=[pltpu.VMEM((TOP_K, SC_CHUNK), I32), pltpu.VMEM((SC_CHUNK, w), rows_src.dtype)],
    )
    def combine(src_hbm, pos_hbm, out_hbm, idx_v, rows_v):
        wid = lax.axis_index("s") * n_cores + lax.axis_index("c")

        @pl.loop(0, chunks_per_worker)
        def _(j):
            chunk = wid * chunks_per_worker + j
            pltpu.sync_copy(pos_hbm.at[chunk], idx_v)
            for kk in range(TOP_K):
                pltpu.sync_copy(src_hbm.at[idx_v.at[kk]], rows_v)
                pltpu.sync_copy(rows_v, out_hbm.at[kk, pl.ds(chunk * SC_CHUNK, SC_CHUNK)])

    return combine(rows_src, pos_chunks)


def _expert_kernel(be_ref, nv_ref, x_ref, *refs, prepare):
    if prepare:
        (wu_hbm, wdn_hbm, p_ref, bg_ref, bl_ref, bd_ref, y_ref, wg_ref, wl_ref, wd_ref,
         wu_buf, wdn_buf, sem) = refs
    else:
        wg_ref, wl_ref, wd_ref, bg_ref, bl_ref, bd_ref, y_ref = refs
    blk = pl.program_id(0)
    nv = nv_ref[blk]

    if prepare:
        expert = be_ref[blk]
        new_expert = (blk == 0) | (expert != be_ref[jnp.maximum(blk - 1, 0)])
        slot = expert % 2

        def fetch(e, s):
            return (pltpu.make_async_copy(wu_hbm.at[e], wu_buf.at[s], sem.at[0, s]),
                    pltpu.make_async_copy(wdn_hbm.at[e], wdn_buf.at[s], sem.at[1, s]))

        @pl.when(new_expert)
        def _():
            @pl.when(blk == 0)
            def _():
                for copy in fetch(expert, slot):
                    copy.start()

            for copy in fetch(expert, slot):
                copy.wait()

            @pl.when(expert + 1 < N_EXPERTS)
            def _():
                for copy in fetch(expert + 1, 1 - slot):
                    copy.start(priority=1)

            half = SPLIT_CHUNK // 2
            for c in range(wu_buf.shape[2] // SPLIT_CHUNK):
                w = wu_buf[slot, :, c * SPLIT_CHUNK:(c + 1) * SPLIT_CHUNK].astype(BF16)
                r = jnp.dot(w, p_ref[...], preferred_element_type=F32)
                wg_ref[0, :, c * half:(c + 1) * half] = r[:, :half].astype(BF16)
                wl_ref[0, :, c * half:(c + 1) * half] = r[:, half:].astype(BF16)
            wd_ref[0] = wdn_buf[slot].astype(BF16)

    @pl.when(nv > 0)
    def _():
        x = _unpack_bf16_pairs(x_ref[...])
        rows = lax.broadcasted_iota(I32, (x.shape[0], 1), 0)
        x = jnp.where(rows < nv, x, 0.0).astype(BF16)
        acc = jnp.zeros((x.shape[0], D_MODEL), F32) + bd_ref[0]
        for c in range(D_FF // FF_CHUNK):
            sl = slice(c * FF_CHUNK, (c + 1) * FF_CHUNK)
            hg = jnp.dot(x, wg_ref[0, :, sl], preferred_element_type=F32) + bg_ref[0, :, sl]
            hl = jnp.dot(x, wl_ref[0, :, sl], preferred_element_type=F32) + bl_ref[0, :, sl]
            glu = jnp.minimum(hg, SWIGLU_LIMIT)
            lin = jnp.clip(hl, -SWIGLU_LIMIT, SWIGLU_LIMIT)
            act = glu * jax.nn.sigmoid(SWIGLU_ALPHA * glu) * (lin + 1.0)
            acc = acc + jnp.dot(act.astype(BF16), wd_ref[0, sl, :], preferred_element_type=F32)
        y_ref[...] = _pack_bf16_pairs(acc)

    @pl.when(nv == 0)
    def _():
        y_ref[...] = jnp.zeros_like(y_ref)


def _expert_ffn(xs_rows, block_e, n_valid, weights, b_glu, b_lin, b_down):
    n_rows = xs_rows.shape[0]
    bm = EXPERT_BLOCK
    prepare = len(weights) == 2
    wspec = lambda r, c: pl.BlockSpec((1, r, c), lambda b, be, nv: (be[b], 0, 0))
    rows_spec = pl.BlockSpec((bm, D_MODEL // 2), lambda b, be, nv: (b, 0))
    bias_specs = [wspec(1, D_FF), wspec(1, D_FF), wspec(1, D_MODEL)]
    y_shape = jax.ShapeDtypeStruct((n_rows, D_MODEL // 2), U32)
    if prepare:
        half = SPLIT_CHUNK // 2
        src = jnp.arange(SPLIT_CHUNK, dtype=I32)[:, None]
        dst = jnp.arange(SPLIT_CHUNK, dtype=I32)[None, :]
        sel = jnp.where(dst < half, src == 2 * dst, src == 2 * (dst - half) + 1).astype(BF16)
        assert weights[0].shape[0] == N_EXPERTS
        operands = (*weights, sel)
        weight_specs = [pl.BlockSpec(memory_space=pl.ANY), pl.BlockSpec(memory_space=pl.ANY),
                        pl.BlockSpec((SPLIT_CHUNK, SPLIT_CHUNK), lambda b, be, nv: (0, 0))]
        prepared = jax.ShapeDtypeStruct((N_EXPERTS, D_MODEL, D_FF), BF16)
        out_specs = [rows_spec, wspec(D_MODEL, D_FF), wspec(D_MODEL, D_FF), wspec(D_FF, D_MODEL)]
        out_shape = [y_shape, prepared, prepared, prepared]
        scratch = [pltpu.VMEM((2, D_MODEL, 2 * D_FF), F32), pltpu.VMEM((2, D_FF, D_MODEL), F32),
                   pltpu.SemaphoreType.DMA((2, 2))]
    else:
        operands = tuple(weights)
        weight_specs = [wspec(D_MODEL, D_FF), wspec(D_MODEL, D_FF), wspec(D_FF, D_MODEL)]
        out_specs, out_shape = rows_spec, y_shape
        scratch = []
    grid_spec = pltpu.PrefetchScalarGridSpec(
        num_scalar_prefetch=2,
        grid=(n_rows // bm,),
        in_specs=[rows_spec, *weight_specs, *bias_specs],
        out_specs=out_specs,
        scratch_shapes=scratch,
    )
    out = pl.pallas_call(
        functools.partial(_expert_kernel, prepare=prepare),
        grid_spec=grid_spec,
        out_shape=out_shape,
        compiler_params=_cparams("arbitrary"),
        name="expert_ffn",
    )(block_e, n_valid, xs_rows, *operands, b_glu, b_lin, b_down)
    return (out[0], tuple(out[1:])) if prepare else (out, tuple(weights))


def _final_kernel(x1_ref, yg_ref, gt_ref, g_ref, b_ref, *rest):
    o_ref = rest[-1]
    gates = gt_ref[...]
    pad = jnp.zeros((LANES - TOP_K, gates.shape[1]), F32)
    gt = jnp.concatenate([gates, pad], axis=0).T
    ffn = jnp.zeros(x1_ref.shape, F32)
    for kk in range(TOP_K):
        ffn = ffn + gt[:, kk:kk + 1] * _unpack_bf16_pairs(yg_ref[kk])
    o_ref[...] = _layer_norm(DEEPNORM_ALPHA * x1_ref[...] + ffn, g_ref[...], b_ref[...])


def _final(x1, yg, gates, ln_g, ln_b, src_row0, n_rows, out_rows, dst_row0, out_prev=None):
    tm = FINAL_TILE
    assert src_row0 % tm == 0 and dst_row0 % tm == 0 and n_rows % tm == 0
    src, dst = src_row0 // tm, dst_row0 // tm
    in_specs = [
        pl.BlockSpec((tm, D_MODEL), lambda i: (src + i, 0)),
        pl.BlockSpec((TOP_K, tm, D_MODEL // 2), lambda i: (0, src + i, 0)),
        pl.BlockSpec((TOP_K, tm), lambda i: (0, src + i)),
        pl.BlockSpec((1, D_MODEL), lambda i: (0, 0)),
        pl.BlockSpec((1, D_MODEL), lambda i: (0, 0)),
    ]
    args = [x1, yg, gates, ln_g.reshape(1, -1), ln_b.reshape(1, -1)]
    aliases = {}
    if out_prev is not None:
        in_specs.append(pl.BlockSpec(memory_space=pl.ANY))
        args.append(out_prev)
        aliases = {len(args) - 1: 0}
    return pl.pallas_call(
        _final_kernel,
        grid=(n_rows // tm,),
        in_specs=in_specs,
        out_specs=pl.BlockSpec((tm, D_MODEL), lambda i: (dst + i, 0)),
        out_shape=jax.ShapeDtypeStruct((out_rows, D_MODEL), F32),
        input_output_aliases=aliases,
        compiler_params=_cparams("parallel"),
        name="combine_ln2",
    )(*args)


def _routing_tables(idx, rank, counts, n_blocks):
    bm = EXPERT_BLOCK
    padded = jnp.maximum((counts + bm - 1) // bm, 1) * bm
    pad_end = jnp.cumsum(padded)
    pad_start = pad_end - padded
    experts = jnp.arange(N_EXPERTS, dtype=I32)

    def lookup(table, e):
        shape = (N_EXPERTS,) + (1,) * e.ndim
        return jnp.sum(jnp.where(e[None] == experts.reshape(shape), table.reshape(shape), 0), axis=0)

    pos = lookup(pad_start, idx) + rank
    blk_row0 = jnp.arange(n_blocks, dtype=I32) * bm
    block_e = jnp.minimum(jnp.sum((pad_end[:, None] <= blk_row0[None, :]).astype(I32), axis=0), N_EXPERTS - 1)
    n_valid = jnp.clip(lookup(counts, block_e) - (blk_row0 - lookup(pad_start, block_e)), 0, bm)
    return pos.astype(I32), block_e.astype(I32), n_valid.astype(I32)


def _moe(pk, idx, rank, counts, weights, b_glu, b_lin, b_down):
    n_t = pk.shape[0]
    n_blocks = (n_t * TOP_K) // EXPERT_BLOCK + N_EXPERTS
    pos, block_e, n_valid = _routing_tables(idx, rank, counts, n_blocks)
    pos_chunks = pos.reshape(TOP_K, n_t // SC_CHUNK, SC_CHUNK).transpose(1, 0, 2)
    xs_rows = _sc_dispatch(pk, pos_chunks, n_blocks * EXPERT_BLOCK)
    y_rows, prepared = _expert_ffn(xs_rows, block_e, n_valid, weights, b_glu, b_lin, b_down)
    return _sc_combine(y_rows, pos_chunks, n_t), prepared


def _rope_tables(seq_max):
    inv = ROPE_THETA ** (-jnp.arange(0, HEAD_DIM, 2, dtype=F32) / HEAD_DIM)
    ang = jnp.arange(seq_max).astype(F32)[:, None] * inv[None, :]
    cos, sin = jnp.cos(ang), jnp.sin(ang)
    reps = LANES // HEAD_DIM
    cos_t = jnp.tile(jnp.concatenate([cos, cos], axis=1), (1, reps))
    sin_t = jnp.tile(jnp.concatenate([-sin, sin], axis=1), (1, reps))
    return cos_t, sin_t


def _encoder_layer(xp, xs, seq_prompt, batch_sample, seq_sample, w_in, w_attn_out, w_four_out, w_o, ln1_g, ln1_b,
                   w_router, b_router, w_up, b_up, w_down, b_down, ln2_g, ln2_b):
    n_p, n_s = xp.shape[0], xs.shape[0]
    assert n_p == seq_prompt and n_s == batch_sample * seq_sample
    assert seq_prompt % (DFT_STAGE1 * 16) == 0 and seq_sample % TOKEN_TILE == 0 and n_p % seq_sample == 0
    n_t = n_p + n_s
    assert n_t % (max(DILATIONS) * min(ATTN_TILES)) == 0 and seq_sample % (max(DILATIONS) * ATTN_SUB) == 0

    w_proj = w_in[:, :PROJ_WIDTH].astype(BF16)
    w_gate = w_in[:, PROJ_WIDTH:].astype(BF16)
    cos_t, sin_t = _rope_tables(max(seq_prompt, seq_sample))
    q_views, k_views, v_views, f = _inproj(xp, xs, w_proj, cos_t, sin_t, seq_sample)

    o_list, l_list = [], []
    for gi, dil in enumerate(DILATIONS):
        o, lse = _attention_group(q_views[gi], k_views[gi], v_views[gi], dil, n_p, seq_sample)
        o_list.append(o)
        l_list.append(lse)

    four_p = _fourier_prompt(f, seq_prompt)
    four_s = _fourier_sample(f, n_p, batch_sample, seq_sample)

    b_glu = b_up[:, None, 0::2]
    b_lin = b_up[:, None, 1::2]
    w_ao, w_fo, w_ob = w_attn_out.astype(BF16), w_four_out.astype(BF16), w_o.astype(BF16)

    tiles = n_t // TOKEN_TILE
    align = math.lcm(SC_ROUND_ROWS, FINAL_TILE) // TOKEN_TILE
    bounds = [0]
    for share in MOE_PART_SHARES[:-1]:
        bounds.append(min(tiles, bounds[-1] + max(align, round(tiles * share / align) * align)))
    bounds.append(tiles)
    n_parts = len(bounds) - 1
    routed = []
    for p in range(n_parts):
        routed.append(_merge_route(xp, xs, w_gate, o_list, l_list, four_p, four_s, w_ao, w_fo, w_ob, ln1_g, ln1_b,
                                   w_router, b_router, bounds[p], bounds[p + 1] - bounds[p]))
    combined = []
    weights = (w_up, w_down)
    for x1, pk, idx, gates, rank, cnt in routed:
        yg, weights = _moe(pk, idx, rank, cnt[:, 0], weights, b_glu, b_lin, b_down[:, None, :])
        combined.append(yg)

    outs = [None, None]
    spans = [(0, n_p), (n_p, n_t)]
    for p, ((x1, pk, idx, gates, rank, cnt), yg) in enumerate(zip(routed, combined)):
        lo, hi = bounds[p] * TOKEN_TILE, bounds[p + 1] * TOKEN_TILE
        for which, (a, b) in enumerate(spans):
            s, e = max(lo, a), min(hi, b)
            if s < e:
                outs[which] = _final(x1, yg, gates, ln2_g, ln2_b, s - lo, e - s, b - a, s - a, outs[which])
    return outs[0], outs[1]


def kernel(x_prompt, x_sample, w_in, w_attn_out, w_four_out, w_o, ln1_g, ln1_b, w_router, b_router, w_up, b_up,
           w_down, b_down, ln2_g, ln2_b):
    assert w_in.shape[0] == DEPTH
    bp, sp, d = x_prompt.shape
    bs, ss, _ = x_sample.shape
    assert bp == 1 and d == D_MODEL
    y_p, y_s = _encoder_layer(
        x_prompt.reshape(sp, d), x_sample.reshape(bs * ss, d), sp, bs, ss,
        w_in[0], w_attn_out[0], w_four_out[0], w_o[0], ln1_g[0], ln1_b[0], w_router[0], b_router[0],
        w_up[0], b_up[0], w_down[0], b_down[0], ln2_g[0], ln2_b[0])
    return y_p.reshape(x_prompt.shape), y_s.reshape(x_sample.shape)
```
